```python
import jax, jax.numpy as jnp
from jax import lax
import numpy as np

D_MODEL = 1024
BATCH = 4
SEQ = 8192
DEPTH = 4

GRID_W = 64
CTX_LEN = 256
D_FF = 2816
FFN_RESIDUAL = 0.5
N_MOD = 9
ROPE_BASE = 10000.0
EPS = 1e-6

POOL_GROUPS = 4
POOL_CG = 64
POOL_WIDTH = POOL_GROUPS * POOL_CG
POOL_WINDOWS = (2, 4, 8, 16)
RET_HEADS = 6
RET_DK = 64
RET_DV = 128
RET_CHUNK = 128
RET_QK_W = RET_HEADS * RET_DK
RET_V_W = RET_HEADS * RET_DV
AB_SPLITS = (POOL_WIDTH, POOL_WIDTH + RET_QK_W, POOL_WIDTH + 2 * RET_QK_W, POOL_WIDTH + 2 * RET_QK_W + RET_V_W)
AB_IN = POOL_WIDTH + 2 * RET_QK_W + 2 * RET_V_W
AB_OUT = POOL_WIDTH + RET_V_W

MLA_HEADS = 8
MLA_Q_RANK = 384
MLA_KV_RANK = 256
MLA_NOPE = 64
MLA_ROPE = 32
MLA_V = 128
MLA_QK = MLA_NOPE + MLA_ROPE
C_IN = MLA_Q_RANK + MLA_KV_RANK + MLA_ROPE
Q_BLOCK = 128

N_EVEN = (DEPTH + 1) // 2
N_ODD = DEPTH // 2

kernel_name = "hybrid_pool_retention_mla_prefix_dit"

f32 = jnp.float32


def rms_norm(x, g):
    xf = x.astype(f32)
    y = xf * lax.rsqrt(jnp.mean(xf * xf, axis=-1, keepdims=True) + EPS)
    return (y * g.astype(f32)).astype(x.dtype)


def mod_slice(mod, n):
    return mod[..., n, :][..., None, :]


def ada_modulation(cond, w, b):
    m = jax.nn.silu(cond) @ w + b
    return m.reshape(*cond.shape[:-1], N_MOD, D_MODEL)


def modulate(x, g, shift, scale):
    return rms_norm(x, g) * (1.0 + scale) + shift


def swiglu(h, w_in, w_out):
    a, b = jnp.split(h @ w_in, 2, axis=-1)
    return (jax.nn.silu(a) * b) @ w_out


def macaron_half(x, mod, k, g, w_in, w_out):
    shift, scale, gate = mod_slice(mod, 3 * k), mod_slice(mod, 3 * k + 1), mod_slice(mod, 3 * k + 2)
    return x + FFN_RESIDUAL * gate * swiglu(modulate(x, g, shift, scale), w_in, w_out)


def axial_rope_table(row, col, rot_dim):
    n_freq = rot_dim // 4
    inv = ROPE_BASE ** (-jnp.arange(n_freq, dtype=f32) / n_freq)
    ang = jnp.concatenate([row.astype(f32)[:, None] * inv, col.astype(f32)[:, None] * inv], axis=-1)
    return jnp.cos(ang), jnp.sin(ang)


def apply_rope(x, cos, sin):
    xf = x.astype(f32).reshape(*x.shape[:-1], x.shape[-1] // 2, 2)
    x1, x2 = xf[..., 0], xf[..., 1]
    c, s = cos[:, None, :], sin[:, None, :]
    out = jnp.stack([x1 * c - x2 * s, x1 * s + x2 * c], axis=-1).reshape(x.shape)
    return out.astype(x.dtype)


def heads(t, d):
    return t.reshape(t.shape[0], t.shape[1], -1, d)


def centred_pool_minus_self(p, window):
    L = p.shape[1]
    lo = window // 2
    hi = window - 1 - lo
    pf = p.astype(f32)
    cs = jnp.concatenate([jnp.zeros_like(pf[:, :1]), jnp.cumsum(pf, axis=1)], axis=1)
    t = jnp.arange(L)
    start = jnp.maximum(t - lo, 0)
    end = jnp.minimum(t + hi + 1, L)
    total = jnp.take(cs, end, axis=1) - jnp.take(cs, start, axis=1)
    count = (end - start).astype(f32)[None, :, None]
    return (total / count - pf).astype(p.dtype)


def pool_mixer(p, w_pool, p_scale):
    B, L, _ = p.shape
    groups = jnp.split(p, POOL_GROUPS, axis=-1)
    diffs = jnp.stack([centred_pool_minus_self(gp, w) for gp, w in zip(groups, POOL_WINDOWS)], axis=2)
    y = jnp.einsum('blgc,gcd->blgd', diffs, w_pool)
    return y.reshape(B, L, POOL_WIDTH) * p_scale


def retention_chunked(q, k, v, log_decay, state0):
    B, L, H, dk = q.shape
    dv = v.shape[-1]
    n = L // RET_CHUNK
    qc = q.astype(f32).reshape(B, n, RET_CHUNK, H, dk)
    kc = k.astype(f32).reshape(B, n, RET_CHUNK, H, dk)
    vc = v.astype(f32).reshape(B, n, RET_CHUNK, H, dv)
    lg = log_decay.astype(f32)
    i = jnp.arange(RET_CHUNK, dtype=f32)
    rel = i[:, None] - i[None, :]
    dmask = jnp.where(rel[None] >= 0, jnp.exp(lg[:, None, None] * jnp.maximum(rel, 0.0)[None]), 0.0)
    scores = jnp.einsum('bnihd,bnjhd->bnhij', qc, kc) * dmask
    intra = jnp.einsum('bnhij,bnjhe->bnihe', scores, vc)
    zeta = jnp.exp(lg[:, None] * (RET_CHUNK - 1 - i))
    xi = jnp.exp(lg[:, None] * (i + 1.0))
    chunk_kv = jnp.einsum('bnjhd,hj,bnjhe->nbhde', kc, zeta, vc)
    chunk_decay = jnp.exp(lg * RET_CHUNK)[None, :, None, None]

    def step(state, kv):
        return chunk_decay * state + kv, state

    final, prev = lax.scan(step, state0.astype(f32), chunk_kv)
    cross = jnp.einsum('bnihd,hi,nbhde->bnihe', qc, xi, prev)
    out = (intra + cross).reshape(B, L, H, dv)
    return out.astype(v.dtype), final


def retention_final_state(k, v, log_decay):
    L = k.shape[1]
    w = jnp.exp(log_decay.astype(f32)[:, None] * (L - 1 - jnp.arange(L, dtype=f32)))
    return jnp.einsum('blhd,hl,blhe->bhde', k.astype(f32), w, v.astype(f32))


def bidir_retention(q, k, v, log_decay_fb, state_f, state_b):
    out_f, _ = retention_chunked(q, k, v, log_decay_fb[0], state_f)
    out_b, _ = retention_chunked(q[:, ::-1], k[:, ::-1], v[:, ::-1], log_decay_fb[1], state_b)
    return out_f + out_b[:, ::-1]


def retention_head_out(ret, g, norm_g):
    B, L = ret.shape[:2]
    y = rms_norm(ret, norm_g.reshape(RET_HEADS, RET_DV)).reshape(B, L, RET_V_W)
    return y * jax.nn.silu(g)


def split_ab(proj):
    p, q, k, v, g = jnp.split(proj, AB_SPLITS, axis=-1)
    return p, heads(q, RET_DK), heads(k, RET_DK) * (RET_DK ** -0.5), heads(v, RET_DV), g


def pool_retention_mixer(h, hc, rope, w_in, w_pool, p_scale, log_decay_fb, norm_g, w_out, need_ctx_out):
    p, q, k, v, g = split_ab(h @ w_in)
    pc, qc, kc, vc, gc = split_ab(hc @ w_in)
    state_f = retention_final_state(kc, vc, log_decay_fb[0])
    state_b = retention_final_state(kc[:, ::-1], vc[:, ::-1], log_decay_fb[1])
    q = apply_rope(q, *rope)
    k = apply_rope(k, *rope)
    ret = bidir_retention(q, k, v, log_decay_fb, state_f, state_b)
    y = jnp.concatenate([pool_mixer(p, w_pool, p_scale), retention_head_out(ret, g, norm_g)], axis=-1) @ w_out
    yc = None
    if need_ctx_out:
        zeros = jnp.zeros_like(state_f)
        ret_c = bidir_retention(qc, kc, vc, log_decay_fb, zeros, zeros)
        yc = jnp.concatenate([pool_mixer(pc, w_pool, p_scale), retention_head_out(ret_c, gc, norm_g)], axis=-1) @ w_out
    return y, yc


def mla_queries(qa, qa_g, w_qb, qn_g):
    q = heads(rms_norm(qa, qa_g) @ w_qb, MLA_QK)
    return rms_norm(q, qn_g)


def mla_keys_values(kva, kr, kva_g, w_kvb, kn_g):
    B, L = kva.shape[:2]
    kv = heads(rms_norm(kva, kva_g) @ w_kvb, MLA_NOPE + MLA_V)
    k_nope, v = kv[..., :MLA_NOPE], kv[..., MLA_NOPE:]
    k = jnp.concatenate([k_nope, jnp.broadcast_to(kr[:, :, None, :], (B, L, MLA_HEADS, MLA_ROPE))], axis=-1)
    return rms_norm(k, kn_g), v


def rope_tail(t, cos, sin):
    return jnp.concatenate([t[..., :MLA_NOPE], apply_rope(t[..., MLA_NOPE:], cos, sin)], axis=-1)


def attend(q, k, v):
    s = jnp.einsum('bqhd,bkhd->bhqk', q, k, preferred_element_type=f32) * (q.shape[-1] ** -0.5)
    p = jax.nn.softmax(s, axis=-1).astype(v.dtype)
    return jnp.einsum('bhqk,bkhe->bqhe', p, v)


def latent_attention(q, k, v, k_ctx, v_ctx):
    B, L, H, d = q.shape
    k_all = jnp.concatenate([k, k_ctx], axis=1)
    v_all = jnp.concatenate([v, v_ctx], axis=1)
    nb = L // Q_BLOCK
    qb = q.reshape(B, nb, Q_BLOCK, H, d).transpose(1, 0, 2, 3, 4)
    o = lax.map(lambda qi: attend(qi, k_all, v_all), qb)
    return o.transpose(1, 0, 2, 3, 4).reshape(B, L, H * v.shape[-1])


def mla_mixer(h, hc, rope, w_in, qa_g, kva_g, w_qb, w_kvb, qn_g, kn_g, w_out, need_ctx_out):
    splits = (MLA_Q_RANK, MLA_Q_RANK + MLA_KV_RANK)
    qa, kva, kr = jnp.split(h @ w_in, splits, axis=-1)
    qac, kvac, krc = jnp.split(hc @ w_in, splits, axis=-1)
    q = rope_tail(mla_queries(qa, qa_g, w_qb, qn_g), *rope)
    k, v = mla_keys_values(kva, kr, kva_g, w_kvb, kn_g)
    k = rope_tail(k, *rope)
    k_c, v_c = mla_keys_values(kvac, krc, kva_g, w_kvb, kn_g)
    y = latent_attention(q, k, v, k_c, v_c) @ w_out
    yc = None
    if need_ctx_out:
        q_c = mla_queries(qac, qa_g, w_qb, qn_g)
        o_c = attend(q_c, k_c, v_c)
        yc = o_c.reshape(o_c.shape[0], o_c.shape[1], MLA_HEADS * MLA_V) @ w_out
    return y, yc


def setup_inputs(seed: int = 0) -> dict:
    key = jax.random.key(seed)
    ks = jax.random.split(key, 32)

    def nrm(k, shape, fan_in, gain=1.0):
        return gain * jax.random.normal(k, shape, f32) * (fan_in ** -0.5)

    def gain_vec(k, shape):
        return 1.0 + 0.02 * jax.random.normal(k, shape, f32)

    base_ld = jnp.log1p(-(2.0 ** (-5.0 - jnp.arange(RET_HEADS, dtype=f32))))
    ret_log_decay = base_ld[None, None, :] * (1.0 + 0.05 * jax.random.normal(ks[12], (N_EVEN, 2, RET_HEADS), f32))
    return {
        "x": jax.random.normal(ks[0], (BATCH, SEQ, D_MODEL), f32),
        "c": jax.random.normal(ks[1], (BATCH, D_MODEL), f32),
        "ctx": jax.random.normal(ks[2], (BATCH, CTX_LEN, D_MODEL), f32),
        "c_ctx": jax.random.normal(ks[3], (D_MODEL,), f32),
        "ada_w": nrm(ks[4], (DEPTH, D_MODEL, N_MOD * D_MODEL), D_MODEL, 0.5),
        "ada_b": 0.02 * jax.random.normal(ks[5], (DEPTH, N_MOD * D_MODEL), f32),
        "norm_g": gain_vec(ks[6], (DEPTH, 3, D_MODEL)),
        "ffn_w_in": nrm(ks[7], (DEPTH, 2, D_MODEL, 2 * D_FF), D_MODEL),
        "ffn_w_out": nrm(ks[8], (DEPTH, 2, D_FF, D_MODEL), D_FF),
        "ab_w_in": nrm(ks[9], (N_EVEN, D_MODEL, AB_IN), D_MODEL),
        "pool_w": nrm(ks[10], (N_EVEN, POOL_GROUPS, POOL_CG, POOL_CG), POOL_CG),
        "pool_scale": gain_vec(ks[11], (N_EVEN, POOL_WIDTH)),
        "ret_log_decay": ret_log_decay,
        "ret_norm_g": gain_vec(ks[13], (N_EVEN, RET_V_W)),
        "ab_w_out": nrm(ks[14], (N_EVEN, AB_OUT, D_MODEL), AB_OUT),
        "mla_w_in": nrm(ks[15], (N_ODD, D_MODEL, C_IN), D_MODEL),
        "mla_qa_g": gain_vec(ks[16], (N_ODD, MLA_Q_RANK)),
        "mla_kva_g": gain_vec(ks[17], (N_ODD, MLA_KV_RANK)),
        "mla_w_qb": nrm(ks[18], (N_ODD, MLA_Q_RANK, MLA_HEADS * MLA_QK), MLA_Q_RANK),
        "mla_w_kvb": nrm(ks[19], (N_ODD, MLA_KV_RANK, MLA_HEADS * (MLA_NOPE + MLA_V)), MLA_KV_RANK),
        "mla_qn_g": gain_vec(ks[20], (N_ODD, MLA_QK)),
        "mla_kn_g": gain_vec(ks[21], (N_ODD, MLA_QK)),
        "mla_w_out": nrm(ks[22], (N_ODD, MLA_HEADS * MLA_V, D_MODEL), MLA_HEADS * MLA_V),
    }


def reference(x, c, ctx, c_ctx, ada_w, ada_b, norm_g, ffn_w_in, ffn_w_out,
              ab_w_in, pool_w, pool_scale, ret_log_decay, ret_norm_g, ab_w_out,
              mla_w_in, mla_qa_g, mla_kva_g, mla_w_qb, mla_w_kvb, mla_qn_g, mla_kn_g, mla_w_out):
    L = x.shape[1]
    ROWS = L // GRID_W
    row = jnp.repeat(jnp.arange(ROWS), GRID_W)
    col = jnp.tile(jnp.arange(GRID_W), ROWS)
    rope_ret = axial_rope_table(row, col, RET_DK)
    rope_mla = axial_rope_table(row, col, MLA_ROPE)
    xc = ctx
    for i in range(DEPTH):
        last = i == DEPTH - 1
        j = i // 2
        mod = ada_modulation(c, ada_w[i], ada_b[i])
        mod_c = ada_modulation(c_ctx, ada_w[i], ada_b[i])
        x = macaron_half(x, mod, 0, norm_g[i, 0], ffn_w_in[i, 0], ffn_w_out[i, 0])
        xc = macaron_half(xc, mod_c, 0, norm_g[i, 0], ffn_w_in[i, 0], ffn_w_out[i, 0])
        h = modulate(x, norm_g[i, 1], mod_slice(mod, 3), mod_slice(mod, 4))
        hc = modulate(xc, norm_g[i, 1], mod_slice(mod_c, 3), mod_slice(mod_c, 4))
        if i % 2 == 0:
            y, yc = pool_retention_mixer(h, hc, rope_ret, ab_w_in[j], pool_w[j], pool_scale[j],
                                         ret_log_decay[j], ret_norm_g[j], ab_w_out[j], not last)
        else:
            y, yc = mla_mixer(h, hc, rope_mla, mla_w_in[j], mla_qa_g[j], mla_kva_g[j], mla_w_qb[j],
                              mla_w_kvb[j], mla_qn_g[j], mla_kn_g[j], mla_w_out[j], not last)
        x = x + mod_slice(mod, 5) * y
        x = macaron_half(x, mod, 2, norm_g[i, 2], ffn_w_in[i, 1], ffn_w_out[i, 1])
        if not last:
            xc = xc + mod_slice(mod_c, 5) * yc
            xc = macaron_half(xc, mod_c, 2, norm_g[i, 2], ffn_w_in[i, 1], ffn_w_out[i, 1])
    return x
```

```python
import functools

import numpy as np
import jax
import jax.numpy as jnp
from jax import lax
from jax.experimental import pallas as pl
from jax.experimental.pallas import tpu as pltpu

F32 = jnp.float32
BF16 = jnp.bfloat16

D_MODEL = 1024
GRID_W = 64
D_FF = 2816
FFN_RESIDUAL = 0.5
N_MOD = 9
ROPE_BASE = 10000.0
EPS = 1e-6
POOL_GROUPS = 4
POOL_CG = 64
POOL_WIDTH = POOL_GROUPS * POOL_CG
POOL_WINDOWS = (2, 4, 8, 16)
POOL_HALO = 8
POOL_PAD = 128
RET_HEADS = 6
RET_PAIRS = RET_HEADS // 2
RET_DK = 64
RET_DV = 128
RET_CHUNK = 128
RET_QK_W = RET_HEADS * RET_DK
RET_V_W = RET_HEADS * RET_DV
AB_IN = POOL_WIDTH + 2 * RET_QK_W + 2 * RET_V_W
AB_EXT = AB_IN + 2 * RET_QK_W
MLA_HEADS = 8
MLA_Q_RANK = 384
MLA_KV_RANK = 256
MLA_NOPE = 64
MLA_ROPE = 32
MLA_V = 128
MLA_QK = MLA_NOPE + MLA_ROPE
HEAD_SLAB = 128

LANES = 128
V7X_VMEM_BYTES = 64 * 1024 * 1024
VMEM_LIMIT_BYTES = 56 * 1024 * 1024

ADA_TN = 1536
FFN_TM = 256
MIX_TM = 256
MLA_TM = 256
ATT_TQ = 512
ATT_TK = 512


def _cparams(sem):
    return pltpu.CompilerParams(dimension_semantics=sem, vmem_limit_bytes=VMEM_LIMIT_BYTES)


def _resident(shape):
    nd = len(shape)
    return pl.BlockSpec(shape, lambda *_: (0,) * nd, pipeline_mode=pl.Buffered(1))


def _silu(x):
    return x * (1.0 / (1.0 + jnp.exp(-x)))


def _rms_mod(x, g, shift, scale):
    ms = jnp.mean(x * x, axis=-1, keepdims=True)
    return (x * lax.rsqrt(ms + EPS) * g) * (1.0 + scale) + shift


def _ada_kernel(c_ref, w_ref, b_ref, o_ref):
    s = _silu(c_ref[...])
    o_ref[...] = jnp.dot(s, w_ref[...], precision=lax.Precision.HIGHEST,
                         preferred_element_type=F32) + b_ref[...]


def _ada_modulation(cond8, ada_w, ada_b):
    depth, d, n = ada_w.shape
    return pl.pallas_call(
        _ada_kernel,
        grid=(depth, n // ADA_TN),
        in_specs=[
            pl.BlockSpec((8, d), lambda i, j: (0, 0)),
            pl.BlockSpec((None, d, ADA_TN), lambda i, j: (i, 0, j)),
            pl.BlockSpec((None, 1, ADA_TN), lambda i, j: (i, 0, j)),
        ],
        out_specs=pl.BlockSpec((None, 8, ADA_TN), lambda i, j: (i, 0, j)),
        out_shape=jax.ShapeDtypeStruct((depth, 8, n), F32),
        compiler_params=_cparams(("parallel", "parallel")),
        name="ada_mod",
    )(cond8, ada_w, ada_b.reshape(depth, 1, n))


def _ffn_kernel(x_ref, mod_ref, g_ref, win_ref, wout_ref, o_ref, *, k):
    x = x_ref[...]
    h = _rms_mod(x, g_ref[...], mod_ref[3 * k:3 * k + 1, :], mod_ref[3 * k + 1:3 * k + 2, :])
    ab = jnp.dot(h.astype(BF16), win_ref[...], preferred_element_type=F32)
    a = ab[:, :D_FF]
    b = ab[:, D_FF:]
    act = (_silu(a) * b).astype(BF16)
    y = jnp.dot(act, wout_ref[...], preferred_element_type=F32)
    o_ref[...] = x + (FFN_RESIDUAL * mod_ref[3 * k + 2:3 * k + 3, :]) * y


def _ffn_half(x, mod8, mod_row, k, g, w_in, w_out):
    bsz, seq, d = x.shape
    tm = min(FFN_TM, seq)
    return pl.pallas_call(
        functools.partial(_ffn_kernel, k=k),
        grid=(bsz, seq // tm),
        in_specs=[
            pl.BlockSpec((None, tm, d), lambda b, t: (b, t, 0)),
            pl.BlockSpec((None, N_MOD, d), lambda b, t: (mod_row(b), 0, 0)),
            pl.BlockSpec((1, d), lambda b, t: (0, 0)),
            _resident(w_in.shape),
            _resident(w_out.shape),
        ],
        out_specs=pl.BlockSpec((None, tm, d), lambda b, t: (b, t, 0)),
        out_shape=jax.ShapeDtypeStruct(x.shape, F32),
        compiler_params=_cparams(("parallel", "parallel")),
        name="ffn_half",
    )(x, mod8, g.reshape(1, d), w_in, w_out)


def _pair_tables(lg_row, expo):
    return jnp.exp(lg_row * expo)


def _bd_mask():
    r = lax.broadcasted_iota(jnp.int32, (2 * RET_DK, 2 * RET_DV), 0)
    c = lax.broadcasted_iota(jnp.int32, (2 * RET_DK, 2 * RET_DV), 1)
    same_head = jnp.where(r < RET_DK, 0, 1) == jnp.where(c < RET_DV, 0, 1)
    return jnp.where(same_head, 1.0, 0.0)


def _abproj_kernel(x_ref, mod_ref, g_ref, w_ref, cos_ref, sin_ref, lgq_ref, lgv_ref, s0_ref,
                   p_ref, q_ref, k_ref, v_ref, sg_ref, cb_ref, sfin_ref, st_scr, *, tm, nt):
    t = pl.program_id(1)

    @pl.when(t == 0)
    def _():
        st_scr[...] = s0_ref[...]

    h = _rms_mod(x_ref[...], g_ref[...], mod_ref[3:4, :], mod_ref[4:5, :])
    proj = jnp.dot(h.astype(BF16), w_ref[...], preferred_element_type=F32)
    p_ref[...] = proj[:, :POOL_WIDTH]
    q0, k0 = POOL_WIDTH, POOL_WIDTH + RET_QK_W
    v0, g0 = k0 + RET_QK_W, k0 + RET_QK_W + RET_V_W
    qs0, ks0 = AB_IN, AB_IN + RET_QK_W
    v_ref[...] = proj[:, v0:v0 + RET_V_W].astype(BF16)
    sg_ref[...] = _silu(proj[:, g0:g0 + RET_V_W]).astype(BF16)

    cos = cos_ref[...]
    sin = sin_ref[...]
    row = lax.broadcasted_iota(jnp.int32, (RET_CHUNK, LANES), 0).astype(F32)
    bdm = _bd_mask()
    k_scale = RET_DK ** -0.5
    for j in range(RET_PAIRS):
        sl = slice(LANES * j, LANES * (j + 1))
        qr = proj[:, q0 + LANES * j:q0 + LANES * (j + 1)] * cos \
            + proj[:, qs0 + LANES * j:qs0 + LANES * (j + 1)] * sin
        kr = (proj[:, k0 + LANES * j:k0 + LANES * (j + 1)] * cos
              + proj[:, ks0 + LANES * j:ks0 + LANES * (j + 1)] * sin) * k_scale
        q_ref[:, sl] = qr.astype(BF16)
        k_ref[:, sl] = kr.astype(BF16)
        lgq = lgq_ref[:, sl]
        lgv = lgv_ref[:, 2 * LANES * j:2 * LANES * (j + 1)]
        xi = _pair_tables(lgq, float(RET_CHUNK) - row)
        zeta = _pair_tables(lgq, row)
        dec = jnp.exp(lgv * float(RET_CHUNK))
        vsl = slice(2 * LANES * j, 2 * LANES * (j + 1))
        for c in reversed(range(tm // RET_CHUNK)):
            rs = slice(c * RET_CHUNK, (c + 1) * RET_CHUNK)
            bd = st_scr[j]
            qx = (qr[rs] * xi).astype(BF16)
            cb_ref[rs, vsl] = jnp.dot(qx, bd.astype(BF16), preferred_element_type=F32)
            kz = (kr[rs] * zeta).astype(BF16)
            upd = lax.dot_general(kz, v_ref[rs, vsl], (((0,), (0,)), ((), ())),
                                  preferred_element_type=F32)
            st_scr[j] = dec * bd + bdm * upd

    @pl.when(t == nt - 1)
    def _():
        sfin_ref[...] = st_scr[...]


def _ab_project(x, mod8, mod_row, g, w_ext, cos, sin, lg_q, lg_v, s0):
    bsz, seq, d = x.shape
    tm = min(MIX_TM, seq)
    nt = seq // tm
    rev = lambda b, t: (b, nt - 1 - t, 0)
    tok = lambda w, dt: jax.ShapeDtypeStruct((bsz, seq, w), dt)
    st_shape = (RET_PAIRS, 2 * RET_DK, 2 * RET_DV)
    return pl.pallas_call(
        functools.partial(_abproj_kernel, tm=tm, nt=nt),
        grid=(bsz, nt),
        in_specs=[
            pl.BlockSpec((None, tm, d), rev),
            pl.BlockSpec((None, N_MOD, d), lambda b, t: (mod_row(b), 0, 0)),
            pl.BlockSpec((1, d), lambda b, t: (0, 0)),
            _resident(w_ext.shape),
            pl.BlockSpec((tm, LANES), lambda b, t: (nt - 1 - t, 0)),
            pl.BlockSpec((tm, LANES), lambda b, t: (nt - 1 - t, 0)),
            pl.BlockSpec((1, RET_QK_W), lambda b, t: (0, 0)),
            pl.BlockSpec((1, RET_V_W), lambda b, t: (0, 0)),
            pl.BlockSpec((None,) + st_shape, lambda b, t: (b, 0, 0, 0)),
        ],
        out_specs=[
            pl.BlockSpec((None, tm, POOL_WIDTH), rev),
            pl.BlockSpec((None, tm, RET_QK_W), rev),
            pl.BlockSpec((None, tm, RET_QK_W), rev),
            pl.BlockSpec((None, tm, RET_V_W), rev),
            pl.BlockSpec((None, tm, RET_V_W), rev),
            pl.BlockSpec((None, tm, RET_V_W), rev),
            pl.BlockSpec((None,) + st_shape, lambda b, t: (b, 0, 0, 0)),
        ],
        out_shape=[tok(POOL_WIDTH, F32), tok(RET_QK_W, BF16), tok(RET_QK_W, BF16),
                   tok(RET_V_W, BF16), tok(RET_V_W, BF16), tok(RET_V_W, F32),
                   jax.ShapeDtypeStruct((bsz,) + st_shape, F32)],
        scratch_shapes=[pltpu.VMEM(st_shape, F32)],
        compiler_params=_cparams(("arbitrary", "arbitrary")),
        name="ab_project",
    )(x, mod8, g.reshape(1, d), w_ext, cos, sin, lg_q, lg_v, s0)


def _retout_kernel(x_ref, mod_ref, pprev_ref, pcur_ref, pnext_ref, pm_ref, invc_ref,
                   q_ref, k_ref, v_ref, sg_ref, cb_ref, wpool_ref, pscale_ref, rng_ref, wout_ref,
                   lgfq_ref, lgfv_ref, lgbv_ref, s0_ref,
                   o_ref, sfin_ref, st_scr, ycat_scr, pe_scr, *, tm, nt):
    t = pl.program_id(1)

    @pl.when(t == 0)
    def _():
        st_scr[...] = s0_ref[...]

    pcur = pcur_ref[...]
    pe_scr[0:tm, :] = pcur
    pe_scr[tm:tm + POOL_HALO, :] = pprev_ref[...]
    pe_scr[tm + POOL_HALO:tm + 2 * POOL_HALO, :] = pnext_ref[...]
    pe_scr[tm + 2 * POOL_HALO:, :] = jnp.zeros((POOL_PAD - 2 * POOL_HALO, POOL_WIDTH), F32)
    pe = pe_scr[...]
    pe_hi = pe.astype(BF16)
    pe_lo = (pe - pe_hi.astype(F32)).astype(BF16)
    lane_grp = jnp.right_shift(lax.broadcasted_iota(jnp.int32, (tm, POOL_WIDTH), 1), 6)
    pooled = jnp.zeros((tm, POOL_WIDTH), F32)
    for gi in range(POOL_GROUPS):
        win = pm_ref[gi]
        tot = jnp.dot(win, pe_hi, preferred_element_type=F32) \
            + jnp.dot(win, pe_lo, preferred_element_type=F32)
        pooled = jnp.where(lane_grp == gi, tot, pooled)
    diffs = pooled * invc_ref[...] - pcur
    pool_y = jnp.dot(diffs.astype(BF16), wpool_ref[...], preferred_element_type=F32) * pscale_ref[...]
    ycat_scr[:, 0:POOL_WIDTH] = pool_y.astype(BF16)

    ri = lax.broadcasted_iota(jnp.int32, (RET_CHUNK, RET_CHUNK), 0)
    ci = lax.broadcasted_iota(jnp.int32, (RET_CHUNK, RET_CHUNK), 1)
    rel = (ri - ci).astype(F32)
    row = lax.broadcasted_iota(jnp.int32, (RET_CHUNK, LANES), 0).astype(F32)
    lane = lax.broadcasted_iota(jnp.int32, (RET_CHUNK, LANES), 1)
    bdm = _bd_mask()
    zero_bf = jnp.zeros((RET_CHUNK, LANES), BF16)
    for j in range(RET_PAIRS):
        sl = slice(LANES * j, LANES * (j + 1))
        vsl = slice(2 * LANES * j, 2 * LANES * (j + 1))
        lgq = lgfq_ref[:, sl]
        xi = _pair_tables(lgq, row + 1.0)
        zeta = _pair_tables(lgq, float(RET_CHUNK - 1) - row)
        dec = jnp.exp(lgfv_ref[:, vsl] * float(RET_CHUNK))
        masks = []
        for a in range(2):
            hs = slice(2 * LANES * j + LANES * a, 2 * LANES * j + LANES * (a + 1))
            mf = jnp.exp(lgfv_ref[:, hs] * jnp.maximum(rel, 0.0))
            mb = jnp.exp(lgbv_ref[:, hs] * jnp.maximum(-rel, 0.0))
            masks.append(jnp.where(rel > 0, mf, jnp.where(rel < 0, mb, 2.0)))
        for c in range(tm // RET_CHUNK):
            rs = slice(c * RET_CHUNK, (c + 1) * RET_CHUNK)
            q2 = q_ref[rs, sl]
            k2 = k_ref[rs, sl]
            v2 = v_ref[rs, vsl]
            bd = st_scr[j]
            qx = (q2.astype(F32) * xi).astype(BF16)
            ret = jnp.dot(qx, bd.astype(BF16), preferred_element_type=F32) + cb_ref[rs, vsl]
            for a in range(2):
                qa = jnp.where(lane < RET_DK if a == 0 else lane >= RET_DK, q2, zero_bf)
                s = lax.dot_general(qa, k2, (((1,), (1,)), ((), ())), preferred_element_type=F32)
                pa = (s * masks[a]).astype(BF16)
                hs_v = slice(LANES * a, LANES * (a + 1))
                r = ret[:, hs_v] + jnp.dot(pa, v2[:, hs_v], preferred_element_type=F32)
                ms = jnp.mean(r * r, axis=-1, keepdims=True)
                hcol = 2 * LANES * j + LANES * a
                y = (r * lax.rsqrt(ms + EPS) * rng_ref[:, hcol:hcol + LANES]) \
                    * sg_ref[rs, hcol:hcol + LANES].astype(F32)
                ycat_scr[rs, POOL_WIDTH + hcol:POOL_WIDTH + hcol + LANES] = y.astype(BF16)
            kz = (k2.astype(F32) * zeta).astype(BF16)
            upd = lax.dot_general(kz, v2, (((0,), (0,)), ((), ())), preferred_element_type=F32)
            st_scr[j] = dec * bd + bdm * upd

    yy = jnp.dot(ycat_scr[...], wout_ref[...], preferred_element_type=F32)
    o_ref[...] = x_ref[...] + mod_ref[5:6, :] * yy

    @pl.when(t == nt - 1)
    def _():
        sfin_ref[...] = st_scr[...]


def _pool_constants(tm, seq):
    nt = seq // tm
    kinds = [0] if nt == 1 else [0, 1, nt - 1]
    mats = np.zeros((len(kinds), POOL_GROUPS, tm, tm + POOL_PAD), np.float32)
    invc = np.zeros((len(kinds), tm, POOL_WIDTH), np.float32)
    i = np.arange(tm)
    for vi, tile in enumerate(kinds):
        pos = tile * tm + i
        colpos = np.full(tm + POOL_PAD, -1)
        colpos[:tm] = pos
        colpos[tm:tm + POOL_HALO] = tile * tm - POOL_HALO + np.arange(POOL_HALO)
        colpos[tm + POOL_HALO:tm + 2 * POOL_HALO] = (tile + 1) * tm + np.arange(POOL_HALO)
        for gi, w in enumerate(POOL_WINDOWS):
            lo = w // 2
            hi = w - 1 - lo
            start = np.maximum(pos - lo, 0)
            end = np.minimum(pos + hi + 1, seq)
            mats[vi, gi] = (colpos[None, :] >= start[:, None]) & (colpos[None, :] < end[:, None])
            invc[vi, :, gi * POOL_CG:(gi + 1) * POOL_CG] = (1.0 / (end - start))[:, None]
    return jnp.asarray(mats, BF16), jnp.asarray(invc, F32), nt


def _ret_out(x, mod8, mod_row, p, q, k, v, sg, cb, wpool_bd, pscale, rng, w_out, lgf_q, lgf_v, lgb_v, s0):
    bsz, seq, d = x.shape
    tm = min(MIX_TM, seq)
    pm, invc, nt = _pool_constants(tm, seq)
    hb = tm // POOL_HALO
    nhb = seq // POOL_HALO

    def variant(t):
        if nt == 1:
            return 0
        return jnp.where(t == 0, 0, jnp.where(t == nt - 1, 2, 1))

    cur = lambda b, t: (b, t, 0)
    st_shape = (RET_PAIRS, 2 * RET_DK, 2 * RET_DV)
    tokspec = lambda w: pl.BlockSpec((None, tm, w), cur)
    return pl.pallas_call(
        functools.partial(_retout_kernel, tm=tm, nt=nt),
        grid=(bsz, nt),
        in_specs=[
            tokspec(d),
            pl.BlockSpec((None, N_MOD, d), lambda b, t: (mod_row(b), 0, 0)),
            pl.BlockSpec((None, POOL_HALO, POOL_WIDTH), lambda b, t: (b, jnp.maximum(t * hb - 1, 0), 0)),
            tokspec(POOL_WIDTH),
            pl.BlockSpec((None, POOL_HALO, POOL_WIDTH),
                         lambda b, t: (b, jnp.minimum((t + 1) * hb, nhb - 1), 0)),
            pl.BlockSpec((None, POOL_GROUPS, tm, tm + POOL_PAD), lambda b, t: (variant(t), 0, 0, 0)),
            pl.BlockSpec((None, tm, POOL_WIDTH), lambda b, t: (variant(t), 0, 0)),
            tokspec(RET_QK_W), tokspec(RET_QK_W), tokspec(RET_V_W), tokspec(RET_V_W), tokspec(RET_V_W),
            _resident(wpool_bd.shape),
            pl.BlockSpec((1, POOL_WIDTH), lambda b, t: (0, 0)),
            pl.BlockSpec((1, RET_V_W), lambda b, t: (0, 0)),
            _resident(w_out.shape),
            pl.BlockSpec((1, RET_QK_W), lambda b, t: (0, 0)),
            pl.BlockSpec((1, RET_V_W), lambda b, t: (0, 0)),
            pl.BlockSpec((1, RET_V_W), lambda b, t: (0, 0)),
            pl.BlockSpec((None,) + st_shape, lambda b, t: (b, 0, 0, 0)),
        ],
        out_specs=[
            tokspec(d),
            pl.BlockSpec((None,) + st_shape, lambda b, t: (b, 0, 0, 0)),
        ],
        out_shape=[jax.ShapeDtypeStruct(x.shape, F32),
                   jax.ShapeDtypeStruct((bsz,) + st_shape, F32)],
        scratch_shapes=[pltpu.VMEM(st_shape, F32),
                        pltpu.VMEM((tm, d), BF16),
                        pltpu.VMEM((tm + POOL_PAD, POOL_WIDTH), F32)],
        compiler_params=_cparams(("arbitrary", "arbitrary")),
        name="ret_out",
    )(x, mod8, p, p, p, pm, invc, q, k, v, sg, cb, wpool_bd, pscale, rng, w_out,
      lgf_q, lgf_v, lgb_v, s0)


def _head_norm_rope(slab, gain, ta, tb):
    lane = lax.broadcasted_iota(jnp.int32, slab.shape, 1)
    sq = jnp.where(lane < MLA_QK, slab * slab, 0.0)
    ms = jnp.sum(sq, axis=-1, keepdims=True) * (1.0 / MLA_QK)
    n = slab * lax.rsqrt(ms + EPS) * gain
    return n * ta + pltpu.roll(n, HEAD_SLAB - MLA_ROPE, axis=1) * tb


def _mlaproj_kernel(x_ref, mod_ref, g_ref, win_ref, qag_ref, kvag_ref, wqb_ref, wkk_ref, wkv_ref,
                    qng_ref, kng_ref, qa_t_ref, qb_t_ref, ka_t_ref, kb_t_ref,
                    q_ref, k_ref, v_ref):
    h = _rms_mod(x_ref[...], g_ref[...], mod_ref[3:4, :], mod_ref[4:5, :])
    proj = jnp.dot(h.astype(BF16), win_ref[...], preferred_element_type=F32)
    qa = proj[:, :MLA_Q_RANK]
    kva = proj[:, MLA_Q_RANK:MLA_Q_RANK + MLA_KV_RANK]
    krs = proj[:, MLA_Q_RANK + MLA_KV_RANK:]
    qan = qa * lax.rsqrt(jnp.mean(qa * qa, axis=-1, keepdims=True) + EPS) * qag_ref[...]
    kvn = kva * lax.rsqrt(jnp.mean(kva * kva, axis=-1, keepdims=True) + EPS) * kvag_ref[...]
    qh = jnp.dot(qan.astype(BF16), wqb_ref[...], preferred_element_type=F32)
    kvn_b = kvn.astype(BF16)
    kh = jnp.dot(kvn_b, wkk_ref[...], preferred_element_type=F32)
    vh = jnp.dot(kvn_b, wkv_ref[...], preferred_element_type=F32)
    qta, qtb = qa_t_ref[...], qb_t_ref[...]
    kta, ktb = ka_t_ref[...], kb_t_ref[...]
    qg, kg = qng_ref[...], kng_ref[...]
    for hd in range(MLA_HEADS):
        sl = slice(HEAD_SLAB * hd, HEAD_SLAB * (hd + 1))
        q_ref[hd] = _head_norm_rope(qh[:, sl], qg, qta, qtb).astype(BF16)
        k_ref[hd] = _head_norm_rope(kh[:, sl] + krs, kg, kta, ktb).astype(BF16)
        v_ref[hd] = vh[:, sl].astype(BF16)


def _mla_project(x, mod8, mod_row, g, w_in, qa_g, kva_g, w_qb, w_kk, w_kv, qn_g, kn_g, q_ta, q_tb, k_ta, k_tb):
    bsz, seq, d = x.shape
    tm = min(MLA_TM, seq)
    head_out = jax.ShapeDtypeStruct((bsz, MLA_HEADS, seq, HEAD_SLAB), BF16)
    hspec = pl.BlockSpec((None, MLA_HEADS, tm, HEAD_SLAB), lambda b, t: (b, 0, t, 0))
    tspec = pl.BlockSpec((tm, HEAD_SLAB), lambda b, t: (t, 0))
    row = lambda n: pl.BlockSpec((1, n), lambda b, t: (0, 0))
    return pl.pallas_call(
        _mlaproj_kernel,
        grid=(bsz, seq // tm),
        in_specs=[
            pl.BlockSpec((None, tm, d), lambda b, t: (b, t, 0)),
            pl.BlockSpec((None, N_MOD, d), lambda b, t: (mod_row(b), 0, 0)),
            row(d),
            _resident(w_in.shape), row(MLA_Q_RANK), row(MLA_KV_RANK),
            _resident(w_qb.shape), _resident(w_kk.shape), _resident(w_kv.shape),
            row(HEAD_SLAB), row(HEAD_SLAB),
            tspec, tspec, tspec, tspec,
        ],
        out_specs=[hspec, hspec, hspec],
        out_shape=[head_out, head_out, head_out],
        compiler_params=_cparams(("parallel", "parallel")),
        name="mla_project",
    )(x, mod8, g.reshape(1, d), w_in, qa_g, kva_g, w_qb, w_kk, w_kv, qn_g, kn_g, q_ta, q_tb, k_ta, k_tb)


def _attn_kernel(*refs, n_lat_blocks, tk):
    if n_lat_blocks:
        q_ref, kc_ref, vc_ref, k_ref, v_ref, o_ref, m_scr, l_scr, acc_scr = refs
    else:
        q_ref, kc_ref, vc_ref, o_ref, m_scr, l_scr, acc_scr = refs
    q = q_ref[...]

    def block(kb, vb, first):
        s = lax.dot_general(q, kb, (((1,), (1,)), ((), ())), preferred_element_type=F32)
        m_cur = jnp.max(s, axis=-1, keepdims=True)
        if first:
            m_new = jnp.broadcast_to(m_cur, m_scr.shape)
        else:
            m_prev = m_scr[...]
            m_new = jnp.maximum(m_prev, m_cur)
        p = jnp.exp(s - jnp.concatenate([m_new] * (s.shape[1] // LANES), axis=1))
        l_cur = jnp.sum(p, axis=-1, keepdims=True)
        pv = jnp.dot(p.astype(BF16), vb, preferred_element_type=F32)
        if first:
            l_scr[...] = jnp.broadcast_to(l_cur, l_scr.shape)
            acc_scr[...] = pv
        else:
            alpha = jnp.exp(m_prev - m_new)
            l_scr[...] = alpha * l_scr[...] + l_cur
            acc_scr[...] = alpha * acc_scr[...] + pv
        m_scr[...] = m_new

    block(kc_ref[...], vc_ref[...], True)
    if n_lat_blocks:
        def body(i, carry):
            off = pl.multiple_of(i * tk, tk)
            block(k_ref[pl.ds(off, tk), :], v_ref[pl.ds(off, tk), :], False)
            return carry
        lax.fori_loop(0, n_lat_blocks, body, 0)
    o_ref[...] = (acc_scr[...] / l_scr[...]).astype(o_ref.dtype)


def _attention(q, kc, vc, k=None, v=None):
    bsz, nh, lq, hs = q.shape
    lc = kc.shape[2]
    tq = min(ATT_TQ, lq)
    qspec = pl.BlockSpec((None, None, tq, hs), lambda b, h, i: (b, h, i, 0))
    full = lambda n: pl.BlockSpec((None, None, n, hs), lambda b, h, i: (b, h, 0, 0))
    in_specs = [qspec, full(lc), full(lc)]
    args = [q, kc, vc]
    n_lat = 0
    tk = ATT_TK
    if k is not None:
        lk = k.shape[2]
        tk = min(ATT_TK, lk)
        n_lat = lk // tk
        in_specs += [full(lk), full(lk)]
        args += [k, v]
    return pl.pallas_call(
        functools.partial(_attn_kernel, n_lat_blocks=n_lat, tk=tk),
        grid=(bsz, nh, lq // tq),
        in_specs=in_specs,
        out_specs=pl.BlockSpec((None, tq, hs), lambda b, h, i: (b, i, h)),
        out_shape=jax.ShapeDtypeStruct((bsz, lq, nh * hs), BF16),
        scratch_shapes=[pltpu.VMEM((tq, LANES), F32), pltpu.VMEM((tq, LANES), F32),
                        pltpu.VMEM((tq, hs), F32)],
        compiler_params=_cparams(("parallel", "parallel", "arbitrary")),
        name="mla_attention",
    )(*args)


def _mlaout_kernel(x_ref, mod_ref, o_ref_in, w_ref, out_ref):
    y = jnp.dot(o_ref_in[...], w_ref[...], preferred_element_type=F32)
    out_ref[...] = x_ref[...] + mod_ref[5:6, :] * y


def _mla_out(x, mod8, mod_row, o, w_out):
    bsz, seq, d = x.shape
    tm = min(MLA_TM, seq)
    return pl.pallas_call(
        _mlaout_kernel,
        grid=(bsz, seq // tm),
        in_specs=[
            pl.BlockSpec((None, tm, d), lambda b, t: (b, t, 0)),
            pl.BlockSpec((None, N_MOD, d), lambda b, t: (mod_row(b), 0, 0)),
            pl.BlockSpec((None, tm, o.shape[-1]), lambda b, t: (b, t, 0)),
            _resident(w_out.shape),
        ],
        out_specs=pl.BlockSpec((None, tm, d), lambda b, t: (b, t, 0)),
        out_shape=jax.ShapeDtypeStruct(x.shape, F32),
        compiler_params=_cparams(("parallel", "parallel")),
        name="mla_out",
    )(x, mod8, o, w_out)


def _pair_swap_cols(w):
    w2 = w.reshape(w.shape[:-1] + (w.shape[-1] // 2, 2))
    return jnp.stack([-w2[..., 1], w2[..., 0]], axis=-1).reshape(w.shape)


def _rope_angles(seq, rot_dim):
    pos = jnp.arange(seq)
    row = (pos // GRID_W).astype(F32)
    col = (pos % GRID_W).astype(F32)
    n_freq = rot_dim // 4
    inv = ROPE_BASE ** (-jnp.arange(n_freq, dtype=F32) / n_freq)
    ang = jnp.concatenate([row[:, None] * inv, col[:, None] * inv], axis=-1)
    return jnp.repeat(jnp.cos(ang), 2, axis=-1), jnp.repeat(jnp.sin(ang), 2, axis=-1)


def _ret_tables(seq, ctx_len):
    cos, sin = _rope_angles(seq, RET_DK)
    cos2, sin2 = jnp.tile(cos, (1, 2)), jnp.tile(sin, (1, 2))
    return cos2, sin2, jnp.ones((ctx_len, LANES), F32), jnp.zeros((ctx_len, LANES), F32)


def _mla_tables(seq, ctx_len):
    cos, sin = _rope_angles(seq, MLA_ROPE)
    one = jnp.ones((seq, MLA_NOPE), F32)
    zero_tail = jnp.zeros((seq, HEAD_SLAB - MLA_QK), F32)
    zero_head = jnp.zeros((seq, MLA_NOPE), F32)
    ta = jnp.concatenate([one, cos, zero_tail], axis=-1)
    tb = jnp.concatenate([zero_head, sin, zero_tail], axis=-1)
    ident = jnp.concatenate([jnp.ones((ctx_len, MLA_QK), F32),
                             jnp.zeros((ctx_len, HEAD_SLAB - MLA_QK), F32)], axis=-1)
    return ta, tb, ident, jnp.zeros((ctx_len, HEAD_SLAB), F32)


def _slab_gain(gvec):
    rope = gvec[MLA_NOPE:].reshape(MLA_ROPE // 2, 2)
    swapped = jnp.stack([rope[:, 1], rope[:, 0]], axis=-1).reshape(MLA_ROPE)
    return jnp.concatenate([gvec, swapped]).reshape(1, HEAD_SLAB)


def _prep_even(ab_w_in, pool_w, ab_w_out):
    wq = ab_w_in[:, POOL_WIDTH:POOL_WIDTH + RET_QK_W]
    wk = ab_w_in[:, POOL_WIDTH + RET_QK_W:POOL_WIDTH + 2 * RET_QK_W]
    w_ext = jnp.concatenate([ab_w_in, _pair_swap_cols(wq), _pair_swap_cols(wk)], axis=-1).astype(BF16)
    wpool_bd = jnp.zeros((POOL_WIDTH, POOL_WIDTH), F32)
    for gi in range(POOL_GROUPS):
        s = slice(gi * POOL_CG, (gi + 1) * POOL_CG)
        wpool_bd = wpool_bd.at[s, s].set(pool_w[gi])
    return w_ext, wpool_bd.astype(BF16), ab_w_out.astype(BF16)


def _prep_odd(w_in, w_qb, w_kvb, w_out):
    d = w_in.shape[0]
    wkr = w_in[:, MLA_Q_RANK + MLA_KV_RANK:]
    kr_slab = jnp.concatenate([jnp.zeros((d, MLA_NOPE), F32), wkr, _pair_swap_cols(wkr)], axis=-1)
    w_in_ext = jnp.concatenate([w_in[:, :MLA_Q_RANK + MLA_KV_RANK], kr_slab], axis=-1).astype(BF16)
    wq3 = w_qb.reshape(MLA_Q_RANK, MLA_HEADS, MLA_QK)
    wq_slab = jnp.concatenate([wq3, _pair_swap_cols(wq3[..., MLA_NOPE:])], axis=-1)
    w_qb_ext = wq_slab.reshape(MLA_Q_RANK, MLA_HEADS * HEAD_SLAB).astype(BF16)
    wkv3 = w_kvb.reshape(MLA_KV_RANK, MLA_HEADS, MLA_NOPE + MLA_V)
    wk_slab = jnp.concatenate([wkv3[..., :MLA_NOPE],
                               jnp.zeros((MLA_KV_RANK, MLA_HEADS, HEAD_SLAB - MLA_NOPE), F32)], axis=-1)
    w_kk = wk_slab.reshape(MLA_KV_RANK, MLA_HEADS * HEAD_SLAB).astype(BF16)
    w_kv = wkv3[..., MLA_NOPE:].reshape(MLA_KV_RANK, MLA_HEADS * MLA_V).astype(BF16)
    return w_in_ext, w_qb_ext, w_kk, w_kv, w_out.astype(BF16)


def kernel(x, c, ctx, c_ctx, ada_w, ada_b, norm_g, ffn_w_in, ffn_w_out, ab_w_in, pool_w, pool_scale,
           ret_log_decay, ret_norm_g, ab_w_out, mla_w_in, mla_qa_g, mla_kva_g, mla_w_qb, mla_w_kvb,
           mla_qn_g, mla_kn_g, mla_w_out):
    bsz, seq, d = x.shape
    ctx_len = ctx.shape[1]
    depth = ada_w.shape[0]
    ctx_row = bsz
    cond8 = jnp.zeros((8, d), F32).at[:bsz].set(c).at[ctx_row].set(c_ctx)
    mod_all = _ada_modulation(cond8, ada_w, ada_b).reshape(depth, 8, N_MOD, d)
    lat_row = lambda b: b
    ctx_mod_row = lambda b: ctx_row

    ret_cos, ret_sin, ret_cos_c, ret_sin_c = _ret_tables(seq, ctx_len)
    m_ta, m_tb, m_ta_c, m_tb_c = _mla_tables(seq, ctx_len)
    q_scale = MLA_QK ** -0.5
    zero_state = jnp.zeros((bsz, RET_PAIRS, 2 * RET_DK, 2 * RET_DV), F32)

    xc = ctx
    for i in range(depth):
        last = i == depth - 1
        j = i // 2
        mod8 = mod_all[i]
        w1_in, w1_out = ffn_w_in[i, 0].astype(BF16), ffn_w_out[i, 0].astype(BF16)
        w2_in, w2_out = ffn_w_in[i, 1].astype(BF16), ffn_w_out[i, 1].astype(BF16)
        x = _ffn_half(x, mod8, lat_row, 0, norm_g[i, 0], w1_in, w1_out)
        xc = _ffn_half(xc, mod8, ctx_mod_row, 0, norm_g[i, 0], w1_in, w1_out)
        if i % 2 == 0:
            w_ext, wpool_bd, w_out = _prep_even(ab_w_in[j], pool_w[j], ab_w_out[j])
            lg = ret_log_decay[j]
            lgf_q = jnp.repeat(lg[0], RET_DK).reshape(1, RET_QK_W)
            lgb_q = jnp.repeat(lg[1], RET_DK).reshape(1, RET_QK_W)
            lgf_v = jnp.repeat(lg[0], RET_DV).reshape(1, RET_V_W)
            lgb_v = jnp.repeat(lg[1], RET_DV).reshape(1, RET_V_W)
            pscale = pool_scale[j].reshape(1, POOL_WIDTH)
            rng = ret_norm_g[j].reshape(1, RET_V_W)
            pc, qc, kc, vc, sgc, cbc, state_b = _ab_project(
                xc, mod8, ctx_mod_row, norm_g[i, 1], w_ext, ret_cos_c, ret_sin_c, lgb_q, lgb_v, zero_state)
            xc_mixed, state_f = _ret_out(xc, mod8, ctx_mod_row, pc, qc, kc, vc, sgc, cbc, wpool_bd, pscale,
                                         rng, w_out, lgf_q, lgf_v, lgb_v, zero_state)
            p, q, k, v, sg, cb, _ = _ab_project(
                x, mod8, lat_row, norm_g[i, 1], w_ext, ret_cos, ret_sin, lgb_q, lgb_v, state_b)
            x, _ = _ret_out(x, mod8, lat_row, p, q, k, v, sg, cb, wpool_bd, pscale, rng, w_out,
                            lgf_q, lgf_v, lgb_v, state_f)
        else:
            w_in_ext, w_qb_ext, w_kk, w_kv, w_out = _prep_odd(mla_w_in[j], mla_w_qb[j], mla_w_kvb[j],
                                                              mla_w_out[j])
            qa_g = mla_qa_g[j].reshape(1, MLA_Q_RANK)
            kva_g = mla_kva_g[j].reshape(1, MLA_KV_RANK)
            qn_g, kn_g = _slab_gain(mla_qn_g[j]), _slab_gain(mla_kn_g[j])
            qc, kc, vc = _mla_project(xc, mod8, ctx_mod_row, norm_g[i, 1], w_in_ext, qa_g, kva_g, w_qb_ext,
                                      w_kk, w_kv, qn_g, kn_g, m_ta_c * q_scale, m_tb_c, m_ta_c, m_tb_c)
            q, k, v = _mla_project(x, mod8, lat_row, norm_g[i, 1], w_in_ext, qa_g, kva_g, w_qb_ext,
                                   w_kk, w_kv, qn_g, kn_g, m_ta * q_scale, m_tb * q_scale, m_ta, m_tb)
            o = _attention(q, kc, vc, k, v)
            x = _mla_out(x, mod8, lat_row, o, w_out)
            if not last:
                oc = _attention(qc, kc, vc)
                xc_mixed = _mla_out(xc, mod8, ctx_mod_row, oc, w_out)
        x = _ffn_half(x, mod8, lat_row, 2, norm_g[i, 2], w2_in, w2_out)
        if not last:
            xc = _ffn_half(xc_mixed, mod8, ctx_mod_row, 2, norm_g[i, 2], w2_in, w2_out)
    return x
```

```python
import functools

import numpy as np
import jax
import jax.numpy as jnp
from jax import lax
from jax.experimental import pallas as pl
from jax.experimental.pallas import tpu as pltpu

F32 = jnp.float32
BF16 = jnp.bfloat16

D_MODEL = 1024
GRID_W = 64
D_FF = 2816
FFN_RESIDUAL = 0.5
N_MOD = 9
ROPE_BASE = 10000.0
EPS = 1e-6
POOL_GROUPS = 4
POOL_CG = 64
POOL_WIDTH = POOL_GROUPS * POOL_CG
POOL_WINDOWS = (2, 4, 8, 16)
POOL_HALO = 8
POOL_PAD = 128
RET_HEADS = 6
RET_PAIRS = RET_HEADS // 2
RET_DK = 64
RET_DV = 128
RET_CHUNK = 128
RET_QK_W = RET_HEADS * RET_DK
RET_V_W = RET_HEADS * RET_DV
AB_IN = POOL_WIDTH + 2 * RET_QK_W + 2 * RET_V_W
AB_EXT = AB_IN + 2 * RET_QK_W
MLA_HEADS = 8
MLA_Q_RANK = 384
MLA_KV_RANK = 256
MLA_NOPE = 64
MLA_ROPE = 32
MLA_V = 128
MLA_QK = MLA_NOPE + MLA_ROPE
HEAD_SLAB = 128
V_ROWS = MLA_V + 16
LOG2_E = 1.4426950408889634

LANES = 128
V7X_VMEM_BYTES = 64 * 1024 * 1024
VMEM_LIMIT_BYTES = 56 * 1024 * 1024

ADA_TN = 1536
FFN_TM = 256
MIX_TM = 256
MLA_TM = 256
ATT_TQ = 512
ATT_TK = 512
ATT_GROUP = 4


def _cparams(sem):
    return pltpu.CompilerParams(dimension_semantics=sem, vmem_limit_bytes=VMEM_LIMIT_BYTES)


def _resident(shape):
    nd = len(shape)
    return pl.BlockSpec(shape, lambda *_: (0,) * nd, pipeline_mode=pl.Buffered(1))


def _silu(x):
    return x * (1.0 / (1.0 + jnp.exp(-x)))


def _rms_mod(x, g, shift, scale):
    ms = jnp.mean(x * x, axis=-1, keepdims=True)
    return (x * lax.rsqrt(ms + EPS) * g) * (1.0 + scale) + shift


def _ada_kernel(c_ref, w_ref, b_ref, o_ref):
    s = _silu(c_ref[...])
    o_ref[...] = jnp.dot(s, w_ref[...], precision=lax.Precision.HIGHEST,
                         preferred_element_type=F32) + b_ref[...]


def _ada_modulation(cond8, ada_w, ada_b):
    depth, d, n = ada_w.shape
    return pl.pallas_call(
        _ada_kernel,
        grid=(depth, n // ADA_TN),
        in_specs=[
            pl.BlockSpec((8, d), lambda i, j: (0, 0)),
            pl.BlockSpec((None, d, ADA_TN), lambda i, j: (i, 0, j)),
            pl.BlockSpec((None, 1, ADA_TN), lambda i, j: (i, 0, j)),
        ],
        out_specs=pl.BlockSpec((None, 8, ADA_TN), lambda i, j: (i, 0, j)),
        out_shape=jax.ShapeDtypeStruct((depth, 8, n), F32),
        compiler_params=_cparams(("parallel", "parallel")),
        name="ada_mod",
    )(cond8, ada_w, ada_b.reshape(depth, 1, n))


def _ffn_kernel(x_ref, mod_ref, g_ref, win_ref, wout_ref, o_ref, *, k):
    x = x_ref[...]
    h = _rms_mod(x, g_ref[...], mod_ref[3 * k:3 * k + 1, :], mod_ref[3 * k + 1:3 * k + 2, :])
    ab = jnp.dot(h.astype(BF16), win_ref[...], preferred_element_type=F32)
    a = ab[:, :D_FF]
    b = ab[:, D_FF:]
    act = (_silu(a) * b).astype(BF16)
    y = jnp.dot(act, wout_ref[...], preferred_element_type=F32)
    o_ref[...] = x + (FFN_RESIDUAL * mod_ref[3 * k + 2:3 * k + 3, :]) * y


def _ffn_half(x, mod8, mod_row, k, g, w_in, w_out):
    bsz, seq, d = x.shape
    tm = min(FFN_TM, seq)
    return pl.pallas_call(
        functools.partial(_ffn_kernel, k=k),
        grid=(bsz, seq // tm),
        in_specs=[
            pl.BlockSpec((None, tm, d), lambda b, t: (b, t, 0)),
            pl.BlockSpec((None, N_MOD, d), lambda b, t: (mod_row(b), 0, 0)),
            pl.BlockSpec((1, d), lambda b, t: (0, 0)),
            _resident(w_in.shape),
            _resident(w_out.shape),
        ],
        out_specs=pl.BlockSpec((None, tm, d), lambda b, t: (b, t, 0)),
        out_shape=jax.ShapeDtypeStruct(x.shape, F32),
        compiler_params=_cparams(("parallel", "parallel")),
        name="ffn_half",
    )(x, mod8, g.reshape(1, d), w_in, w_out)


def _pair_tables(lg_row, expo):
    return jnp.exp(lg_row * expo)


def _bd_mask():
    r = lax.broadcasted_iota(jnp.int32, (2 * RET_DK, 2 * RET_DV), 0)
    c = lax.broadcasted_iota(jnp.int32, (2 * RET_DK, 2 * RET_DV), 1)
    same_head = jnp.where(r < RET_DK, 0, 1) == jnp.where(c < RET_DV, 0, 1)
    return jnp.where(same_head, 1.0, 0.0)


def _abproj_kernel(x_ref, mod_ref, g_ref, w_ref, cos_ref, sin_ref, lgq_ref, lgv_ref, s0_ref,
                   p_ref, q_ref, k_ref, v_ref, sg_ref, cb_ref, sfin_ref, st_scr, *, tm, nt):
    t = pl.program_id(1)

    @pl.when(t == 0)
    def _():
        st_scr[...] = s0_ref[...]

    h = _rms_mod(x_ref[...], g_ref[...], mod_ref[3:4, :], mod_ref[4:5, :])
    proj = jnp.dot(h.astype(BF16), w_ref[...], preferred_element_type=F32)
    p_ref[...] = proj[:, :POOL_WIDTH]
    q0, k0 = POOL_WIDTH, POOL_WIDTH + RET_QK_W
    v0, g0 = k0 + RET_QK_W, k0 + RET_QK_W + RET_V_W
    qs0, ks0 = AB_IN, AB_IN + RET_QK_W
    v_ref[...] = proj[:, v0:v0 + RET_V_W].astype(BF16)
    sg_ref[...] = _silu(proj[:, g0:g0 + RET_V_W]).astype(BF16)

    cos = cos_ref[...]
    sin = sin_ref[...]
    row = lax.broadcasted_iota(jnp.int32, (RET_CHUNK, LANES), 0).astype(F32)
    bdm = _bd_mask()
    k_scale = RET_DK ** -0.5
    for j in range(RET_PAIRS):
        sl = slice(LANES * j, LANES * (j + 1))
        qr = proj[:, q0 + LANES * j:q0 + LANES * (j + 1)] * cos \
            + proj[:, qs0 + LANES * j:qs0 + LANES * (j + 1)] * sin
        kr = (proj[:, k0 + LANES * j:k0 + LANES * (j + 1)] * cos
              + proj[:, ks0 + LANES * j:ks0 + LANES * (j + 1)] * sin) * k_scale
        q_ref[:, sl] = qr.astype(BF16)
        k_ref[:, sl] = kr.astype(BF16)
        lgq = lgq_ref[:, sl]
        lgv = lgv_ref[:, 2 * LANES * j:2 * LANES * (j + 1)]
        xi = _pair_tables(lgq, float(RET_CHUNK) - row)
        zeta = _pair_tables(lgq, row)
        dec = jnp.exp(lgv * float(RET_CHUNK))
        vsl = slice(2 * LANES * j, 2 * LANES * (j + 1))
        for c in reversed(range(tm // RET_CHUNK)):
            rs = slice(c * RET_CHUNK, (c + 1) * RET_CHUNK)
            bd = st_scr[j]
            qx = (qr[rs] * xi).astype(BF16)
            cb_ref[rs, vsl] = jnp.dot(qx, bd.astype(BF16), preferred_element_type=F32)
            kz = (kr[rs] * zeta).astype(BF16)
            upd = lax.dot_general(kz, v_ref[rs, vsl], (((0,), (0,)), ((), ())),
                                  preferred_element_type=F32)
            st_scr[j] = dec * bd + bdm * upd

    @pl.when(t == nt - 1)
    def _():
        sfin_ref[...] = st_scr[...]


def _ab_project(x, mod8, mod_row, g, w_ext, cos, sin, lg_q, lg_v, s0):
    bsz, seq, d = x.shape
    tm = min(MIX_TM, seq)
    nt = seq // tm
    rev = lambda b, t: (b, nt - 1 - t, 0)
    tok = lambda w, dt: jax.ShapeDtypeStruct((bsz, seq, w), dt)
    st_shape = (RET_PAIRS, 2 * RET_DK, 2 * RET_DV)
    return pl.pallas_call(
        functools.partial(_abproj_kernel, tm=tm, nt=nt),
        grid=(bsz, nt),
        in_specs=[
            pl.BlockSpec((None, tm, d), rev),
            pl.BlockSpec((None, N_MOD, d), lambda b, t: (mod_row(b), 0, 0)),
            pl.BlockSpec((1, d), lambda b, t: (0, 0)),
            _resident(w_ext.shape),
            pl.BlockSpec((tm, LANES), lambda b, t: (nt - 1 - t, 0)),
            pl.BlockSpec((tm, LANES), lambda b, t: (nt - 1 - t, 0)),
            pl.BlockSpec((1, RET_QK_W), lambda b, t: (0, 0)),
            pl.BlockSpec((1, RET_V_W), lambda b, t: (0, 0)),
            pl.BlockSpec((None,) + st_shape, lambda b, t: (b, 0, 0, 0)),
        ],
        out_specs=[
            pl.BlockSpec((None, tm, POOL_WIDTH), rev),
            pl.BlockSpec((None, tm, RET_QK_W), rev),
            pl.BlockSpec((None, tm, RET_QK_W), rev),
            pl.BlockSpec((None, tm, RET_V_W), rev),
            pl.BlockSpec((None, tm, RET_V_W), rev),
            pl.BlockSpec((None, tm, RET_V_W), rev),
            pl.BlockSpec((None,) + st_shape, lambda b, t: (b, 0, 0, 0)),
        ],
        out_shape=[tok(POOL_WIDTH, F32), tok(RET_QK_W, BF16), tok(RET_QK_W, BF16),
                   tok(RET_V_W, BF16), tok(RET_V_W, BF16), tok(RET_V_W, F32),
                   jax.ShapeDtypeStruct((bsz,) + st_shape, F32)],
        scratch_shapes=[pltpu.VMEM(st_shape, F32)],
        compiler_params=_cparams(("arbitrary", "arbitrary")),
        name="ab_project",
    )(x, mod8, g.reshape(1, d), w_ext, cos, sin, lg_q, lg_v, s0)


def _retout_kernel(x_ref, mod_ref, pprev_ref, pcur_ref, pnext_ref, pm_ref, invc_ref,
                   q_ref, k_ref, v_ref, sg_ref, cb_ref, wpool_ref, pscale_ref, rng_ref, wout_ref,
                   lgfq_ref, lgfv_ref, lgbv_ref, s0_ref,
                   o_ref, sfin_ref, st_scr, ycat_scr, pe_scr, *, tm, nt):
    t = pl.program_id(1)

    @pl.when(t == 0)
    def _():
        st_scr[...] = s0_ref[...]

    pcur = pcur_ref[...]
    pe_scr[0:tm, :] = pcur
    pe_scr[tm:tm + POOL_HALO, :] = pprev_ref[...]
    pe_scr[tm + POOL_HALO:tm + 2 * POOL_HALO, :] = pnext_ref[...]
    pe_scr[tm + 2 * POOL_HALO:, :] = jnp.zeros((POOL_PAD - 2 * POOL_HALO, POOL_WIDTH), F32)
    pe = pe_scr[...]
    pe_hi = pe.astype(BF16)
    pe_lo = (pe - pe_hi.astype(F32)).astype(BF16)
    lane_grp = jnp.right_shift(lax.broadcasted_iota(jnp.int32, (tm, POOL_WIDTH), 1), 6)
    pooled = jnp.zeros((tm, POOL_WIDTH), F32)
    for gi in range(POOL_GROUPS):
        win = pm_ref[gi]
        tot = jnp.dot(win, pe_hi, preferred_element_type=F32) \
            + jnp.dot(win, pe_lo, preferred_element_type=F32)
        pooled = jnp.where(lane_grp == gi, tot, pooled)
    diffs = pooled * invc_ref[...] - pcur
    pool_y = jnp.dot(diffs.astype(BF16), wpool_ref[...], preferred_element_type=F32) * pscale_ref[...]
    ycat_scr[:, 0:POOL_WIDTH] = pool_y.astype(BF16)

    ri = lax.broadcasted_iota(jnp.int32, (RET_CHUNK, RET_CHUNK), 0)
    ci = lax.broadcasted_iota(jnp.int32, (RET_CHUNK, RET_CHUNK), 1)
    rel = (ri - ci).astype(F32)
    row = lax.broadcasted_iota(jnp.int32, (RET_CHUNK, LANES), 0).astype(F32)
    lane = lax.broadcasted_iota(jnp.int32, (RET_CHUNK, LANES), 1)
    bdm = _bd_mask()
    zero_bf = jnp.zeros((RET_CHUNK, LANES), BF16)
    for j in range(RET_PAIRS):
        sl = slice(LANES * j, LANES * (j + 1))
        vsl = slice(2 * LANES * j, 2 * LANES * (j + 1))
        lgq = lgfq_ref[:, sl]
        xi = _pair_tables(lgq, row + 1.0)
        zeta = _pair_tables(lgq, float(RET_CHUNK - 1) - row)
        dec = jnp.exp(lgfv_ref[:, vsl] * float(RET_CHUNK))
        masks = []
        for a in range(2):
            hs = slice(2 * LANES * j + LANES * a, 2 * LANES * j + LANES * (a + 1))
            mf = jnp.exp(lgfv_ref[:, hs] * jnp.maximum(rel, 0.0))
            mb = jnp.exp(lgbv_ref[:, hs] * jnp.maximum(-rel, 0.0))
            masks.append(jnp.where(rel > 0, mf, jnp.where(rel < 0, mb, 2.0)))
        for c in range(tm // RET_CHUNK):
            rs = slice(c * RET_CHUNK, (c + 1) * RET_CHUNK)
            q2 = q_ref[rs, sl]
            k2 = k_ref[rs, sl]
            v2 = v_ref[rs, vsl]
            bd = st_scr[j]
            qx = (q2.astype(F32) * xi).astype(BF16)
            ret = jnp.dot(qx, bd.astype(BF16), preferred_element_type=F32) + cb_ref[rs, vsl]
            for a in range(2):
                qa = jnp.where(lane < RET_DK if a == 0 else lane >= RET_DK, q2, zero_bf)
                s = lax.dot_general(qa, k2, (((1,), (1,)), ((), ())), preferred_element_type=F32)
                pa = (s * masks[a]).astype(BF16)
                hs_v = slice(LANES * a, LANES * (a + 1))
                r = ret[:, hs_v] + jnp.dot(pa, v2[:, hs_v], preferred_element_type=F32)
                ms = jnp.mean(r * r, axis=-1, keepdims=True)
                hcol = 2 * LANES * j + LANES * a
                y = (r * lax.rsqrt(ms + EPS) * rng_ref[:, hcol:hcol + LANES]) \
                    * sg_ref[rs, hcol:hcol + LANES].astype(F32)
                ycat_scr[rs, POOL_WIDTH + hcol:POOL_WIDTH + hcol + LANES] = y.astype(BF16)
            kz = (k2.astype(F32) * zeta).astype(BF16)
            upd = lax.dot_general(kz, v2, (((0,), (0,)), ((), ())), preferred_element_type=F32)
            st_scr[j] = dec * bd + bdm * upd

    yy = jnp.dot(ycat_scr[...], wout_ref[...], preferred_element_type=F32)
    o_ref[...] = x_ref[...] + mod_ref[5:6, :] * yy

    @pl.when(t == nt - 1)
    def _():
        sfin_ref[...] = st_scr[...]


def _pool_constants(tm, seq):
    nt = seq // tm
    kinds = [0] if nt == 1 else [0, 1, nt - 1]
    mats = np.zeros((len(kinds), POOL_GROUPS, tm, tm + POOL_PAD), np.float32)
    invc = np.zeros((len(kinds), tm, POOL_WIDTH), np.float32)
    i = np.arange(tm)
    for vi, tile in enumerate(kinds):
        pos = tile * tm + i
        colpos = np.full(tm + POOL_PAD, -1)
        colpos[:tm] = pos
        colpos[tm:tm + POOL_HALO] = tile * tm - POOL_HALO + np.arange(POOL_HALO)
        colpos[tm + POOL_HALO:tm + 2 * POOL_HALO] = (tile + 1) * tm + np.arange(POOL_HALO)
        for gi, w in enumerate(POOL_WINDOWS):
            lo = w // 2
            hi = w - 1 - lo
            start = np.maximum(pos - lo, 0)
            end = np.minimum(pos + hi + 1, seq)
            mats[vi, gi] = (colpos[None, :] >= start[:, None]) & (colpos[None, :] < end[:, None])
            invc[vi, :, gi * POOL_CG:(gi + 1) * POOL_CG] = (1.0 / (end - start))[:, None]
    return jnp.asarray(mats, BF16), jnp.asarray(invc, F32), nt


def _ret_out(x, mod8, mod_row, p, q, k, v, sg, cb, wpool_bd, pscale, rng, w_out, lgf_q, lgf_v, lgb_v, s0):
    bsz, seq, d = x.shape
    tm = min(MIX_TM, seq)
    pm, invc, nt = _pool_constants(tm, seq)
    hb = tm // POOL_HALO
    nhb = seq // POOL_HALO

    def variant(t):
        if nt == 1:
            return 0
        return jnp.where(t == 0, 0, jnp.where(t == nt - 1, 2, 1))

    cur = lambda b, t: (b, t, 0)
    st_shape = (RET_PAIRS, 2 * RET_DK, 2 * RET_DV)
    tokspec = lambda w: pl.BlockSpec((None, tm, w), cur)
    return pl.pallas_call(
        functools.partial(_retout_kernel, tm=tm, nt=nt),
        grid=(bsz, nt),
        in_specs=[
            tokspec(d),
            pl.BlockSpec((None, N_MOD, d), lambda b, t: (mod_row(b), 0, 0)),
            pl.BlockSpec((None, POOL_HALO, POOL_WIDTH), lambda b, t: (b, jnp.maximum(t * hb - 1, 0), 0)),
            tokspec(POOL_WIDTH),
            pl.BlockSpec((None, POOL_HALO, POOL_WIDTH),
                         lambda b, t: (b, jnp.minimum((t + 1) * hb, nhb - 1), 0)),
            pl.BlockSpec((None, POOL_GROUPS, tm, tm + POOL_PAD), lambda b, t: (variant(t), 0, 0, 0)),
            pl.BlockSpec((None, tm, POOL_WIDTH), lambda b, t: (variant(t), 0, 0)),
            tokspec(RET_QK_W), tokspec(RET_QK_W), tokspec(RET_V_W), tokspec(RET_V_W), tokspec(RET_V_W),
            _resident(wpool_bd.shape),
            pl.BlockSpec((1, POOL_WIDTH), lambda b, t: (0, 0)),
            pl.BlockSpec((1, RET_V_W), lambda b, t: (0, 0)),
            _resident(w_out.shape),
            pl.BlockSpec((1, RET_QK_W), lambda b, t: (0, 0)),
            pl.BlockSpec((1, RET_V_W), lambda b, t: (0, 0)),
            pl.BlockSpec((1, RET_V_W), lambda b, t: (0, 0)),
            pl.BlockSpec((None,) + st_shape, lambda b, t: (b, 0, 0, 0)),
        ],
        out_specs=[
            tokspec(d),
            pl.BlockSpec((None,) + st_shape, lambda b, t: (b, 0, 0, 0)),
        ],
        out_shape=[jax.ShapeDtypeStruct(x.shape, F32),
                   jax.ShapeDtypeStruct((bsz,) + st_shape, F32)],
        scratch_shapes=[pltpu.VMEM(st_shape, F32),
                        pltpu.VMEM((tm, d), BF16),
                        pltpu.VMEM((tm + POOL_PAD, POOL_WIDTH), F32)],
        compiler_params=_cparams(("arbitrary", "arbitrary")),
        name="ret_out",
    )(x, mod8, p, p, p, pm, invc, q, k, v, sg, cb, wpool_bd, pscale, rng, w_out,
      lgf_q, lgf_v, lgb_v, s0)


def _head_norm_rope(slab, gain, ta, tb):
    lane = lax.broadcasted_iota(jnp.int32, slab.shape, 1)
    sq = jnp.where(lane < MLA_QK, slab * slab, 0.0)
    ms = jnp.sum(sq, axis=-1, keepdims=True) * (1.0 / MLA_QK)
    n = slab * lax.rsqrt(ms + EPS) * gain
    return n * ta + pltpu.roll(n, HEAD_SLAB - MLA_ROPE, axis=1) * tb


def _mlaproj_kernel(x_ref, mod_ref, g_ref, win_ref, qag_ref, kvag_ref, wqb_ref, wkk_ref, wkv_ref,
                    qng_ref, kng_ref, qa_t_ref, qb_t_ref, ka_t_ref, kb_t_ref,
                    q_ref, k_ref, v_ref):
    h = _rms_mod(x_ref[...], g_ref[...], mod_ref[3:4, :], mod_ref[4:5, :])
    proj = jnp.dot(h.astype(BF16), win_ref[...], preferred_element_type=F32)
    qa = proj[:, :MLA_Q_RANK]
    kva = proj[:, MLA_Q_RANK:MLA_Q_RANK + MLA_KV_RANK]
    krs = proj[:, MLA_Q_RANK + MLA_KV_RANK:]
    qan = qa * lax.rsqrt(jnp.mean(qa * qa, axis=-1, keepdims=True) + EPS) * qag_ref[...]
    kvn = kva * lax.rsqrt(jnp.mean(kva * kva, axis=-1, keepdims=True) + EPS) * kvag_ref[...]
    qh = jnp.dot(qan.astype(BF16), wqb_ref[...], preferred_element_type=F32)
    kvn_b = kvn.astype(BF16)
    kh = jnp.dot(kvn_b, wkk_ref[...], preferred_element_type=F32)
    vh = jnp.dot(kvn_b, wkv_ref[...], preferred_element_type=F32)
    qta, qtb = qa_t_ref[...], qb_t_ref[...]
    kta, ktb = ka_t_ref[...], kb_t_ref[...]
    qg, kg = qng_ref[...], kng_ref[...]
    tm = proj.shape[0]
    ones_rows = jnp.where(lax.broadcasted_iota(jnp.int32, (V_ROWS - MLA_V, tm), 0) == 0,
                          1.0, 0.0).astype(BF16)
    for hd in range(MLA_HEADS):
        sl = slice(HEAD_SLAB * hd, HEAD_SLAB * (hd + 1))
        q_ref[hd] = _head_norm_rope(qh[:, sl], qg, qta, qtb).astype(BF16)
        k_ref[hd] = _head_norm_rope(kh[:, sl] + krs, kg, kta, ktb).astype(BF16)
        v_ref[hd, 0:MLA_V, :] = vh[:, sl].T.astype(BF16)
        v_ref[hd, MLA_V:V_ROWS, :] = ones_rows


def _mla_project(x, mod8, mod_row, g, w_in, qa_g, kva_g, w_qb, w_kk, w_kv, qn_g, kn_g, q_ta, q_tb, k_ta, k_tb):
    bsz, seq, d = x.shape
    tm = min(MLA_TM, seq)
    head_out = jax.ShapeDtypeStruct((bsz, MLA_HEADS, seq, HEAD_SLAB), BF16)
    hspec = pl.BlockSpec((None, MLA_HEADS, tm, HEAD_SLAB), lambda b, t: (b, 0, t, 0))
    tspec = pl.BlockSpec((tm, HEAD_SLAB), lambda b, t: (t, 0))
    row = lambda n: pl.BlockSpec((1, n), lambda b, t: (0, 0))
    return pl.pallas_call(
        _mlaproj_kernel,
        grid=(bsz, seq // tm),
        in_specs=[
            pl.BlockSpec((None, tm, d), lambda b, t: (b, t, 0)),
            pl.BlockSpec((None, N_MOD, d), lambda b, t: (mod_row(b), 0, 0)),
            row(d),
            _resident(w_in.shape), row(MLA_Q_RANK), row(MLA_KV_RANK),
            _resident(w_qb.shape), _resident(w_kk.shape), _resident(w_kv.shape),
            row(HEAD_SLAB), row(HEAD_SLAB),
            tspec, tspec, tspec, tspec,
        ],
        out_specs=[hspec, hspec,
                   pl.BlockSpec((None, MLA_HEADS, V_ROWS, tm), lambda b, t: (b, 0, 0, t))],
        out_shape=[head_out, head_out,
                   jax.ShapeDtypeStruct((bsz, MLA_HEADS, V_ROWS, seq), BF16)],
        compiler_params=_cparams(("parallel", "parallel")),
        name="mla_project",
    )(x, mod8, g.reshape(1, d), w_in, qa_g, kva_g, w_qb, w_kk, w_kv, qn_g, kn_g, q_ta, q_tb, k_ta, k_tb)


def _attn_kernel(*refs, n_lat_blocks, tk, group_blocks):
    if n_lat_blocks:
        q_ref, kc_ref, vct_ref, k_ref, vt_ref, o_ref, m_scr, acc_scr, s_scr = refs
    else:
        q_ref, kc_ref, vct_ref, o_ref, m_scr, acc_scr = refs
    q = q_ref[...]

    def scores(kb):
        return lax.dot_general(kb, q, (((1,), (1,)), ((), ())), preferred_element_type=F32)

    def softmax_pv(st, vtb, first):
        m_cur = jnp.max(st, axis=0, keepdims=True)
        if first:
            m_new = m_cur
        else:
            m_prev = m_scr[...]
            m_new = jnp.maximum(m_prev, m_cur)
        p = jnp.exp2((st - m_new).astype(BF16))
        pv = jnp.dot(vtb, p, preferred_element_type=F32)
        if first:
            acc_scr[...] = pv
        else:
            acc_scr[...] = jnp.exp2(m_prev - m_new) * acc_scr[...] + pv
        m_scr[...] = m_new

    if n_lat_blocks:
        kblk = lambda o: k_ref[pl.ds(o, tk), :]
        vblk = lambda o: vt_ref[:, pl.ds(o, tk)]
        s_scr[0] = scores(k_ref[0:tk, :])
        softmax_pv(scores(kc_ref[...]), vct_ref[...], True)

        def group(i, carry):
            base = i * (group_blocks * tk)
            for u in range(group_blocks):
                cur = pl.multiple_of(base + u * tk, tk)
                nxt = pl.multiple_of(base + (u + 1) * tk, tk)
                s_scr[(u + 1) % 2] = scores(kblk(nxt))
                softmax_pv(s_scr[u % 2], vblk(cur), False)
            return carry
        lax.fori_loop(0, n_lat_blocks // group_blocks - 1, group, 0)
        e0 = n_lat_blocks - group_blocks
        for u in range(group_blocks):
            cur = (e0 + u) * tk
            if u + 1 < group_blocks:
                s_scr[(u + 1) % 2] = scores(k_ref[cur + tk:cur + 2 * tk, :])
            softmax_pv(s_scr[u % 2], vt_ref[:, cur:cur + tk], False)
    else:
        softmax_pv(scores(kc_ref[...]), vct_ref[...], True)
    acc = acc_scr[...]
    out_t = acc[0:MLA_V, :] * (1.0 / acc[MLA_V:MLA_V + 1, :])
    o_ref[...] = out_t.T.astype(o_ref.dtype)


def _attention(q, kc, vct, k=None, vt=None):
    bsz, nh, lq, hs = q.shape
    lc = kc.shape[2]
    tq = min(ATT_TQ, lq)
    qspec = pl.BlockSpec((None, None, tq, hs), lambda b, h, i: (b, h, i, 0))
    kfull = lambda n: pl.BlockSpec((None, None, n, hs), lambda b, h, i: (b, h, 0, 0))
    vfull = lambda n: pl.BlockSpec((None, None, V_ROWS, n), lambda b, h, i: (b, h, 0, 0))
    in_specs = [qspec, kfull(lc), vfull(lc)]
    args = [q, kc, vct]
    n_lat = 0
    grp = 0
    tk = ATT_TK
    if k is not None:
        lk = k.shape[2]
        tk = min(ATT_TK, lk)
        n_lat = lk // tk
        grp = min(ATT_GROUP, n_lat)
        assert n_lat % grp == 0, "latent key blocks are consumed in whole groups"
        in_specs += [kfull(lk), vfull(lk)]
        args += [k, vt]
    scratch = [pltpu.VMEM((1, tq), F32), pltpu.VMEM((V_ROWS, tq), F32)]
    if n_lat:
        scratch.append(pltpu.VMEM((2, tk, tq), F32))
    return pl.pallas_call(
        functools.partial(_attn_kernel, n_lat_blocks=n_lat, tk=tk, group_blocks=grp),
        grid=(bsz, nh, lq // tq),
        in_specs=in_specs,
        out_specs=pl.BlockSpec((None, tq, hs), lambda b, h, i: (b, i, h)),
        out_shape=jax.ShapeDtypeStruct((bsz, lq, nh * hs), BF16),
        scratch_shapes=scratch,
        compiler_params=_cparams(("parallel", "parallel", "arbitrary")),
        name="mla_attention",
    )(*args)


def _mlaout_kernel(x_ref, mod_ref, o_ref_in, w_ref, out_ref):
    y = jnp.dot(o_ref_in[...], w_ref[...], preferred_element_type=F32)
    out_ref[...] = x_ref[...] + mod_ref[5:6, :] * y


def _mla_out(x, mod8, mod_row, o, w_out):
    bsz, seq, d = x.shape
    tm = min(MLA_TM, seq)
    return pl.pallas_call(
        _mlaout_kernel,
        grid=(bsz, seq // tm),
        in_specs=[
            pl.BlockSpec((None, tm, d), lambda b, t: (b, t, 0)),
            pl.BlockSpec((None, N_MOD, d), lambda b, t: (mod_row(b), 0, 0)),
            pl.BlockSpec((None, tm, o.shape[-1]), lambda b, t: (b, t, 0)),
            _resident(w_out.shape),
        ],
        out_specs=pl.BlockSpec((None, tm, d), lambda b, t: (b, t, 0)),
        out_shape=jax.ShapeDtypeStruct(x.shape, F32),
        compiler_params=_cparams(("parallel", "parallel")),
        name="mla_out",
    )(x, mod8, o, w_out)


def _pair_swap_cols(w):
    w2 = w.reshape(w.shape[:-1] + (w.shape[-1] // 2, 2))
    return jnp.stack([-w2[..., 1], w2[..., 0]], axis=-1).reshape(w.shape)


def _rope_angles(seq, rot_dim):
    pos = jnp.arange(seq)
    row = (pos // GRID_W).astype(F32)
    col = (pos % GRID_W).astype(F32)
    n_freq = rot_dim // 4
    inv = ROPE_BASE ** (-jnp.arange(n_freq, dtype=F32) / n_freq)
    ang = jnp.concatenate([row[:, None] * inv, col[:, None] * inv], axis=-1)
    return jnp.repeat(jnp.cos(ang), 2, axis=-1), jnp.repeat(jnp.sin(ang), 2, axis=-1)


def _ret_tables(seq, ctx_len):
    cos, sin = _rope_angles(seq, RET_DK)
    cos2, sin2 = jnp.tile(cos, (1, 2)), jnp.tile(sin, (1, 2))
    return cos2, sin2, jnp.ones((ctx_len, LANES), F32), jnp.zeros((ctx_len, LANES), F32)


def _mla_tables(seq, ctx_len):
    cos, sin = _rope_angles(seq, MLA_ROPE)
    one = jnp.ones((seq, MLA_NOPE), F32)
    zero_tail = jnp.zeros((seq, HEAD_SLAB - MLA_QK), F32)
    zero_head = jnp.zeros((seq, MLA_NOPE), F32)
    ta = jnp.concatenate([one, cos, zero_tail], axis=-1)
    tb = jnp.concatenate([zero_head, sin, zero_tail], axis=-1)
    ident = jnp.concatenate([jnp.ones((ctx_len, MLA_QK), F32),
                             jnp.zeros((ctx_len, HEAD_SLAB - MLA_QK), F32)], axis=-1)
    return ta, tb, ident, jnp.zeros((ctx_len, HEAD_SLAB), F32)


def _slab_gain(gvec):
    rope = gvec[MLA_NOPE:].reshape(MLA_ROPE // 2, 2)
    swapped = jnp.stack([rope[:, 1], rope[:, 0]], axis=-1).reshape(MLA_ROPE)
    return jnp.concatenate([gvec, swapped]).reshape(1, HEAD_SLAB)


def _prep_even(ab_w_in, pool_w, ab_w_out):
    wq = ab_w_in[:, POOL_WIDTH:POOL_WIDTH + RET_QK_W]
    wk = ab_w_in[:, POOL_WIDTH + RET_QK_W:POOL_WIDTH + 2 * RET_QK_W]
    w_ext = jnp.concatenate([ab_w_in, _pair_swap_cols(wq), _pair_swap_cols(wk)], axis=-1).astype(BF16)
    wpool_bd = jnp.zeros((POOL_WIDTH, POOL_WIDTH), F32)
    for gi in range(POOL_GROUPS):
        s = slice(gi * POOL_CG, (gi + 1) * POOL_CG)
        wpool_bd = wpool_bd.at[s, s].set(pool_w[gi])
    return w_ext, wpool_bd.astype(BF16), ab_w_out.astype(BF16)


def _prep_odd(w_in, w_qb, w_kvb, w_out):
    d = w_in.shape[0]
    wkr = w_in[:, MLA_Q_RANK + MLA_KV_RANK:]
    kr_slab = jnp.concatenate([jnp.zeros((d, MLA_NOPE), F32), wkr, _pair_swap_cols(wkr)], axis=-1)
    w_in_ext = jnp.concatenate([w_in[:, :MLA_Q_RANK + MLA_KV_RANK], kr_slab], axis=-1).astype(BF16)
    wq3 = w_qb.reshape(MLA_Q_RANK, MLA_HEADS, MLA_QK)
    wq_slab = jnp.concatenate([wq3, _pair_swap_cols(wq3[..., MLA_NOPE:])], axis=-1)
    w_qb_ext = wq_slab.reshape(MLA_Q_RANK, MLA_HEADS * HEAD_SLAB).astype(BF16)
    wkv3 = w_kvb.reshape(MLA_KV_RANK, MLA_HEADS, MLA_NOPE + MLA_V)
    wk_slab = jnp.concatenate([wkv3[..., :MLA_NOPE],
                               jnp.zeros((MLA_KV_RANK, MLA_HEADS, HEAD_SLAB - MLA_NOPE), F32)], axis=-1)
    w_kk = wk_slab.reshape(MLA_KV_RANK, MLA_HEADS * HEAD_SLAB).astype(BF16)
    w_kv = wkv3[..., MLA_NOPE:].reshape(MLA_KV_RANK, MLA_HEADS * MLA_V).astype(BF16)
    return w_in_ext, w_qb_ext, w_kk, w_kv, w_out.astype(BF16)


def kernel(x, c, ctx, c_ctx, ada_w, ada_b, norm_g, ffn_w_in, ffn_w_out, ab_w_in, pool_w, pool_scale,
           ret_log_decay, ret_norm_g, ab_w_out, mla_w_in, mla_qa_g, mla_kva_g, mla_w_qb, mla_w_kvb,
           mla_qn_g, mla_kn_g, mla_w_out):
    bsz, seq, d = x.shape
    ctx_len = ctx.shape[1]
    depth = ada_w.shape[0]
    ctx_row = bsz
    cond8 = jnp.zeros((8, d), F32).at[:bsz].set(c).at[ctx_row].set(c_ctx)
    mod_all = _ada_modulation(cond8, ada_w, ada_b).reshape(depth, 8, N_MOD, d)
    lat_row = lambda b: b
    ctx_mod_row = lambda b: ctx_row

    ret_cos, ret_sin, ret_cos_c, ret_sin_c = _ret_tables(seq, ctx_len)
    m_ta, m_tb, m_ta_c, m_tb_c = _mla_tables(seq, ctx_len)
    q_scale = MLA_QK ** -0.5 * LOG2_E
    zero_state = jnp.zeros((bsz, RET_PAIRS, 2 * RET_DK, 2 * RET_DV), F32)

    xc = ctx
    for i in range(depth):
        last = i == depth - 1
        j = i // 2
        mod8 = mod_all[i]
        w1_in, w1_out = ffn_w_in[i, 0].astype(BF16), ffn_w_out[i, 0].astype(BF16)
        w2_in, w2_out = ffn_w_in[i, 1].astype(BF16), ffn_w_out[i, 1].astype(BF16)
        x = _ffn_half(x, mod8, lat_row, 0, norm_g[i, 0], w1_in, w1_out)
        xc = _ffn_half(xc, mod8, ctx_mod_row, 0, norm_g[i, 0], w1_in, w1_out)
        if i % 2 == 0:
            w_ext, wpool_bd, w_out = _prep_even(ab_w_in[j], pool_w[j], ab_w_out[j])
            lg = ret_log_decay[j]
            lgf_q = jnp.repeat(lg[0], RET_DK).reshape(1, RET_QK_W)
            lgb_q = jnp.repeat(lg[1], RET_DK).reshape(1, RET_QK_W)
            lgf_v = jnp.repeat(lg[0], RET_DV).reshape(1, RET_V_W)
            lgb_v = jnp.repeat(lg[1], RET_DV).reshape(1, RET_V_W)
            pscale = pool_scale[j].reshape(1, POOL_WIDTH)
            rng = ret_norm_g[j].reshape(1, RET_V_W)
            pc, qc, kc, vc, sgc, cbc, state_b = _ab_project(
                xc, mod8, ctx_mod_row, norm_g[i, 1], w_ext, ret_cos_c, ret_sin_c, lgb_q, lgb_v, zero_state)
            xc_mixed, state_f = _ret_out(xc, mod8, ctx_mod_row, pc, qc, kc, vc, sgc, cbc, wpool_bd, pscale,
                                         rng, w_out, lgf_q, lgf_v, lgb_v, zero_state)
            p, q, k, v, sg, cb, _ = _ab_project(
                x, mod8, lat_row, norm_g[i, 1], w_ext, ret_cos, ret_sin, lgb_q, lgb_v, state_b)
            x, _ = _ret_out(x, mod8, lat_row, p, q, k, v, sg, cb, wpool_bd, pscale, rng, w_out,
                            lgf_q, lgf_v, lgb_v, state_f)
        else:
            w_in_ext, w_qb_ext, w_kk, w_kv, w_out = _prep_odd(mla_w_in[j], mla_w_qb[j], mla_w_kvb[j],
                                                              mla_w_out[j])
            qa_g = mla_qa_g[j].reshape(1, MLA_Q_RANK)
            kva_g = mla_kva_g[j].reshape(1, MLA_KV_RANK)
            qn_g, kn_g = _slab_gain(mla_qn_g[j]), _slab_gain(mla_kn_g[j])
            qc, kc, vc = _mla_project(xc, mod8, ctx_mod_row, norm_g[i, 1], w_in_ext, qa_g, kva_g, w_qb_ext,
                                      w_kk, w_kv, qn_g, kn_g, m_ta_c * q_scale, m_tb_c, m_ta_c, m_tb_c)
            q, k, v = _mla_project(x, mod8, lat_row, norm_g[i, 1], w_in_ext, qa_g, kva_g, w_qb_ext,
                                   w_kk, w_kv, qn_g, kn_g, m_ta * q_scale, m_tb * q_scale, m_ta, m_tb)
            o = _attention(q, kc, vc, k, v)
            x = _mla_out(x, mod8, lat_row, o, w_out)
            if not last:
                oc = _attention(qc, kc, vc)
                xc_mixed = _mla_out(xc, mod8, ctx_mod_row, oc, w_out)
        x = _ffn_half(x, mod8, lat_row, 2, norm_g[i, 2], w2_in, w2_out)
        if not last:
            xc = _ffn_half(xc_mixed, mod8, ctx_mod_row, 2, norm_g[i, 2], w2_in, w2_out)
    return x
```

```python
import functools

import numpy as np
import jax
import jax.numpy as jnp
from jax import lax
from jax.experimental import pallas as pl
from jax.experimental.pallas import tpu as pltpu

F32 = jnp.float32
BF16 = jnp.bfloat16

D_MODEL = 1024
GRID_W = 64
D_FF = 2816
FFN_RESIDUAL = 0.5
N_MOD = 9
ROPE_BASE = 10000.0
EPS = 1e-6
POOL_GROUPS = 4
POOL_CG = 64
POOL_WIDTH = POOL_GROUPS * POOL_CG
POOL_WINDOWS = (2, 4, 8, 16)
POOL_HALO = 8
POOL_PAD = 128
RET_HEADS = 6
RET_PAIRS = RET_HEADS // 2
RET_DK = 64
RET_DV = 128
RET_CHUNK = 128
RET_QK_W = RET_HEADS * RET_DK
RET_V_W = RET_HEADS * RET_DV
AB_IN = POOL_WIDTH + 2 * RET_QK_W + 2 * RET_V_W
AB_EXT = AB_IN + 2 * RET_QK_W
MLA_HEADS = 8
MLA_Q_RANK = 384
MLA_KV_RANK = 256
MLA_NOPE = 64
MLA_ROPE = 32
MLA_V = 128
MLA_QK = MLA_NOPE + MLA_ROPE
HEAD_SLAB = 128
V_ROWS = MLA_V + 16
LOG2_E = 1.4426950408889634

LANES = 128
V7X_VMEM_BYTES = 64 * 1024 * 1024
VMEM_LIMIT_BYTES = 56 * 1024 * 1024

ADA_TN = 1536
FFN_TM = 512
MIX_TM = 256
MLA_TM = 256
MLA_HEAD_UNROLL = 8
ATT_TQ = 512
ATT_TK = 512
ATT_GROUP = 4


def _cparams(sem):
    return pltpu.CompilerParams(dimension_semantics=sem, vmem_limit_bytes=VMEM_LIMIT_BYTES)


def _resident(shape):
    nd = len(shape)
    return pl.BlockSpec(shape, lambda *_: (0,) * nd, pipeline_mode=pl.Buffered(1))


def _silu(x):
    return x * (1.0 / (1.0 + jnp.exp(-x)))


def _rms_mod(x, g, shift, scale):
    ms = jnp.mean(x * x, axis=-1, keepdims=True)
    return (x * lax.rsqrt(ms + EPS) * g) * (1.0 + scale) + shift


def _ada_kernel(c_ref, w_ref, b_ref, o_ref):
    s = _silu(c_ref[...])
    o_ref[...] = jnp.dot(s, w_ref[...], precision=lax.Precision.HIGHEST,
                         preferred_element_type=F32) + b_ref[...]


def _ada_modulation(cond8, ada_w, ada_b):
    depth, d, n = ada_w.shape
    return pl.pallas_call(
        _ada_kernel,
        grid=(depth, n // ADA_TN),
        in_specs=[
            pl.BlockSpec((8, d), lambda i, j: (0, 0)),
            pl.BlockSpec((None, d, ADA_TN), lambda i, j: (i, 0, j)),
            pl.BlockSpec((None, 1, ADA_TN), lambda i, j: (i, 0, j)),
        ],
        out_specs=pl.BlockSpec((None, 8, ADA_TN), lambda i, j: (i, 0, j)),
        out_shape=jax.ShapeDtypeStruct((depth, 8, n), F32),
        compiler_params=_cparams(("parallel", "parallel")),
        name="ada_mod",
    )(cond8, ada_w, ada_b.reshape(depth, 1, n))


def _ffn_kernel(x_ref, mod_ref, g_ref, win_ref, wout_ref, o_ref, *, k):
    x = x_ref[...]
    h = _rms_mod(x, g_ref[...], mod_ref[3 * k:3 * k + 1, :], mod_ref[3 * k + 1:3 * k + 2, :])
    ab = jnp.dot(h.astype(BF16), win_ref[...], preferred_element_type=F32)
    a = ab[:, :D_FF]
    b = ab[:, D_FF:]
    act = (_silu(a) * b).astype(BF16)
    y = jnp.dot(act, wout_ref[...], preferred_element_type=F32)
    o_ref[...] = x + (FFN_RESIDUAL * mod_ref[3 * k + 2:3 * k + 3, :]) * y


def _ffn_half(x, mod8, mod_row, k, g, w_in, w_out):
    bsz, seq, d = x.shape
    tm = min(FFN_TM, seq)
    return pl.pallas_call(
        functools.partial(_ffn_kernel, k=k),
        grid=(bsz, seq // tm),
        in_specs=[
            pl.BlockSpec((None, tm, d), lambda b, t: (b, t, 0)),
            pl.BlockSpec((None, N_MOD, d), lambda b, t: (mod_row(b), 0, 0)),
            pl.BlockSpec((1, d), lambda b, t: (0, 0)),
            _resident(w_in.shape),
            _resident(w_out.shape),
        ],
        out_specs=pl.BlockSpec((None, tm, d), lambda b, t: (b, t, 0)),
        out_shape=jax.ShapeDtypeStruct(x.shape, F32),
        compiler_params=_cparams(("parallel", "parallel")),
        name="ffn_half",
    )(x, mod8, g.reshape(1, d), w_in, w_out)


def _pair_tables(lg_row, expo):
    return jnp.exp(lg_row * expo)


def _bd_mask():
    r = lax.broadcasted_iota(jnp.int32, (2 * RET_DK, 2 * RET_DV), 0)
    c = lax.broadcasted_iota(jnp.int32, (2 * RET_DK, 2 * RET_DV), 1)
    same_head = jnp.where(r < RET_DK, 0, 1) == jnp.where(c < RET_DV, 0, 1)
    return jnp.where(same_head, 1.0, 0.0)


def _abproj_kernel(x_ref, mod_ref, g_ref, w_ref, cos_ref, sin_ref, lgq_ref, lgv_ref, s0_ref,
                   p_ref, q_ref, k_ref, v_ref, sg_ref, cb_ref, sfin_ref, st_scr, *, tm, nt):
    t = pl.program_id(1)

    @pl.when(t == 0)
    def _():
        st_scr[...] = s0_ref[...]

    h = _rms_mod(x_ref[...], g_ref[...], mod_ref[3:4, :], mod_ref[4:5, :])
    proj = jnp.dot(h.astype(BF16), w_ref[...], preferred_element_type=F32)
    p_ref[...] = proj[:, :POOL_WIDTH]
    q0, k0 = POOL_WIDTH, POOL_WIDTH + RET_QK_W
    v0, g0 = k0 + RET_QK_W, k0 + RET_QK_W + RET_V_W
    qs0, ks0 = AB_IN, AB_IN + RET_QK_W
    v_ref[...] = proj[:, v0:v0 + RET_V_W].astype(BF16)
    sg_ref[...] = _silu(proj[:, g0:g0 + RET_V_W]).astype(BF16)

    cos = cos_ref[...]
    sin = sin_ref[...]
    row = lax.broadcasted_iota(jnp.int32, (RET_CHUNK, LANES), 0).astype(F32)
    bdm = _bd_mask()
    k_scale = RET_DK ** -0.5
    for j in range(RET_PAIRS):
        sl = slice(LANES * j, LANES * (j + 1))
        qr = proj[:, q0 + LANES * j:q0 + LANES * (j + 1)] * cos \
            + proj[:, qs0 + LANES * j:qs0 + LANES * (j + 1)] * sin
        kr = (proj[:, k0 + LANES * j:k0 + LANES * (j + 1)] * cos
              + proj[:, ks0 + LANES * j:ks0 + LANES * (j + 1)] * sin) * k_scale
        q_ref[:, sl] = qr.astype(BF16)
        k_ref[:, sl] = kr.astype(BF16)
        lgq = lgq_ref[:, sl]
        lgv = lgv_ref[:, 2 * LANES * j:2 * LANES * (j + 1)]
        xi = _pair_tables(lgq, float(RET_CHUNK) - row)
        zeta = _pair_tables(lgq, row)
        dec = jnp.exp(lgv * float(RET_CHUNK))
        vsl = slice(2 * LANES * j, 2 * LANES * (j + 1))
        for c in reversed(range(tm // RET_CHUNK)):
            rs = slice(c * RET_CHUNK, (c + 1) * RET_CHUNK)
            bd = st_scr[j]
            qx = (qr[rs] * xi).astype(BF16)
            cb_ref[rs, vsl] = jnp.dot(qx, bd.astype(BF16), preferred_element_type=F32)
            kz = (kr[rs] * zeta).astype(BF16)
            upd = lax.dot_general(kz, v_ref[rs, vsl], (((0,), (0,)), ((), ())),
                                  preferred_element_type=F32)
            st_scr[j] = dec * bd + bdm * upd

    @pl.when(t == nt - 1)
    def _():
        sfin_ref[...] = st_scr[...]


def _ab_project(x, mod8, mod_row, g, w_ext, cos, sin, lg_q, lg_v, s0):
    bsz, seq, d = x.shape
    tm = min(MIX_TM, seq)
    nt = seq // tm
    rev = lambda b, t: (b, nt - 1 - t, 0)
    tok = lambda w, dt: jax.ShapeDtypeStruct((bsz, seq, w), dt)
    st_shape = (RET_PAIRS, 2 * RET_DK, 2 * RET_DV)
    return pl.pallas_call(
        functools.partial(_abproj_kernel, tm=tm, nt=nt),
        grid=(bsz, nt),
        in_specs=[
            pl.BlockSpec((None, tm, d), rev),
            pl.BlockSpec((None, N_MOD, d), lambda b, t: (mod_row(b), 0, 0)),
            pl.BlockSpec((1, d), lambda b, t: (0, 0)),
            _resident(w_ext.shape),
            pl.BlockSpec((tm, LANES), lambda b, t: (nt - 1 - t, 0)),
            pl.BlockSpec((tm, LANES), lambda b, t: (nt - 1 - t, 0)),
            pl.BlockSpec((1, RET_QK_W), lambda b, t: (0, 0)),
            pl.BlockSpec((1, RET_V_W), lambda b, t: (0, 0)),
            pl.BlockSpec((None,) + st_shape, lambda b, t: (b, 0, 0, 0)),
        ],
        out_specs=[
            pl.BlockSpec((None, tm, POOL_WIDTH), rev),
            pl.BlockSpec((None, tm, RET_QK_W), rev),
            pl.BlockSpec((None, tm, RET_QK_W), rev),
            pl.BlockSpec((None, tm, RET_V_W), rev),
            pl.BlockSpec((None, tm, RET_V_W), rev),
            pl.BlockSpec((None, tm, RET_V_W), rev),
            pl.BlockSpec((None,) + st_shape, lambda b, t: (b, 0, 0, 0)),
        ],
        out_shape=[tok(POOL_WIDTH, F32), tok(RET_QK_W, BF16), tok(RET_QK_W, BF16),
                   tok(RET_V_W, BF16), tok(RET_V_W, BF16), tok(RET_V_W, F32),
                   jax.ShapeDtypeStruct((bsz,) + st_shape, F32)],
        scratch_shapes=[pltpu.VMEM(st_shape, F32)],
        compiler_params=_cparams(("arbitrary", "arbitrary")),
        name="ab_project",
    )(x, mod8, g.reshape(1, d), w_ext, cos, sin, lg_q, lg_v, s0)


def _retout_kernel(x_ref, mod_ref, pprev_ref, pcur_ref, pnext_ref, pm_ref, invc_ref,
                   q_ref, k_ref, v_ref, sg_ref, cb_ref, wpool_ref, pscale_ref, rng_ref, wout_ref,
                   lgfq_ref, lgfv_ref, lgbv_ref, s0_ref,
                   o_ref, sfin_ref, st_scr, ycat_scr, pe_scr, *, tm, nt):
    t = pl.program_id(1)

    @pl.when(t == 0)
    def _():
        st_scr[...] = s0_ref[...]

    pcur = pcur_ref[...]
    pe_scr[0:tm, :] = pcur
    pe_scr[tm:tm + POOL_HALO, :] = pprev_ref[...]
    pe_scr[tm + POOL_HALO:tm + 2 * POOL_HALO, :] = pnext_ref[...]
    pe_scr[tm + 2 * POOL_HALO:, :] = jnp.zeros((POOL_PAD - 2 * POOL_HALO, POOL_WIDTH), F32)
    pe = pe_scr[...]
    pe_hi = pe.astype(BF16)
    pe_lo = (pe - pe_hi.astype(F32)).astype(BF16)
    lane_grp = jnp.right_shift(lax.broadcasted_iota(jnp.int32, (tm, POOL_WIDTH), 1), 6)
    pooled = jnp.zeros((tm, POOL_WIDTH), F32)
    for gi in range(POOL_GROUPS):
        win = pm_ref[gi]
        tot = jnp.dot(win, pe_hi, preferred_element_type=F32) \
            + jnp.dot(win, pe_lo, preferred_element_type=F32)
        pooled = jnp.where(lane_grp == gi, tot, pooled)
    diffs = pooled * invc_ref[...] - pcur
    pool_y = jnp.dot(diffs.astype(BF16), wpool_ref[...], preferred_element_type=F32) * pscale_ref[...]
    ycat_scr[:, 0:POOL_WIDTH] = pool_y.astype(BF16)

    ri = lax.broadcasted_iota(jnp.int32, (RET_CHUNK, RET_CHUNK), 0)
    ci = lax.broadcasted_iota(jnp.int32, (RET_CHUNK, RET_CHUNK), 1)
    rel = (ri - ci).astype(F32)
    row = lax.broadcasted_iota(jnp.int32, (RET_CHUNK, LANES), 0).astype(F32)
    lane = lax.broadcasted_iota(jnp.int32, (RET_CHUNK, LANES), 1)
    bdm = _bd_mask()
    zero_bf = jnp.zeros((RET_CHUNK, LANES), BF16)
    for j in range(RET_PAIRS):
        sl = slice(LANES * j, LANES * (j + 1))
        vsl = slice(2 * LANES * j, 2 * LANES * (j + 1))
        lgq = lgfq_ref[:, sl]
        xi = _pair_tables(lgq, row + 1.0)
        zeta = _pair_tables(lgq, float(RET_CHUNK - 1) - row)
        dec = jnp.exp(lgfv_ref[:, vsl] * float(RET_CHUNK))
        masks = []
        for a in range(2):
            hs = slice(2 * LANES * j + LANES * a, 2 * LANES * j + LANES * (a + 1))
            mf = jnp.exp(lgfv_ref[:, hs] * jnp.maximum(rel, 0.0))
            mb = jnp.exp(lgbv_ref[:, hs] * jnp.maximum(-rel, 0.0))
            masks.append(jnp.where(rel > 0, mf, jnp.where(rel < 0, mb, 2.0)))
        for c in range(tm // RET_CHUNK):
            rs = slice(c * RET_CHUNK, (c + 1) * RET_CHUNK)
            q2 = q_ref[rs, sl]
            k2 = k_ref[rs, sl]
            v2 = v_ref[rs, vsl]
            bd = st_scr[j]
            qx = (q2.astype(F32) * xi).astype(BF16)
            ret = jnp.dot(qx, bd.astype(BF16), preferred_element_type=F32) + cb_ref[rs, vsl]
            for a in range(2):
                qa = jnp.where(lane < RET_DK if a == 0 else lane >= RET_DK, q2, zero_bf)
                s = lax.dot_general(qa, k2, (((1,), (1,)), ((), ())), preferred_element_type=F32)
                pa = (s * masks[a]).astype(BF16)
                hs_v = slice(LANES * a, LANES * (a + 1))
                r = ret[:, hs_v] + jnp.dot(pa, v2[:, hs_v], preferred_element_type=F32)
                ms = jnp.mean(r * r, axis=-1, keepdims=True)
                hcol = 2 * LANES * j + LANES * a
                y = (r * lax.rsqrt(ms + EPS) * rng_ref[:, hcol:hcol + LANES]) \
                    * sg_ref[rs, hcol:hcol + LANES].astype(F32)
                ycat_scr[rs, POOL_WIDTH + hcol:POOL_WIDTH + hcol + LANES] = y.astype(BF16)
            kz = (k2.astype(F32) * zeta).astype(BF16)
            upd = lax.dot_general(kz, v2, (((0,), (0,)), ((), ())), preferred_element_type=F32)
            st_scr[j] = dec * bd + bdm * upd

    yy = jnp.dot(ycat_scr[...], wout_ref[...], preferred_element_type=F32)
    o_ref[...] = x_ref[...] + mod_ref[5:6, :] * yy

    @pl.when(t == nt - 1)
    def _():
        sfin_ref[...] = st_scr[...]


def _pool_constants(tm, seq):
    nt = seq // tm
    kinds = [0] if nt == 1 else [0, 1, nt - 1]
    mats = np.zeros((len(kinds), POOL_GROUPS, tm, tm + POOL_PAD), np.float32)
    invc = np.zeros((len(kinds), tm, POOL_WIDTH), np.float32)
    i = np.arange(tm)
    for vi, tile in enumerate(kinds):
        pos = tile * tm + i
        colpos = np.full(tm + POOL_PAD, -1)
        colpos[:tm] = pos
        colpos[tm:tm + POOL_HALO] = tile * tm - POOL_HALO + np.arange(POOL_HALO)
        colpos[tm + POOL_HALO:tm + 2 * POOL_HALO] = (tile + 1) * tm + np.arange(POOL_HALO)
        for gi, w in enumerate(POOL_WINDOWS):
            lo = w // 2
            hi = w - 1 - lo
            start = np.maximum(pos - lo, 0)
            end = np.minimum(pos + hi + 1, seq)
            mats[vi, gi] = (colpos[None, :] >= start[:, None]) & (colpos[None, :] < end[:, None])
            invc[vi, :, gi * POOL_CG:(gi + 1) * POOL_CG] = (1.0 / (end - start))[:, None]
    return jnp.asarray(mats, BF16), jnp.asarray(invc, F32), nt


def _ret_out(x, mod8, mod_row, p, q, k, v, sg, cb, wpool_bd, pscale, rng, w_out, lgf_q, lgf_v, lgb_v, s0):
    bsz, seq, d = x.shape
    tm = min(MIX_TM, seq)
    pm, invc, nt = _pool_constants(tm, seq)
    hb = tm // POOL_HALO
    nhb = seq // POOL_HALO

    def variant(t):
        if nt == 1:
            return 0
        return jnp.where(t == 0, 0, jnp.where(t == nt - 1, 2, 1))

    cur = lambda b, t: (b, t, 0)
    st_shape = (RET_PAIRS, 2 * RET_DK, 2 * RET_DV)
    tokspec = lambda w: pl.BlockSpec((None, tm, w), cur)
    return pl.pallas_call(
        functools.partial(_retout_kernel, tm=tm, nt=nt),
        grid=(bsz, nt),
        in_specs=[
            tokspec(d),
            pl.BlockSpec((None, N_MOD, d), lambda b, t: (mod_row(b), 0, 0)),
            pl.BlockSpec((None, POOL_HALO, POOL_WIDTH), lambda b, t: (b, jnp.maximum(t * hb - 1, 0), 0)),
            tokspec(POOL_WIDTH),
            pl.BlockSpec((None, POOL_HALO, POOL_WIDTH),
                         lambda b, t: (b, jnp.minimum((t + 1) * hb, nhb - 1), 0)),
            pl.BlockSpec((None, POOL_GROUPS, tm, tm + POOL_PAD), lambda b, t: (variant(t), 0, 0, 0)),
            pl.BlockSpec((None, tm, POOL_WIDTH), lambda b, t: (variant(t), 0, 0)),
            tokspec(RET_QK_W), tokspec(RET_QK_W), tokspec(RET_V_W), tokspec(RET_V_W), tokspec(RET_V_W),
            _resident(wpool_bd.shape),
            pl.BlockSpec((1, POOL_WIDTH), lambda b, t: (0, 0)),
            pl.BlockSpec((1, RET_V_W), lambda b, t: (0, 0)),
            _resident(w_out.shape),
            pl.BlockSpec((1, RET_QK_W), lambda b, t: (0, 0)),
            pl.BlockSpec((1, RET_V_W), lambda b, t: (0, 0)),
            pl.BlockSpec((1, RET_V_W), lambda b, t: (0, 0)),
            pl.BlockSpec((None,) + st_shape, lambda b, t: (b, 0, 0, 0)),
        ],
        out_specs=[
            tokspec(d),
            pl.BlockSpec((None,) + st_shape, lambda b, t: (b, 0, 0, 0)),
        ],
        out_shape=[jax.ShapeDtypeStruct(x.shape, F32),
                   jax.ShapeDtypeStruct((bsz,) + st_shape, F32)],
        scratch_shapes=[pltpu.VMEM(st_shape, F32),
                        pltpu.VMEM((tm, d), BF16),
                        pltpu.VMEM((tm + POOL_PAD, POOL_WIDTH), F32)],
        compiler_params=_cparams(("arbitrary", "arbitrary")),
        name="ret_out",
    )(x, mod8, p, p, p, pm, invc, q, k, v, sg, cb, wpool_bd, pscale, rng, w_out,
      lgf_q, lgf_v, lgb_v, s0)


def _head_norm_rope(main, aux, g_main, g_aux):
    ms = jnp.sum(main * main, axis=-1, keepdims=True) * (1.0 / MLA_QK)
    r = lax.rsqrt(ms + EPS)
    return (main * r) * g_main + (aux * r) * g_aux


def _mlaproj_kernel(x_ref, mod_ref, g_ref, win_ref, qag_ref, kvag_ref, wqb_ref, wkv_ref,
                    qg_ref, qgs_ref, kg_ref, kgs_ref, qa_t_ref, qb_t_ref, ka_t_ref, kb_t_ref,
                    q_ref, k_ref, v_ref, qan_scr, kvn_scr, kr_scr, gt_scr):
    h = _rms_mod(x_ref[...], g_ref[...], mod_ref[3:4, :], mod_ref[4:5, :])
    proj = jnp.dot(h.astype(BF16), win_ref[...], preferred_element_type=F32)
    qa = proj[:, :MLA_Q_RANK]
    kva = proj[:, MLA_Q_RANK:MLA_Q_RANK + MLA_KV_RANK]
    r0 = MLA_Q_RANK + MLA_KV_RANK
    kr_scr[0] = proj[:, r0:r0 + HEAD_SLAB]
    kr_scr[1] = proj[:, r0 + HEAD_SLAB:r0 + 2 * HEAD_SLAB]
    qan = qa * lax.rsqrt(jnp.mean(qa * qa, axis=-1, keepdims=True) + EPS) * qag_ref[...]
    kvn = kva * lax.rsqrt(jnp.mean(kva * kva, axis=-1, keepdims=True) + EPS) * kvag_ref[...]
    qan_scr[...] = qan.astype(BF16)
    kvn_scr[...] = kvn.astype(BF16)
    gt_scr[0] = qg_ref[...] * qa_t_ref[...]
    gt_scr[1] = qgs_ref[...] * qb_t_ref[...]
    gt_scr[2] = kg_ref[...] * ka_t_ref[...]
    gt_scr[3] = kgs_ref[...] * kb_t_ref[...]
    tm = proj.shape[0]
    ones_rows = jnp.where(lax.broadcasted_iota(jnp.int32, (V_ROWS - MLA_V, tm), 0) == 0,
                          1.0, 0.0).astype(BF16)

    def head(hd, carry):
        c0 = pl.multiple_of(hd * (2 * HEAD_SLAB), 2 * HEAD_SLAB)
        qh = jnp.dot(qan_scr[...], wqb_ref[:, pl.ds(c0, 2 * HEAD_SLAB)], preferred_element_type=F32)
        kvh = jnp.dot(kvn_scr[...], wkv_ref[:, pl.ds(c0, 2 * HEAD_SLAB)], preferred_element_type=F32)
        q_ref[hd] = _head_norm_rope(qh[:, :HEAD_SLAB], qh[:, HEAD_SLAB:], gt_scr[0], gt_scr[1]).astype(BF16)
        k_ref[hd] = _head_norm_rope(kvh[:, :HEAD_SLAB] + kr_scr[0], kr_scr[1],
                                    gt_scr[2], gt_scr[3]).astype(BF16)
        v_ref[hd, 0:MLA_V, :] = kvh[:, HEAD_SLAB:].T.astype(BF16)
        v_ref[hd, MLA_V:V_ROWS, :] = ones_rows
        return carry
    lax.fori_loop(0, MLA_HEADS, head, 0, unroll=MLA_HEAD_UNROLL)


def _mla_project(x, mod8, mod_row, g, w_in, qa_g, kva_g, w_qb, w_kv, qn_g, qn_gs, kn_g, kn_gs,
                 q_ta, q_tb, k_ta, k_tb):
    bsz, seq, d = x.shape
    tm = min(MLA_TM, seq)
    head_out = jax.ShapeDtypeStruct((bsz, MLA_HEADS, seq, HEAD_SLAB), BF16)
    hspec = pl.BlockSpec((None, MLA_HEADS, tm, HEAD_SLAB), lambda b, t: (b, 0, t, 0))
    tspec = pl.BlockSpec((tm, HEAD_SLAB), lambda b, t: (t, 0))
    row = lambda n: pl.BlockSpec((1, n), lambda b, t: (0, 0))
    return pl.pallas_call(
        _mlaproj_kernel,
        grid=(bsz, seq // tm),
        in_specs=[
            pl.BlockSpec((None, tm, d), lambda b, t: (b, t, 0)),
            pl.BlockSpec((None, N_MOD, d), lambda b, t: (mod_row(b), 0, 0)),
            row(d),
            _resident(w_in.shape), row(MLA_Q_RANK), row(MLA_KV_RANK),
            _resident(w_qb.shape), _resident(w_kv.shape),
            row(HEAD_SLAB), row(HEAD_SLAB), row(HEAD_SLAB), row(HEAD_SLAB),
            tspec, tspec, tspec, tspec,
        ],
        out_specs=[hspec, hspec,
                   pl.BlockSpec((None, MLA_HEADS, V_ROWS, tm), lambda b, t: (b, 0, 0, t))],
        out_shape=[head_out, head_out,
                   jax.ShapeDtypeStruct((bsz, MLA_HEADS, V_ROWS, seq), BF16)],
        scratch_shapes=[pltpu.VMEM((tm, MLA_Q_RANK), BF16), pltpu.VMEM((tm, MLA_KV_RANK), BF16),
                        pltpu.VMEM((2, tm, HEAD_SLAB), F32), pltpu.VMEM((4, tm, HEAD_SLAB), F32)],
        compiler_params=_cparams(("parallel", "parallel")),
        name="mla_project",
    )(x, mod8, g.reshape(1, d), w_in, qa_g, kva_g, w_qb, w_kv, qn_g, qn_gs, kn_g, kn_gs,
      q_ta, q_tb, k_ta, k_tb)


def _attn_kernel(*refs, n_lat_blocks, tk, group_blocks):
    if n_lat_blocks:
        q_ref, kc_ref, vct_ref, k_ref, vt_ref, o_ref, m_scr, acc_scr, s_scr, mb_scr = refs
    else:
        q_ref, kc_ref, vct_ref, o_ref, m_scr, acc_scr = refs
    q = q_ref[...]

    def scores(kb):
        return lax.dot_general(kb, q, (((1,), (1,)), ((), ())), preferred_element_type=F32)

    def colmax(st):
        return jnp.max(st, axis=0, keepdims=True)

    def produce(slot, kb):
        st = scores(kb)
        s_scr[slot] = st
        mb_scr[slot] = colmax(st)

    def softmax_pv(st, m_blk, vtb, first):
        if first:
            m_new = m_blk
        else:
            m_prev = m_scr[...]
            m_new = jnp.maximum(m_prev, m_blk)
        p = jnp.exp2((st - m_new).astype(BF16))
        pv = jnp.dot(vtb, p, preferred_element_type=F32)
        if first:
            acc_scr[...] = pv
        else:
            acc_scr[...] = jnp.exp2(m_prev - m_new) * acc_scr[...] + pv
        m_scr[...] = m_new

    if n_lat_blocks:
        kblk = lambda o: k_ref[pl.ds(o, tk), :]
        vblk = lambda o: vt_ref[:, pl.ds(o, tk)]
        produce(0, k_ref[0:tk, :])
        st_c = scores(kc_ref[...])
        softmax_pv(st_c, colmax(st_c), vct_ref[...], True)

        def group(i, carry):
            base = i * (group_blocks * tk)
            for u in range(group_blocks):
                cur = pl.multiple_of(base + u * tk, tk)
                nxt = pl.multiple_of(base + (u + 1) * tk, tk)
                produce((u + 1) % 2, kblk(nxt))
                softmax_pv(s_scr[u % 2], mb_scr[u % 2], vblk(cur), False)
            return carry
        lax.fori_loop(0, n_lat_blocks // group_blocks - 1, group, 0)
        e0 = n_lat_blocks - group_blocks
        for u in range(group_blocks):
            cur = (e0 + u) * tk
            if u + 1 < group_blocks:
                produce((u + 1) % 2, k_ref[cur + tk:cur + 2 * tk, :])
            softmax_pv(s_scr[u % 2], mb_scr[u % 2], vt_ref[:, cur:cur + tk], False)
    else:
        st_c = scores(kc_ref[...])
        softmax_pv(st_c, colmax(st_c), vct_ref[...], True)
    acc = acc_scr[...]
    out_t = acc[0:MLA_V, :] * (1.0 / acc[MLA_V:MLA_V + 1, :])
    o_ref[...] = out_t.T.astype(o_ref.dtype)


def _attention(q, kc, vct, k=None, vt=None):
    bsz, nh, lq, hs = q.shape
    lc = kc.shape[2]
    tq = min(ATT_TQ, lq)
    qspec = pl.BlockSpec((None, None, tq, hs), lambda b, h, i: (b, h, i, 0))
    kfull = lambda n: pl.BlockSpec((None, None, n, hs), lambda b, h, i: (b, h, 0, 0))
    vfull = lambda n: pl.BlockSpec((None, None, V_ROWS, n), lambda b, h, i: (b, h, 0, 0))
    in_specs = [qspec, kfull(lc), vfull(lc)]
    args = [q, kc, vct]
    n_lat = 0
    grp = 0
    tk = ATT_TK
    if k is not None:
        lk = k.shape[2]
        tk = min(ATT_TK, lk)
        n_lat = lk // tk
        grp = min(ATT_GROUP, n_lat)
        assert n_lat % grp == 0, "latent key blocks are consumed in whole groups"
        in_specs += [kfull(lk), vfull(lk)]
        args += [k, vt]
    scratch = [pltpu.VMEM((1, tq), F32), pltpu.VMEM((V_ROWS, tq), F32)]
    if n_lat:
        scratch += [pltpu.VMEM((2, tk, tq), F32), pltpu.VMEM((2, 1, tq), F32)]
    return pl.pallas_call(
        functools.partial(_attn_kernel, n_lat_blocks=n_lat, tk=tk, group_blocks=grp),
        grid=(bsz, nh, lq // tq),
        in_specs=in_specs,
        out_specs=pl.BlockSpec((None, tq, hs), lambda b, h, i: (b, i, h)),
        out_shape=jax.ShapeDtypeStruct((bsz, lq, nh * hs), BF16),
        scratch_shapes=scratch,
        compiler_params=_cparams(("parallel", "parallel", "arbitrary")),
        name="mla_attention",
    )(*args)


def _mlaout_kernel(x_ref, mod_ref, o_ref_in, w_ref, out_ref):
    y = jnp.dot(o_ref_in[...], w_ref[...], preferred_element_type=F32)
    out_ref[...] = x_ref[...] + mod_ref[5:6, :] * y


def _mla_out(x, mod8, mod_row, o, w_out):
    bsz, seq, d = x.shape
    tm = min(MLA_TM, seq)
    return pl.pallas_call(
        _mlaout_kernel,
        grid=(bsz, seq // tm),
        in_specs=[
            pl.BlockSpec((None, tm, d), lambda b, t: (b, t, 0)),
            pl.BlockSpec((None, N_MOD, d), lambda b, t: (mod_row(b), 0, 0)),
            pl.BlockSpec((None, tm, o.shape[-1]), lambda b, t: (b, t, 0)),
            _resident(w_out.shape),
        ],
        out_specs=pl.BlockSpec((None, tm, d), lambda b, t: (b, t, 0)),
        out_shape=jax.ShapeDtypeStruct(x.shape, F32),
        compiler_params=_cparams(("parallel", "parallel")),
        name="mla_out",
    )(x, mod8, o, w_out)


def _pair_swap_cols(w):
    w2 = w.reshape(w.shape[:-1] + (w.shape[-1] // 2, 2))
    return jnp.stack([-w2[..., 1], w2[..., 0]], axis=-1).reshape(w.shape)


def _rope_angles(seq, rot_dim):
    pos = jnp.arange(seq)
    row = (pos // GRID_W).astype(F32)
    col = (pos % GRID_W).astype(F32)
    n_freq = rot_dim // 4
    inv = ROPE_BASE ** (-jnp.arange(n_freq, dtype=F32) / n_freq)
    ang = jnp.concatenate([row[:, None] * inv, col[:, None] * inv], axis=-1)
    return jnp.repeat(jnp.cos(ang), 2, axis=-1), jnp.repeat(jnp.sin(ang), 2, axis=-1)


def _ret_tables(seq, ctx_len):
    cos, sin = _rope_angles(seq, RET_DK)
    cos2, sin2 = jnp.tile(cos, (1, 2)), jnp.tile(sin, (1, 2))
    return cos2, sin2, jnp.ones((ctx_len, LANES), F32), jnp.zeros((ctx_len, LANES), F32)


def _mla_tables(seq, ctx_len):
    cos, sin = _rope_angles(seq, MLA_ROPE)
    one = jnp.ones((seq, MLA_NOPE), F32)
    zero_tail = jnp.zeros((seq, HEAD_SLAB - MLA_QK), F32)
    zero_head = jnp.zeros((seq, MLA_NOPE), F32)
    ta = jnp.concatenate([one, cos, zero_tail], axis=-1)
    tb = jnp.concatenate([zero_head, sin, zero_tail], axis=-1)
    ident = jnp.concatenate([jnp.ones((ctx_len, MLA_QK), F32),
                             jnp.zeros((ctx_len, HEAD_SLAB - MLA_QK), F32)], axis=-1)
    return ta, tb, ident, jnp.zeros((ctx_len, HEAD_SLAB), F32)


def _rope_slab(rope_cols):
    lead = rope_cols.shape[:-1]
    return jnp.concatenate([jnp.zeros(lead + (MLA_NOPE,), F32), rope_cols,
                            jnp.zeros(lead + (HEAD_SLAB - MLA_QK,), F32)], axis=-1)


def _slab_gains(gvec):
    rope = gvec[MLA_NOPE:].reshape(MLA_ROPE // 2, 2)
    swapped = jnp.stack([rope[:, 1], rope[:, 0]], axis=-1).reshape(MLA_ROPE)
    main = jnp.concatenate([gvec, jnp.zeros((HEAD_SLAB - MLA_QK,), F32)])
    return main.reshape(1, HEAD_SLAB), _rope_slab(swapped).reshape(1, HEAD_SLAB)


def _prep_even(ab_w_in, pool_w, ab_w_out):
    wq = ab_w_in[:, POOL_WIDTH:POOL_WIDTH + RET_QK_W]
    wk = ab_w_in[:, POOL_WIDTH + RET_QK_W:POOL_WIDTH + 2 * RET_QK_W]
    w_ext = jnp.concatenate([ab_w_in, _pair_swap_cols(wq), _pair_swap_cols(wk)], axis=-1).astype(BF16)
    wpool_bd = jnp.zeros((POOL_WIDTH, POOL_WIDTH), F32)
    for gi in range(POOL_GROUPS):
        s = slice(gi * POOL_CG, (gi + 1) * POOL_CG)
        wpool_bd = wpool_bd.at[s, s].set(pool_w[gi])
    return w_ext, wpool_bd.astype(BF16), ab_w_out.astype(BF16)


def _prep_odd(w_in, w_qb, w_kvb, w_out):
    wkr = w_in[:, MLA_Q_RANK + MLA_KV_RANK:]
    w_in_ext = jnp.concatenate([w_in[:, :MLA_Q_RANK + MLA_KV_RANK], _rope_slab(wkr),
                                _rope_slab(_pair_swap_cols(wkr))], axis=-1).astype(BF16)
    wq3 = w_qb.reshape(MLA_Q_RANK, MLA_HEADS, MLA_QK)
    wq_main = jnp.concatenate([wq3, jnp.zeros((MLA_Q_RANK, MLA_HEADS, HEAD_SLAB - MLA_QK), F32)], axis=-1)
    wq_aux = _rope_slab(_pair_swap_cols(wq3[..., MLA_NOPE:]))
    w_qb_ext = jnp.concatenate([wq_main, wq_aux], axis=-1).reshape(
        MLA_Q_RANK, MLA_HEADS * 2 * HEAD_SLAB).astype(BF16)
    wkv3 = w_kvb.reshape(MLA_KV_RANK, MLA_HEADS, MLA_NOPE + MLA_V)
    wk_slab = jnp.concatenate([wkv3[..., :MLA_NOPE],
                               jnp.zeros((MLA_KV_RANK, MLA_HEADS, HEAD_SLAB - MLA_NOPE), F32)], axis=-1)
    w_kv_ext = jnp.concatenate([wk_slab, wkv3[..., MLA_NOPE:]], axis=-1).reshape(
        MLA_KV_RANK, MLA_HEADS * 2 * HEAD_SLAB).astype(BF16)
    return w_in_ext, w_qb_ext, w_kv_ext, w_out.astype(BF16)


def kernel(x, c, ctx, c_ctx, ada_w, ada_b, norm_g, ffn_w_in, ffn_w_out, ab_w_in, pool_w, pool_scale,
           ret_log_decay, ret_norm_g, ab_w_out, mla_w_in, mla_qa_g, mla_kva_g, mla_w_qb, mla_w_kvb,
           mla_qn_g, mla_kn_g, mla_w_out):
    bsz, seq, d = x.shape
    ctx_len = ctx.shape[1]
    depth = ada_w.shape[0]
    ctx_row = bsz
    cond8 = jnp.zeros((8, d), F32).at[:bsz].set(c).at[ctx_row].set(c_ctx)
    mod_all = _ada_modulation(cond8, ada_w, ada_b).reshape(depth, 8, N_MOD, d)
    lat_row = lambda b: b
    ctx_mod_row = lambda b: ctx_row

    ret_cos, ret_sin, ret_cos_c, ret_sin_c = _ret_tables(seq, ctx_len)
    m_ta, m_tb, m_ta_c, m_tb_c = _mla_tables(seq, ctx_len)
    q_scale = MLA_QK ** -0.5 * LOG2_E
    zero_state = jnp.zeros((bsz, RET_PAIRS, 2 * RET_DK, 2 * RET_DV), F32)

    xc = ctx
    for i in range(depth):
        last = i == depth - 1
        j = i // 2
        mod8 = mod_all[i]
        w1_in, w1_out = ffn_w_in[i, 0].astype(BF16), ffn_w_out[i, 0].astype(BF16)
        w2_in, w2_out = ffn_w_in[i, 1].astype(BF16), ffn_w_out[i, 1].astype(BF16)
        x = _ffn_half(x, mod8, lat_row, 0, norm_g[i, 0], w1_in, w1_out)
        xc = _ffn_half(xc, mod8, ctx_mod_row, 0, norm_g[i, 0], w1_in, w1_out)
        if i % 2 == 0:
            w_ext, wpool_bd, w_out = _prep_even(ab_w_in[j], pool_w[j], ab_w_out[j])
            lg = ret_log_decay[j]
            lgf_q = jnp.repeat(lg[0], RET_DK).reshape(1, RET_QK_W)
            lgb_q = jnp.repeat(lg[1], RET_DK).reshape(1, RET_QK_W)
            lgf_v = jnp.repeat(lg[0], RET_DV).reshape(1, RET_V_W)
            lgb_v = jnp.repeat(lg[1], RET_DV).reshape(1, RET_V_W)
            pscale = pool_scale[j].reshape(1, POOL_WIDTH)
            rng = ret_norm_g[j].reshape(1, RET_V_W)
            pc, qc, kc, vc, sgc, cbc, state_b = _ab_project(
                xc, mod8, ctx_mod_row, norm_g[i, 1], w_ext, ret_cos_c, ret_sin_c, lgb_q, lgb_v, zero_state)
            xc_mixed, state_f = _ret_out(xc, mod8, ctx_mod_row, pc, qc, kc, vc, sgc, cbc, wpool_bd, pscale,
                                         rng, w_out, lgf_q, lgf_v, lgb_v, zero_state)
            p, q, k, v, sg, cb, _ = _ab_project(
                x, mod8, lat_row, norm_g[i, 1], w_ext, ret_cos, ret_sin, lgb_q, lgb_v, state_b)
            x, _ = _ret_out(x, mod8, lat_row, p, q, k, v, sg, cb, wpool_bd, pscale, rng, w_out,
                            lgf_q, lgf_v, lgb_v, state_f)
        else:
            w_in_ext, w_qb_ext, w_kv_ext, w_out = _prep_odd(mla_w_in[j], mla_w_qb[j], mla_w_kvb[j],
                                                            mla_w_out[j])
            qa_g = mla_qa_g[j].reshape(1, MLA_Q_RANK)
            kva_g = mla_kva_g[j].reshape(1, MLA_KV_RANK)
            qn_g, qn_gs = _slab_gains(mla_qn_g[j])
            kn_g, kn_gs = _slab_gains(mla_kn_g[j])
            qc, kc, vc = _mla_project(xc, mod8, ctx_mod_row, norm_g[i, 1], w_in_ext, qa_g, kva_g, w_qb_ext,
                                      w_kv_ext, qn_g, qn_gs, kn_g, kn_gs,
                                      m_ta_c * q_scale, m_tb_c, m_ta_c, m_tb_c)
            q, k, v = _mla_project(x, mod8, lat_row, norm_g[i, 1], w_in_ext, qa_g, kva_g, w_qb_ext,
                                   w_kv_ext, qn_g, qn_gs, kn_g, kn_gs,
                                   m_ta * q_scale, m_tb * q_scale, m_ta, m_tb)
            o = _attention(q, kc, vc, k, v)
            x = _mla_out(x, mod8, lat_row, o, w_out)
            if not last:
                oc = _attention(qc, kc, vc)
                xc_mixed = _mla_out(xc, mod8, ctx_mod_row, oc, w_out)
        x = _ffn_half(x, mod8, lat_row, 2, norm_g[i, 2], w2_in, w2_out)
        if not last:
            xc = _ffn_half(xc_mixed, mod8, ctx_mod_row, 2, norm_g[i, 2], w2_in, w2_out)
    return x
```

```python
import functools

import numpy as np
import jax
import jax.numpy as jnp
from jax import lax
from jax.experimental import pallas as pl
from jax.experimental.pallas import tpu as pltpu

F32 = jnp.float32
BF16 = jnp.bfloat16

D_MODEL = 1024
GRID_W = 64
D_FF = 2816
FFN_RESIDUAL = 0.5
N_MOD = 9
ROPE_BASE = 10000.0
EPS = 1e-6
POOL_GROUPS = 4
POOL_CG = 64
POOL_WIDTH = POOL_GROUPS * POOL_CG
POOL_WINDOWS = (2, 4, 8, 16)
POOL_HALO = 8
POOL_PAD = 128
RET_HEADS = 6
RET_PAIRS = RET_HEADS // 2
RET_DK = 64
RET_DV = 128
RET_CHUNK = 128
RET_QK_W = RET_HEADS * RET_DK
RET_V_W = RET_HEADS * RET_DV
AB_IN = POOL_WIDTH + 2 * RET_QK_W + 2 * RET_V_W
AB_EXT = AB_IN + 2 * RET_QK_W
MLA_HEADS = 8
MLA_Q_RANK = 384
MLA_KV_RANK = 256
MLA_NOPE = 64
MLA_ROPE = 32
MLA_V = 128
MLA_QK = MLA_NOPE + MLA_ROPE
HEAD_SLAB = 128
V_ROWS = MLA_V + 16
LOG2_E = 1.4426950408889634

LANES = 128
V7X_VMEM_BYTES = 64 * 1024 * 1024
VMEM_LIMIT_BYTES = 56 * 1024 * 1024

ADA_TN = 1536
FFN_TM = 512
MIX_TM = 256
MLA_TM = 256
MLA_HEAD_UNROLL = 8
ATT_TQ = 1024
ATT_TK = 512
ATT_GROUP = 2


def _cparams(sem):
    return pltpu.CompilerParams(dimension_semantics=sem, vmem_limit_bytes=VMEM_LIMIT_BYTES)


def _resident(shape):
    nd = len(shape)
    return pl.BlockSpec(shape, lambda *_: (0,) * nd, pipeline_mode=pl.Buffered(1))


def _silu(x):
    return x * (1.0 / (1.0 + jnp.exp(-x)))


def _rms_mod(x, g, shift, scale):
    ms = jnp.mean(x * x, axis=-1, keepdims=True)
    return (x * lax.rsqrt(ms + EPS) * g) * (1.0 + scale) + shift


def _ada_kernel(c_ref, w_ref, b_ref, o_ref):
    s = _silu(c_ref[...])
    o_ref[...] = jnp.dot(s, w_ref[...], precision=lax.Precision.HIGHEST,
                         preferred_element_type=F32) + b_ref[...]


def _ada_modulation(cond8, ada_w, ada_b):
    depth, d, n = ada_w.shape
    return pl.pallas_call(
        _ada_kernel,
        grid=(depth, n // ADA_TN),
        in_specs=[
            pl.BlockSpec((8, d), lambda i, j: (0, 0)),
            pl.BlockSpec((None, d, ADA_TN), lambda i, j: (i, 0, j)),
            pl.BlockSpec((None, 1, ADA_TN), lambda i, j: (i, 0, j)),
        ],
        out_specs=pl.BlockSpec((None, 8, ADA_TN), lambda i, j: (i, 0, j)),
        out_shape=jax.ShapeDtypeStruct((depth, 8, n), F32),
        compiler_params=_cparams(("parallel", "parallel")),
        name="ada_mod",
    )(cond8, ada_w, ada_b.reshape(depth, 1, n))


def _ffn_kernel(x_ref, mod_ref, g_ref, win_ref, wout_ref, o_ref, *, k):
    x = x_ref[...]
    h = _rms_mod(x, g_ref[...], mod_ref[3 * k:3 * k + 1, :], mod_ref[3 * k + 1:3 * k + 2, :])
    ab = jnp.dot(h.astype(BF16), win_ref[...], preferred_element_type=F32)
    a = ab[:, :D_FF]
    b = ab[:, D_FF:]
    act = (_silu(a) * b).astype(BF16)
    y = jnp.dot(act, wout_ref[...], preferred_element_type=F32)
    o_ref[...] = x + (FFN_RESIDUAL * mod_ref[3 * k + 2:3 * k + 3, :]) * y


def _ffn_half(x, mod8, mod_row, k, g, w_in, w_out):
    bsz, seq, d = x.shape
    tm = min(FFN_TM, seq)
    return pl.pallas_call(
        functools.partial(_ffn_kernel, k=k),
        grid=(bsz, seq // tm),
        in_specs=[
            pl.BlockSpec((None, tm, d), lambda b, t: (b, t, 0)),
            pl.BlockSpec((None, N_MOD, d), lambda b, t: (mod_row(b), 0, 0)),
            pl.BlockSpec((1, d), lambda b, t: (0, 0)),
            _resident(w_in.shape),
            _resident(w_out.shape),
        ],
        out_specs=pl.BlockSpec((None, tm, d), lambda b, t: (b, t, 0)),
        out_shape=jax.ShapeDtypeStruct(x.shape, F32),
        compiler_params=_cparams(("parallel", "parallel")),
        name="ffn_half",
    )(x, mod8, g.reshape(1, d), w_in, w_out)


def _pair_tables(lg_row, expo):
    return jnp.exp(lg_row * expo)


def _bd_mask():
    r = lax.broadcasted_iota(jnp.int32, (2 * RET_DK, 2 * RET_DV), 0)
    c = lax.broadcasted_iota(jnp.int32, (2 * RET_DK, 2 * RET_DV), 1)
    same_head = jnp.where(r < RET_DK, 0, 1) == jnp.where(c < RET_DV, 0, 1)
    return jnp.where(same_head, 1.0, 0.0)


def _abproj_kernel(x_ref, mod_ref, g_ref, w_ref, cos_ref, sin_ref, lgq_ref, lgv_ref, s0_ref,
                   p_ref, q_ref, k_ref, v_ref, sg_ref, cb_ref, sfin_ref, st_scr, *, tm, nt):
    t = pl.program_id(1)

    @pl.when(t == 0)
    def _():
        st_scr[...] = s0_ref[...]

    h = _rms_mod(x_ref[...], g_ref[...], mod_ref[3:4, :], mod_ref[4:5, :])
    proj = jnp.dot(h.astype(BF16), w_ref[...], preferred_element_type=F32)
    p_ref[...] = proj[:, :POOL_WIDTH]
    q0, k0 = POOL_WIDTH, POOL_WIDTH + RET_QK_W
    v0, g0 = k0 + RET_QK_W, k0 + RET_QK_W + RET_V_W
    qs0, ks0 = AB_IN, AB_IN + RET_QK_W
    v_ref[...] = proj[:, v0:v0 + RET_V_W].astype(BF16)
    sg_ref[...] = _silu(proj[:, g0:g0 + RET_V_W]).astype(BF16)

    cos = cos_ref[...]
    sin = sin_ref[...]
    row = lax.broadcasted_iota(jnp.int32, (RET_CHUNK, LANES), 0).astype(F32)
    bdm = _bd_mask()
    k_scale = RET_DK ** -0.5
    for j in range(RET_PAIRS):
        sl = slice(LANES * j, LANES * (j + 1))
        qr = proj[:, q0 + LANES * j:q0 + LANES * (j + 1)] * cos \
            + proj[:, qs0 + LANES * j:qs0 + LANES * (j + 1)] * sin
        kr = (proj[:, k0 + LANES * j:k0 + LANES * (j + 1)] * cos
              + proj[:, ks0 + LANES * j:ks0 + LANES * (j + 1)] * sin) * k_scale
        q_ref[:, sl] = qr.astype(BF16)
        k_ref[:, sl] = kr.astype(BF16)
        lgq = lgq_ref[:, sl]
        lgv = lgv_ref[:, 2 * LANES * j:2 * LANES * (j + 1)]
        xi = _pair_tables(lgq, float(RET_CHUNK) - row)
        zeta = _pair_tables(lgq, row)
        dec = jnp.exp(lgv * float(RET_CHUNK))
        vsl = slice(2 * LANES * j, 2 * LANES * (j + 1))
        for c in reversed(range(tm // RET_CHUNK)):
            rs = slice(c * RET_CHUNK, (c + 1) * RET_CHUNK)
            bd = st_scr[j]
            qx = (qr[rs] * xi).astype(BF16)
            cb_ref[rs, vsl] = jnp.dot(qx, bd.astype(BF16), preferred_element_type=F32)
            kz = (kr[rs] * zeta).astype(BF16)
            upd = lax.dot_general(kz, v_ref[rs, vsl], (((0,), (0,)), ((), ())),
                                  preferred_element_type=F32)
            st_scr[j] = dec * bd + bdm * upd

    @pl.when(t == nt - 1)
    def _():
        sfin_ref[...] = st_scr[...]


def _ab_project(x, mod8, mod_row, g, w_ext, cos, sin, lg_q, lg_v, s0):
    bsz, seq, d = x.shape
    tm = min(MIX_TM, seq)
    nt = seq // tm
    rev = lambda b, t: (b, nt - 1 - t, 0)
    tok = lambda w, dt: jax.ShapeDtypeStruct((bsz, seq, w), dt)
    st_shape = (RET_PAIRS, 2 * RET_DK, 2 * RET_DV)
    return pl.pallas_call(
        functools.partial(_abproj_kernel, tm=tm, nt=nt),
        grid=(bsz, nt),
        in_specs=[
            pl.BlockSpec((None, tm, d), rev),
            pl.BlockSpec((None, N_MOD, d), lambda b, t: (mod_row(b), 0, 0)),
            pl.BlockSpec((1, d), lambda b, t: (0, 0)),
            _resident(w_ext.shape),
            pl.BlockSpec((tm, LANES), lambda b, t: (nt - 1 - t, 0)),
            pl.BlockSpec((tm, LANES), lambda b, t: (nt - 1 - t, 0)),
            pl.BlockSpec((1, RET_QK_W), lambda b, t: (0, 0)),
            pl.BlockSpec((1, RET_V_W), lambda b, t: (0, 0)),
            pl.BlockSpec((None,) + st_shape, lambda b, t: (b, 0, 0, 0)),
        ],
        out_specs=[
            pl.BlockSpec((None, tm, POOL_WIDTH), rev),
            pl.BlockSpec((None, tm, RET_QK_W), rev),
            pl.BlockSpec((None, tm, RET_QK_W), rev),
            pl.BlockSpec((None, tm, RET_V_W), rev),
            pl.BlockSpec((None, tm, RET_V_W), rev),
            pl.BlockSpec((None, tm, RET_V_W), rev),
            pl.BlockSpec((None,) + st_shape, lambda b, t: (b, 0, 0, 0)),
        ],
        out_shape=[tok(POOL_WIDTH, F32), tok(RET_QK_W, BF16), tok(RET_QK_W, BF16),
                   tok(RET_V_W, BF16), tok(RET_V_W, BF16), tok(RET_V_W, F32),
                   jax.ShapeDtypeStruct((bsz,) + st_shape, F32)],
        scratch_shapes=[pltpu.VMEM(st_shape, F32)],
        compiler_params=_cparams(("arbitrary", "arbitrary")),
        name="ab_project",
    )(x, mod8, g.reshape(1, d), w_ext, cos, sin, lg_q, lg_v, s0)


def _retout_kernel(x_ref, mod_ref, pprev_ref, pcur_ref, pnext_ref, pm_ref, invc_ref,
                   q_ref, k_ref, v_ref, sg_ref, cb_ref, wpool_ref, pscale_ref, rng_ref, wout_ref,
                   lgfq_ref, lgfv_ref, lgbv_ref, s0_ref,
                   o_ref, sfin_ref, st_scr, ycat_scr, pe_scr, *, tm, nt):
    t = pl.program_id(1)

    @pl.when(t == 0)
    def _():
        st_scr[...] = s0_ref[...]

    pcur = pcur_ref[...]
    pe_scr[0:tm, :] = pcur
    pe_scr[tm:tm + POOL_HALO, :] = pprev_ref[...]
    pe_scr[tm + POOL_HALO:tm + 2 * POOL_HALO, :] = pnext_ref[...]
    pe_scr[tm + 2 * POOL_HALO:, :] = jnp.zeros((POOL_PAD - 2 * POOL_HALO, POOL_WIDTH), F32)
    pe = pe_scr[...]
    pe_hi = pe.astype(BF16)
    pe_lo = (pe - pe_hi.astype(F32)).astype(BF16)
    lane_grp = jnp.right_shift(lax.broadcasted_iota(jnp.int32, (tm, POOL_WIDTH), 1), 6)
    pooled = jnp.zeros((tm, POOL_WIDTH), F32)
    for gi in range(POOL_GROUPS):
        win = pm_ref[gi]
        tot = jnp.dot(win, pe_hi, preferred_element_type=F32) \
            + jnp.dot(win, pe_lo, preferred_element_type=F32)
        pooled = jnp.where(lane_grp == gi, tot, pooled)
    diffs = pooled * invc_ref[...] - pcur
    pool_y = jnp.dot(diffs.astype(BF16), wpool_ref[...], preferred_element_type=F32) * pscale_ref[...]
    ycat_scr[:, 0:POOL_WIDTH] = pool_y.astype(BF16)

    ri = lax.broadcasted_iota(jnp.int32, (RET_CHUNK, RET_CHUNK), 0)
    ci = lax.broadcasted_iota(jnp.int32, (RET_CHUNK, RET_CHUNK), 1)
    rel = (ri - ci).astype(F32)
    row = lax.broadcasted_iota(jnp.int32, (RET_CHUNK, LANES), 0).astype(F32)
    lane = lax.broadcasted_iota(jnp.int32, (RET_CHUNK, LANES), 1)
    bdm = _bd_mask()
    zero_bf = jnp.zeros((RET_CHUNK, LANES), BF16)
    for j in range(RET_PAIRS):
        sl = slice(LANES * j, LANES * (j + 1))
        vsl = slice(2 * LANES * j, 2 * LANES * (j + 1))
        lgq = lgfq_ref[:, sl]
        xi = _pair_tables(lgq, row + 1.0)
        zeta = _pair_tables(lgq, float(RET_CHUNK - 1) - row)
        dec = jnp.exp(lgfv_ref[:, vsl] * float(RET_CHUNK))
        masks = []
        for a in range(2):
            hs = slice(2 * LANES * j + LANES * a, 2 * LANES * j + LANES * (a + 1))
            mf = jnp.exp(lgfv_ref[:, hs] * jnp.maximum(rel, 0.0))
            mb = jnp.exp(lgbv_ref[:, hs] * jnp.maximum(-rel, 0.0))
            masks.append(jnp.where(rel > 0, mf, jnp.where(rel < 0, mb, 2.0)))
        for c in range(tm // RET_CHUNK):
            rs = slice(c * RET_CHUNK, (c + 1) * RET_CHUNK)
            q2 = q_ref[rs, sl]
            k2 = k_ref[rs, sl]
            v2 = v_ref[rs, vsl]
            bd = st_scr[j]
            qx = (q2.astype(F32) * xi).astype(BF16)
            ret = jnp.dot(qx, bd.astype(BF16), preferred_element_type=F32) + cb_ref[rs, vsl]
            for a in range(2):
                qa = jnp.where(lane < RET_DK if a == 0 else lane >= RET_DK, q2, zero_bf)
                s = lax.dot_general(qa, k2, (((1,), (1,)), ((), ())), preferred_element_type=F32)
                pa = (s * masks[a]).astype(BF16)
                hs_v = slice(LANES * a, LANES * (a + 1))
                r = ret[:, hs_v] + jnp.dot(pa, v2[:, hs_v], preferred_element_type=F32)
                ms = jnp.mean(r * r, axis=-1, keepdims=True)
                hcol = 2 * LANES * j + LANES * a
                y = (r * lax.rsqrt(ms + EPS) * rng_ref[:, hcol:hcol + LANES]) \
                    * sg_ref[rs, hcol:hcol + LANES].astype(F32)
                ycat_scr[rs, POOL_WIDTH + hcol:POOL_WIDTH + hcol + LANES] = y.astype(BF16)
            kz = (k2.astype(F32) * zeta).astype(BF16)
            upd = lax.dot_general(kz, v2, (((0,), (0,)), ((), ())), preferred_element_type=F32)
            st_scr[j] = dec * bd + bdm * upd

    yy = jnp.dot(ycat_scr[...], wout_ref[...], preferred_element_type=F32)
    o_ref[...] = x_ref[...] + mod_ref[5:6, :] * yy

    @pl.when(t == nt - 1)
    def _():
        sfin_ref[...] = st_scr[...]


def _pool_constants(tm, seq):
    nt = seq // tm
    kinds = [0] if nt == 1 else [0, 1, nt - 1]
    mats = np.zeros((len(kinds), POOL_GROUPS, tm, tm + POOL_PAD), np.float32)
    invc = np.zeros((len(kinds), tm, POOL_WIDTH), np.float32)
    i = np.arange(tm)
    for vi, tile in enumerate(kinds):
        pos = tile * tm + i
        colpos = np.full(tm + POOL_PAD, -1)
        colpos[:tm] = pos
        colpos[tm:tm + POOL_HALO] = tile * tm - POOL_HALO + np.arange(POOL_HALO)
        colpos[tm + POOL_HALO:tm + 2 * POOL_HALO] = (tile + 1) * tm + np.arange(POOL_HALO)
        for gi, w in enumerate(POOL_WINDOWS):
            lo = w // 2
            hi = w - 1 - lo
            start = np.maximum(pos - lo, 0)
            end = np.minimum(pos + hi + 1, seq)
            mats[vi, gi] = (colpos[None, :] >= start[:, None]) & (colpos[None, :] < end[:, None])
            invc[vi, :, gi * POOL_CG:(gi + 1) * POOL_CG] = (1.0 / (end - start))[:, None]
    return jnp.asarray(mats, BF16), jnp.asarray(invc, F32), nt


def _ret_out(x, mod8, mod_row, p, q, k, v, sg, cb, wpool_bd, pscale, rng, w_out, lgf_q, lgf_v, lgb_v, s0):
    bsz, seq, d = x.shape
    tm = min(MIX_TM, seq)
    pm, invc, nt = _pool_constants(tm, seq)
    hb = tm // POOL_HALO
    nhb = seq // POOL_HALO

    def variant(t):
        if nt == 1:
            return 0
        return jnp.where(t == 0, 0, jnp.where(t == nt - 1, 2, 1))

    cur = lambda b, t: (b, t, 0)
    st_shape = (RET_PAIRS, 2 * RET_DK, 2 * RET_DV)
    tokspec = lambda w: pl.BlockSpec((None, tm, w), cur)
    return pl.pallas_call(
        functools.partial(_retout_kernel, tm=tm, nt=nt),
        grid=(bsz, nt),
        in_specs=[
            tokspec(d),
            pl.BlockSpec((None, N_MOD, d), lambda b, t: (mod_row(b), 0, 0)),
            pl.BlockSpec((None, POOL_HALO, POOL_WIDTH), lambda b, t: (b, jnp.maximum(t * hb - 1, 0), 0)),
            tokspec(POOL_WIDTH),
            pl.BlockSpec((None, POOL_HALO, POOL_WIDTH),
                         lambda b, t: (b, jnp.minimum((t + 1) * hb, nhb - 1), 0)),
            pl.BlockSpec((None, POOL_GROUPS, tm, tm + POOL_PAD), lambda b, t: (variant(t), 0, 0, 0)),
            pl.BlockSpec((None, tm, POOL_WIDTH), lambda b, t: (variant(t), 0, 0)),
            tokspec(RET_QK_W), tokspec(RET_QK_W), tokspec(RET_V_W), tokspec(RET_V_W), tokspec(RET_V_W),
            _resident(wpool_bd.shape),
            pl.BlockSpec((1, POOL_WIDTH), lambda b, t: (0, 0)),
            pl.BlockSpec((1, RET_V_W), lambda b, t: (0, 0)),
            _resident(w_out.shape),
            pl.BlockSpec((1, RET_QK_W), lambda b, t: (0, 0)),
            pl.BlockSpec((1, RET_V_W), lambda b, t: (0, 0)),
            pl.BlockSpec((1, RET_V_W), lambda b, t: (0, 0)),
            pl.BlockSpec((None,) + st_shape, lambda b, t: (b, 0, 0, 0)),
        ],
        out_specs=[
            tokspec(d),
            pl.BlockSpec((None,) + st_shape, lambda b, t: (b, 0, 0, 0)),
        ],
        out_shape=[jax.ShapeDtypeStruct(x.shape, F32),
                   jax.ShapeDtypeStruct((bsz,) + st_shape, F32)],
        scratch_shapes=[pltpu.VMEM(st_shape, F32),
                        pltpu.VMEM((tm, d), BF16),
                        pltpu.VMEM((tm + POOL_PAD, POOL_WIDTH), F32)],
        compiler_params=_cparams(("arbitrary", "arbitrary")),
        name="ret_out",
    )(x, mod8, p, p, p, pm, invc, q, k, v, sg, cb, wpool_bd, pscale, rng, w_out,
      lgf_q, lgf_v, lgb_v, s0)


def _head_norm_rope(main, aux, g_main, g_aux):
    ms = jnp.sum(main * main, axis=-1, keepdims=True) * (1.0 / MLA_QK)
    r = lax.rsqrt(ms + EPS)
    return (main * r) * g_main + (aux * r) * g_aux


def _mlaproj_kernel(x_ref, mod_ref, g_ref, win_ref, qag_ref, kvag_ref, wqb_ref, wkv_ref,
                    qg_ref, qgs_ref, kg_ref, kgs_ref, qa_t_ref, qb_t_ref, ka_t_ref, kb_t_ref,
                    q_ref, k_ref, v_ref, qan_scr, kvn_scr, kr_scr, gt_scr):
    h = _rms_mod(x_ref[...], g_ref[...], mod_ref[3:4, :], mod_ref[4:5, :])
    proj = jnp.dot(h.astype(BF16), win_ref[...], preferred_element_type=F32)
    qa = proj[:, :MLA_Q_RANK]
    kva = proj[:, MLA_Q_RANK:MLA_Q_RANK + MLA_KV_RANK]
    r0 = MLA_Q_RANK + MLA_KV_RANK
    kr_scr[0] = proj[:, r0:r0 + HEAD_SLAB]
    kr_scr[1] = proj[:, r0 + HEAD_SLAB:r0 + 2 * HEAD_SLAB]
    qan = qa * lax.rsqrt(jnp.mean(qa * qa, axis=-1, keepdims=True) + EPS) * qag_ref[...]
    kvn = kva * lax.rsqrt(jnp.mean(kva * kva, axis=-1, keepdims=True) + EPS) * kvag_ref[...]
    qan_scr[...] = qan.astype(BF16)
    kvn_scr[...] = kvn.astype(BF16)
    gt_scr[0] = qg_ref[...] * qa_t_ref[...]
    gt_scr[1] = qgs_ref[...] * qb_t_ref[...]
    gt_scr[2] = kg_ref[...] * ka_t_ref[...]
    gt_scr[3] = kgs_ref[...] * kb_t_ref[...]
    tm = proj.shape[0]
    ones_rows = jnp.where(lax.broadcasted_iota(jnp.int32, (V_ROWS - MLA_V, tm), 0) == 0,
                          1.0, 0.0).astype(BF16)

    def head(hd, carry):
        c0 = pl.multiple_of(hd * (2 * HEAD_SLAB), 2 * HEAD_SLAB)
        qh = jnp.dot(qan_scr[...], wqb_ref[:, pl.ds(c0, 2 * HEAD_SLAB)], preferred_element_type=F32)
        kvh = jnp.dot(kvn_scr[...], wkv_ref[:, pl.ds(c0, 2 * HEAD_SLAB)], preferred_element_type=F32)
        q_ref[hd] = _head_norm_rope(qh[:, :HEAD_SLAB], qh[:, HEAD_SLAB:], gt_scr[0], gt_scr[1]).astype(BF16)
        k_ref[hd] = _head_norm_rope(kvh[:, :HEAD_SLAB] + kr_scr[0], kr_scr[1],
                                    gt_scr[2], gt_scr[3]).astype(BF16)
        v_ref[hd, 0:MLA_V, :] = kvh[:, HEAD_SLAB:].T.astype(BF16)
        v_ref[hd, MLA_V:V_ROWS, :] = ones_rows
        return carry
    lax.fori_loop(0, MLA_HEADS, head, 0, unroll=MLA_HEAD_UNROLL)


def _mla_project(x, mod8, mod_row, g, w_in, qa_g, kva_g, w_qb, w_kv, qn_g, qn_gs, kn_g, kn_gs,
                 q_ta, q_tb, k_ta, k_tb):
    bsz, seq, d = x.shape
    tm = min(MLA_TM, seq)
    head_out = jax.ShapeDtypeStruct((bsz, MLA_HEADS, seq, HEAD_SLAB), BF16)
    hspec = pl.BlockSpec((None, MLA_HEADS, tm, HEAD_SLAB), lambda b, t: (b, 0, t, 0))
    tspec = pl.BlockSpec((tm, HEAD_SLAB), lambda b, t: (t, 0))
    row = lambda n: pl.BlockSpec((1, n), lambda b, t: (0, 0))
    return pl.pallas_call(
        _mlaproj_kernel,
        grid=(bsz, seq // tm),
        in_specs=[
            pl.BlockSpec((None, tm, d), lambda b, t: (b, t, 0)),
            pl.BlockSpec((None, N_MOD, d), lambda b, t: (mod_row(b), 0, 0)),
            row(d),
            _resident(w_in.shape), row(MLA_Q_RANK), row(MLA_KV_RANK),
            _resident(w_qb.shape), _resident(w_kv.shape),
            row(HEAD_SLAB), row(HEAD_SLAB), row(HEAD_SLAB), row(HEAD_SLAB),
            tspec, tspec, tspec, tspec,
        ],
        out_specs=[hspec, hspec,
                   pl.BlockSpec((None, MLA_HEADS, V_ROWS, tm), lambda b, t: (b, 0, 0, t))],
        out_shape=[head_out, head_out,
                   jax.ShapeDtypeStruct((bsz, MLA_HEADS, V_ROWS, seq), BF16)],
        scratch_shapes=[pltpu.VMEM((tm, MLA_Q_RANK), BF16), pltpu.VMEM((tm, MLA_KV_RANK), BF16),
                        pltpu.VMEM((2, tm, HEAD_SLAB), F32), pltpu.VMEM((4, tm, HEAD_SLAB), F32)],
        compiler_params=_cparams(("parallel", "parallel")),
        name="mla_project",
    )(x, mod8, g.reshape(1, d), w_in, qa_g, kva_g, w_qb, w_kv, qn_g, qn_gs, kn_g, kn_gs,
      q_ta, q_tb, k_ta, k_tb)


def _attn_kernel(*refs, n_lat_blocks, tk, group_blocks):
    if n_lat_blocks:
        q_ref, kc_ref, vct_ref, k_ref, vt_ref, o_ref, m_scr, acc_scr, s_scr, mb_scr = refs
    else:
        q_ref, kc_ref, vct_ref, o_ref, m_scr, acc_scr = refs
    q = q_ref[...]

    def scores(kb):
        return lax.dot_general(kb, q, (((1,), (1,)), ((), ())), preferred_element_type=F32)

    def colmax(st):
        return jnp.max(st, axis=0, keepdims=True)

    def produce(slot, kb):
        st = scores(kb)
        s_scr[slot] = st
        mb_scr[slot] = colmax(st)

    def softmax_pv(st, m_blk, vtb, first):
        if first:
            m_new = m_blk
        else:
            m_prev = m_scr[...]
            m_new = jnp.maximum(m_prev, m_blk)
        p = jnp.exp2(st - m_new).astype(BF16)
        pv = jnp.dot(vtb, p, preferred_element_type=F32)
        if first:
            acc_scr[...] = pv
        else:
            acc_scr[...] = jnp.exp2(m_prev - m_new) * acc_scr[...] + pv
        m_scr[...] = m_new

    if n_lat_blocks:
        kblk = lambda o: k_ref[pl.ds(o, tk), :]
        vblk = lambda o: vt_ref[:, pl.ds(o, tk)]
        produce(0, k_ref[0:tk, :])
        st_c = scores(kc_ref[...])
        softmax_pv(st_c, colmax(st_c), vct_ref[...], True)

        def group(i, carry):
            base = i * (group_blocks * tk)
            for u in range(group_blocks):
                cur = pl.multiple_of(base + u * tk, tk)
                nxt = pl.multiple_of(base + (u + 1) * tk, tk)
                produce((u + 1) % 2, kblk(nxt))
                softmax_pv(s_scr[u % 2], mb_scr[u % 2], vblk(cur), False)
            return carry
        lax.fori_loop(0, n_lat_blocks // group_blocks - 1, group, 0)
        e0 = n_lat_blocks - group_blocks
        for u in range(group_blocks):
            cur = (e0 + u) * tk
            if u + 1 < group_blocks:
                produce((u + 1) % 2, k_ref[cur + tk:cur + 2 * tk, :])
            softmax_pv(s_scr[u % 2], mb_scr[u % 2], vt_ref[:, cur:cur + tk], False)
    else:
        st_c = scores(kc_ref[...])
        softmax_pv(st_c, colmax(st_c), vct_ref[...], True)
    acc = acc_scr[...]
    out_t = acc[0:MLA_V, :] * (1.0 / acc[MLA_V:MLA_V + 1, :])
    o_ref[...] = out_t.T.astype(o_ref.dtype)


def _attention(q, kc, vct, k=None, vt=None):
    bsz, nh, lq, hs = q.shape
    lc = kc.shape[2]
    tq = min(ATT_TQ, lq)
    qspec = pl.BlockSpec((None, None, tq, hs), lambda b, h, i: (b, h, i, 0))
    kfull = lambda n: pl.BlockSpec((None, None, n, hs), lambda b, h, i: (b, h, 0, 0))
    vfull = lambda n: pl.BlockSpec((None, None, V_ROWS, n), lambda b, h, i: (b, h, 0, 0))
    in_specs = [qspec, kfull(lc), vfull(lc)]
    args = [q, kc, vct]
    n_lat = 0
    grp = 0
    tk = ATT_TK
    if k is not None:
        lk = k.shape[2]
        tk = min(ATT_TK, lk)
        n_lat = lk // tk
        grp = min(ATT_GROUP, n_lat)
        assert n_lat % grp == 0, "latent key blocks are consumed in whole groups"
        in_specs += [kfull(lk), vfull(lk)]
        args += [k, vt]
    scratch = [pltpu.VMEM((1, tq), F32), pltpu.VMEM((V_ROWS, tq), F32)]
    if n_lat:
        scratch += [pltpu.VMEM((2, tk, tq), F32), pltpu.VMEM((2, 1, tq), F32)]
    return pl.pallas_call(
        functools.partial(_attn_kernel, n_lat_blocks=n_lat, tk=tk, group_blocks=grp),
        grid=(bsz, nh, lq // tq),
        in_specs=in_specs,
        out_specs=pl.BlockSpec((None, tq, hs), lambda b, h, i: (b, i, h)),
        out_shape=jax.ShapeDtypeStruct((bsz, lq, nh * hs), BF16),
        scratch_shapes=scratch,
        compiler_params=_cparams(("parallel", "parallel", "arbitrary")),
        name="mla_attention",
    )(*args)


def _mlaout_kernel(x_ref, mod_ref, o_ref_in, w_ref, out_ref):
    y = jnp.dot(o_ref_in[...], w_ref[...], preferred_element_type=F32)
    out_ref[...] = x_ref[...] + mod_ref[5:6, :] * y


def _mla_out(x, mod8, mod_row, o, w_out):
    bsz, seq, d = x.shape
    tm = min(MLA_TM, seq)
    return pl.pallas_call(
        _mlaout_kernel,
        grid=(bsz, seq // tm),
        in_specs=[
            pl.BlockSpec((None, tm, d), lambda b, t: (b, t, 0)),
            pl.BlockSpec((None, N_MOD, d), lambda b, t: (mod_row(b), 0, 0)),
            pl.BlockSpec((None, tm, o.shape[-1]), lambda b, t: (b, t, 0)),
            _resident(w_out.shape),
        ],
        out_specs=pl.BlockSpec((None, tm, d), lambda b, t: (b, t, 0)),
        out_shape=jax.ShapeDtypeStruct(x.shape, F32),
        compiler_params=_cparams(("parallel", "parallel")),
        name="mla_out",
    )(x, mod8, o, w_out)


def _pair_swap_cols(w):
    w2 = w.reshape(w.shape[:-1] + (w.shape[-1] // 2, 2))
    return jnp.stack([-w2[..., 1], w2[..., 0]], axis=-1).reshape(w.shape)


def _rope_angles(seq, rot_dim):
    pos = jnp.arange(seq)
    row = (pos // GRID_W).astype(F32)
    col = (pos % GRID_W).astype(F32)
    n_freq = rot_dim // 4
    inv = ROPE_BASE ** (-jnp.arange(n_freq, dtype=F32) / n_freq)
    ang = jnp.concatenate([row[:, None] * inv, col[:, None] * inv], axis=-1)
    return jnp.repeat(jnp.cos(ang), 2, axis=-1), jnp.repeat(jnp.sin(ang), 2, axis=-1)


def _ret_tables(seq, ctx_len):
    cos, sin = _rope_angles(seq, RET_DK)
    cos2, sin2 = jnp.tile(cos, (1, 2)), jnp.tile(sin, (1, 2))
    return cos2, sin2, jnp.ones((ctx_len, LANES), F32), jnp.zeros((ctx_len, LANES), F32)


def _mla_tables(seq, ctx_len):
    cos, sin = _rope_angles(seq, MLA_ROPE)
    one = jnp.ones((seq, MLA_NOPE), F32)
    zero_tail = jnp.zeros((seq, HEAD_SLAB - MLA_QK), F32)
    zero_head = jnp.zeros((seq, MLA_NOPE), F32)
    ta = jnp.concatenate([one, cos, zero_tail], axis=-1)
    tb = jnp.concatenate([zero_head, sin, zero_tail], axis=-1)
    ident = jnp.concatenate([jnp.ones((ctx_len, MLA_QK), F32),
                             jnp.zeros((ctx_len, HEAD_SLAB - MLA_QK), F32)], axis=-1)
    return ta, tb, ident, jnp.zeros((ctx_len, HEAD_SLAB), F32)


def _rope_slab(rope_cols):
    lead = rope_cols.shape[:-1]
    return jnp.concatenate([jnp.zeros(lead + (MLA_NOPE,), F32), rope_cols,
                            jnp.zeros(lead + (HEAD_SLAB - MLA_QK,), F32)], axis=-1)


def _slab_gains(gvec):
    rope = gvec[MLA_NOPE:].reshape(MLA_ROPE // 2, 2)
    swapped = jnp.stack([rope[:, 1], rope[:, 0]], axis=-1).reshape(MLA_ROPE)
    main = jnp.concatenate([gvec, jnp.zeros((HEAD_SLAB - MLA_QK,), F32)])
    return main.reshape(1, HEAD_SLAB), _rope_slab(swapped).reshape(1, HEAD_SLAB)


def _prep_even(ab_w_in, pool_w, ab_w_out):
    wq = ab_w_in[:, POOL_WIDTH:POOL_WIDTH + RET_QK_W]
    wk = ab_w_in[:, POOL_WIDTH + RET_QK_W:POOL_WIDTH + 2 * RET_QK_W]
    w_ext = jnp.concatenate([ab_w_in, _pair_swap_cols(wq), _pair_swap_cols(wk)], axis=-1).astype(BF16)
    wpool_bd = jnp.zeros((POOL_WIDTH, POOL_WIDTH), F32)
    for gi in range(POOL_GROUPS):
        s = slice(gi * POOL_CG, (gi + 1) * POOL_CG)
        wpool_bd = wpool_bd.at[s, s].set(pool_w[gi])
    return w_ext, wpool_bd.astype(BF16), ab_w_out.astype(BF16)


def _prep_odd(w_in, w_qb, w_kvb, w_out):
    wkr = w_in[:, MLA_Q_RANK + MLA_KV_RANK:]
    w_in_ext = jnp.concatenate([w_in[:, :MLA_Q_RANK + MLA_KV_RANK], _rope_slab(wkr),
                                _rope_slab(_pair_swap_cols(wkr))], axis=-1).astype(BF16)
    wq3 = w_qb.reshape(MLA_Q_RANK, MLA_HEADS, MLA_QK)
    wq_main = jnp.concatenate([wq3, jnp.zeros((MLA_Q_RANK, MLA_HEADS, HEAD_SLAB - MLA_QK), F32)], axis=-1)
    wq_aux = _rope_slab(_pair_swap_cols(wq3[..., MLA_NOPE:]))
    w_qb_ext = jnp.concatenate([wq_main, wq_aux], axis=-1).reshape(
        MLA_Q_RANK, MLA_HEADS * 2 * HEAD_SLAB).astype(BF16)
    wkv3 = w_kvb.reshape(MLA_KV_RANK, MLA_HEADS, MLA_NOPE + MLA_V)
    wk_slab = jnp.concatenate([wkv3[..., :MLA_NOPE],
                               jnp.zeros((MLA_KV_RANK, MLA_HEADS, HEAD_SLAB - MLA_NOPE), F32)], axis=-1)
    w_kv_ext = jnp.concatenate([wk_slab, wkv3[..., MLA_NOPE:]], axis=-1).reshape(
        MLA_KV_RANK, MLA_HEADS * 2 * HEAD_SLAB).astype(BF16)
    return w_in_ext, w_qb_ext, w_kv_ext, w_out.astype(BF16)


def kernel(x, c, ctx, c_ctx, ada_w, ada_b, norm_g, ffn_w_in, ffn_w_out, ab_w_in, pool_w, pool_scale,
           ret_log_decay, ret_norm_g, ab_w_out, mla_w_in, mla_qa_g, mla_kva_g, mla_w_qb, mla_w_kvb,
           mla_qn_g, mla_kn_g, mla_w_out):
    bsz, seq, d = x.shape
    ctx_len = ctx.shape[1]
    depth = ada_w.shape[0]
    ctx_row = bsz
    cond8 = jnp.zeros((8, d), F32).at[:bsz].set(c).at[ctx_row].set(c_ctx)
    mod_all = _ada_modulation(cond8, ada_w, ada_b).reshape(depth, 8, N_MOD, d)
    lat_row = lambda b: b
    ctx_mod_row = lambda b: ctx_row

    ret_cos, ret_sin, ret_cos_c, ret_sin_c = _ret_tables(seq, ctx_len)
    m_ta, m_tb, m_ta_c, m_tb_c = _mla_tables(seq, ctx_len)
    q_scale = MLA_QK ** -0.5 * LOG2_E
    zero_state = jnp.zeros((bsz, RET_PAIRS, 2 * RET_DK, 2 * RET_DV), F32)

    xc = ctx
    for i in range(depth):
        last = i == depth - 1
        j = i // 2
        mod8 = mod_all[i]
        w1_in, w1_out = ffn_w_in[i, 0].astype(BF16), ffn_w_out[i, 0].astype(BF16)
        w2_in, w2_out = ffn_w_in[i, 1].astype(BF16), ffn_w_out[i, 1].astype(BF16)
        x = _ffn_half(x, mod8, lat_row, 0, norm_g[i, 0], w1_in, w1_out)
        xc = _ffn_half(xc, mod8, ctx_mod_row, 0, norm_g[i, 0], w1_in, w1_out)
        if i % 2 == 0:
            w_ext, wpool_bd, w_out = _prep_even(ab_w_in[j], pool_w[j], ab_w_out[j])
            lg = ret_log_decay[j]
            lgf_q = jnp.repeat(lg[0], RET_DK).reshape(1, RET_QK_W)
            lgb_q = jnp.repeat(lg[1], RET_DK).reshape(1, RET_QK_W)
            lgf_v = jnp.repeat(lg[0], RET_DV).reshape(1, RET_V_W)
            lgb_v = jnp.repeat(lg[1], RET_DV).reshape(1, RET_V_W)
            pscale = pool_scale[j].reshape(1, POOL_WIDTH)
            rng = ret_norm_g[j].reshape(1, RET_V_W)
            pc, qc, kc, vc, sgc, cbc, state_b = _ab_project(
                xc, mod8, ctx_mod_row, norm_g[i, 1], w_ext, ret_cos_c, ret_sin_c, lgb_q, lgb_v, zero_state)
            xc_mixed, state_f = _ret_out(xc, mod8, ctx_mod_row, pc, qc, kc, vc, sgc, cbc, wpool_bd, pscale,
                                         rng, w_out, lgf_q, lgf_v, lgb_v, zero_state)
            p, q, k, v, sg, cb, _ = _ab_project(
                x, mod8, lat_row, norm_g[i, 1], w_ext, ret_cos, ret_sin, lgb_q, lgb_v, state_b)
            x, _ = _ret_out(x, mod8, lat_row, p, q, k, v, sg, cb, wpool_bd, pscale, rng, w_out,
                            lgf_q, lgf_v, lgb_v, state_f)
        else:
            w_in_ext, w_qb_ext, w_kv_ext, w_out = _prep_odd(mla_w_in[j], mla_w_qb[j], mla_w_kvb[j],
                                                            mla_w_out[j])
            qa_g = mla_qa_g[j].reshape(1, MLA_Q_RANK)
            kva_g = mla_kva_g[j].reshape(1, MLA_KV_RANK)
            qn_g, qn_gs = _slab_gains(mla_qn_g[j])
            kn_g, kn_gs = _slab_gains(mla_kn_g[j])
            qc, kc, vc = _mla_project(xc, mod8, ctx_mod_row, norm_g[i, 1], w_in_ext, qa_g, kva_g, w_qb_ext,
                                      w_kv_ext, qn_g, qn_gs, kn_g, kn_gs,
                                      m_ta_c * q_scale, m_tb_c, m_ta_c, m_tb_c)
            q, k, v = _mla_project(x, mod8, lat_row, norm_g[i, 1], w_in_ext, qa_g, kva_g, w_qb_ext,
                                   w_kv_ext, qn_g, qn_gs, kn_g, kn_gs,
                                   m_ta * q_scale, m_tb * q_scale, m_ta, m_tb)
            o = _attention(q, kc, vc, k, v)
            x = _mla_out(x, mod8, lat_row, o, w_out)
            if not last:
                oc = _attention(qc, kc, vc)
                xc_mixed = _mla_out(xc, mod8, ctx_mod_row, oc, w_out)
        x = _ffn_half(x, mod8, lat_row, 2, norm_g[i, 2], w2_in, w2_out)
        if not last:
            xc = _ffn_half(xc_mixed, mod8, ctx_mod_row, 2, norm_g[i, 2], w2_in, w2_out)
    return x
```

```python
import functools

import numpy as np
import jax
import jax.numpy as jnp
from jax import lax
from jax.experimental import pallas as pl
from jax.experimental.pallas import tpu as pltpu

F32 = jnp.float32
BF16 = jnp.bfloat16

D_MODEL = 1024
GRID_W = 64
D_FF = 2816
FFN_RESIDUAL = 0.5
N_MOD = 9
ROPE_BASE = 10000.0
EPS = 1e-6
POOL_GROUPS = 4
POOL_CG = 64
POOL_WIDTH = POOL_GROUPS * POOL_CG
POOL_WINDOWS = (2, 4, 8, 16)
POOL_HALO = 8
POOL_PAD = 128
RET_HEADS = 6
RET_PAIRS = RET_HEADS // 2
RET_DK = 64
RET_DV = 128
RET_CHUNK = 128
RET_QK_W = RET_HEADS * RET_DK
RET_V_W = RET_HEADS * RET_DV
AB_IN = POOL_WIDTH + 2 * RET_QK_W + 2 * RET_V_W
AB_EXT = AB_IN + 2 * RET_QK_W
MLA_HEADS = 8
MLA_Q_RANK = 384
MLA_KV_RANK = 256
MLA_NOPE = 64
MLA_ROPE = 32
MLA_V = 128
MLA_QK = MLA_NOPE + MLA_ROPE
HEAD_SLAB = 128
V_ROWS = MLA_V + 16
LOG2_E = 1.4426950408889634

LANES = 128
V7X_VMEM_BYTES = 64 * 1024 * 1024
VMEM_LIMIT_BYTES = 56 * 1024 * 1024

ADA_TN = 1536
FFN_TM = 512
MIX_TM = 256
MLA_TM = 256
MLA_HEAD_UNROLL = 8
ATT_TQ = 1024
ATT_TK = 512
ATT_GROUP = 2


def _cparams(sem):
    return pltpu.CompilerParams(dimension_semantics=sem, vmem_limit_bytes=VMEM_LIMIT_BYTES)


def _resident(shape):
    nd = len(shape)
    return pl.BlockSpec(shape, lambda *_: (0,) * nd, pipeline_mode=pl.Buffered(1))


def _silu(x):
    return x * (1.0 / (1.0 + jnp.exp(-x)))


def _rms_mod(x, g, shift, scale):
    ms = jnp.mean(x * x, axis=-1, keepdims=True)
    return (x * lax.rsqrt(ms + EPS) * g) * (1.0 + scale) + shift


def _ada_kernel(c_ref, w_ref, b_ref, o_ref):
    s = _silu(c_ref[...])
    o_ref[...] = jnp.dot(s, w_ref[...], precision=lax.Precision.HIGHEST,
                         preferred_element_type=F32) + b_ref[...]


def _ada_modulation(cond8, ada_w, ada_b):
    depth, d, n = ada_w.shape
    return pl.pallas_call(
        _ada_kernel,
        grid=(depth, n // ADA_TN),
        in_specs=[
            pl.BlockSpec((8, d), lambda i, j: (0, 0)),
            pl.BlockSpec((None, d, ADA_TN), lambda i, j: (i, 0, j)),
            pl.BlockSpec((None, 1, ADA_TN), lambda i, j: (i, 0, j)),
        ],
        out_specs=pl.BlockSpec((None, 8, ADA_TN), lambda i, j: (i, 0, j)),
        out_shape=jax.ShapeDtypeStruct((depth, 8, n), F32),
        compiler_params=_cparams(("parallel", "parallel")),
        name="ada_mod",
    )(cond8, ada_w, ada_b.reshape(depth, 1, n))


def _ffn_kernel(*refs, k, mixed):
    if mixed:
        x_ref, mod_ref, g_ref, win_ref, wout_ref, mix_ref, wmix_ref, o_ref = refs
        x = x_ref[...] + mod_ref[5:6, :] * jnp.dot(mix_ref[...], wmix_ref[...],
                                                    preferred_element_type=F32)
    else:
        x_ref, mod_ref, g_ref, win_ref, wout_ref, o_ref = refs
        x = x_ref[...]
    h = _rms_mod(x, g_ref[...], mod_ref[3 * k:3 * k + 1, :], mod_ref[3 * k + 1:3 * k + 2, :])
    ab = jnp.dot(h.astype(BF16), win_ref[...], preferred_element_type=F32)
    a = ab[:, :D_FF]
    b = ab[:, D_FF:]
    act = (_silu(a) * b).astype(BF16)
    y = jnp.dot(act, wout_ref[...], preferred_element_type=F32)
    o_ref[...] = x + (FFN_RESIDUAL * mod_ref[3 * k + 2:3 * k + 3, :]) * y


def _ffn_half(x, mod8, mod_row, k, g, w_in, w_out, mix=None, w_mix=None):
    bsz, seq, d = x.shape
    tm = min(FFN_TM, seq)
    in_specs = [
        pl.BlockSpec((None, tm, d), lambda b, t: (b, t, 0)),
        pl.BlockSpec((None, N_MOD, d), lambda b, t: (mod_row(b), 0, 0)),
        pl.BlockSpec((1, d), lambda b, t: (0, 0)),
        _resident(w_in.shape),
        _resident(w_out.shape),
    ]
    args = [x, mod8, g.reshape(1, d), w_in, w_out]
    if mix is not None:
        in_specs += [pl.BlockSpec((None, tm, mix.shape[-1]), lambda b, t: (b, t, 0)),
                     _resident(w_mix.shape)]
        args += [mix, w_mix]
    return pl.pallas_call(
        functools.partial(_ffn_kernel, k=k, mixed=mix is not None),
        grid=(bsz, seq // tm),
        in_specs=in_specs,
        out_specs=pl.BlockSpec((None, tm, d), lambda b, t: (b, t, 0)),
        out_shape=jax.ShapeDtypeStruct(x.shape, F32),
        compiler_params=_cparams(("parallel", "parallel")),
        name="ffn_half",
    )(*args)


def _pair_tables(lg_row, expo):
    return jnp.exp(lg_row * expo)


def _bd_mask():
    r = lax.broadcasted_iota(jnp.int32, (2 * RET_DK, 2 * RET_DV), 0)
    c = lax.broadcasted_iota(jnp.int32, (2 * RET_DK, 2 * RET_DV), 1)
    same_head = jnp.where(r < RET_DK, 0, 1) == jnp.where(c < RET_DV, 0, 1)
    return jnp.where(same_head, 1.0, 0.0)


def _abproj_kernel(x_ref, mod_ref, g_ref, w_ref, cos_ref, sin_ref, lgq_ref, lgv_ref, s0_ref,
                   p_ref, q_ref, k_ref, v_ref, sg_ref, cb_ref, sfin_ref, st_scr, *, tm, nt):
    t = pl.program_id(1)

    @pl.when(t == 0)
    def _():
        st_scr[...] = s0_ref[...]

    h = _rms_mod(x_ref[...], g_ref[...], mod_ref[3:4, :], mod_ref[4:5, :])
    proj = jnp.dot(h.astype(BF16), w_ref[...], preferred_element_type=F32)
    p_ref[...] = proj[:, :POOL_WIDTH]
    q0, k0 = POOL_WIDTH, POOL_WIDTH + RET_QK_W
    v0, g0 = k0 + RET_QK_W, k0 + RET_QK_W + RET_V_W
    qs0, ks0 = AB_IN, AB_IN + RET_QK_W
    v_ref[...] = proj[:, v0:v0 + RET_V_W].astype(BF16)
    sg_ref[...] = _silu(proj[:, g0:g0 + RET_V_W]).astype(BF16)

    cos = cos_ref[...]
    sin = sin_ref[...]
    row = lax.broadcasted_iota(jnp.int32, (RET_CHUNK, LANES), 0).astype(F32)
    bdm = _bd_mask()
    k_scale = RET_DK ** -0.5
    for j in range(RET_PAIRS):
        sl = slice(LANES * j, LANES * (j + 1))
        qr = proj[:, q0 + LANES * j:q0 + LANES * (j + 1)] * cos \
            + proj[:, qs0 + LANES * j:qs0 + LANES * (j + 1)] * sin
        kr = (proj[:, k0 + LANES * j:k0 + LANES * (j + 1)] * cos
              + proj[:, ks0 + LANES * j:ks0 + LANES * (j + 1)] * sin) * k_scale
        q_ref[:, sl] = qr.astype(BF16)
        k_ref[:, sl] = kr.astype(BF16)
        lgq = lgq_ref[:, sl]
        lgv = lgv_ref[:, 2 * LANES * j:2 * LANES * (j + 1)]
        xi = _pair_tables(lgq, float(RET_CHUNK) - row)
        zeta = _pair_tables(lgq, row)
        dec = jnp.exp(lgv * float(RET_CHUNK))
        vsl = slice(2 * LANES * j, 2 * LANES * (j + 1))
        for c in reversed(range(tm // RET_CHUNK)):
            rs = slice(c * RET_CHUNK, (c + 1) * RET_CHUNK)
            bd = st_scr[j]
            qx = (qr[rs] * xi).astype(BF16)
            cb_ref[rs, vsl] = jnp.dot(qx, bd.astype(BF16), preferred_element_type=F32)
            kz = (kr[rs] * zeta).astype(BF16)
            upd = lax.dot_general(kz, v_ref[rs, vsl], (((0,), (0,)), ((), ())),
                                  preferred_element_type=F32)
            st_scr[j] = dec * bd + bdm * upd

    @pl.when(t == nt - 1)
    def _():
        sfin_ref[...] = st_scr[...]


def _ab_project(x, mod8, mod_row, g, w_ext, cos, sin, lg_q, lg_v, s0):
    bsz, seq, d = x.shape
    tm = min(MIX_TM, seq)
    nt = seq // tm
    rev = lambda b, t: (b, nt - 1 - t, 0)
    tok = lambda w, dt: jax.ShapeDtypeStruct((bsz, seq, w), dt)
    st_shape = (RET_PAIRS, 2 * RET_DK, 2 * RET_DV)
    return pl.pallas_call(
        functools.partial(_abproj_kernel, tm=tm, nt=nt),
        grid=(bsz, nt),
        in_specs=[
            pl.BlockSpec((None, tm, d), rev),
            pl.BlockSpec((None, N_MOD, d), lambda b, t: (mod_row(b), 0, 0)),
            pl.BlockSpec((1, d), lambda b, t: (0, 0)),
            _resident(w_ext.shape),
            pl.BlockSpec((tm, LANES), lambda b, t: (nt - 1 - t, 0)),
            pl.BlockSpec((tm, LANES), lambda b, t: (nt - 1 - t, 0)),
            pl.BlockSpec((1, RET_QK_W), lambda b, t: (0, 0)),
            pl.BlockSpec((1, RET_V_W), lambda b, t: (0, 0)),
            pl.BlockSpec((None,) + st_shape, lambda b, t: (b, 0, 0, 0)),
        ],
        out_specs=[
            pl.BlockSpec((None, tm, POOL_WIDTH), rev),
            pl.BlockSpec((None, tm, RET_QK_W), rev),
            pl.BlockSpec((None, tm, RET_QK_W), rev),
            pl.BlockSpec((None, tm, RET_V_W), rev),
            pl.BlockSpec((None, tm, RET_V_W), rev),
            pl.BlockSpec((None, tm, RET_V_W), rev),
            pl.BlockSpec((None,) + st_shape, lambda b, t: (b, 0, 0, 0)),
        ],
        out_shape=[tok(POOL_WIDTH, F32), tok(RET_QK_W, BF16), tok(RET_QK_W, BF16),
                   tok(RET_V_W, BF16), tok(RET_V_W, BF16), tok(RET_V_W, F32),
                   jax.ShapeDtypeStruct((bsz,) + st_shape, F32)],
        scratch_shapes=[pltpu.VMEM(st_shape, F32)],
        compiler_params=_cparams(("arbitrary", "arbitrary")),
        name="ab_project",
    )(x, mod8, g.reshape(1, d), w_ext, cos, sin, lg_q, lg_v, s0)


def _retout_kernel(x_ref, mod_ref, pprev_ref, pcur_ref, pnext_ref, pm_ref, invc_ref,
                   q_ref, k_ref, v_ref, sg_ref, cb_ref, wpool_ref, pscale_ref, rng_ref, wout_ref,
                   lgfq_ref, lgfv_ref, lgbv_ref, s0_ref,
                   o_ref, sfin_ref, st_scr, ycat_scr, pe_scr, *, tm, nt):
    t = pl.program_id(1)

    @pl.when(t == 0)
    def _():
        st_scr[...] = s0_ref[...]

    pcur = pcur_ref[...]
    pe_scr[0:tm, :] = pcur
    pe_scr[tm:tm + POOL_HALO, :] = pprev_ref[...]
    pe_scr[tm + POOL_HALO:tm + 2 * POOL_HALO, :] = pnext_ref[...]
    pe_scr[tm + 2 * POOL_HALO:, :] = jnp.zeros((POOL_PAD - 2 * POOL_HALO, POOL_WIDTH), F32)
    pe = pe_scr[...]
    pe_hi = pe.astype(BF16)
    pe_lo = (pe - pe_hi.astype(F32)).astype(BF16)
    lane_grp = jnp.right_shift(lax.broadcasted_iota(jnp.int32, (tm, POOL_WIDTH), 1), 6)
    pooled = jnp.zeros((tm, POOL_WIDTH), F32)
    for gi in range(POOL_GROUPS):
        win = pm_ref[gi]
        tot = jnp.dot(win, pe_hi, preferred_element_type=F32) \
            + jnp.dot(win, pe_lo, preferred_element_type=F32)
        pooled = jnp.where(lane_grp == gi, tot, pooled)
    diffs = pooled * invc_ref[...] - pcur
    pool_y = jnp.dot(diffs.astype(BF16), wpool_ref[...], preferred_element_type=F32) * pscale_ref[...]
    ycat_scr[:, 0:POOL_WIDTH] = pool_y.astype(BF16)

    ri = lax.broadcasted_iota(jnp.int32, (RET_CHUNK, RET_CHUNK), 0)
    ci = lax.broadcasted_iota(jnp.int32, (RET_CHUNK, RET_CHUNK), 1)
    rel = (ri - ci).astype(F32)
    row = lax.broadcasted_iota(jnp.int32, (RET_CHUNK, LANES), 0).astype(F32)
    lane = lax.broadcasted_iota(jnp.int32, (RET_CHUNK, LANES), 1)
    bdm = _bd_mask()
    zero_bf = jnp.zeros((RET_CHUNK, LANES), BF16)
    for j in range(RET_PAIRS):
        sl = slice(LANES * j, LANES * (j + 1))
        vsl = slice(2 * LANES * j, 2 * LANES * (j + 1))
        lgq = lgfq_ref[:, sl]
        xi = _pair_tables(lgq, row + 1.0)
        zeta = _pair_tables(lgq, float(RET_CHUNK - 1) - row)
        dec = jnp.exp(lgfv_ref[:, vsl] * float(RET_CHUNK))
        masks = []
        for a in range(2):
            hs = slice(2 * LANES * j + LANES * a, 2 * LANES * j + LANES * (a + 1))
            mf = jnp.exp(lgfv_ref[:, hs] * jnp.maximum(rel, 0.0))
            mb = jnp.exp(lgbv_ref[:, hs] * jnp.maximum(-rel, 0.0))
            masks.append(jnp.where(rel > 0, mf, jnp.where(rel < 0, mb, 2.0)))
        for c in range(tm // RET_CHUNK):
            rs = slice(c * RET_CHUNK, (c + 1) * RET_CHUNK)
            q2 = q_ref[rs, sl]
            k2 = k_ref[rs, sl]
            v2 = v_ref[rs, vsl]
            bd = st_scr[j]
            qx = (q2.astype(F32) * xi).astype(BF16)
            ret = jnp.dot(qx, bd.astype(BF16), preferred_element_type=F32) + cb_ref[rs, vsl]
            for a in range(2):
                qa = jnp.where(lane < RET_DK if a == 0 else lane >= RET_DK, q2, zero_bf)
                s = lax.dot_general(qa, k2, (((1,), (1,)), ((), ())), preferred_element_type=F32)
                pa = (s * masks[a]).astype(BF16)
                hs_v = slice(LANES * a, LANES * (a + 1))
                r = ret[:, hs_v] + jnp.dot(pa, v2[:, hs_v], preferred_element_type=F32)
                ms = jnp.mean(r * r, axis=-1, keepdims=True)
                hcol = 2 * LANES * j + LANES * a
                y = (r * lax.rsqrt(ms + EPS) * rng_ref[:, hcol:hcol + LANES]) \
                    * sg_ref[rs, hcol:hcol + LANES].astype(F32)
                ycat_scr[rs, POOL_WIDTH + hcol:POOL_WIDTH + hcol + LANES] = y.astype(BF16)
            kz = (k2.astype(F32) * zeta).astype(BF16)
            upd = lax.dot_general(kz, v2, (((0,), (0,)), ((), ())), preferred_element_type=F32)
            st_scr[j] = dec * bd + bdm * upd

    yy = jnp.dot(ycat_scr[...], wout_ref[...], preferred_element_type=F32)
    o_ref[...] = x_ref[...] + mod_ref[5:6, :] * yy

    @pl.when(t == nt - 1)
    def _():
        sfin_ref[...] = st_scr[...]


def _pool_constants(tm, seq):
    nt = seq // tm
    kinds = [0] if nt == 1 else [0, 1, nt - 1]
    mats = np.zeros((len(kinds), POOL_GROUPS, tm, tm + POOL_PAD), np.float32)
    invc = np.zeros((len(kinds), tm, POOL_WIDTH), np.float32)
    i = np.arange(tm)
    for vi, tile in enumerate(kinds):
        pos = tile * tm + i
        colpos = np.full(tm + POOL_PAD, -1)
        colpos[:tm] = pos
        colpos[tm:tm + POOL_HALO] = tile * tm - POOL_HALO + np.arange(POOL_HALO)
        colpos[tm + POOL_HALO:tm + 2 * POOL_HALO] = (tile + 1) * tm + np.arange(POOL_HALO)
        for gi, w in enumerate(POOL_WINDOWS):
            lo = w // 2
            hi = w - 1 - lo
            start = np.maximum(pos - lo, 0)
            end = np.minimum(pos + hi + 1, seq)
            mats[vi, gi] = (colpos[None, :] >= start[:, None]) & (colpos[None, :] < end[:, None])
            invc[vi, :, gi * POOL_CG:(gi + 1) * POOL_CG] = (1.0 / (end - start))[:, None]
    return jnp.asarray(mats, BF16), jnp.asarray(invc, F32), nt


def _ret_out(x, mod8, mod_row, p, q, k, v, sg, cb, wpool_bd, pscale, rng, w_out, lgf_q, lgf_v, lgb_v, s0):
    bsz, seq, d = x.shape
    tm = min(MIX_TM, seq)
    pm, invc, nt = _pool_constants(tm, seq)
    hb = tm // POOL_HALO
    nhb = seq // POOL_HALO

    def variant(t):
        if nt == 1:
            return 0
        return jnp.where(t == 0, 0, jnp.where(t == nt - 1, 2, 1))

    cur = lambda b, t: (b, t, 0)
    st_shape = (RET_PAIRS, 2 * RET_DK, 2 * RET_DV)
    tokspec = lambda w: pl.BlockSpec((None, tm, w), cur)
    return pl.pallas_call(
        functools.partial(_retout_kernel, tm=tm, nt=nt),
        grid=(bsz, nt),
        in_specs=[
            tokspec(d),
            pl.BlockSpec((None, N_MOD, d), lambda b, t: (mod_row(b), 0, 0)),
            pl.BlockSpec((None, POOL_HALO, POOL_WIDTH), lambda b, t: (b, jnp.maximum(t * hb - 1, 0), 0)),
            tokspec(POOL_WIDTH),
            pl.BlockSpec((None, POOL_HALO, POOL_WIDTH),
                         lambda b, t: (b, jnp.minimum((t + 1) * hb, nhb - 1), 0)),
            pl.BlockSpec((None, POOL_GROUPS, tm, tm + POOL_PAD), lambda b, t: (variant(t), 0, 0, 0)),
            pl.BlockSpec((None, tm, POOL_WIDTH), lambda b, t: (variant(t), 0, 0)),
            tokspec(RET_QK_W), tokspec(RET_QK_W), tokspec(RET_V_W), tokspec(RET_V_W), tokspec(RET_V_W),
            _resident(wpool_bd.shape),
            pl.BlockSpec((1, POOL_WIDTH), lambda b, t: (0, 0)),
            pl.BlockSpec((1, RET_V_W), lambda b, t: (0, 0)),
            _resident(w_out.shape),
            pl.BlockSpec((1, RET_QK_W), lambda b, t: (0, 0)),
            pl.BlockSpec((1, RET_V_W), lambda b, t: (0, 0)),
            pl.BlockSpec((1, RET_V_W), lambda b, t: (0, 0)),
            pl.BlockSpec((None,) + st_shape, lambda b, t: (b, 0, 0, 0)),
        ],
        out_specs=[
            tokspec(d),
            pl.BlockSpec((None,) + st_shape, lambda b, t: (b, 0, 0, 0)),
        ],
        out_shape=[jax.ShapeDtypeStruct(x.shape, F32),
                   jax.ShapeDtypeStruct((bsz,) + st_shape, F32)],
        scratch_shapes=[pltpu.VMEM(st_shape, F32),
                        pltpu.VMEM((tm, d), BF16),
                        pltpu.VMEM((tm + POOL_PAD, POOL_WIDTH), F32)],
        compiler_params=_cparams(("arbitrary", "arbitrary")),
        name="ret_out",
    )(x, mod8, p, p, p, pm, invc, q, k, v, sg, cb, wpool_bd, pscale, rng, w_out,
      lgf_q, lgf_v, lgb_v, s0)


def _head_norm_rope(main, aux, g_main, g_aux):
    ms = jnp.sum(main * main, axis=-1, keepdims=True) * (1.0 / MLA_QK)
    r = lax.rsqrt(ms + EPS)
    return (main * r) * g_main + (aux * r) * g_aux


def _mlaproj_kernel(x_ref, mod_ref, g_ref, win_ref, qag_ref, kvag_ref, wqb_ref, wkv_ref,
                    qg_ref, qgs_ref, kg_ref, kgs_ref, qa_t_ref, qb_t_ref, ka_t_ref, kb_t_ref,
                    q_ref, k_ref, v_ref, qan_scr, kvn_scr, kr_scr, gt_scr):
    h = _rms_mod(x_ref[...], g_ref[...], mod_ref[3:4, :], mod_ref[4:5, :])
    proj = jnp.dot(h.astype(BF16), win_ref[...], preferred_element_type=F32)
    qa = proj[:, :MLA_Q_RANK]
    kva = proj[:, MLA_Q_RANK:MLA_Q_RANK + MLA_KV_RANK]
    r0 = MLA_Q_RANK + MLA_KV_RANK
    kr_scr[0] = proj[:, r0:r0 + HEAD_SLAB]
    kr_scr[1] = proj[:, r0 + HEAD_SLAB:r0 + 2 * HEAD_SLAB]
    qan = qa * lax.rsqrt(jnp.mean(qa * qa, axis=-1, keepdims=True) + EPS) * qag_ref[...]
    kvn = kva * lax.rsqrt(jnp.mean(kva * kva, axis=-1, keepdims=True) + EPS) * kvag_ref[...]
    qan_scr[...] = qan.astype(BF16)
    kvn_scr[...] = kvn.astype(BF16)
    gt_scr[0] = qg_ref[...] * qa_t_ref[...]
    gt_scr[1] = qgs_ref[...] * qb_t_ref[...]
    gt_scr[2] = kg_ref[...] * ka_t_ref[...]
    gt_scr[3] = kgs_ref[...] * kb_t_ref[...]
    tm = proj.shape[0]
    ones_rows = jnp.where(lax.broadcasted_iota(jnp.int32, (V_ROWS - MLA_V, tm), 0) == 0,
                          1.0, 0.0).astype(BF16)

    def head(hd, carry):
        c0 = pl.multiple_of(hd * (2 * HEAD_SLAB), 2 * HEAD_SLAB)
        qh = jnp.dot(qan_scr[...], wqb_ref[:, pl.ds(c0, 2 * HEAD_SLAB)], preferred_element_type=F32)
        kvh = jnp.dot(kvn_scr[...], wkv_ref[:, pl.ds(c0, 2 * HEAD_SLAB)], preferred_element_type=F32)
        q_ref[hd] = _head_norm_rope(qh[:, :HEAD_SLAB], qh[:, HEAD_SLAB:], gt_scr[0], gt_scr[1]).T.astype(BF16)
        k_ref[hd] = _head_norm_rope(kvh[:, :HEAD_SLAB] + kr_scr[0], kr_scr[1],
                                    gt_scr[2], gt_scr[3]).astype(BF16)
        v_ref[hd, 0:MLA_V, :] = kvh[:, HEAD_SLAB:].T.astype(BF16)
        v_ref[hd, MLA_V:V_ROWS, :] = ones_rows
        return carry
    lax.fori_loop(0, MLA_HEADS, head, 0, unroll=MLA_HEAD_UNROLL)


def _mla_project(x, mod8, mod_row, g, w_in, qa_g, kva_g, w_qb, w_kv, qn_g, qn_gs, kn_g, kn_gs,
                 q_ta, q_tb, k_ta, k_tb):
    bsz, seq, d = x.shape
    tm = min(MLA_TM, seq)
    head_out = jax.ShapeDtypeStruct((bsz, MLA_HEADS, seq, HEAD_SLAB), BF16)
    hspec = pl.BlockSpec((None, MLA_HEADS, tm, HEAD_SLAB), lambda b, t: (b, 0, t, 0))
    tspec = pl.BlockSpec((tm, HEAD_SLAB), lambda b, t: (t, 0))
    row = lambda n: pl.BlockSpec((1, n), lambda b, t: (0, 0))
    return pl.pallas_call(
        _mlaproj_kernel,
        grid=(bsz, seq // tm),
        in_specs=[
            pl.BlockSpec((None, tm, d), lambda b, t: (b, t, 0)),
            pl.BlockSpec((None, N_MOD, d), lambda b, t: (mod_row(b), 0, 0)),
            row(d),
            _resident(w_in.shape), row(MLA_Q_RANK), row(MLA_KV_RANK),
            _resident(w_qb.shape), _resident(w_kv.shape),
            row(HEAD_SLAB), row(HEAD_SLAB), row(HEAD_SLAB), row(HEAD_SLAB),
            tspec, tspec, tspec, tspec,
        ],
        out_specs=[pl.BlockSpec((None, MLA_HEADS, HEAD_SLAB, tm), lambda b, t: (b, 0, 0, t)),
                   hspec,
                   pl.BlockSpec((None, MLA_HEADS, V_ROWS, tm), lambda b, t: (b, 0, 0, t))],
        out_shape=[jax.ShapeDtypeStruct((bsz, MLA_HEADS, HEAD_SLAB, seq), BF16),
                   head_out,
                   jax.ShapeDtypeStruct((bsz, MLA_HEADS, V_ROWS, seq), BF16)],
        scratch_shapes=[pltpu.VMEM((tm, MLA_Q_RANK), BF16), pltpu.VMEM((tm, MLA_KV_RANK), BF16),
                        pltpu.VMEM((2, tm, HEAD_SLAB), F32), pltpu.VMEM((4, tm, HEAD_SLAB), F32)],
        compiler_params=_cparams(("parallel", "parallel")),
        name="mla_project",
    )(x, mod8, g.reshape(1, d), w_in, qa_g, kva_g, w_qb, w_kv, qn_g, qn_gs, kn_g, kn_gs,
      q_ta, q_tb, k_ta, k_tb)


def _attn_kernel(*refs, n_lat_blocks, tk, group_blocks):
    if n_lat_blocks:
        q_ref, kc_ref, vct_ref, k_ref, vt_ref, o_ref, m_scr, acc_scr, s_scr, mb_scr = refs
    else:
        q_ref, kc_ref, vct_ref, o_ref, m_scr, acc_scr = refs
    q = q_ref[...]

    def scores(kb):
        return jnp.dot(kb, q, preferred_element_type=F32)

    def colmax(st):
        return jnp.max(st, axis=0, keepdims=True)

    def produce(slot, kb):
        st = scores(kb)
        s_scr[slot] = st
        mb_scr[slot] = colmax(st)

    def softmax_pv(st, m_blk, vtb, first):
        if first:
            m_new = m_blk
        else:
            m_prev = m_scr[...]
            m_new = jnp.maximum(m_prev, m_blk)
        p = jnp.exp2(st - m_new).astype(BF16)
        pv = jnp.dot(vtb, p, preferred_element_type=F32)
        if first:
            acc_scr[...] = pv
        else:
            acc_scr[...] = jnp.exp2(m_prev - m_new) * acc_scr[...] + pv
        m_scr[...] = m_new

    if n_lat_blocks:
        kblk = lambda o: k_ref[pl.ds(o, tk), :]
        vblk = lambda o: vt_ref[:, pl.ds(o, tk)]
        produce(0, k_ref[0:tk, :])
        st_c = scores(kc_ref[...])
        softmax_pv(st_c, colmax(st_c), vct_ref[...], True)

        def group(i, carry):
            base = i * (group_blocks * tk)
            for u in range(group_blocks):
                cur = pl.multiple_of(base + u * tk, tk)
                nxt = pl.multiple_of(base + (u + 1) * tk, tk)
                produce((u + 1) % 2, kblk(nxt))
                softmax_pv(s_scr[u % 2], mb_scr[u % 2], vblk(cur), False)
            return carry
        lax.fori_loop(0, n_lat_blocks // group_blocks - 1, group, 0)
        e0 = n_lat_blocks - group_blocks
        for u in range(group_blocks):
            cur = (e0 + u) * tk
            if u + 1 < group_blocks:
                produce((u + 1) % 2, k_ref[cur + tk:cur + 2 * tk, :])
            softmax_pv(s_scr[u % 2], mb_scr[u % 2], vt_ref[:, cur:cur + tk], False)
    else:
        st_c = scores(kc_ref[...])
        softmax_pv(st_c, colmax(st_c), vct_ref[...], True)
    acc = acc_scr[...]
    out_t = acc[0:MLA_V, :] * (1.0 / acc[MLA_V:MLA_V + 1, :])
    o_ref[...] = out_t.T.astype(o_ref.dtype)


def _attention(q, kc, vct, k=None, vt=None):
    bsz, nh, hs, lq = q.shape
    lc = kc.shape[2]
    tq = min(ATT_TQ, lq)
    qspec = pl.BlockSpec((None, None, hs, tq), lambda b, h, i: (b, h, 0, i))
    kfull = lambda n: pl.BlockSpec((None, None, n, hs), lambda b, h, i: (b, h, 0, 0))
    vfull = lambda n: pl.BlockSpec((None, None, V_ROWS, n), lambda b, h, i: (b, h, 0, 0))
    in_specs = [qspec, kfull(lc), vfull(lc)]
    args = [q, kc, vct]
    n_lat = 0
    grp = 0
    tk = ATT_TK
    if k is not None:
        lk = k.shape[2]
        tk = min(ATT_TK, lk)
        n_lat = lk // tk
        grp = min(ATT_GROUP, n_lat)
        assert n_lat % grp == 0, "latent key blocks are consumed in whole groups"
        in_specs += [kfull(lk), vfull(lk)]
        args += [k, vt]
    scratch = [pltpu.VMEM((1, tq), F32), pltpu.VMEM((V_ROWS, tq), F32)]
    if n_lat:
        scratch += [pltpu.VMEM((2, tk, tq), F32), pltpu.VMEM((2, 1, tq), F32)]
    return pl.pallas_call(
        functools.partial(_attn_kernel, n_lat_blocks=n_lat, tk=tk, group_blocks=grp),
        grid=(bsz, nh, lq // tq),
        in_specs=in_specs,
        out_specs=pl.BlockSpec((None, tq, hs), lambda b, h, i: (b, i, h)),
        out_shape=jax.ShapeDtypeStruct((bsz, lq, nh * hs), BF16),
        scratch_shapes=scratch,
        compiler_params=_cparams(("parallel", "parallel", "arbitrary")),
        name="mla_attention",
    )(*args)


def _pair_swap_cols(w):
    w2 = w.reshape(w.shape[:-1] + (w.shape[-1] // 2, 2))
    return jnp.stack([-w2[..., 1], w2[..., 0]], axis=-1).reshape(w.shape)


def _rope_angles(seq, rot_dim):
    pos = jnp.arange(seq)
    row = (pos // GRID_W).astype(F32)
    col = (pos % GRID_W).astype(F32)
    n_freq = rot_dim // 4
    inv = ROPE_BASE ** (-jnp.arange(n_freq, dtype=F32) / n_freq)
    ang = jnp.concatenate([row[:, None] * inv, col[:, None] * inv], axis=-1)
    return jnp.repeat(jnp.cos(ang), 2, axis=-1), jnp.repeat(jnp.sin(ang), 2, axis=-1)


def _ret_tables(seq, ctx_len):
    cos, sin = _rope_angles(seq, RET_DK)
    cos2, sin2 = jnp.tile(cos, (1, 2)), jnp.tile(sin, (1, 2))
    return cos2, sin2, jnp.ones((ctx_len, LANES), F32), jnp.zeros((ctx_len, LANES), F32)


def _mla_tables(seq, ctx_len):
    cos, sin = _rope_angles(seq, MLA_ROPE)
    one = jnp.ones((seq, MLA_NOPE), F32)
    zero_tail = jnp.zeros((seq, HEAD_SLAB - MLA_QK), F32)
    zero_head = jnp.zeros((seq, MLA_NOPE), F32)
    ta = jnp.concatenate([one, cos, zero_tail], axis=-1)
    tb = jnp.concatenate([zero_head, sin, zero_tail], axis=-1)
    ident = jnp.concatenate([jnp.ones((ctx_len, MLA_QK), F32),
                             jnp.zeros((ctx_len, HEAD_SLAB - MLA_QK), F32)], axis=-1)
    return ta, tb, ident, jnp.zeros((ctx_len, HEAD_SLAB), F32)


def _rope_slab(rope_cols):
    lead = rope_cols.shape[:-1]
    return jnp.concatenate([jnp.zeros(lead + (MLA_NOPE,), F32), rope_cols,
                            jnp.zeros(lead + (HEAD_SLAB - MLA_QK,), F32)], axis=-1)


def _slab_gains(gvec):
    rope = gvec[MLA_NOPE:].reshape(MLA_ROPE // 2, 2)
    swapped = jnp.stack([rope[:, 1], rope[:, 0]], axis=-1).reshape(MLA_ROPE)
    main = jnp.concatenate([gvec, jnp.zeros((HEAD_SLAB - MLA_QK,), F32)])
    return main.reshape(1, HEAD_SLAB), _rope_slab(swapped).reshape(1, HEAD_SLAB)


def _prep_even(ab_w_in, pool_w, ab_w_out):
    wq = ab_w_in[:, POOL_WIDTH:POOL_WIDTH + RET_QK_W]
    wk = ab_w_in[:, POOL_WIDTH + RET_QK_W:POOL_WIDTH + 2 * RET_QK_W]
    w_ext = jnp.concatenate([ab_w_in, _pair_swap_cols(wq), _pair_swap_cols(wk)], axis=-1).astype(BF16)
    wpool_bd = jnp.zeros((POOL_WIDTH, POOL_WIDTH), F32)
    for gi in range(POOL_GROUPS):
        s = slice(gi * POOL_CG, (gi + 1) * POOL_CG)
        wpool_bd = wpool_bd.at[s, s].set(pool_w[gi])
    return w_ext, wpool_bd.astype(BF16), ab_w_out.astype(BF16)


def _prep_odd(w_in, w_qb, w_kvb, w_out):
    wkr = w_in[:, MLA_Q_RANK + MLA_KV_RANK:]
    w_in_ext = jnp.concatenate([w_in[:, :MLA_Q_RANK + MLA_KV_RANK], _rope_slab(wkr),
                                _rope_slab(_pair_swap_cols(wkr))], axis=-1).astype(BF16)
    wq3 = w_qb.reshape(MLA_Q_RANK, MLA_HEADS, MLA_QK)
    wq_main = jnp.concatenate([wq3, jnp.zeros((MLA_Q_RANK, MLA_HEADS, HEAD_SLAB - MLA_QK), F32)], axis=-1)
    wq_aux = _rope_slab(_pair_swap_cols(wq3[..., MLA_NOPE:]))
    w_qb_ext = jnp.concatenate([wq_main, wq_aux], axis=-1).reshape(
        MLA_Q_RANK, MLA_HEADS * 2 * HEAD_SLAB).astype(BF16)
    wkv3 = w_kvb.reshape(MLA_KV_RANK, MLA_HEADS, MLA_NOPE + MLA_V)
    wk_slab = jnp.concatenate([wkv3[..., :MLA_NOPE],
                               jnp.zeros((MLA_KV_RANK, MLA_HEADS, HEAD_SLAB - MLA_NOPE), F32)], axis=-1)
    w_kv_ext = jnp.concatenate([wk_slab, wkv3[..., MLA_NOPE:]], axis=-1).reshape(
        MLA_KV_RANK, MLA_HEADS * 2 * HEAD_SLAB).astype(BF16)
    return w_in_ext, w_qb_ext, w_kv_ext, w_out.astype(BF16)


def kernel(x, c, ctx, c_ctx, ada_w, ada_b, norm_g, ffn_w_in, ffn_w_out, ab_w_in, pool_w, pool_scale,
           ret_log_decay, ret_norm_g, ab_w_out, mla_w_in, mla_qa_g, mla_kva_g, mla_w_qb, mla_w_kvb,
           mla_qn_g, mla_kn_g, mla_w_out):
    bsz, seq, d = x.shape
    ctx_len = ctx.shape[1]
    depth = ada_w.shape[0]
    ctx_row = bsz
    cond8 = jnp.zeros((8, d), F32).at[:bsz].set(c).at[ctx_row].set(c_ctx)
    mod_all = _ada_modulation(cond8, ada_w, ada_b).reshape(depth, 8, N_MOD, d)
    lat_row = lambda b: b
    ctx_mod_row = lambda b: ctx_row

    ret_cos, ret_sin, ret_cos_c, ret_sin_c = _ret_tables(seq, ctx_len)
    m_ta, m_tb, m_ta_c, m_tb_c = _mla_tables(seq, ctx_len)
    q_scale = MLA_QK ** -0.5 * LOG2_E
    zero_state = jnp.zeros((bsz, RET_PAIRS, 2 * RET_DK, 2 * RET_DV), F32)

    xc = ctx
    for i in range(depth):
        last = i == depth - 1
        j = i // 2
        mod8 = mod_all[i]
        w1_in, w1_out = ffn_w_in[i, 0].astype(BF16), ffn_w_out[i, 0].astype(BF16)
        w2_in, w2_out = ffn_w_in[i, 1].astype(BF16), ffn_w_out[i, 1].astype(BF16)
        x = _ffn_half(x, mod8, lat_row, 0, norm_g[i, 0], w1_in, w1_out)
        xc = _ffn_half(xc, mod8, ctx_mod_row, 0, norm_g[i, 0], w1_in, w1_out)
        mix = mix_c = w_mix = None
        if i % 2 == 0:
            w_ext, wpool_bd, w_out = _prep_even(ab_w_in[j], pool_w[j], ab_w_out[j])
            lg = ret_log_decay[j]
            lgf_q = jnp.repeat(lg[0], RET_DK).reshape(1, RET_QK_W)
            lgb_q = jnp.repeat(lg[1], RET_DK).reshape(1, RET_QK_W)
            lgf_v = jnp.repeat(lg[0], RET_DV).reshape(1, RET_V_W)
            lgb_v = jnp.repeat(lg[1], RET_DV).reshape(1, RET_V_W)
            pscale = pool_scale[j].reshape(1, POOL_WIDTH)
            rng = ret_norm_g[j].reshape(1, RET_V_W)
            pc, qc, kc, vc, sgc, cbc, state_b = _ab_project(
                xc, mod8, ctx_mod_row, norm_g[i, 1], w_ext, ret_cos_c, ret_sin_c, lgb_q, lgb_v, zero_state)
            xc_mixed, state_f = _ret_out(xc, mod8, ctx_mod_row, pc, qc, kc, vc, sgc, cbc, wpool_bd, pscale,
                                         rng, w_out, lgf_q, lgf_v, lgb_v, zero_state)
            p, q, k, v, sg, cb, _ = _ab_project(
                x, mod8, lat_row, norm_g[i, 1], w_ext, ret_cos, ret_sin, lgb_q, lgb_v, state_b)
            x, _ = _ret_out(x, mod8, lat_row, p, q, k, v, sg, cb, wpool_bd, pscale, rng, w_out,
                            lgf_q, lgf_v, lgb_v, state_f)
        else:
            w_in_ext, w_qb_ext, w_kv_ext, w_out = _prep_odd(mla_w_in[j], mla_w_qb[j], mla_w_kvb[j],
                                                            mla_w_out[j])
            qa_g = mla_qa_g[j].reshape(1, MLA_Q_RANK)
            kva_g = mla_kva_g[j].reshape(1, MLA_KV_RANK)
            qn_g, qn_gs = _slab_gains(mla_qn_g[j])
            kn_g, kn_gs = _slab_gains(mla_kn_g[j])
            qc, kc, vc = _mla_project(xc, mod8, ctx_mod_row, norm_g[i, 1], w_in_ext, qa_g, kva_g, w_qb_ext,
                                      w_kv_ext, qn_g, qn_gs, kn_g, kn_gs,
                                      m_ta_c * q_scale, m_tb_c, m_ta_c, m_tb_c)
            q, k, v = _mla_project(x, mod8, lat_row, norm_g[i, 1], w_in_ext, qa_g, kva_g, w_qb_ext,
                                   w_kv_ext, qn_g, qn_gs, kn_g, kn_gs,
                                   m_ta * q_scale, m_tb * q_scale, m_ta, m_tb)
            mix, w_mix = _attention(q, kc, vc, k, v), w_out
            if not last:
                mix_c = _attention(qc, kc, vc)
                xc_mixed = xc
        x = _ffn_half(x, mod8, lat_row, 2, norm_g[i, 2], w2_in, w2_out, mix, w_mix)
        if not last:
            xc = _ffn_half(xc_mixed, mod8, ctx_mod_row, 2, norm_g[i, 2], w2_in, w2_out, mix_c, w_mix)
    return x
```

```python
import functools

import numpy as np
import jax
import jax.numpy as jnp
from jax import lax
from jax.experimental import pallas as pl
from jax.experimental.pallas import tpu as pltpu

F32 = jnp.float32
BF16 = jnp.bfloat16

D_MODEL = 1024
GRID_W = 64
D_FF = 2816
FFN_RESIDUAL = 0.5
N_MOD = 9
ROPE_BASE = 10000.0
EPS = 1e-6
POOL_GROUPS = 4
POOL_CG = 64
POOL_WIDTH = POOL_GROUPS * POOL_CG
POOL_WINDOWS = (2, 4, 8, 16)
POOL_HALO = 8
POOL_PAD = 128
RET_HEADS = 6
RET_PAIRS = RET_HEADS // 2
RET_DK = 64
RET_DV = 128
RET_CHUNK = 128
RET_QK_W = RET_HEADS * RET_DK
RET_V_W = RET_HEADS * RET_DV
AB_IN = POOL_WIDTH + 2 * RET_QK_W + 2 * RET_V_W
AB_EXT = AB_IN + 2 * RET_QK_W
MLA_HEADS = 8
MLA_Q_RANK = 384
MLA_KV_RANK = 256
MLA_NOPE = 64
MLA_ROPE = 32
MLA_V = 128
MLA_QK = MLA_NOPE + MLA_ROPE
HEAD_SLAB = 128
V_ROWS = MLA_V + 16
LOG2_E = 1.4426950408889634

LANES = 128
V7X_VMEM_BYTES = 64 * 1024 * 1024
VMEM_LIMIT_BYTES = 56 * 1024 * 1024

ADA_TN = 1536
FFN_TM = 512
FFN_SPLIT = 2
MIX_TM = 256
AB_TM = 512
AB_SUB = 256
MLA_TM = 256
MLA_HEAD_UNROLL = 8
ATT_TQ = 1024
ATT_TK = 512
ATT_GROUP = 2


def _cparams(sem):
    return pltpu.CompilerParams(dimension_semantics=sem, vmem_limit_bytes=VMEM_LIMIT_BYTES)


def _resident(shape):
    nd = len(shape)
    return pl.BlockSpec(shape, lambda *_: (0,) * nd, pipeline_mode=pl.Buffered(1))


def _silu(x):
    return x * (1.0 / (1.0 + jnp.exp(-x)))


def _rms_mod(x, g, shift, scale):
    ms = jnp.mean(x * x, axis=-1, keepdims=True)
    return (x * lax.rsqrt(ms + EPS) * g) * (1.0 + scale) + shift


def _ada_kernel(c_ref, w_ref, b_ref, o_ref):
    s = _silu(c_ref[...])
    o_ref[...] = jnp.dot(s, w_ref[...], precision=lax.Precision.HIGHEST,
                         preferred_element_type=F32) + b_ref[...]


def _ada_modulation(cond8, ada_w, ada_b):
    depth, d, n = ada_w.shape
    return pl.pallas_call(
        _ada_kernel,
        grid=(depth, n // ADA_TN),
        in_specs=[
            pl.BlockSpec((8, d), lambda i, j: (0, 0)),
            pl.BlockSpec((None, d, ADA_TN), lambda i, j: (i, 0, j)),
            pl.BlockSpec((None, 1, ADA_TN), lambda i, j: (i, 0, j)),
        ],
        out_specs=pl.BlockSpec((None, 8, ADA_TN), lambda i, j: (i, 0, j)),
        out_shape=jax.ShapeDtypeStruct((depth, 8, n), F32),
        compiler_params=_cparams(("parallel", "parallel")),
        name="ada_mod",
    )(cond8, ada_w, ada_b.reshape(depth, 1, n))


def _ffn_kernel(*refs, k, mixed):
    if mixed:
        x_ref, mod_ref, g_ref, win_ref, wout_ref, mix_ref, wmix_ref, o_ref = refs
    else:
        x_ref, mod_ref, g_ref, win_ref, wout_ref, o_ref = refs
    tm = x_ref.shape[0]
    sub = tm // FFN_SPLIT
    for s in range(FFN_SPLIT):
        rs = slice(s * sub, (s + 1) * sub)
        x = x_ref[rs, :]
        if mixed:
            x = x + mod_ref[5:6, :] * jnp.dot(mix_ref[rs, :], wmix_ref[...], preferred_element_type=F32)
        h = _rms_mod(x, g_ref[...], mod_ref[3 * k:3 * k + 1, :], mod_ref[3 * k + 1:3 * k + 2, :])
        ab = jnp.dot(h.astype(BF16), win_ref[...], preferred_element_type=F32)
        act = (_silu(ab[:, :D_FF]) * ab[:, D_FF:]).astype(BF16)
        y = jnp.dot(act, wout_ref[...], preferred_element_type=F32)
        o_ref[rs, :] = x + (FFN_RESIDUAL * mod_ref[3 * k + 2:3 * k + 3, :]) * y


def _ffn_half(x, mod8, mod_row, k, g, w_in, w_out, mix=None, w_mix=None):
    bsz, seq, d = x.shape
    tm = min(FFN_TM, seq)
    in_specs = [
        pl.BlockSpec((None, tm, d), lambda b, t: (b, t, 0)),
        pl.BlockSpec((None, N_MOD, d), lambda b, t: (mod_row(b), 0, 0)),
        pl.BlockSpec((1, d), lambda b, t: (0, 0)),
        _resident(w_in.shape),
        _resident(w_out.shape),
    ]
    args = [x, mod8, g.reshape(1, d), w_in, w_out]
    if mix is not None:
        in_specs += [pl.BlockSpec((None, tm, mix.shape[-1]), lambda b, t: (b, t, 0)),
                     _resident(w_mix.shape)]
        args += [mix, w_mix]
    return pl.pallas_call(
        functools.partial(_ffn_kernel, k=k, mixed=mix is not None),
        grid=(bsz, seq // tm),
        in_specs=in_specs,
        out_specs=pl.BlockSpec((None, tm, d), lambda b, t: (b, t, 0)),
        out_shape=jax.ShapeDtypeStruct(x.shape, F32),
        compiler_params=_cparams(("parallel", "parallel")),
        name="ffn_half",
    )(*args)


def _pair_tables(lg_row, expo):
    return jnp.exp(lg_row * expo)


def _bd_mask():
    r = lax.broadcasted_iota(jnp.int32, (2 * RET_DK, 2 * RET_DV), 0)
    c = lax.broadcasted_iota(jnp.int32, (2 * RET_DK, 2 * RET_DV), 1)
    same_head = jnp.where(r < RET_DK, 0, 1) == jnp.where(c < RET_DV, 0, 1)
    return jnp.where(same_head, 1.0, 0.0)


def _abproj_kernel(x_ref, mod_ref, g_ref, w_ref, cos_ref, sin_ref, lgq_ref, lgv_ref, s0_ref,
                   p_ref, q_ref, k_ref, v_ref, sg_ref, cb_ref, sfin_ref, st_scr, *, tm, nt):
    t = pl.program_id(1)

    @pl.when(t == 0)
    def _():
        st_scr[...] = s0_ref[...]

    q0, k0 = POOL_WIDTH, POOL_WIDTH + RET_QK_W
    v0, g0 = k0 + RET_QK_W, k0 + RET_QK_W + RET_V_W
    qs0, ks0 = AB_IN, AB_IN + RET_QK_W
    row = lax.broadcasted_iota(jnp.int32, (RET_CHUNK, LANES), 0).astype(F32)
    bdm = _bd_mask()
    k_scale = RET_DK ** -0.5
    sub = min(AB_SUB, tm)
    for s in reversed(range(tm // sub)):
        r0 = s * sub
        ts = slice(r0, r0 + sub)
        h = _rms_mod(x_ref[ts, :], g_ref[...], mod_ref[3:4, :], mod_ref[4:5, :])
        proj = jnp.dot(h.astype(BF16), w_ref[...], preferred_element_type=F32)
        p_ref[ts, :] = proj[:, :POOL_WIDTH]
        v_ref[ts, :] = proj[:, v0:v0 + RET_V_W].astype(BF16)
        sg_ref[ts, :] = _silu(proj[:, g0:g0 + RET_V_W]).astype(BF16)
        cos = cos_ref[ts, :]
        sin = sin_ref[ts, :]
        for j in range(RET_PAIRS):
            sl = slice(LANES * j, LANES * (j + 1))
            qr = proj[:, q0 + LANES * j:q0 + LANES * (j + 1)] * cos \
                + proj[:, qs0 + LANES * j:qs0 + LANES * (j + 1)] * sin
            kr = (proj[:, k0 + LANES * j:k0 + LANES * (j + 1)] * cos
                  + proj[:, ks0 + LANES * j:ks0 + LANES * (j + 1)] * sin) * k_scale
            q_ref[ts, sl] = qr.astype(BF16)
            k_ref[ts, sl] = kr.astype(BF16)
            lgq = lgq_ref[:, sl]
            lgv = lgv_ref[:, 2 * LANES * j:2 * LANES * (j + 1)]
            xi = _pair_tables(lgq, float(RET_CHUNK) - row)
            zeta = _pair_tables(lgq, row)
            dec = jnp.exp(lgv * float(RET_CHUNK))
            vsl = slice(2 * LANES * j, 2 * LANES * (j + 1))
            for c in reversed(range(sub // RET_CHUNK)):
                ls = slice(c * RET_CHUNK, (c + 1) * RET_CHUNK)
                rs = slice(r0 + c * RET_CHUNK, r0 + (c + 1) * RET_CHUNK)
                bd = st_scr[j]
                qx = (qr[ls] * xi).astype(BF16)
                cb_ref[rs, vsl] = jnp.dot(qx, bd.astype(BF16), preferred_element_type=F32)
                kz = (kr[ls] * zeta).astype(BF16)
                upd = lax.dot_general(kz, v_ref[rs, vsl], (((0,), (0,)), ((), ())),
                                      preferred_element_type=F32)
                st_scr[j] = dec * bd + bdm * upd

    @pl.when(t == nt - 1)
    def _():
        sfin_ref[...] = st_scr[...]


def _ab_project(x, mod8, mod_row, g, w_ext, cos, sin, lg_q, lg_v, s0):
    bsz, seq, d = x.shape
    tm = min(AB_TM, seq)
    nt = seq // tm
    rev = lambda b, t: (b, nt - 1 - t, 0)
    tok = lambda w, dt: jax.ShapeDtypeStruct((bsz, seq, w), dt)
    st_shape = (RET_PAIRS, 2 * RET_DK, 2 * RET_DV)
    return pl.pallas_call(
        functools.partial(_abproj_kernel, tm=tm, nt=nt),
        grid=(bsz, nt),
        in_specs=[
            pl.BlockSpec((None, tm, d), rev),
            pl.BlockSpec((None, N_MOD, d), lambda b, t: (mod_row(b), 0, 0)),
            pl.BlockSpec((1, d), lambda b, t: (0, 0)),
            _resident(w_ext.shape),
            pl.BlockSpec((tm, LANES), lambda b, t: (nt - 1 - t, 0)),
            pl.BlockSpec((tm, LANES), lambda b, t: (nt - 1 - t, 0)),
            pl.BlockSpec((1, RET_QK_W), lambda b, t: (0, 0)),
            pl.BlockSpec((1, RET_V_W), lambda b, t: (0, 0)),
            pl.BlockSpec((None,) + st_shape, lambda b, t: (b, 0, 0, 0)),
        ],
        out_specs=[
            pl.BlockSpec((None, tm, POOL_WIDTH), rev),
            pl.BlockSpec((None, tm, RET_QK_W), rev),
            pl.BlockSpec((None, tm, RET_QK_W), rev),
            pl.BlockSpec((None, tm, RET_V_W), rev),
            pl.BlockSpec((None, tm, RET_V_W), rev),
            pl.BlockSpec((None, tm, RET_V_W), rev),
            pl.BlockSpec((None,) + st_shape, lambda b, t: (b, 0, 0, 0)),
        ],
        out_shape=[tok(POOL_WIDTH, F32), tok(RET_QK_W, BF16), tok(RET_QK_W, BF16),
                   tok(RET_V_W, BF16), tok(RET_V_W, BF16), tok(RET_V_W, F32),
                   jax.ShapeDtypeStruct((bsz,) + st_shape, F32)],
        scratch_shapes=[pltpu.VMEM(st_shape, F32)],
        compiler_params=_cparams(("arbitrary", "arbitrary")),
        name="ab_project",
    )(x, mod8, g.reshape(1, d), w_ext, cos, sin, lg_q, lg_v, s0)


def _retout_kernel(x_ref, mod_ref, pprev_ref, pcur_ref, pnext_ref, pm_ref, invc_ref,
                   q_ref, k_ref, v_ref, sg_ref, cb_ref, wpool_ref, pscale_ref, rng_ref, wout_ref,
                   lgfq_ref, lgfv_ref, lgbv_ref, s0_ref,
                   o_ref, sfin_ref, st_scr, ycat_scr, pe_scr, *, tm, nt):
    t = pl.program_id(1)

    @pl.when(t == 0)
    def _():
        st_scr[...] = s0_ref[...]

    pcur = pcur_ref[...]
    pe_scr[0:tm, :] = pcur
    pe_scr[tm:tm + POOL_HALO, :] = pprev_ref[...]
    pe_scr[tm + POOL_HALO:tm + 2 * POOL_HALO, :] = pnext_ref[...]
    pe_scr[tm + 2 * POOL_HALO:, :] = jnp.zeros((POOL_PAD - 2 * POOL_HALO, POOL_WIDTH), F32)
    pe = pe_scr[...]
    pe_hi = pe.astype(BF16)
    pe_lo = (pe - pe_hi.astype(F32)).astype(BF16)
    lane_grp = jnp.right_shift(lax.broadcasted_iota(jnp.int32, (tm, POOL_WIDTH), 1), 6)
    pooled = jnp.zeros((tm, POOL_WIDTH), F32)
    for gi in range(POOL_GROUPS):
        win = pm_ref[gi]
        tot = jnp.dot(win, pe_hi, preferred_element_type=F32) \
            + jnp.dot(win, pe_lo, preferred_element_type=F32)
        pooled = jnp.where(lane_grp == gi, tot, pooled)
    diffs = pooled * invc_ref[...] - pcur
    pool_y = jnp.dot(diffs.astype(BF16), wpool_ref[...], preferred_element_type=F32) * pscale_ref[...]
    ycat_scr[:, 0:POOL_WIDTH] = pool_y.astype(BF16)

    ri = lax.broadcasted_iota(jnp.int32, (RET_CHUNK, RET_CHUNK), 0)
    ci = lax.broadcasted_iota(jnp.int32, (RET_CHUNK, RET_CHUNK), 1)
    rel = (ri - ci).astype(F32)
    row = lax.broadcasted_iota(jnp.int32, (RET_CHUNK, LANES), 0).astype(F32)
    lane = lax.broadcasted_iota(jnp.int32, (RET_CHUNK, LANES), 1)
    bdm = _bd_mask()
    zero_bf = jnp.zeros((RET_CHUNK, LANES), BF16)
    tables = []
    for j in range(RET_PAIRS):
        sl = slice(LANES * j, LANES * (j + 1))
        vsl = slice(2 * LANES * j, 2 * LANES * (j + 1))
        lgq = lgfq_ref[:, sl]
        xi = _pair_tables(lgq, row + 1.0)
        zeta = _pair_tables(lgq, float(RET_CHUNK - 1) - row)
        dec = jnp.exp(lgfv_ref[:, vsl] * float(RET_CHUNK))
        masks = []
        for a in range(2):
            hs = slice(2 * LANES * j + LANES * a, 2 * LANES * j + LANES * (a + 1))
            mf = jnp.exp(lgfv_ref[:, hs] * jnp.maximum(rel, 0.0))
            mb = jnp.exp(lgbv_ref[:, hs] * jnp.maximum(-rel, 0.0))
            masks.append(jnp.where(rel > 0, mf, jnp.where(rel < 0, mb, 2.0)))
        tables.append((xi, zeta, dec, masks))
    for c in range(tm // RET_CHUNK):
        rs = slice(c * RET_CHUNK, (c + 1) * RET_CHUNK)
        for j in range(RET_PAIRS):
            sl = slice(LANES * j, LANES * (j + 1))
            vsl = slice(2 * LANES * j, 2 * LANES * (j + 1))
            xi, zeta, dec, masks = tables[j]
            q2 = q_ref[rs, sl]
            k2 = k_ref[rs, sl]
            v2 = v_ref[rs, vsl]
            bd = st_scr[j]
            qx = (q2.astype(F32) * xi).astype(BF16)
            ret = jnp.dot(qx, bd.astype(BF16), preferred_element_type=F32) + cb_ref[rs, vsl]
            for a in range(2):
                qa = jnp.where(lane < RET_DK if a == 0 else lane >= RET_DK, q2, zero_bf)
                s = lax.dot_general(qa, k2, (((1,), (1,)), ((), ())), preferred_element_type=F32)
                pa = (s * masks[a]).astype(BF16)
                hs_v = slice(LANES * a, LANES * (a + 1))
                r = ret[:, hs_v] + jnp.dot(pa, v2[:, hs_v], preferred_element_type=F32)
                ms = jnp.mean(r * r, axis=-1, keepdims=True)
                hcol = 2 * LANES * j + LANES * a
                y = (r * lax.rsqrt(ms + EPS) * rng_ref[:, hcol:hcol + LANES]) \
                    * sg_ref[rs, hcol:hcol + LANES].astype(F32)
                ycat_scr[rs, POOL_WIDTH + hcol:POOL_WIDTH + hcol + LANES] = y.astype(BF16)
            kz = (k2.astype(F32) * zeta).astype(BF16)
            upd = lax.dot_general(kz, v2, (((0,), (0,)), ((), ())), preferred_element_type=F32)
            st_scr[j] = dec * bd + bdm * upd
        yy = jnp.dot(ycat_scr[rs, :], wout_ref[...], preferred_element_type=F32)
        o_ref[rs, :] = x_ref[rs, :] + mod_ref[5:6, :] * yy

    @pl.when(t == nt - 1)
    def _():
        sfin_ref[...] = st_scr[...]


def _pool_constants(tm, seq):
    nt = seq // tm
    kinds = [0] if nt == 1 else [0, 1, nt - 1]
    mats = np.zeros((len(kinds), POOL_GROUPS, tm, tm + POOL_PAD), np.float32)
    invc = np.zeros((len(kinds), tm, POOL_WIDTH), np.float32)
    i = np.arange(tm)
    for vi, tile in enumerate(kinds):
        pos = tile * tm + i
        colpos = np.full(tm + POOL_PAD, -1)
        colpos[:tm] = pos
        colpos[tm:tm + POOL_HALO] = tile * tm - POOL_HALO + np.arange(POOL_HALO)
        colpos[tm + POOL_HALO:tm + 2 * POOL_HALO] = (tile + 1) * tm + np.arange(POOL_HALO)
        for gi, w in enumerate(POOL_WINDOWS):
            lo = w // 2
            hi = w - 1 - lo
            start = np.maximum(pos - lo, 0)
            end = np.minimum(pos + hi + 1, seq)
            mats[vi, gi] = (colpos[None, :] >= start[:, None]) & (colpos[None, :] < end[:, None])
            invc[vi, :, gi * POOL_CG:(gi + 1) * POOL_CG] = (1.0 / (end - start))[:, None]
    return jnp.asarray(mats, BF16), jnp.asarray(invc, F32), nt


def _ret_out(x, mod8, mod_row, p, q, k, v, sg, cb, wpool_bd, pscale, rng, w_out, lgf_q, lgf_v, lgb_v, s0):
    bsz, seq, d = x.shape
    tm = min(MIX_TM, seq)
    pm, invc, nt = _pool_constants(tm, seq)
    hb = tm // POOL_HALO
    nhb = seq // POOL_HALO

    def variant(t):
        if nt == 1:
            return 0
        return jnp.where(t == 0, 0, jnp.where(t == nt - 1, 2, 1))

    cur = lambda b, t: (b, t, 0)
    st_shape = (RET_PAIRS, 2 * RET_DK, 2 * RET_DV)
    tokspec = lambda w: pl.BlockSpec((None, tm, w), cur)
    return pl.pallas_call(
        functools.partial(_retout_kernel, tm=tm, nt=nt),
        grid=(bsz, nt),
        in_specs=[
            tokspec(d),
            pl.BlockSpec((None, N_MOD, d), lambda b, t: (mod_row(b), 0, 0)),
            pl.BlockSpec((None, POOL_HALO, POOL_WIDTH), lambda b, t: (b, jnp.maximum(t * hb - 1, 0), 0)),
            tokspec(POOL_WIDTH),
            pl.BlockSpec((None, POOL_HALO, POOL_WIDTH),
                         lambda b, t: (b, jnp.minimum((t + 1) * hb, nhb - 1), 0)),
            pl.BlockSpec((None, POOL_GROUPS, tm, tm + POOL_PAD), lambda b, t: (variant(t), 0, 0, 0)),
            pl.BlockSpec((None, tm, POOL_WIDTH), lambda b, t: (variant(t), 0, 0)),
            tokspec(RET_QK_W), tokspec(RET_QK_W), tokspec(RET_V_W), tokspec(RET_V_W), tokspec(RET_V_W),
            _resident(wpool_bd.shape),
            pl.BlockSpec((1, POOL_WIDTH), lambda b, t: (0, 0)),
            pl.BlockSpec((1, RET_V_W), lambda b, t: (0, 0)),
            _resident(w_out.shape),
            pl.BlockSpec((1, RET_QK_W), lambda b, t: (0, 0)),
            pl.BlockSpec((1, RET_V_W), lambda b, t: (0, 0)),
            pl.BlockSpec((1, RET_V_W), lambda b, t: (0, 0)),
            pl.BlockSpec((None,) + st_shape, lambda b, t: (b, 0, 0, 0)),
        ],
        out_specs=[
            tokspec(d),
            pl.BlockSpec((None,) + st_shape, lambda b, t: (b, 0, 0, 0)),
        ],
        out_shape=[jax.ShapeDtypeStruct(x.shape, F32),
                   jax.ShapeDtypeStruct((bsz,) + st_shape, F32)],
        scratch_shapes=[pltpu.VMEM(st_shape, F32),
                        pltpu.VMEM((tm, d), BF16),
                        pltpu.VMEM((tm + POOL_PAD, POOL_WIDTH), F32)],
        compiler_params=_cparams(("arbitrary", "arbitrary")),
        name="ret_out",
    )(x, mod8, p, p, p, pm, invc, q, k, v, sg, cb, wpool_bd, pscale, rng, w_out,
      lgf_q, lgf_v, lgb_v, s0)


def _head_norm_rope(main, aux, g_main, g_aux):
    ms = jnp.sum(main * main, axis=-1, keepdims=True) * (1.0 / MLA_QK)
    r = lax.rsqrt(ms + EPS)
    return (main * r) * g_main + (aux * r) * g_aux


def _mlaproj_kernel(x_ref, mod_ref, g_ref, win_ref, qag_ref, kvag_ref, wqb_ref, wkv_ref,
                    qg_ref, qgs_ref, kg_ref, kgs_ref, qa_t_ref, qb_t_ref, ka_t_ref, kb_t_ref,
                    q_ref, k_ref, v_ref, qan_scr, kvn_scr, kr_scr, gt_scr):
    h = _rms_mod(x_ref[...], g_ref[...], mod_ref[3:4, :], mod_ref[4:5, :])
    proj = jnp.dot(h.astype(BF16), win_ref[...], preferred_element_type=F32)
    qa = proj[:, :MLA_Q_RANK]
    kva = proj[:, MLA_Q_RANK:MLA_Q_RANK + MLA_KV_RANK]
    r0 = MLA_Q_RANK + MLA_KV_RANK
    kr_scr[0] = proj[:, r0:r0 + HEAD_SLAB]
    kr_scr[1] = proj[:, r0 + HEAD_SLAB:r0 + 2 * HEAD_SLAB]
    qan = qa * lax.rsqrt(jnp.mean(qa * qa, axis=-1, keepdims=True) + EPS) * qag_ref[...]
    kvn = kva * lax.rsqrt(jnp.mean(kva * kva, axis=-1, keepdims=True) + EPS) * kvag_ref[...]
    qan_scr[...] = qan.astype(BF16)
    kvn_scr[...] = kvn.astype(BF16)
    gt_scr[0] = qg_ref[...] * qa_t_ref[...]
    gt_scr[1] = qgs_ref[...] * qb_t_ref[...]
    gt_scr[2] = kg_ref[...] * ka_t_ref[...]
    gt_scr[3] = kgs_ref[...] * kb_t_ref[...]
    tm = proj.shape[0]
    ones_rows = jnp.where(lax.broadcasted_iota(jnp.int32, (V_ROWS - MLA_V, tm), 0) == 0,
                          1.0, 0.0).astype(BF16)

    def head(hd, carry):
        c0 = pl.multiple_of(hd * (2 * HEAD_SLAB), 2 * HEAD_SLAB)
        qh = jnp.dot(qan_scr[...], wqb_ref[:, pl.ds(c0, 2 * HEAD_SLAB)], preferred_element_type=F32)
        kvh = jnp.dot(kvn_scr[...], wkv_ref[:, pl.ds(c0, 2 * HEAD_SLAB)], preferred_element_type=F32)
        q_ref[hd] = _head_norm_rope(qh[:, :HEAD_SLAB], qh[:, HEAD_SLAB:], gt_scr[0], gt_scr[1]).T.astype(BF16)
        k_ref[hd] = _head_norm_rope(kvh[:, :HEAD_SLAB] + kr_scr[0], kr_scr[1],
                                    gt_scr[2], gt_scr[3]).astype(BF16)
        v_ref[hd, 0:MLA_V, :] = kvh[:, HEAD_SLAB:].T.astype(BF16)
        v_ref[hd, MLA_V:V_ROWS, :] = ones_rows
        return carry
    lax.fori_loop(0, MLA_HEADS, head, 0, unroll=MLA_HEAD_UNROLL)


def _mla_project(x, mod8, mod_row, g, w_in, qa_g, kva_g, w_qb, w_kv, qn_g, qn_gs, kn_g, kn_gs,
                 q_ta, q_tb, k_ta, k_tb):
    bsz, seq, d = x.shape
    tm = min(MLA_TM, seq)
    head_out = jax.ShapeDtypeStruct((bsz, MLA_HEADS, seq, HEAD_SLAB), BF16)
    hspec = pl.BlockSpec((None, MLA_HEADS, tm, HEAD_SLAB), lambda b, t: (b, 0, t, 0))
    tspec = pl.BlockSpec((tm, HEAD_SLAB), lambda b, t: (t, 0))
    row = lambda n: pl.BlockSpec((1, n), lambda b, t: (0, 0))
    return pl.pallas_call(
        _mlaproj_kernel,
        grid=(bsz, seq // tm),
        in_specs=[
            pl.BlockSpec((None, tm, d), lambda b, t: (b, t, 0)),
            pl.BlockSpec((None, N_MOD, d), lambda b, t: (mod_row(b), 0, 0)),
            row(d),
            _resident(w_in.shape), row(MLA_Q_RANK), row(MLA_KV_RANK),
            _resident(w_qb.shape), _resident(w_kv.shape),
            row(HEAD_SLAB), row(HEAD_SLAB), row(HEAD_SLAB), row(HEAD_SLAB),
            tspec, tspec, tspec, tspec,
        ],
        out_specs=[pl.BlockSpec((None, MLA_HEADS, HEAD_SLAB, tm), lambda b, t: (b, 0, 0, t)),
                   hspec,
                   pl.BlockSpec((None, MLA_HEADS, V_ROWS, tm), lambda b, t: (b, 0, 0, t))],
        out_shape=[jax.ShapeDtypeStruct((bsz, MLA_HEADS, HEAD_SLAB, seq), BF16),
                   head_out,
                   jax.ShapeDtypeStruct((bsz, MLA_HEADS, V_ROWS, seq), BF16)],
        scratch_shapes=[pltpu.VMEM((tm, MLA_Q_RANK), BF16), pltpu.VMEM((tm, MLA_KV_RANK), BF16),
                        pltpu.VMEM((2, tm, HEAD_SLAB), F32), pltpu.VMEM((4, tm, HEAD_SLAB), F32)],
        compiler_params=_cparams(("parallel", "parallel")),
        name="mla_project",
    )(x, mod8, g.reshape(1, d), w_in, qa_g, kva_g, w_qb, w_kv, qn_g, qn_gs, kn_g, kn_gs,
      q_ta, q_tb, k_ta, k_tb)


def _attn_kernel(*refs, n_lat_blocks, tk, group_blocks):
    if n_lat_blocks:
        q_ref, kc_ref, vct_ref, k_ref, vt_ref, o_ref, m_scr, acc_scr, s_scr, mb_scr = refs
    else:
        q_ref, kc_ref, vct_ref, o_ref, m_scr, acc_scr = refs
    q = q_ref[...]

    def scores(kb):
        return jnp.dot(kb, q, preferred_element_type=F32)

    def colmax(st):
        return jnp.max(st, axis=0, keepdims=True)

    def produce(slot, kb):
        st = scores(kb)
        s_scr[slot] = st
        mb_scr[slot] = colmax(st)

    def softmax_pv(st, m_blk, vtb, first):
        if first:
            m_new = m_blk
        else:
            m_prev = m_scr[...]
            m_new = jnp.maximum(m_prev, m_blk)
        p = jnp.exp2(st - m_new).astype(BF16)
        pv = jnp.dot(vtb, p, preferred_element_type=F32)
        if first:
            acc_scr[...] = pv
        else:
            acc_scr[...] = jnp.exp2(m_prev - m_new) * acc_scr[...] + pv
        m_scr[...] = m_new

    if n_lat_blocks:
        kblk = lambda o: k_ref[pl.ds(o, tk), :]
        vblk = lambda o: vt_ref[:, pl.ds(o, tk)]
        produce(0, k_ref[0:tk, :])
        st_c = scores(kc_ref[...])
        softmax_pv(st_c, colmax(st_c), vct_ref[...], True)

        def group(i, carry):
            base = i * (group_blocks * tk)
            for u in range(group_blocks):
                cur = pl.multiple_of(base + u * tk, tk)
                nxt = pl.multiple_of(base + (u + 1) * tk, tk)
                produce((u + 1) % 2, kblk(nxt))
                softmax_pv(s_scr[u % 2], mb_scr[u % 2], vblk(cur), False)
            return carry
        lax.fori_loop(0, n_lat_blocks // group_blocks - 1, group, 0)
        e0 = n_lat_blocks - group_blocks
        for u in range(group_blocks):
            cur = (e0 + u) * tk
            if u + 1 < group_blocks:
                produce((u + 1) % 2, k_ref[cur + tk:cur + 2 * tk, :])
            softmax_pv(s_scr[u % 2], mb_scr[u % 2], vt_ref[:, cur:cur + tk], False)
    else:
        st_c = scores(kc_ref[...])
        softmax_pv(st_c, colmax(st_c), vct_ref[...], True)
    acc = acc_scr[...]
    out_t = acc[0:MLA_V, :] * (1.0 / acc[MLA_V:MLA_V + 1, :])
    o_ref[...] = out_t.T.astype(o_ref.dtype)


def _attention(q, kc, vct, k=None, vt=None):
    bsz, nh, hs, lq = q.shape
    lc = kc.shape[2]
    tq = min(ATT_TQ, lq)
    qspec = pl.BlockSpec((None, None, hs, tq), lambda b, h, i: (b, h, 0, i))
    kfull = lambda n: pl.BlockSpec((None, None, n, hs), lambda b, h, i: (b, h, 0, 0))
    vfull = lambda n: pl.BlockSpec((None, None, V_ROWS, n), lambda b, h, i: (b, h, 0, 0))
    in_specs = [qspec, kfull(lc), vfull(lc)]
    args = [q, kc, vct]
    n_lat = 0
    grp = 0
    tk = ATT_TK
    if k is not None:
        lk = k.shape[2]
        tk = min(ATT_TK, lk)
        n_lat = lk // tk
        grp = min(ATT_GROUP, n_lat)
        assert n_lat % grp == 0, "latent key blocks are consumed in whole groups"
        in_specs += [kfull(lk), vfull(lk)]
        args += [k, vt]
    scratch = [pltpu.VMEM((1, tq), F32), pltpu.VMEM((V_ROWS, tq), F32)]
    if n_lat:
        scratch += [pltpu.VMEM((2, tk, tq), F32), pltpu.VMEM((2, 1, tq), F32)]
    return pl.pallas_call(
        functools.partial(_attn_kernel, n_lat_blocks=n_lat, tk=tk, group_blocks=grp),
        grid=(bsz, nh, lq // tq),
        in_specs=in_specs,
        out_specs=pl.BlockSpec((None, tq, hs), lambda b, h, i: (b, i, h)),
        out_shape=jax.ShapeDtypeStruct((bsz, lq, nh * hs), BF16),
        scratch_shapes=scratch,
        compiler_params=_cparams(("parallel", "parallel", "arbitrary")),
        name="mla_attention",
    )(*args)


def _pair_swap_cols(w):
    w2 = w.reshape(w.shape[:-1] + (w.shape[-1] // 2, 2))
    return jnp.stack([-w2[..., 1], w2[..., 0]], axis=-1).reshape(w.shape)


def _rope_angles(seq, rot_dim):
    pos = jnp.arange(seq)
    row = (pos // GRID_W).astype(F32)
    col = (pos % GRID_W).astype(F32)
    n_freq = rot_dim // 4
    inv = ROPE_BASE ** (-jnp.arange(n_freq, dtype=F32) / n_freq)
    ang = jnp.concatenate([row[:, None] * inv, col[:, None] * inv], axis=-1)
    return jnp.repeat(jnp.cos(ang), 2, axis=-1), jnp.repeat(jnp.sin(ang), 2, axis=-1)


def _ret_tables(seq, ctx_len):
    cos, sin = _rope_angles(seq, RET_DK)
    cos2, sin2 = jnp.tile(cos, (1, 2)), jnp.tile(sin, (1, 2))
    return cos2, sin2, jnp.ones((ctx_len, LANES), F32), jnp.zeros((ctx_len, LANES), F32)


def _mla_tables(seq, ctx_len):
    cos, sin = _rope_angles(seq, MLA_ROPE)
    one = jnp.ones((seq, MLA_NOPE), F32)
    zero_tail = jnp.zeros((seq, HEAD_SLAB - MLA_QK), F32)
    zero_head = jnp.zeros((seq, MLA_NOPE), F32)
    ta = jnp.concatenate([one, cos, zero_tail], axis=-1)
    tb = jnp.concatenate([zero_head, sin, zero_tail], axis=-1)
    ident = jnp.concatenate([jnp.ones((ctx_len, MLA_QK), F32),
                             jnp.zeros((ctx_len, HEAD_SLAB - MLA_QK), F32)], axis=-1)
    return ta, tb, ident, jnp.zeros((ctx_len, HEAD_SLAB), F32)


def _rope_slab(rope_cols):
    lead = rope_cols.shape[:-1]
    return jnp.concatenate([jnp.zeros(lead + (MLA_NOPE,), F32), rope_cols,
                            jnp.zeros(lead + (HEAD_SLAB - MLA_QK,), F32)], axis=-1)


def _slab_gains(gvec):
    rope = gvec[MLA_NOPE:].reshape(MLA_ROPE // 2, 2)
    swapped = jnp.stack([rope[:, 1], rope[:, 0]], axis=-1).reshape(MLA_ROPE)
    main = jnp.concatenate([gvec, jnp.zeros((HEAD_SLAB - MLA_QK,), F32)])
    return main.reshape(1, HEAD_SLAB), _rope_slab(swapped).reshape(1, HEAD_SLAB)


def _prep_even(ab_w_in, pool_w, ab_w_out):
    wq = ab_w_in[:, POOL_WIDTH:POOL_WIDTH + RET_QK_W]
    wk = ab_w_in[:, POOL_WIDTH + RET_QK_W:POOL_WIDTH + 2 * RET_QK_W]
    w_ext = jnp.concatenate([ab_w_in, _pair_swap_cols(wq), _pair_swap_cols(wk)], axis=-1).astype(BF16)
    wpool_bd = jnp.zeros((POOL_WIDTH, POOL_WIDTH), F32)
    for gi in range(POOL_GROUPS):
        s = slice(gi * POOL_CG, (gi + 1) * POOL_CG)
        wpool_bd = wpool_bd.at[s, s].set(pool_w[gi])
    return w_ext, wpool_bd.astype(BF16), ab_w_out.astype(BF16)


def _prep_odd(w_in, w_qb, w_kvb, w_out):
    wkr = w_in[:, MLA_Q_RANK + MLA_KV_RANK:]
    w_in_ext = jnp.concatenate([w_in[:, :MLA_Q_RANK + MLA_KV_RANK], _rope_slab(wkr),
                                _rope_slab(_pair_swap_cols(wkr))], axis=-1).astype(BF16)
    wq3 = w_qb.reshape(MLA_Q_RANK, MLA_HEADS, MLA_QK)
    wq_main = jnp.concatenate([wq3, jnp.zeros((MLA_Q_RANK, MLA_HEADS, HEAD_SLAB - MLA_QK), F32)], axis=-1)
    wq_aux = _rope_slab(_pair_swap_cols(wq3[..., MLA_NOPE:]))
    w_qb_ext = jnp.concatenate([wq_main, wq_aux], axis=-1).reshape(
        MLA_Q_RANK, MLA_HEADS * 2 * HEAD_SLAB).astype(BF16)
    wkv3 = w_kvb.reshape(MLA_KV_RANK, MLA_HEADS, MLA_NOPE + MLA_V)
    wk_slab = jnp.concatenate([wkv3[..., :MLA_NOPE],
                               jnp.zeros((MLA_KV_RANK, MLA_HEADS, HEAD_SLAB - MLA_NOPE), F32)], axis=-1)
    w_kv_ext = jnp.concatenate([wk_slab, wkv3[..., MLA_NOPE:]], axis=-1).reshape(
        MLA_KV_RANK, MLA_HEADS * 2 * HEAD_SLAB).astype(BF16)
    return w_in_ext, w_qb_ext, w_kv_ext, w_out.astype(BF16)


def kernel(x, c, ctx, c_ctx, ada_w, ada_b, norm_g, ffn_w_in, ffn_w_out, ab_w_in, pool_w, pool_scale,
           ret_log_decay, ret_norm_g, ab_w_out, mla_w_in, mla_qa_g, mla_kva_g, mla_w_qb, mla_w_kvb,
           mla_qn_g, mla_kn_g, mla_w_out):
    bsz, seq, d = x.shape
    ctx_len = ctx.shape[1]
    depth = ada_w.shape[0]
    ctx_row = bsz
    cond8 = jnp.zeros((8, d), F32).at[:bsz].set(c).at[ctx_row].set(c_ctx)
    mod_all = _ada_modulation(cond8, ada_w, ada_b).reshape(depth, 8, N_MOD, d)
    lat_row = lambda b: b
    ctx_mod_row = lambda b: ctx_row

    ret_cos, ret_sin, ret_cos_c, ret_sin_c = _ret_tables(seq, ctx_len)
    m_ta, m_tb, m_ta_c, m_tb_c = _mla_tables(seq, ctx_len)
    q_scale = MLA_QK ** -0.5 * LOG2_E
    zero_state = jnp.zeros((bsz, RET_PAIRS, 2 * RET_DK, 2 * RET_DV), F32)

    xc = ctx
    for i in range(depth):
        last = i == depth - 1
        j = i // 2
        mod8 = mod_all[i]
        w1_in, w1_out = ffn_w_in[i, 0].astype(BF16), ffn_w_out[i, 0].astype(BF16)
        w2_in, w2_out = ffn_w_in[i, 1].astype(BF16), ffn_w_out[i, 1].astype(BF16)
        x = _ffn_half(x, mod8, lat_row, 0, norm_g[i, 0], w1_in, w1_out)
        xc = _ffn_half(xc, mod8, ctx_mod_row, 0, norm_g[i, 0], w1_in, w1_out)
        mix = mix_c = w_mix = None
        if i % 2 == 0:
            w_ext, wpool_bd, w_out = _prep_even(ab_w_in[j], pool_w[j], ab_w_out[j])
            lg = ret_log_decay[j]
            lgf_q = jnp.repeat(lg[0], RET_DK).reshape(1, RET_QK_W)
            lgb_q = jnp.repeat(lg[1], RET_DK).reshape(1, RET_QK_W)
            lgf_v = jnp.repeat(lg[0], RET_DV).reshape(1, RET_V_W)
            lgb_v = jnp.repeat(lg[1], RET_DV).reshape(1, RET_V_W)
            pscale = pool_scale[j].reshape(1, POOL_WIDTH)
            rng = ret_norm_g[j].reshape(1, RET_V_W)
            pc, qc, kc, vc, sgc, cbc, state_b = _ab_project(
                xc, mod8, ctx_mod_row, norm_g[i, 1], w_ext, ret_cos_c, ret_sin_c, lgb_q, lgb_v, zero_state)
            xc_mixed, state_f = _ret_out(xc, mod8, ctx_mod_row, pc, qc, kc, vc, sgc, cbc, wpool_bd, pscale,
                                         rng, w_out, lgf_q, lgf_v, lgb_v, zero_state)
            p, q, k, v, sg, cb, _ = _ab_project(
                x, mod8, lat_row, norm_g[i, 1], w_ext, ret_cos, ret_sin, lgb_q, lgb_v, state_b)
            x, _ = _ret_out(x, mod8, lat_row, p, q, k, v, sg, cb, wpool_bd, pscale, rng, w_out,
                            lgf_q, lgf_v, lgb_v, state_f)
        else:
            w_in_ext, w_qb_ext, w_kv_ext, w_out = _prep_odd(mla_w_in[j], mla_w_qb[j], mla_w_kvb[j],
                                                            mla_w_out[j])
            qa_g = mla_qa_g[j].reshape(1, MLA_Q_RANK)
            kva_g = mla_kva_g[j].reshape(1, MLA_KV_RANK)
            qn_g, qn_gs = _slab_gains(mla_qn_g[j])
            kn_g, kn_gs = _slab_gains(mla_kn_g[j])
            qc, kc, vc = _mla_project(xc, mod8, ctx_mod_row, norm_g[i, 1], w_in_ext, qa_g, kva_g, w_qb_ext,
                                      w_kv_ext, qn_g, qn_gs, kn_g, kn_gs,
                                      m_ta_c * q_scale, m_tb_c, m_ta_c, m_tb_c)
            q, k, v = _mla_project(x, mod8, lat_row, norm_g[i, 1], w_in_ext, qa_g, kva_g, w_qb_ext,
                                   w_kv_ext, qn_g, qn_gs, kn_g, kn_gs,
                                   m_ta * q_scale, m_tb * q_scale, m_ta, m_tb)
            mix, w_mix = _attention(q, kc, vc, k, v), w_out
            if not last:
                mix_c = _attention(qc, kc, vc)
                xc_mixed = xc
        x = _ffn_half(x, mod8, lat_row, 2, norm_g[i, 2], w2_in, w2_out, mix, w_mix)
        if not last:
            xc = _ffn_half(xc_mixed, mod8, ctx_mod_row, 2, norm_g[i, 2], w2_in, w2_out, mix_c, w_mix)
    return x
```

```python
import functools

import numpy as np
import jax
import jax.numpy as jnp
from jax import lax
from jax.experimental import pallas as pl
from jax.experimental.pallas import tpu as pltpu

F32 = jnp.float32
BF16 = jnp.bfloat16

D_MODEL = 1024
GRID_W = 64
D_FF = 2816
FFN_RESIDUAL = 0.5
N_MOD = 9
ROPE_BASE = 10000.0
EPS = 1e-6
POOL_GROUPS = 4
POOL_CG = 64
POOL_WIDTH = POOL_GROUPS * POOL_CG
POOL_WINDOWS = (2, 4, 8, 16)
POOL_HALO = 8
POOL_PAD = 128
RET_HEADS = 6
RET_PAIRS = RET_HEADS // 2
RET_DK = 64
RET_DV = 128
RET_CHUNK = 128
RET_QK_W = RET_HEADS * RET_DK
RET_V_W = RET_HEADS * RET_DV
AB_IN = POOL_WIDTH + 2 * RET_QK_W + 2 * RET_V_W
AB_EXT = AB_IN + 2 * RET_QK_W
MLA_HEADS = 8
MLA_Q_RANK = 384
MLA_KV_RANK = 256
MLA_NOPE = 64
MLA_ROPE = 32
MLA_V = 128
MLA_QK = MLA_NOPE + MLA_ROPE
HEAD_SLAB = 128
V_ROWS = MLA_V + 16
LOG2_E = 1.4426950408889634

LANES = 128
V7X_VMEM_BYTES = 64 * 1024 * 1024
VMEM_LIMIT_BYTES = 56 * 1024 * 1024

ADA_TN = 1536
FFN_TM = 512
FFN_SPLIT = 2
MIX_TM = 256
AB_TM = 512
AB_SUB = 256
MLA_TM = 256
MLA_HEAD_UNROLL = 8
ATT_TQ = 1024
ATT_TK = 512
ATT_GROUP = 4


def _cparams(sem):
    return pltpu.CompilerParams(dimension_semantics=sem, vmem_limit_bytes=VMEM_LIMIT_BYTES)


def _resident(shape):
    nd = len(shape)
    return pl.BlockSpec(shape, lambda *_: (0,) * nd, pipeline_mode=pl.Buffered(1))


def _silu(x):
    return x * (1.0 / (1.0 + jnp.exp(-x)))


def _rms_mod(x, g, shift, scale):
    ms = jnp.mean(x * x, axis=-1, keepdims=True)
    return (x * lax.rsqrt(ms + EPS) * g) * (1.0 + scale) + shift


def _ada_kernel(c_ref, w_ref, b_ref, o_ref):
    s = _silu(c_ref[...])
    o_ref[...] = jnp.dot(s, w_ref[...], precision=lax.Precision.HIGHEST,
                         preferred_element_type=F32) + b_ref[...]


def _ada_modulation(cond8, ada_w, ada_b):
    depth, d, n = ada_w.shape
    return pl.pallas_call(
        _ada_kernel,
        grid=(depth, n // ADA_TN),
        in_specs=[
            pl.BlockSpec((8, d), lambda i, j: (0, 0)),
            pl.BlockSpec((None, d, ADA_TN), lambda i, j: (i, 0, j)),
            pl.BlockSpec((None, 1, ADA_TN), lambda i, j: (i, 0, j)),
        ],
        out_specs=pl.BlockSpec((None, 8, ADA_TN), lambda i, j: (i, 0, j)),
        out_shape=jax.ShapeDtypeStruct((depth, 8, n), F32),
        compiler_params=_cparams(("parallel", "parallel")),
        name="ada_mod",
    )(cond8, ada_w, ada_b.reshape(depth, 1, n))


def _ffn_kernel(*refs, k, mixed):
    if mixed:
        x_ref, mod_ref, g_ref, win_ref, wout_ref, mix_ref, wmix_ref, o_ref = refs
    else:
        x_ref, mod_ref, g_ref, win_ref, wout_ref, o_ref = refs
    tm = x_ref.shape[0]
    sub = tm // FFN_SPLIT
    for s in range(FFN_SPLIT):
        rs = slice(s * sub, (s + 1) * sub)
        x = x_ref[rs, :]
        if mixed:
            x = x + mod_ref[5:6, :] * jnp.dot(mix_ref[rs, :], wmix_ref[...], preferred_element_type=F32)
        h = _rms_mod(x, g_ref[...], mod_ref[3 * k:3 * k + 1, :], mod_ref[3 * k + 1:3 * k + 2, :])
        ab = jnp.dot(h.astype(BF16), win_ref[...], preferred_element_type=F32)
        act = (_silu(ab[:, :D_FF]) * ab[:, D_FF:]).astype(BF16)
        y = jnp.dot(act, wout_ref[...], preferred_element_type=F32)
        o_ref[rs, :] = x + (FFN_RESIDUAL * mod_ref[3 * k + 2:3 * k + 3, :]) * y


def _ffn_half(x, mod8, mod_row, k, g, w_in, w_out, mix=None, w_mix=None):
    bsz, seq, d = x.shape
    tm = min(FFN_TM, seq)
    in_specs = [
        pl.BlockSpec((None, tm, d), lambda b, t: (b, t, 0)),
        pl.BlockSpec((None, N_MOD, d), lambda b, t: (mod_row(b), 0, 0)),
        pl.BlockSpec((1, d), lambda b, t: (0, 0)),
        _resident(w_in.shape),
        _resident(w_out.shape),
    ]
    args = [x, mod8, g.reshape(1, d), w_in, w_out]
    if mix is not None:
        in_specs += [pl.BlockSpec((None, tm, mix.shape[-1]), lambda b, t: (b, t, 0)),
                     _resident(w_mix.shape)]
        args += [mix, w_mix]
    return pl.pallas_call(
        functools.partial(_ffn_kernel, k=k, mixed=mix is not None),
        grid=(bsz, seq // tm),
        in_specs=in_specs,
        out_specs=pl.BlockSpec((None, tm, d), lambda b, t: (b, t, 0)),
        out_shape=jax.ShapeDtypeStruct(x.shape, F32),
        compiler_params=_cparams(("parallel", "parallel")),
        name="ffn_half",
    )(*args)


def _pair_tables(lg_row, expo):
    return jnp.exp(lg_row * expo)


def _bd_mask():
    r = lax.broadcasted_iota(jnp.int32, (2 * RET_DK, 2 * RET_DV), 0)
    c = lax.broadcasted_iota(jnp.int32, (2 * RET_DK, 2 * RET_DV), 1)
    same_head = jnp.where(r < RET_DK, 0, 1) == jnp.where(c < RET_DV, 0, 1)
    return jnp.where(same_head, 1.0, 0.0)


def _abproj_kernel(x_ref, mod_ref, g_ref, w_ref, cos_ref, sin_ref, lgq_ref, lgv_ref, s0_ref,
                   p_ref, q_ref, k_ref, v_ref, sg_ref, cb_ref, sfin_ref, st_scr, *, tm, nt):
    t = pl.program_id(1)

    @pl.when(t == 0)
    def _():
        st_scr[...] = s0_ref[...]

    q0, k0 = POOL_WIDTH, POOL_WIDTH + RET_QK_W
    v0, g0 = k0 + RET_QK_W, k0 + RET_QK_W + RET_V_W
    qs0, ks0 = AB_IN, AB_IN + RET_QK_W
    row = lax.broadcasted_iota(jnp.int32, (RET_CHUNK, LANES), 0).astype(F32)
    bdm = _bd_mask()
    k_scale = RET_DK ** -0.5
    sub = min(AB_SUB, tm)
    for s in reversed(range(tm // sub)):
        r0 = s * sub
        ts = slice(r0, r0 + sub)
        h = _rms_mod(x_ref[ts, :], g_ref[...], mod_ref[3:4, :], mod_ref[4:5, :])
        proj = jnp.dot(h.astype(BF16), w_ref[...], preferred_element_type=F32)
        p_ref[ts, :] = proj[:, :POOL_WIDTH]
        v_ref[ts, :] = proj[:, v0:v0 + RET_V_W].astype(BF16)
        sg_ref[ts, :] = _silu(proj[:, g0:g0 + RET_V_W]).astype(BF16)
        cos = cos_ref[ts, :]
        sin = sin_ref[ts, :]
        for j in range(RET_PAIRS):
            sl = slice(LANES * j, LANES * (j + 1))
            qr = proj[:, q0 + LANES * j:q0 + LANES * (j + 1)] * cos \
                + proj[:, qs0 + LANES * j:qs0 + LANES * (j + 1)] * sin
            kr = (proj[:, k0 + LANES * j:k0 + LANES * (j + 1)] * cos
                  + proj[:, ks0 + LANES * j:ks0 + LANES * (j + 1)] * sin) * k_scale
            q_ref[ts, sl] = qr.astype(BF16)
            k_ref[ts, sl] = kr.astype(BF16)
            lgq = lgq_ref[:, sl]
            lgv = lgv_ref[:, 2 * LANES * j:2 * LANES * (j + 1)]
            xi = _pair_tables(lgq, float(RET_CHUNK) - row)
            zeta = _pair_tables(lgq, row)
            dec = jnp.exp(lgv * float(RET_CHUNK))
            vsl = slice(2 * LANES * j, 2 * LANES * (j + 1))
            for c in reversed(range(sub // RET_CHUNK)):
                ls = slice(c * RET_CHUNK, (c + 1) * RET_CHUNK)
                rs = slice(r0 + c * RET_CHUNK, r0 + (c + 1) * RET_CHUNK)
                bd = st_scr[j]
                qx = (qr[ls] * xi).astype(BF16)
                cb_ref[rs, vsl] = jnp.dot(qx, bd.astype(BF16), preferred_element_type=F32)
                kz = (kr[ls] * zeta).astype(BF16)
                upd = lax.dot_general(kz, v_ref[rs, vsl], (((0,), (0,)), ((), ())),
                                      preferred_element_type=F32)
                st_scr[j] = dec * bd + bdm * upd

    @pl.when(t == nt - 1)
    def _():
        sfin_ref[...] = st_scr[...]


def _ab_project(x, mod8, mod_row, g, w_ext, cos, sin, lg_q, lg_v, s0):
    bsz, seq, d = x.shape
    tm = min(AB_TM, seq)
    nt = seq // tm
    rev = lambda b, t: (b, nt - 1 - t, 0)
    tok = lambda w, dt: jax.ShapeDtypeStruct((bsz, seq, w), dt)
    st_shape = (RET_PAIRS, 2 * RET_DK, 2 * RET_DV)
    return pl.pallas_call(
        functools.partial(_abproj_kernel, tm=tm, nt=nt),
        grid=(bsz, nt),
        in_specs=[
            pl.BlockSpec((None, tm, d), rev),
            pl.BlockSpec((None, N_MOD, d), lambda b, t: (mod_row(b), 0, 0)),
            pl.BlockSpec((1, d), lambda b, t: (0, 0)),
            _resident(w_ext.shape),
            pl.BlockSpec((tm, LANES), lambda b, t: (nt - 1 - t, 0)),
            pl.BlockSpec((tm, LANES), lambda b, t: (nt - 1 - t, 0)),
            pl.BlockSpec((1, RET_QK_W), lambda b, t: (0, 0)),
            pl.BlockSpec((1, RET_V_W), lambda b, t: (0, 0)),
            pl.BlockSpec((None,) + st_shape, lambda b, t: (b, 0, 0, 0)),
        ],
        out_specs=[
            pl.BlockSpec((None, tm, POOL_WIDTH), rev),
            pl.BlockSpec((None, tm, RET_QK_W), rev),
            pl.BlockSpec((None, tm, RET_QK_W), rev),
            pl.BlockSpec((None, tm, RET_V_W), rev),
            pl.BlockSpec((None, tm, RET_V_W), rev),
            pl.BlockSpec((None, tm, RET_V_W), rev),
            pl.BlockSpec((None,) + st_shape, lambda b, t: (b, 0, 0, 0)),
        ],
        out_shape=[tok(POOL_WIDTH, F32), tok(RET_QK_W, BF16), tok(RET_QK_W, BF16),
                   tok(RET_V_W, BF16), tok(RET_V_W, BF16), tok(RET_V_W, F32),
                   jax.ShapeDtypeStruct((bsz,) + st_shape, F32)],
        scratch_shapes=[pltpu.VMEM(st_shape, F32)],
        compiler_params=_cparams(("arbitrary", "arbitrary")),
        name="ab_project",
    )(x, mod8, g.reshape(1, d), w_ext, cos, sin, lg_q, lg_v, s0)


def _retout_kernel(x_ref, mod_ref, pprev_ref, pcur_ref, pnext_ref, pm_ref, invc_ref,
                   q_ref, k_ref, v_ref, sg_ref, cb_ref, wpool_ref, pscale_ref, rng_ref, wout_ref,
                   lgfq_ref, lgfv_ref, lgbv_ref, s0_ref,
                   o_ref, sfin_ref, st_scr, ycat_scr, pe_scr, *, tm, nt):
    t = pl.program_id(1)

    @pl.when(t == 0)
    def _():
        st_scr[...] = s0_ref[...]

    pcur = pcur_ref[...]
    pe_scr[0:tm, :] = pcur
    pe_scr[tm:tm + POOL_HALO, :] = pprev_ref[...]
    pe_scr[tm + POOL_HALO:tm + 2 * POOL_HALO, :] = pnext_ref[...]
    pe_scr[tm + 2 * POOL_HALO:, :] = jnp.zeros((POOL_PAD - 2 * POOL_HALO, POOL_WIDTH), F32)
    pe = pe_scr[...]
    pe_hi = pe.astype(BF16)
    pe_lo = (pe - pe_hi.astype(F32)).astype(BF16)
    lane_grp = jnp.right_shift(lax.broadcasted_iota(jnp.int32, (tm, POOL_WIDTH), 1),
                               POOL_CG.bit_length() - 1)
    pooled = jnp.zeros((tm, POOL_WIDTH), F32)
    for gi in range(POOL_GROUPS):
        win = pm_ref[gi]
        tot = jnp.dot(win, pe_hi, preferred_element_type=F32) \
            + jnp.dot(win, pe_lo, preferred_element_type=F32)
        pooled = jnp.where(lane_grp == gi, tot, pooled)
    diffs = pooled * invc_ref[...] - pcur
    pool_y = jnp.dot(diffs.astype(BF16), wpool_ref[...], preferred_element_type=F32) * pscale_ref[...]
    ycat_scr[:, 0:POOL_WIDTH] = pool_y.astype(BF16)

    ri = lax.broadcasted_iota(jnp.int32, (RET_CHUNK, RET_CHUNK), 0)
    ci = lax.broadcasted_iota(jnp.int32, (RET_CHUNK, RET_CHUNK), 1)
    rel = (ri - ci).astype(F32)
    row = lax.broadcasted_iota(jnp.int32, (RET_CHUNK, LANES), 0).astype(F32)
    lane = lax.broadcasted_iota(jnp.int32, (RET_CHUNK, LANES), 1)
    bdm = _bd_mask()
    zero_bf = jnp.zeros((RET_CHUNK, LANES), BF16)
    tables = []
    for j in range(RET_PAIRS):
        sl = slice(LANES * j, LANES * (j + 1))
        vsl = slice(2 * LANES * j, 2 * LANES * (j + 1))
        lgq = lgfq_ref[:, sl]
        xi = _pair_tables(lgq, row + 1.0)
        zeta = _pair_tables(lgq, float(RET_CHUNK - 1) - row)
        dec = jnp.exp(lgfv_ref[:, vsl] * float(RET_CHUNK))
        masks = []
        for a in range(2):
            hs = slice(2 * LANES * j + LANES * a, 2 * LANES * j + LANES * (a + 1))
            mf = jnp.exp(lgfv_ref[:, hs] * jnp.maximum(rel, 0.0))
            mb = jnp.exp(lgbv_ref[:, hs] * jnp.maximum(-rel, 0.0))
            masks.append(jnp.where(rel > 0, mf, jnp.where(rel < 0, mb, 2.0)))
        tables.append((xi, zeta, dec, masks))
    for c in range(tm // RET_CHUNK):
        rs = slice(c * RET_CHUNK, (c + 1) * RET_CHUNK)
        for j in range(RET_PAIRS):
            sl = slice(LANES * j, LANES * (j + 1))
            vsl = slice(2 * LANES * j, 2 * LANES * (j + 1))
            xi, zeta, dec, masks = tables[j]
            q2 = q_ref[rs, sl]
            k2 = k_ref[rs, sl]
            v2 = v_ref[rs, vsl]
            bd = st_scr[j]
            qx = (q2.astype(F32) * xi).astype(BF16)
            ret = jnp.dot(qx, bd.astype(BF16), preferred_element_type=F32) + cb_ref[rs, vsl]
            for a in range(2):
                qa = jnp.where(lane < RET_DK if a == 0 else lane >= RET_DK, q2, zero_bf)
                s = lax.dot_general(qa, k2, (((1,), (1,)), ((), ())), preferred_element_type=F32)
                pa = (s * masks[a]).astype(BF16)
                hs_v = slice(LANES * a, LANES * (a + 1))
                r = ret[:, hs_v] + jnp.dot(pa, v2[:, hs_v], preferred_element_type=F32)
                ms = jnp.mean(r * r, axis=-1, keepdims=True)
                hcol = 2 * LANES * j + LANES * a
                y = (r * lax.rsqrt(ms + EPS) * rng_ref[:, hcol:hcol + LANES]) \
                    * sg_ref[rs, hcol:hcol + LANES].astype(F32)
                ycat_scr[rs, POOL_WIDTH + hcol:POOL_WIDTH + hcol + LANES] = y.astype(BF16)
            kz = (k2.astype(F32) * zeta).astype(BF16)
            upd = lax.dot_general(kz, v2, (((0,), (0,)), ((), ())), preferred_element_type=F32)
            st_scr[j] = dec * bd + bdm * upd
        yy = jnp.dot(ycat_scr[rs, :], wout_ref[...], preferred_element_type=F32)
        o_ref[rs, :] = x_ref[rs, :] + mod_ref[5:6, :] * yy

    @pl.when(t == nt - 1)
    def _():
        sfin_ref[...] = st_scr[...]


def _pool_constants(tm, seq):
    nt = seq // tm
    kinds = [0] if nt == 1 else [0, 1, nt - 1]
    mats = np.zeros((len(kinds), POOL_GROUPS, tm, tm + POOL_PAD), np.float32)
    invc = np.zeros((len(kinds), tm, POOL_WIDTH), np.float32)
    i = np.arange(tm)
    for vi, tile in enumerate(kinds):
        pos = tile * tm + i
        colpos = np.full(tm + POOL_PAD, -1)
        colpos[:tm] = pos
        colpos[tm:tm + POOL_HALO] = tile * tm - POOL_HALO + np.arange(POOL_HALO)
        colpos[tm + POOL_HALO:tm + 2 * POOL_HALO] = (tile + 1) * tm + np.arange(POOL_HALO)
        for gi, w in enumerate(POOL_WINDOWS):
            lo = w // 2
            hi = w - 1 - lo
            start = np.maximum(pos - lo, 0)
            end = np.minimum(pos + hi + 1, seq)
            mats[vi, gi] = (colpos[None, :] >= start[:, None]) & (colpos[None, :] < end[:, None])
            invc[vi, :, gi * POOL_CG:(gi + 1) * POOL_CG] = (1.0 / (end - start))[:, None]
    return jnp.asarray(mats, BF16), jnp.asarray(invc, F32), nt


def _ret_out(x, mod8, mod_row, p, q, k, v, sg, cb, wpool_bd, pscale, rng, w_out, lgf_q, lgf_v, lgb_v, s0):
    bsz, seq, d = x.shape
    tm = min(MIX_TM, seq)
    pm, invc, nt = _pool_constants(tm, seq)
    hb = tm // POOL_HALO
    nhb = seq // POOL_HALO

    def variant(t):
        if nt == 1:
            return 0
        return jnp.where(t == 0, 0, jnp.where(t == nt - 1, 2, 1))

    cur = lambda b, t: (b, t, 0)
    st_shape = (RET_PAIRS, 2 * RET_DK, 2 * RET_DV)
    tokspec = lambda w: pl.BlockSpec((None, tm, w), cur)
    return pl.pallas_call(
        functools.partial(_retout_kernel, tm=tm, nt=nt),
        grid=(bsz, nt),
        in_specs=[
            tokspec(d),
            pl.BlockSpec((None, N_MOD, d), lambda b, t: (mod_row(b), 0, 0)),
            pl.BlockSpec((None, POOL_HALO, POOL_WIDTH), lambda b, t: (b, jnp.maximum(t * hb - 1, 0), 0)),
            tokspec(POOL_WIDTH),
            pl.BlockSpec((None, POOL_HALO, POOL_WIDTH),
                         lambda b, t: (b, jnp.minimum((t + 1) * hb, nhb - 1), 0)),
            pl.BlockSpec((None, POOL_GROUPS, tm, tm + POOL_PAD), lambda b, t: (variant(t), 0, 0, 0)),
            pl.BlockSpec((None, tm, POOL_WIDTH), lambda b, t: (variant(t), 0, 0)),
            tokspec(RET_QK_W), tokspec(RET_QK_W), tokspec(RET_V_W), tokspec(RET_V_W), tokspec(RET_V_W),
            _resident(wpool_bd.shape),
            pl.BlockSpec((1, POOL_WIDTH), lambda b, t: (0, 0)),
            pl.BlockSpec((1, RET_V_W), lambda b, t: (0, 0)),
            _resident(w_out.shape),
            pl.BlockSpec((1, RET_QK_W), lambda b, t: (0, 0)),
            pl.BlockSpec((1, RET_V_W), lambda b, t: (0, 0)),
            pl.BlockSpec((1, RET_V_W), lambda b, t: (0, 0)),
            pl.BlockSpec((None,) + st_shape, lambda b, t: (b, 0, 0, 0)),
        ],
        out_specs=[
            tokspec(d),
            pl.BlockSpec((None,) + st_shape, lambda b, t: (b, 0, 0, 0)),
        ],
        out_shape=[jax.ShapeDtypeStruct(x.shape, F32),
                   jax.ShapeDtypeStruct((bsz,) + st_shape, F32)],
        scratch_shapes=[pltpu.VMEM(st_shape, F32),
                        pltpu.VMEM((tm, d), BF16),
                        pltpu.VMEM((tm + POOL_PAD, POOL_WIDTH), F32)],
        compiler_params=_cparams(("arbitrary", "arbitrary")),
        name="ret_out",
    )(x, mod8, p, p, p, pm, invc, q, k, v, sg, cb, wpool_bd, pscale, rng, w_out,
      lgf_q, lgf_v, lgb_v, s0)


def _head_norm_rope(main, aux, g_main, g_aux):
    ms = jnp.sum(main * main, axis=-1, keepdims=True) * (1.0 / MLA_QK)
    r = lax.rsqrt(ms + EPS)
    return (main * r) * g_main + (aux * r) * g_aux


def _mlaproj_kernel(x_ref, mod_ref, g_ref, win_ref, qag_ref, kvag_ref, wqb_ref, wkv_ref,
                    qg_ref, qgs_ref, kg_ref, kgs_ref, qa_t_ref, qb_t_ref, ka_t_ref, kb_t_ref,
                    q_ref, k_ref, v_ref, qan_scr, kvn_scr, kr_scr, gt_scr):
    h = _rms_mod(x_ref[...], g_ref[...], mod_ref[3:4, :], mod_ref[4:5, :])
    proj = jnp.dot(h.astype(BF16), win_ref[...], preferred_element_type=F32)
    qa = proj[:, :MLA_Q_RANK]
    kva = proj[:, MLA_Q_RANK:MLA_Q_RANK + MLA_KV_RANK]
    r0 = MLA_Q_RANK + MLA_KV_RANK
    kr_scr[0] = proj[:, r0:r0 + HEAD_SLAB]
    kr_scr[1] = proj[:, r0 + HEAD_SLAB:r0 + 2 * HEAD_SLAB]
    qan = qa * lax.rsqrt(jnp.mean(qa * qa, axis=-1, keepdims=True) + EPS) * qag_ref[...]
    kvn = kva * lax.rsqrt(jnp.mean(kva * kva, axis=-1, keepdims=True) + EPS) * kvag_ref[...]
    qan_scr[...] = qan.astype(BF16)
    kvn_scr[...] = kvn.astype(BF16)
    gt_scr[0] = qg_ref[...] * qa_t_ref[...]
    gt_scr[1] = qgs_ref[...] * qb_t_ref[...]
    gt_scr[2] = kg_ref[...] * ka_t_ref[...]
    gt_scr[3] = kgs_ref[...] * kb_t_ref[...]
    tm = proj.shape[0]
    ones_rows = jnp.where(lax.broadcasted_iota(jnp.int32, (V_ROWS - MLA_V, tm), 0) == 0,
                          1.0, 0.0).astype(BF16)

    def head(hd, carry):
        c0 = pl.multiple_of(hd * (2 * HEAD_SLAB), 2 * HEAD_SLAB)
        qh = jnp.dot(qan_scr[...], wqb_ref[:, pl.ds(c0, 2 * HEAD_SLAB)], preferred_element_type=F32)
        kvh = jnp.dot(kvn_scr[...], wkv_ref[:, pl.ds(c0, 2 * HEAD_SLAB)], preferred_element_type=F32)
        q_ref[hd] = _head_norm_rope(qh[:, :HEAD_SLAB], qh[:, HEAD_SLAB:], gt_scr[0], gt_scr[1]).T.astype(BF16)
        k_ref[hd] = _head_norm_rope(kvh[:, :HEAD_SLAB] + kr_scr[0], kr_scr[1],
                                    gt_scr[2], gt_scr[3]).astype(BF16)
        v_ref[hd, 0:MLA_V, :] = kvh[:, HEAD_SLAB:].T.astype(BF16)
        v_ref[hd, MLA_V:V_ROWS, :] = ones_rows
        return carry
    lax.fori_loop(0, MLA_HEADS, head, 0, unroll=MLA_HEAD_UNROLL)


def _mla_project(x, mod8, mod_row, g, w_in, qa_g, kva_g, w_qb, w_kv, qn_g, qn_gs, kn_g, kn_gs,
                 q_ta, q_tb, k_ta, k_tb):
    bsz, seq, d = x.shape
    tm = min(MLA_TM, seq)
    head_out = jax.ShapeDtypeStruct((bsz, MLA_HEADS, seq, HEAD_SLAB), BF16)
    hspec = pl.BlockSpec((None, MLA_HEADS, tm, HEAD_SLAB), lambda b, t: (b, 0, t, 0))
    tspec = pl.BlockSpec((tm, HEAD_SLAB), lambda b, t: (t, 0))
    row = lambda n: pl.BlockSpec((1, n), lambda b, t: (0, 0))
    return pl.pallas_call(
        _mlaproj_kernel,
        grid=(bsz, seq // tm),
        in_specs=[
            pl.BlockSpec((None, tm, d), lambda b, t: (b, t, 0)),
            pl.BlockSpec((None, N_MOD, d), lambda b, t: (mod_row(b), 0, 0)),
            row(d),
            _resident(w_in.shape), row(MLA_Q_RANK), row(MLA_KV_RANK),
            _resident(w_qb.shape), _resident(w_kv.shape),
            row(HEAD_SLAB), row(HEAD_SLAB), row(HEAD_SLAB), row(HEAD_SLAB),
            tspec, tspec, tspec, tspec,
        ],
        out_specs=[pl.BlockSpec((None, MLA_HEADS, HEAD_SLAB, tm), lambda b, t: (b, 0, 0, t)),
                   hspec,
                   pl.BlockSpec((None, MLA_HEADS, V_ROWS, tm), lambda b, t: (b, 0, 0, t))],
        out_shape=[jax.ShapeDtypeStruct((bsz, MLA_HEADS, HEAD_SLAB, seq), BF16),
                   head_out,
                   jax.ShapeDtypeStruct((bsz, MLA_HEADS, V_ROWS, seq), BF16)],
        scratch_shapes=[pltpu.VMEM((tm, MLA_Q_RANK), BF16), pltpu.VMEM((tm, MLA_KV_RANK), BF16),
                        pltpu.VMEM((2, tm, HEAD_SLAB), F32), pltpu.VMEM((4, tm, HEAD_SLAB), F32)],
        compiler_params=_cparams(("parallel", "parallel")),
        name="mla_project",
    )(x, mod8, g.reshape(1, d), w_in, qa_g, kva_g, w_qb, w_kv, qn_g, qn_gs, kn_g, kn_gs,
      q_ta, q_tb, k_ta, k_tb)


def _attn_kernel(*refs, n_lat_blocks, tk, group_blocks):
    if n_lat_blocks:
        q_ref, kc_ref, vct_ref, k_ref, vt_ref, o_ref, m_scr, acc_scr, s_scr, mb_scr = refs
    else:
        q_ref, kc_ref, vct_ref, o_ref, m_scr, acc_scr = refs
    q = q_ref[...]

    def scores(kb):
        return jnp.dot(kb, q, preferred_element_type=F32)

    def colmax(st):
        return jnp.max(st, axis=0, keepdims=True)

    def produce(slot, kb):
        st = scores(kb)
        s_scr[slot, 0:kb.shape[0], :] = st
        mb_scr[slot] = colmax(st)

    def softmax_pv(st, m_blk, vtb, first):
        if first:
            m_new = m_blk
        else:
            m_prev = m_scr[...]
            m_new = jnp.maximum(m_prev, m_blk)
        p = jnp.exp2(st - m_new).astype(BF16)
        pv = jnp.dot(vtb, p, preferred_element_type=F32)
        if first:
            acc_scr[...] = pv
        else:
            acc_scr[...] = jnp.exp2(m_prev - m_new) * acc_scr[...] + pv
        m_scr[...] = m_new

    if n_lat_blocks:
        kblk = lambda o: k_ref[pl.ds(o, tk), :]
        vblk = lambda o: vt_ref[:, pl.ds(o, tk)]
        lc = kc_ref.shape[0]
        m_scr[...] = jnp.full(m_scr.shape, -jnp.inf, F32)
        acc_scr[...] = jnp.zeros(acc_scr.shape, F32)
        produce(0, k_ref[0:tk, :])

        def group(i, carry):
            base = i * (group_blocks * tk)
            for u in range(group_blocks):
                cur = pl.multiple_of(base + u * tk, tk)
                nxt = pl.multiple_of(base + (u + 1) * tk, tk)
                produce((u + 1) % 2, kblk(nxt))
                softmax_pv(s_scr[u % 2], mb_scr[u % 2], vblk(cur), False)
            return carry
        lax.fori_loop(0, n_lat_blocks // group_blocks - 1, group, 0)
        e0 = n_lat_blocks - group_blocks
        for u in range(group_blocks):
            cur = (e0 + u) * tk
            if u + 1 < group_blocks:
                produce((u + 1) % 2, k_ref[cur + tk:cur + 2 * tk, :])
            else:
                produce((u + 1) % 2, kc_ref[...])
            softmax_pv(s_scr[u % 2], mb_scr[u % 2], vt_ref[:, cur:cur + tk], False)
        c_slot = group_blocks % 2
        softmax_pv(s_scr[c_slot, 0:lc, :], mb_scr[c_slot], vct_ref[...], False)
    else:
        st_c = scores(kc_ref[...])
        softmax_pv(st_c, colmax(st_c), vct_ref[...], True)
    acc = acc_scr[...]
    out_t = acc[0:MLA_V, :] * (1.0 / acc[MLA_V:MLA_V + 1, :])
    o_ref[...] = out_t.T.astype(o_ref.dtype)


def _attention(q, kc, vct, k=None, vt=None):
    bsz, nh, hs, lq = q.shape
    lc = kc.shape[2]
    tq = min(ATT_TQ, lq)
    qspec = pl.BlockSpec((None, None, hs, tq), lambda b, h, i: (b, h, 0, i))
    kfull = lambda n: pl.BlockSpec((None, None, n, hs), lambda b, h, i: (b, h, 0, 0))
    vfull = lambda n: pl.BlockSpec((None, None, V_ROWS, n), lambda b, h, i: (b, h, 0, 0))
    in_specs = [qspec, kfull(lc), vfull(lc)]
    args = [q, kc, vct]
    n_lat = 0
    grp = 0
    tk = ATT_TK
    if k is not None:
        lk = k.shape[2]
        tk = min(ATT_TK, lk)
        n_lat = lk // tk
        grp = min(ATT_GROUP, n_lat)
        assert n_lat % grp == 0, "latent key blocks are consumed in whole groups"
        assert lc <= tk, "the context block reuses a latent score slot"
        in_specs += [kfull(lk), vfull(lk)]
        args += [k, vt]
    scratch = [pltpu.VMEM((1, tq), F32), pltpu.VMEM((V_ROWS, tq), F32)]
    if n_lat:
        scratch += [pltpu.VMEM((2, tk, tq), F32), pltpu.VMEM((2, 1, tq), F32)]
    return pl.pallas_call(
        functools.partial(_attn_kernel, n_lat_blocks=n_lat, tk=tk, group_blocks=grp),
        grid=(bsz, nh, lq // tq),
        in_specs=in_specs,
        out_specs=pl.BlockSpec((None, tq, hs), lambda b, h, i: (b, i, h)),
        out_shape=jax.ShapeDtypeStruct((bsz, lq, nh * hs), BF16),
        scratch_shapes=scratch,
        compiler_params=_cparams(("parallel", "parallel", "arbitrary")),
        name="mla_attention",
    )(*args)


def _pair_swap_cols(w):
    w2 = w.reshape(w.shape[:-1] + (w.shape[-1] // 2, 2))
    return jnp.stack([-w2[..., 1], w2[..., 0]], axis=-1).reshape(w.shape)


def _rope_angles(seq, rot_dim):
    pos = jnp.arange(seq)
    row = (pos // GRID_W).astype(F32)
    col = (pos % GRID_W).astype(F32)
    n_freq = rot_dim // 4
    inv = ROPE_BASE ** (-jnp.arange(n_freq, dtype=F32) / n_freq)
    ang = jnp.concatenate([row[:, None] * inv, col[:, None] * inv], axis=-1)
    return jnp.repeat(jnp.cos(ang), 2, axis=-1), jnp.repeat(jnp.sin(ang), 2, axis=-1)


def _ret_tables(seq, ctx_len):
    cos, sin = _rope_angles(seq, RET_DK)
    cos2, sin2 = jnp.tile(cos, (1, 2)), jnp.tile(sin, (1, 2))
    return cos2, sin2, jnp.ones((ctx_len, LANES), F32), jnp.zeros((ctx_len, LANES), F32)


def _mla_tables(seq, ctx_len):
    cos, sin = _rope_angles(seq, MLA_ROPE)
    one = jnp.ones((seq, MLA_NOPE), F32)
    zero_tail = jnp.zeros((seq, HEAD_SLAB - MLA_QK), F32)
    zero_head = jnp.zeros((seq, MLA_NOPE), F32)
    ta = jnp.concatenate([one, cos, zero_tail], axis=-1)
    tb = jnp.concatenate([zero_head, sin, zero_tail], axis=-1)
    ident = jnp.concatenate([jnp.ones((ctx_len, MLA_QK), F32),
                             jnp.zeros((ctx_len, HEAD_SLAB - MLA_QK), F32)], axis=-1)
    return ta, tb, ident, jnp.zeros((ctx_len, HEAD_SLAB), F32)


def _rope_slab(rope_cols):
    lead = rope_cols.shape[:-1]
    return jnp.concatenate([jnp.zeros(lead + (MLA_NOPE,), F32), rope_cols,
                            jnp.zeros(lead + (HEAD_SLAB - MLA_QK,), F32)], axis=-1)


def _slab_gains(gvec):
    rope = gvec[MLA_NOPE:].reshape(MLA_ROPE // 2, 2)
    swapped = jnp.stack([rope[:, 1], rope[:, 0]], axis=-1).reshape(MLA_ROPE)
    main = jnp.concatenate([gvec, jnp.zeros((HEAD_SLAB - MLA_QK,), F32)])
    return main.reshape(1, HEAD_SLAB), _rope_slab(swapped).reshape(1, HEAD_SLAB)


def _prep_even(ab_w_in, pool_w, ab_w_out):
    wq = ab_w_in[:, POOL_WIDTH:POOL_WIDTH + RET_QK_W]
    wk = ab_w_in[:, POOL_WIDTH + RET_QK_W:POOL_WIDTH + 2 * RET_QK_W]
    w_ext = jnp.concatenate([ab_w_in, _pair_swap_cols(wq), _pair_swap_cols(wk)], axis=-1).astype(BF16)
    wpool_bd = jnp.zeros((POOL_WIDTH, POOL_WIDTH), F32)
    for gi in range(POOL_GROUPS):
        s = slice(gi * POOL_CG, (gi + 1) * POOL_CG)
        wpool_bd = wpool_bd.at[s, s].set(pool_w[gi])
    return w_ext, wpool_bd.astype(BF16), ab_w_out.astype(BF16)


def _prep_odd(w_in, w_qb, w_kvb, w_out):
    wkr = w_in[:, MLA_Q_RANK + MLA_KV_RANK:]
    w_in_ext = jnp.concatenate([w_in[:, :MLA_Q_RANK + MLA_KV_RANK], _rope_slab(wkr),
                                _rope_slab(_pair_swap_cols(wkr))], axis=-1).astype(BF16)
    wq3 = w_qb.reshape(MLA_Q_RANK, MLA_HEADS, MLA_QK)
    wq_main = jnp.concatenate([wq3, jnp.zeros((MLA_Q_RANK, MLA_HEADS, HEAD_SLAB - MLA_QK), F32)], axis=-1)
    wq_aux = _rope_slab(_pair_swap_cols(wq3[..., MLA_NOPE:]))
    w_qb_ext = jnp.concatenate([wq_main, wq_aux], axis=-1).reshape(
        MLA_Q_RANK, MLA_HEADS * 2 * HEAD_SLAB).astype(BF16)
    wkv3 = w_kvb.reshape(MLA_KV_RANK, MLA_HEADS, MLA_NOPE + MLA_V)
    wk_slab = jnp.concatenate([wkv3[..., :MLA_NOPE],
                               jnp.zeros((MLA_KV_RANK, MLA_HEADS, HEAD_SLAB - MLA_NOPE), F32)], axis=-1)
    w_kv_ext = jnp.concatenate([wk_slab, wkv3[..., MLA_NOPE:]], axis=-1).reshape(
        MLA_KV_RANK, MLA_HEADS * 2 * HEAD_SLAB).astype(BF16)
    return w_in_ext, w_qb_ext, w_kv_ext, w_out.astype(BF16)


def kernel(x, c, ctx, c_ctx, ada_w, ada_b, norm_g, ffn_w_in, ffn_w_out, ab_w_in, pool_w, pool_scale,
           ret_log_decay, ret_norm_g, ab_w_out, mla_w_in, mla_qa_g, mla_kva_g, mla_w_qb, mla_w_kvb,
           mla_qn_g, mla_kn_g, mla_w_out):
    bsz, seq, d = x.shape
    ctx_len = ctx.shape[1]
    depth = ada_w.shape[0]
    ctx_row = bsz
    cond8 = jnp.zeros((8, d), F32).at[:bsz].set(c).at[ctx_row].set(c_ctx)
    mod_all = _ada_modulation(cond8, ada_w, ada_b).reshape(depth, 8, N_MOD, d)
    lat_row = lambda b: b
    ctx_mod_row = lambda b: ctx_row

    ret_cos, ret_sin, ret_cos_c, ret_sin_c = _ret_tables(seq, ctx_len)
    m_ta, m_tb, m_ta_c, m_tb_c = _mla_tables(seq, ctx_len)
    q_scale = MLA_QK ** -0.5 * LOG2_E
    zero_state = jnp.zeros((bsz, RET_PAIRS, 2 * RET_DK, 2 * RET_DV), F32)

    xc = ctx
    for i in range(depth):
        last = i == depth - 1
        j = i // 2
        mod8 = mod_all[i]
        w1_in, w1_out = ffn_w_in[i, 0].astype(BF16), ffn_w_out[i, 0].astype(BF16)
        w2_in, w2_out = ffn_w_in[i, 1].astype(BF16), ffn_w_out[i, 1].astype(BF16)
        x = _ffn_half(x, mod8, lat_row, 0, norm_g[i, 0], w1_in, w1_out)
        xc = _ffn_half(xc, mod8, ctx_mod_row, 0, norm_g[i, 0], w1_in, w1_out)
        mix = mix_c = w_mix = None
        if i % 2 == 0:
            w_ext, wpool_bd, w_out = _prep_even(ab_w_in[j], pool_w[j], ab_w_out[j])
            lg = ret_log_decay[j]
            lgf_q = jnp.repeat(lg[0], RET_DK).reshape(1, RET_QK_W)
            lgb_q = jnp.repeat(lg[1], RET_DK).reshape(1, RET_QK_W)
            lgf_v = jnp.repeat(lg[0], RET_DV).reshape(1, RET_V_W)
            lgb_v = jnp.repeat(lg[1], RET_DV).reshape(1, RET_V_W)
            pscale = pool_scale[j].reshape(1, POOL_WIDTH)
            rng = ret_norm_g[j].reshape(1, RET_V_W)
            pc, qc, kc, vc, sgc, cbc, state_b = _ab_project(
                xc, mod8, ctx_mod_row, norm_g[i, 1], w_ext, ret_cos_c, ret_sin_c, lgb_q, lgb_v, zero_state)
            xc_mixed, state_f = _ret_out(xc, mod8, ctx_mod_row, pc, qc, kc, vc, sgc, cbc, wpool_bd, pscale,
                                         rng, w_out, lgf_q, lgf_v, lgb_v, zero_state)
            p, q, k, v, sg, cb, _ = _ab_project(
                x, mod8, lat_row, norm_g[i, 1], w_ext, ret_cos, ret_sin, lgb_q, lgb_v, state_b)
            x, _ = _ret_out(x, mod8, lat_row, p, q, k, v, sg, cb, wpool_bd, pscale, rng, w_out,
                            lgf_q, lgf_v, lgb_v, state_f)
        else:
            w_in_ext, w_qb_ext, w_kv_ext, w_out = _prep_odd(mla_w_in[j], mla_w_qb[j], mla_w_kvb[j],
                                                            mla_w_out[j])
            qa_g = mla_qa_g[j].reshape(1, MLA_Q_RANK)
            kva_g = mla_kva_g[j].reshape(1, MLA_KV_RANK)
            qn_g, qn_gs = _slab_gains(mla_qn_g[j])
            kn_g, kn_gs = _slab_gains(mla_kn_g[j])
            qc, kc, vc = _mla_project(xc, mod8, ctx_mod_row, norm_g[i, 1], w_in_ext, qa_g, kva_g, w_qb_ext,
                                      w_kv_ext, qn_g, qn_gs, kn_g, kn_gs,
                                      m_ta_c * q_scale, m_tb_c, m_ta_c, m_tb_c)
            q, k, v = _mla_project(x, mod8, lat_row, norm_g[i, 1], w_in_ext, qa_g, kva_g, w_qb_ext,
                                   w_kv_ext, qn_g, qn_gs, kn_g, kn_gs,
                                   m_ta * q_scale, m_tb * q_scale, m_ta, m_tb)
            mix, w_mix = _attention(q, kc, vc, k, v), w_out
            if not last:
                mix_c = _attention(qc, kc, vc)
                xc_mixed = xc
        x = _ffn_half(x, mod8, lat_row, 2, norm_g[i, 2], w2_in, w2_out, mix, w_mix)
        if not last:
            xc = _ffn_half(xc_mixed, mod8, ctx_mod_row, 2, norm_g[i, 2], w2_in, w2_out, mix_c, w_mix)
    return x
```

```python
import functools

import numpy as np
import jax
import jax.numpy as jnp
from jax import lax
from jax.experimental import pallas as pl
from jax.experimental.pallas import tpu as pltpu

F32 = jnp.float32
BF16 = jnp.bfloat16

D_MODEL = 1024
GRID_W = 64
D_FF = 2816
FFN_RESIDUAL = 0.5
N_MOD = 9
ROPE_BASE = 10000.0
EPS = 1e-6
POOL_GROUPS = 4
POOL_CG = 64
POOL_WIDTH = POOL_GROUPS * POOL_CG
POOL_WINDOWS = (2, 4, 8, 16)
POOL_HALO = 8
POOL_PAD = 128
RET_HEADS = 6
RET_PAIRS = RET_HEADS // 2
RET_DK = 64
RET_DV = 128
RET_CHUNK = 128
RET_QK_W = RET_HEADS * RET_DK
RET_V_W = RET_HEADS * RET_DV
AB_IN = POOL_WIDTH + 2 * RET_QK_W + 2 * RET_V_W
AB_EXT = AB_IN + 2 * RET_QK_W
MLA_HEADS = 8
MLA_Q_RANK = 384
MLA_KV_RANK = 256
MLA_NOPE = 64
MLA_ROPE = 32
MLA_V = 128
MLA_QK = MLA_NOPE + MLA_ROPE
HEAD_SLAB = 128
V_ROWS = MLA_V + 16
LOG2_E = 1.4426950408889634

LANES = 128
V7X_VMEM_BYTES = 64 * 1024 * 1024
VMEM_LIMIT_BYTES = 56 * 1024 * 1024

ADA_TN = 1536
FFN_TM = 1024
FFN_SUB = 256
MIX_TM = 512
AB_TM = 1024
AB_SUB = 256
MLA_TM = 512
MLA_HEAD_UNROLL = 8
ATT_TQ = 1024
ATT_TK = 512
ATT_GROUP = 4


def _cparams(sem):
    return pltpu.CompilerParams(dimension_semantics=sem, vmem_limit_bytes=VMEM_LIMIT_BYTES)


def _resident(shape):
    nd = len(shape)
    return pl.BlockSpec(shape, lambda *_: (0,) * nd, pipeline_mode=pl.Buffered(1))


def _silu(x):
    return x * (1.0 / (1.0 + jnp.exp(-x)))


def _rms_mod(x, g, shift, scale):
    ms = jnp.mean(x * x, axis=-1, keepdims=True)
    return (x * lax.rsqrt(ms + EPS) * g) * (1.0 + scale) + shift


def _ada_kernel(c_ref, w_ref, b_ref, o_ref):
    s = _silu(c_ref[...])
    o_ref[...] = jnp.dot(s, w_ref[...], precision=lax.Precision.HIGHEST,
                         preferred_element_type=F32) + b_ref[...]


def _ada_modulation(cond8, ada_w, ada_b):
    depth, d, n = ada_w.shape
    return pl.pallas_call(
        _ada_kernel,
        grid=(depth, n // ADA_TN),
        in_specs=[
            pl.BlockSpec((8, d), lambda i, j: (0, 0)),
            pl.BlockSpec((None, d, ADA_TN), lambda i, j: (i, 0, j)),
            pl.BlockSpec((None, 1, ADA_TN), lambda i, j: (i, 0, j)),
        ],
        out_specs=pl.BlockSpec((None, 8, ADA_TN), lambda i, j: (i, 0, j)),
        out_shape=jax.ShapeDtypeStruct((depth, 8, n), F32),
        compiler_params=_cparams(("parallel", "parallel")),
        name="ada_mod",
    )(cond8, ada_w, ada_b.reshape(depth, 1, n))


def _ffn_kernel(*refs, k, mixed):
    if mixed:
        x_ref, mod_ref, g_ref, win_ref, wout_ref, mix_ref, wmix_ref, o_ref = refs
    else:
        x_ref, mod_ref, g_ref, win_ref, wout_ref, o_ref = refs
    tm = x_ref.shape[0]
    sub = min(FFN_SUB, tm)
    for s in range(tm // sub):
        rs = slice(s * sub, (s + 1) * sub)
        x = x_ref[rs, :]
        if mixed:
            x = x + mod_ref[5:6, :] * jnp.dot(mix_ref[rs, :], wmix_ref[...], preferred_element_type=F32)
        h = _rms_mod(x, g_ref[...], mod_ref[3 * k:3 * k + 1, :], mod_ref[3 * k + 1:3 * k + 2, :])
        ab = jnp.dot(h.astype(BF16), win_ref[...], preferred_element_type=F32)
        act = (_silu(ab[:, :D_FF]) * ab[:, D_FF:]).astype(BF16)
        y = jnp.dot(act, wout_ref[...], preferred_element_type=F32)
        o_ref[rs, :] = x + (FFN_RESIDUAL * mod_ref[3 * k + 2:3 * k + 3, :]) * y


def _ffn_half(x, mod8, mod_row, k, g, w_in, w_out, mix=None, w_mix=None):
    bsz, seq, d = x.shape
    tm = min(FFN_TM, seq)
    in_specs = [
        pl.BlockSpec((None, tm, d), lambda b, t: (b, t, 0)),
        pl.BlockSpec((None, N_MOD, d), lambda b, t: (mod_row(b), 0, 0)),
        pl.BlockSpec((1, d), lambda b, t: (0, 0)),
        _resident(w_in.shape),
        _resident(w_out.shape),
    ]
    args = [x, mod8, g.reshape(1, d), w_in, w_out]
    if mix is not None:
        in_specs += [pl.BlockSpec((None, tm, mix.shape[-1]), lambda b, t: (b, t, 0)),
                     _resident(w_mix.shape)]
        args += [mix, w_mix]
    return pl.pallas_call(
        functools.partial(_ffn_kernel, k=k, mixed=mix is not None),
        grid=(bsz, seq // tm),
        in_specs=in_specs,
        out_specs=pl.BlockSpec((None, tm, d), lambda b, t: (b, t, 0)),
        out_shape=jax.ShapeDtypeStruct(x.shape, F32),
        compiler_params=_cparams(("parallel", "parallel")),
        name="ffn_half",
    )(*args)


def _pair_tables(lg_row, expo):
    return jnp.exp(lg_row * expo)


def _bd_mask():
    r = lax.broadcasted_iota(jnp.int32, (2 * RET_DK, 2 * RET_DV), 0)
    c = lax.broadcasted_iota(jnp.int32, (2 * RET_DK, 2 * RET_DV), 1)
    same_head = jnp.where(r < RET_DK, 0, 1) == jnp.where(c < RET_DV, 0, 1)
    return jnp.where(same_head, 1.0, 0.0)


def _abproj_kernel(x_ref, mod_ref, g_ref, w_ref, cos_ref, sin_ref, lgq_ref, lgv_ref, s0_ref,
                   p_ref, q_ref, k_ref, v_ref, sg_ref, cb_ref, sfin_ref, st_scr, *, tm, nt):
    t = pl.program_id(1)

    @pl.when(t == 0)
    def _():
        st_scr[...] = s0_ref[...]

    q0, k0 = POOL_WIDTH, POOL_WIDTH + RET_QK_W
    v0, g0 = k0 + RET_QK_W, k0 + RET_QK_W + RET_V_W
    qs0, ks0 = AB_IN, AB_IN + RET_QK_W
    row = lax.broadcasted_iota(jnp.int32, (RET_CHUNK, LANES), 0).astype(F32)
    bdm = _bd_mask()
    k_scale = RET_DK ** -0.5
    sub = min(AB_SUB, tm)
    for s in reversed(range(tm // sub)):
        r0 = s * sub
        ts = slice(r0, r0 + sub)
        h = _rms_mod(x_ref[ts, :], g_ref[...], mod_ref[3:4, :], mod_ref[4:5, :])
        proj = jnp.dot(h.astype(BF16), w_ref[...], preferred_element_type=F32)
        p_ref[ts, :] = proj[:, :POOL_WIDTH]
        v_ref[ts, :] = proj[:, v0:v0 + RET_V_W].astype(BF16)
        sg_ref[ts, :] = _silu(proj[:, g0:g0 + RET_V_W]).astype(BF16)
        cos = cos_ref[ts, :]
        sin = sin_ref[ts, :]
        for j in range(RET_PAIRS):
            sl = slice(LANES * j, LANES * (j + 1))
            qr = proj[:, q0 + LANES * j:q0 + LANES * (j + 1)] * cos \
                + proj[:, qs0 + LANES * j:qs0 + LANES * (j + 1)] * sin
            kr = (proj[:, k0 + LANES * j:k0 + LANES * (j + 1)] * cos
                  + proj[:, ks0 + LANES * j:ks0 + LANES * (j + 1)] * sin) * k_scale
            q_ref[ts, sl] = qr.astype(BF16)
            k_ref[ts, sl] = kr.astype(BF16)
            lgq = lgq_ref[:, sl]
            lgv = lgv_ref[:, 2 * LANES * j:2 * LANES * (j + 1)]
            xi = _pair_tables(lgq, float(RET_CHUNK) - row)
            zeta = _pair_tables(lgq, row)
            dec = jnp.exp(lgv * float(RET_CHUNK))
            vsl = slice(2 * LANES * j, 2 * LANES * (j + 1))
            for c in reversed(range(sub // RET_CHUNK)):
                ls = slice(c * RET_CHUNK, (c + 1) * RET_CHUNK)
                rs = slice(r0 + c * RET_CHUNK, r0 + (c + 1) * RET_CHUNK)
                bd = st_scr[j]
                qx = (qr[ls] * xi).astype(BF16)
                cb_ref[rs, vsl] = jnp.dot(qx, bd.astype(BF16), preferred_element_type=F32)
                kz = (kr[ls] * zeta).astype(BF16)
                upd = lax.dot_general(kz, v_ref[rs, vsl], (((0,), (0,)), ((), ())),
                                      preferred_element_type=F32)
                st_scr[j] = dec * bd + bdm * upd

    @pl.when(t == nt - 1)
    def _():
        sfin_ref[...] = st_scr[...]


def _ab_project(x, mod8, mod_row, g, w_ext, cos, sin, lg_q, lg_v, s0):
    bsz, seq, d = x.shape
    tm = min(AB_TM, seq)
    nt = seq // tm
    rev = lambda b, t: (b, nt - 1 - t, 0)
    tok = lambda w, dt: jax.ShapeDtypeStruct((bsz, seq, w), dt)
    st_shape = (RET_PAIRS, 2 * RET_DK, 2 * RET_DV)
    return pl.pallas_call(
        functools.partial(_abproj_kernel, tm=tm, nt=nt),
        grid=(bsz, nt),
        in_specs=[
            pl.BlockSpec((None, tm, d), rev),
            pl.BlockSpec((None, N_MOD, d), lambda b, t: (mod_row(b), 0, 0)),
            pl.BlockSpec((1, d), lambda b, t: (0, 0)),
            _resident(w_ext.shape),
            pl.BlockSpec((tm, LANES), lambda b, t: (nt - 1 - t, 0)),
            pl.BlockSpec((tm, LANES), lambda b, t: (nt - 1 - t, 0)),
            pl.BlockSpec((1, RET_QK_W), lambda b, t: (0, 0)),
            pl.BlockSpec((1, RET_V_W), lambda b, t: (0, 0)),
            pl.BlockSpec((None,) + st_shape, lambda b, t: (b, 0, 0, 0)),
        ],
        out_specs=[
            pl.BlockSpec((None, tm, POOL_WIDTH), rev),
            pl.BlockSpec((None, tm, RET_QK_W), rev),
            pl.BlockSpec((None, tm, RET_QK_W), rev),
            pl.BlockSpec((None, tm, RET_V_W), rev),
            pl.BlockSpec((None, tm, RET_V_W), rev),
            pl.BlockSpec((None, tm, RET_V_W), rev),
            pl.BlockSpec((None,) + st_shape, lambda b, t: (b, 0, 0, 0)),
        ],
        out_shape=[tok(POOL_WIDTH, F32), tok(RET_QK_W, BF16), tok(RET_QK_W, BF16),
                   tok(RET_V_W, BF16), tok(RET_V_W, BF16), tok(RET_V_W, F32),
                   jax.ShapeDtypeStruct((bsz,) + st_shape, F32)],
        scratch_shapes=[pltpu.VMEM(st_shape, F32)],
        compiler_params=_cparams(("arbitrary", "arbitrary")),
        name="ab_project",
    )(x, mod8, g.reshape(1, d), w_ext, cos, sin, lg_q, lg_v, s0)


def _retout_kernel(x_ref, mod_ref, pprev_ref, pcur_ref, pnext_ref, pm_ref, invc_ref,
                   q_ref, k_ref, v_ref, sg_ref, cb_ref, wpool_ref, pscale_ref, rng_ref, wout_ref,
                   lgfq_ref, lgfv_ref, lgbv_ref, s0_ref,
                   o_ref, sfin_ref, st_scr, ycat_scr, pe_scr, *, tm, nt):
    t = pl.program_id(1)

    @pl.when(t == 0)
    def _():
        st_scr[...] = s0_ref[...]

    pcur = pcur_ref[...]
    pe_scr[0:tm, :] = pcur
    pe_scr[tm:tm + POOL_HALO, :] = pprev_ref[...]
    pe_scr[tm + POOL_HALO:tm + 2 * POOL_HALO, :] = pnext_ref[...]
    pe_scr[tm + 2 * POOL_HALO:, :] = jnp.zeros((POOL_PAD - 2 * POOL_HALO, POOL_WIDTH), F32)
    pe = pe_scr[...]
    pe_hi = pe.astype(BF16)
    pe_lo = (pe - pe_hi.astype(F32)).astype(BF16)
    lane_grp = jnp.right_shift(lax.broadcasted_iota(jnp.int32, (tm, POOL_WIDTH), 1),
                               POOL_CG.bit_length() - 1)
    pooled = jnp.zeros((tm, POOL_WIDTH), F32)
    for gi in range(POOL_GROUPS):
        win = pm_ref[gi]
        tot = jnp.dot(win, pe_hi, preferred_element_type=F32) \
            + jnp.dot(win, pe_lo, preferred_element_type=F32)
        pooled = jnp.where(lane_grp == gi, tot, pooled)
    diffs = pooled * invc_ref[...] - pcur
    pool_y = jnp.dot(diffs.astype(BF16), wpool_ref[...], preferred_element_type=F32) * pscale_ref[...]
    ycat_scr[:, 0:POOL_WIDTH] = pool_y.astype(BF16)

    ri = lax.broadcasted_iota(jnp.int32, (RET_CHUNK, RET_CHUNK), 0)
    ci = lax.broadcasted_iota(jnp.int32, (RET_CHUNK, RET_CHUNK), 1)
    rel = (ri - ci).astype(F32)
    row = lax.broadcasted_iota(jnp.int32, (RET_CHUNK, LANES), 0).astype(F32)
    lane = lax.broadcasted_iota(jnp.int32, (RET_CHUNK, LANES), 1)
    bdm = _bd_mask()
    zero_bf = jnp.zeros((RET_CHUNK, LANES), BF16)
    tables = []
    for j in range(RET_PAIRS):
        sl = slice(LANES * j, LANES * (j + 1))
        vsl = slice(2 * LANES * j, 2 * LANES * (j + 1))
        lgq = lgfq_ref[:, sl]
        xi = _pair_tables(lgq, row + 1.0)
        zeta = _pair_tables(lgq, float(RET_CHUNK - 1) - row)
        dec = jnp.exp(lgfv_ref[:, vsl] * float(RET_CHUNK))
        masks = []
        for a in range(2):
            hs = slice(2 * LANES * j + LANES * a, 2 * LANES * j + LANES * (a + 1))
            mf = jnp.exp(lgfv_ref[:, hs] * jnp.maximum(rel, 0.0))
            mb = jnp.exp(lgbv_ref[:, hs] * jnp.maximum(-rel, 0.0))
            masks.append(jnp.where(rel > 0, mf, jnp.where(rel < 0, mb, 2.0)))
        tables.append((xi, zeta, dec, masks))
    for c in range(tm // RET_CHUNK):
        rs = slice(c * RET_CHUNK, (c + 1) * RET_CHUNK)
        for j in range(RET_PAIRS):
            sl = slice(LANES * j, LANES * (j + 1))
            vsl = slice(2 * LANES * j, 2 * LANES * (j + 1))
            xi, zeta, dec, masks = tables[j]
            q2 = q_ref[rs, sl]
            k2 = k_ref[rs, sl]
            v2 = v_ref[rs, vsl]
            bd = st_scr[j]
            qx = (q2.astype(F32) * xi).astype(BF16)
            ret = jnp.dot(qx, bd.astype(BF16), preferred_element_type=F32) + cb_ref[rs, vsl]
            for a in range(2):
                qa = jnp.where(lane < RET_DK if a == 0 else lane >= RET_DK, q2, zero_bf)
                s = lax.dot_general(qa, k2, (((1,), (1,)), ((), ())), preferred_element_type=F32)
                pa = (s * masks[a]).astype(BF16)
                hs_v = slice(LANES * a, LANES * (a + 1))
                r = ret[:, hs_v] + jnp.dot(pa, v2[:, hs_v], preferred_element_type=F32)
                ms = jnp.mean(r * r, axis=-1, keepdims=True)
                hcol = 2 * LANES * j + LANES * a
                y = (r * lax.rsqrt(ms + EPS) * rng_ref[:, hcol:hcol + LANES]) \
                    * sg_ref[rs, hcol:hcol + LANES].astype(F32)
                ycat_scr[rs, POOL_WIDTH + hcol:POOL_WIDTH + hcol + LANES] = y.astype(BF16)
            kz = (k2.astype(F32) * zeta).astype(BF16)
            upd = lax.dot_general(kz, v2, (((0,), (0,)), ((), ())), preferred_element_type=F32)
            st_scr[j] = dec * bd + bdm * upd
        yy = jnp.dot(ycat_scr[rs, :], wout_ref[...], preferred_element_type=F32)
        o_ref[rs, :] = x_ref[rs, :] + mod_ref[5:6, :] * yy

    @pl.when(t == nt - 1)
    def _():
        sfin_ref[...] = st_scr[...]


def _pool_constants(tm, seq):
    nt = seq // tm
    kinds = [0] if nt == 1 else [0, 1, nt - 1]
    mats = np.zeros((len(kinds), POOL_GROUPS, tm, tm + POOL_PAD), np.float32)
    invc = np.zeros((len(kinds), tm, POOL_WIDTH), np.float32)
    i = np.arange(tm)
    for vi, tile in enumerate(kinds):
        pos = tile * tm + i
        colpos = np.full(tm + POOL_PAD, -1)
        colpos[:tm] = pos
        colpos[tm:tm + POOL_HALO] = tile * tm - POOL_HALO + np.arange(POOL_HALO)
        colpos[tm + POOL_HALO:tm + 2 * POOL_HALO] = (tile + 1) * tm + np.arange(POOL_HALO)
        for gi, w in enumerate(POOL_WINDOWS):
            lo = w // 2
            hi = w - 1 - lo
            start = np.maximum(pos - lo, 0)
            end = np.minimum(pos + hi + 1, seq)
            mats[vi, gi] = (colpos[None, :] >= start[:, None]) & (colpos[None, :] < end[:, None])
            invc[vi, :, gi * POOL_CG:(gi + 1) * POOL_CG] = (1.0 / (end - start))[:, None]
    return jnp.asarray(mats, BF16), jnp.asarray(invc, F32), nt


def _ret_out(x, mod8, mod_row, p, q, k, v, sg, cb, wpool_bd, pscale, rng, w_out, lgf_q, lgf_v, lgb_v, s0):
    bsz, seq, d = x.shape
    tm = min(MIX_TM, seq)
    pm, invc, nt = _pool_constants(tm, seq)
    hb = tm // POOL_HALO
    nhb = seq // POOL_HALO

    def variant(t):
        if nt == 1:
            return 0
        return jnp.where(t == 0, 0, jnp.where(t == nt - 1, 2, 1))

    cur = lambda b, t: (b, t, 0)
    st_shape = (RET_PAIRS, 2 * RET_DK, 2 * RET_DV)
    tokspec = lambda w: pl.BlockSpec((None, tm, w), cur)
    return pl.pallas_call(
        functools.partial(_retout_kernel, tm=tm, nt=nt),
        grid=(bsz, nt),
        in_specs=[
            tokspec(d),
            pl.BlockSpec((None, N_MOD, d), lambda b, t: (mod_row(b), 0, 0)),
            pl.BlockSpec((None, POOL_HALO, POOL_WIDTH), lambda b, t: (b, jnp.maximum(t * hb - 1, 0), 0)),
            tokspec(POOL_WIDTH),
            pl.BlockSpec((None, POOL_HALO, POOL_WIDTH),
                         lambda b, t: (b, jnp.minimum((t + 1) * hb, nhb - 1), 0)),
            pl.BlockSpec((None, POOL_GROUPS, tm, tm + POOL_PAD), lambda b, t: (variant(t), 0, 0, 0)),
            pl.BlockSpec((None, tm, POOL_WIDTH), lambda b, t: (variant(t), 0, 0)),
            tokspec(RET_QK_W), tokspec(RET_QK_W), tokspec(RET_V_W), tokspec(RET_V_W), tokspec(RET_V_W),
            _resident(wpool_bd.shape),
            pl.BlockSpec((1, POOL_WIDTH), lambda b, t: (0, 0)),
            pl.BlockSpec((1, RET_V_W), lambda b, t: (0, 0)),
            _resident(w_out.shape),
            pl.BlockSpec((1, RET_QK_W), lambda b, t: (0, 0)),
            pl.BlockSpec((1, RET_V_W), lambda b, t: (0, 0)),
            pl.BlockSpec((1, RET_V_W), lambda b, t: (0, 0)),
            pl.BlockSpec((None,) + st_shape, lambda b, t: (b, 0, 0, 0)),
        ],
        out_specs=[
            tokspec(d),
            pl.BlockSpec((None,) + st_shape, lambda b, t: (b, 0, 0, 0)),
        ],
        out_shape=[jax.ShapeDtypeStruct(x.shape, F32),
                   jax.ShapeDtypeStruct((bsz,) + st_shape, F32)],
        scratch_shapes=[pltpu.VMEM(st_shape, F32),
                        pltpu.VMEM((tm, d), BF16),
                        pltpu.VMEM((tm + POOL_PAD, POOL_WIDTH), F32)],
        compiler_params=_cparams(("arbitrary", "arbitrary")),
        name="ret_out",
    )(x, mod8, p, p, p, pm, invc, q, k, v, sg, cb, wpool_bd, pscale, rng, w_out,
      lgf_q, lgf_v, lgb_v, s0)


def _head_norm_rope(main, aux, g_main, g_aux):
    ms = jnp.sum(main * main, axis=-1, keepdims=True) * (1.0 / MLA_QK)
    r = lax.rsqrt(ms + EPS)
    return (main * r) * g_main + (aux * r) * g_aux


def _mlaproj_kernel(x_ref, mod_ref, g_ref, win_ref, qag_ref, kvag_ref, wqb_ref, wkv_ref,
                    qg_ref, qgs_ref, kg_ref, kgs_ref, qa_t_ref, qb_t_ref, ka_t_ref, kb_t_ref,
                    q_ref, k_ref, v_ref, qan_scr, kvn_scr, kr_scr, gt_scr):
    h = _rms_mod(x_ref[...], g_ref[...], mod_ref[3:4, :], mod_ref[4:5, :])
    proj = jnp.dot(h.astype(BF16), win_ref[...], preferred_element_type=F32)
    qa = proj[:, :MLA_Q_RANK]
    kva = proj[:, MLA_Q_RANK:MLA_Q_RANK + MLA_KV_RANK]
    r0 = MLA_Q_RANK + MLA_KV_RANK
    kr_scr[0] = proj[:, r0:r0 + HEAD_SLAB]
    kr_scr[1] = proj[:, r0 + HEAD_SLAB:r0 + 2 * HEAD_SLAB]
    qan = qa * lax.rsqrt(jnp.mean(qa * qa, axis=-1, keepdims=True) + EPS) * qag_ref[...]
    kvn = kva * lax.rsqrt(jnp.mean(kva * kva, axis=-1, keepdims=True) + EPS) * kvag_ref[...]
    qan_scr[...] = qan.astype(BF16)
    kvn_scr[...] = kvn.astype(BF16)
    gt_scr[0] = qg_ref[...] * qa_t_ref[...]
    gt_scr[1] = qgs_ref[...] * qb_t_ref[...]
    gt_scr[2] = kg_ref[...] * ka_t_ref[...]
    gt_scr[3] = kgs_ref[...] * kb_t_ref[...]
    tm = proj.shape[0]
    ones_rows = jnp.where(lax.broadcasted_iota(jnp.int32, (V_ROWS - MLA_V, tm), 0) == 0,
                          1.0, 0.0).astype(BF16)

    def head(hd, carry):
        c0 = pl.multiple_of(hd * (2 * HEAD_SLAB), 2 * HEAD_SLAB)
        qh = jnp.dot(qan_scr[...], wqb_ref[:, pl.ds(c0, 2 * HEAD_SLAB)], preferred_element_type=F32)
        kvh = jnp.dot(kvn_scr[...], wkv_ref[:, pl.ds(c0, 2 * HEAD_SLAB)], preferred_element_type=F32)
        q_ref[hd] = _head_norm_rope(qh[:, :HEAD_SLAB], qh[:, HEAD_SLAB:], gt_scr[0], gt_scr[1]).T.astype(BF16)
        k_ref[hd] = _head_norm_rope(kvh[:, :HEAD_SLAB] + kr_scr[0], kr_scr[1],
                                    gt_scr[2], gt_scr[3]).astype(BF16)
        v_ref[hd, 0:MLA_V, :] = kvh[:, HEAD_SLAB:].T.astype(BF16)
        v_ref[hd, MLA_V:V_ROWS, :] = ones_rows
        return carry
    lax.fori_loop(0, MLA_HEADS, head, 0, unroll=MLA_HEAD_UNROLL)


def _mla_project(x, mod8, mod_row, g, w_in, qa_g, kva_g, w_qb, w_kv, qn_g, qn_gs, kn_g, kn_gs,
                 q_ta, q_tb, k_ta, k_tb):
    bsz, seq, d = x.shape
    tm = min(MLA_TM, seq)
    head_out = jax.ShapeDtypeStruct((bsz, MLA_HEADS, seq, HEAD_SLAB), BF16)
    hspec = pl.BlockSpec((None, MLA_HEADS, tm, HEAD_SLAB), lambda b, t: (b, 0, t, 0))
    tspec = pl.BlockSpec((tm, HEAD_SLAB), lambda b, t: (t, 0))
    row = lambda n: pl.BlockSpec((1, n), lambda b, t: (0, 0))
    return pl.pallas_call(
        _mlaproj_kernel,
        grid=(bsz, seq // tm),
        in_specs=[
            pl.BlockSpec((None, tm, d), lambda b, t: (b, t, 0)),
            pl.BlockSpec((None, N_MOD, d), lambda b, t: (mod_row(b), 0, 0)),
            row(d),
            _resident(w_in.shape), row(MLA_Q_RANK), row(MLA_KV_RANK),
            _resident(w_qb.shape), _resident(w_kv.shape),
            row(HEAD_SLAB), row(HEAD_SLAB), row(HEAD_SLAB), row(HEAD_SLAB),
            tspec, tspec, tspec, tspec,
        ],
        out_specs=[pl.BlockSpec((None, MLA_HEADS, HEAD_SLAB, tm), lambda b, t: (b, 0, 0, t)),
                   hspec,
                   pl.BlockSpec((None, MLA_HEADS, V_ROWS, tm), lambda b, t: (b, 0, 0, t))],
        out_shape=[jax.ShapeDtypeStruct((bsz, MLA_HEADS, HEAD_SLAB, seq), BF16),
                   head_out,
                   jax.ShapeDtypeStruct((bsz, MLA_HEADS, V_ROWS, seq), BF16)],
        scratch_shapes=[pltpu.VMEM((tm, MLA_Q_RANK), BF16), pltpu.VMEM((tm, MLA_KV_RANK), BF16),
                        pltpu.VMEM((2, tm, HEAD_SLAB), F32), pltpu.VMEM((4, tm, HEAD_SLAB), F32)],
        compiler_params=_cparams(("parallel", "parallel")),
        name="mla_project",
    )(x, mod8, g.reshape(1, d), w_in, qa_g, kva_g, w_qb, w_kv, qn_g, qn_gs, kn_g, kn_gs,
      q_ta, q_tb, k_ta, k_tb)


def _attn_kernel(*refs, n_lat_blocks, tk, group_blocks):
    if n_lat_blocks:
        q_ref, kc_ref, vct_ref, k_ref, vt_ref, o_ref, m_scr, acc_scr, s_scr, mb_scr = refs
    else:
        q_ref, kc_ref, vct_ref, o_ref, m_scr, acc_scr = refs
    q = q_ref[...]

    def scores(kb):
        return jnp.dot(kb, q, preferred_element_type=F32)

    def colmax(st):
        return jnp.max(st, axis=0, keepdims=True)

    def produce(slot, kb):
        st = scores(kb)
        s_scr[slot, 0:kb.shape[0], :] = st
        mb_scr[slot] = colmax(st)

    def softmax_pv(st, m_blk, vtb, first):
        if first:
            m_new = m_blk
        else:
            m_prev = m_scr[...]
            m_new = jnp.maximum(m_prev, m_blk)
        p = jnp.exp2(st - m_new).astype(BF16)
        pv = jnp.dot(vtb, p, preferred_element_type=F32)
        if first:
            acc_scr[...] = pv
        else:
            acc_scr[...] = jnp.exp2(m_prev - m_new) * acc_scr[...] + pv
        m_scr[...] = m_new

    if n_lat_blocks:
        kblk = lambda o: k_ref[pl.ds(o, tk), :]
        vblk = lambda o: vt_ref[:, pl.ds(o, tk)]
        lc = kc_ref.shape[0]
        m_scr[...] = jnp.full(m_scr.shape, -jnp.inf, F32)
        acc_scr[...] = jnp.zeros(acc_scr.shape, F32)
        produce(0, k_ref[0:tk, :])

        def group(i, carry):
            base = i * (group_blocks * tk)
            for u in range(group_blocks):
                cur = pl.multiple_of(base + u * tk, tk)
                nxt = pl.multiple_of(base + (u + 1) * tk, tk)
                produce((u + 1) % 2, kblk(nxt))
                softmax_pv(s_scr[u % 2], mb_scr[u % 2], vblk(cur), False)
            return carry
        lax.fori_loop(0, n_lat_blocks // group_blocks - 1, group, 0)
        e0 = n_lat_blocks - group_blocks
        for u in range(group_blocks):
            cur = (e0 + u) * tk
            if u + 1 < group_blocks:
                produce((u + 1) % 2, k_ref[cur + tk:cur + 2 * tk, :])
            else:
                produce((u + 1) % 2, kc_ref[...])
            softmax_pv(s_scr[u % 2], mb_scr[u % 2], vt_ref[:, cur:cur + tk], False)
        c_slot = group_blocks % 2
        softmax_pv(s_scr[c_slot, 0:lc, :], mb_scr[c_slot], vct_ref[...], False)
    else:
        st_c = scores(kc_ref[...])
        softmax_pv(st_c, colmax(st_c), vct_ref[...], True)
    acc = acc_scr[...]
    out_t = acc[0:MLA_V, :] * (1.0 / acc[MLA_V:MLA_V + 1, :])
    o_ref[...] = out_t.T.astype(o_ref.dtype)


def _attention(q, kc, vct, k=None, vt=None):
    bsz, nh, hs, lq = q.shape
    lc = kc.shape[2]
    tq = min(ATT_TQ, lq)
    qspec = pl.BlockSpec((None, None, hs, tq), lambda b, h, i: (b, h, 0, i))
    kfull = lambda n: pl.BlockSpec((None, None, n, hs), lambda b, h, i: (b, h, 0, 0))
    vfull = lambda n: pl.BlockSpec((None, None, V_ROWS, n), lambda b, h, i: (b, h, 0, 0))
    in_specs = [qspec, kfull(lc), vfull(lc)]
    args = [q, kc, vct]
    n_lat = 0
    grp = 0
    tk = ATT_TK
    if k is not None:
        lk = k.shape[2]
        tk = min(ATT_TK, lk)
        n_lat = lk // tk
        grp = min(ATT_GROUP, n_lat)
        assert n_lat % grp == 0, "latent key blocks are consumed in whole groups"
        assert lc <= tk, "the context block reuses a latent score slot"
        in_specs += [kfull(lk), vfull(lk)]
        args += [k, vt]
    scratch = [pltpu.VMEM((1, tq), F32), pltpu.VMEM((V_ROWS, tq), F32)]
    if n_lat:
        scratch += [pltpu.VMEM((2, tk, tq), F32), pltpu.VMEM((2, 1, tq), F32)]
    return pl.pallas_call(
        functools.partial(_attn_kernel, n_lat_blocks=n_lat, tk=tk, group_blocks=grp),
        grid=(bsz, nh, lq // tq),
        in_specs=in_specs,
        out_specs=pl.BlockSpec((None, tq, hs), lambda b, h, i: (b, i, h)),
        out_shape=jax.ShapeDtypeStruct((bsz, lq, nh * hs), BF16),
        scratch_shapes=scratch,
        compiler_params=_cparams(("parallel", "parallel", "arbitrary")),
        name="mla_attention",
    )(*args)


def _pair_swap_cols(w):
    w2 = w.reshape(w.shape[:-1] + (w.shape[-1] // 2, 2))
    return jnp.stack([-w2[..., 1], w2[..., 0]], axis=-1).reshape(w.shape)


def _rope_angles(seq, rot_dim):
    pos = jnp.arange(seq)
    row = (pos // GRID_W).astype(F32)
    col = (pos % GRID_W).astype(F32)
    n_freq = rot_dim // 4
    inv = ROPE_BASE ** (-jnp.arange(n_freq, dtype=F32) / n_freq)
    ang = jnp.concatenate([row[:, None] * inv, col[:, None] * inv], axis=-1)
    return jnp.repeat(jnp.cos(ang), 2, axis=-1), jnp.repeat(jnp.sin(ang), 2, axis=-1)


def _ret_tables(seq, ctx_len):
    cos, sin = _rope_angles(seq, RET_DK)
    cos2, sin2 = jnp.tile(cos, (1, 2)), jnp.tile(sin, (1, 2))
    return cos2, sin2, jnp.ones((ctx_len, LANES), F32), jnp.zeros((ctx_len, LANES), F32)


def _mla_tables(seq, ctx_len):
    cos, sin = _rope_angles(seq, MLA_ROPE)
    one = jnp.ones((seq, MLA_NOPE), F32)
    zero_tail = jnp.zeros((seq, HEAD_SLAB - MLA_QK), F32)
    zero_head = jnp.zeros((seq, MLA_NOPE), F32)
    ta = jnp.concatenate([one, cos, zero_tail], axis=-1)
    tb = jnp.concatenate([zero_head, sin, zero_tail], axis=-1)
    ident = jnp.concatenate([jnp.ones((ctx_len, MLA_QK), F32),
                             jnp.zeros((ctx_len, HEAD_SLAB - MLA_QK), F32)], axis=-1)
    return ta, tb, ident, jnp.zeros((ctx_len, HEAD_SLAB), F32)


def _rope_slab(rope_cols):
    lead = rope_cols.shape[:-1]
    return jnp.concatenate([jnp.zeros(lead + (MLA_NOPE,), F32), rope_cols,
                            jnp.zeros(lead + (HEAD_SLAB - MLA_QK,), F32)], axis=-1)


def _slab_gains(gvec):
    rope = gvec[MLA_NOPE:].reshape(MLA_ROPE // 2, 2)
    swapped = jnp.stack([rope[:, 1], rope[:, 0]], axis=-1).reshape(MLA_ROPE)
    main = jnp.concatenate([gvec, jnp.zeros((HEAD_SLAB - MLA_QK,), F32)])
    return main.reshape(1, HEAD_SLAB), _rope_slab(swapped).reshape(1, HEAD_SLAB)


def _prep_even(ab_w_in, pool_w, ab_w_out):
    wq = ab_w_in[:, POOL_WIDTH:POOL_WIDTH + RET_QK_W]
    wk = ab_w_in[:, POOL_WIDTH + RET_QK_W:POOL_WIDTH + 2 * RET_QK_W]
    w_ext = jnp.concatenate([ab_w_in, _pair_swap_cols(wq), _pair_swap_cols(wk)], axis=-1).astype(BF16)
    wpool_bd = jnp.zeros((POOL_WIDTH, POOL_WIDTH), F32)
    for gi in range(POOL_GROUPS):
        s = slice(gi * POOL_CG, (gi + 1) * POOL_CG)
        wpool_bd = wpool_bd.at[s, s].set(pool_w[gi])
    return w_ext, wpool_bd.astype(BF16), ab_w_out.astype(BF16)


def _prep_odd(w_in, w_qb, w_kvb, w_out):
    wkr = w_in[:, MLA_Q_RANK + MLA_KV_RANK:]
    w_in_ext = jnp.concatenate([w_in[:, :MLA_Q_RANK + MLA_KV_RANK], _rope_slab(wkr),
                                _rope_slab(_pair_swap_cols(wkr))], axis=-1).astype(BF16)
    wq3 = w_qb.reshape(MLA_Q_RANK, MLA_HEADS, MLA_QK)
    wq_main = jnp.concatenate([wq3, jnp.zeros((MLA_Q_RANK, MLA_HEADS, HEAD_SLAB - MLA_QK), F32)], axis=-1)
    wq_aux = _rope_slab(_pair_swap_cols(wq3[..., MLA_NOPE:]))
    w_qb_ext = jnp.concatenate([wq_main, wq_aux], axis=-1).reshape(
        MLA_Q_RANK, MLA_HEADS * 2 * HEAD_SLAB).astype(BF16)
    wkv3 = w_kvb.reshape(MLA_KV_RANK, MLA_HEADS, MLA_NOPE + MLA_V)
    wk_slab = jnp.concatenate([wkv3[..., :MLA_NOPE],
                               jnp.zeros((MLA_KV_RANK, MLA_HEADS, HEAD_SLAB - MLA_NOPE), F32)], axis=-1)
    w_kv_ext = jnp.concatenate([wk_slab, wkv3[..., MLA_NOPE:]], axis=-1).reshape(
        MLA_KV_RANK, MLA_HEADS * 2 * HEAD_SLAB).astype(BF16)
    return w_in_ext, w_qb_ext, w_kv_ext, w_out.astype(BF16)


def kernel(x, c, ctx, c_ctx, ada_w, ada_b, norm_g, ffn_w_in, ffn_w_out, ab_w_in, pool_w, pool_scale,
           ret_log_decay, ret_norm_g, ab_w_out, mla_w_in, mla_qa_g, mla_kva_g, mla_w_qb, mla_w_kvb,
           mla_qn_g, mla_kn_g, mla_w_out):
    bsz, seq, d = x.shape
    ctx_len = ctx.shape[1]
    depth = ada_w.shape[0]
    ctx_row = bsz
    cond8 = jnp.zeros((8, d), F32).at[:bsz].set(c).at[ctx_row].set(c_ctx)
    mod_all = _ada_modulation(cond8, ada_w, ada_b).reshape(depth, 8, N_MOD, d)
    lat_row = lambda b: b
    ctx_mod_row = lambda b: ctx_row

    ret_cos, ret_sin, ret_cos_c, ret_sin_c = _ret_tables(seq, ctx_len)
    m_ta, m_tb, m_ta_c, m_tb_c = _mla_tables(seq, ctx_len)
    q_scale = MLA_QK ** -0.5 * LOG2_E
    zero_state = jnp.zeros((bsz, RET_PAIRS, 2 * RET_DK, 2 * RET_DV), F32)

    xc = ctx
    for i in range(depth):
        last = i == depth - 1
        j = i // 2
        mod8 = mod_all[i]
        w1_in, w1_out = ffn_w_in[i, 0].astype(BF16), ffn_w_out[i, 0].astype(BF16)
        w2_in, w2_out = ffn_w_in[i, 1].astype(BF16), ffn_w_out[i, 1].astype(BF16)
        x = _ffn_half(x, mod8, lat_row, 0, norm_g[i, 0], w1_in, w1_out)
        xc = _ffn_half(xc, mod8, ctx_mod_row, 0, norm_g[i, 0], w1_in, w1_out)
        mix = mix_c = w_mix = None
        if i % 2 == 0:
            w_ext, wpool_bd, w_out = _prep_even(ab_w_in[j], pool_w[j], ab_w_out[j])
            lg = ret_log_decay[j]
            lgf_q = jnp.repeat(lg[0], RET_DK).reshape(1, RET_QK_W)
            lgb_q = jnp.repeat(lg[1], RET_DK).reshape(1, RET_QK_W)
            lgf_v = jnp.repeat(lg[0], RET_DV).reshape(1, RET_V_W)
            lgb_v = jnp.repeat(lg[1], RET_DV).reshape(1, RET_V_W)
            pscale = pool_scale[j].reshape(1, POOL_WIDTH)
            rng = ret_norm_g[j].reshape(1, RET_V_W)
            pc, qc, kc, vc, sgc, cbc, state_b = _ab_project(
                xc, mod8, ctx_mod_row, norm_g[i, 1], w_ext, ret_cos_c, ret_sin_c, lgb_q, lgb_v, zero_state)
            xc_mixed, state_f = _ret_out(xc, mod8, ctx_mod_row, pc, qc, kc, vc, sgc, cbc, wpool_bd, pscale,
                                         rng, w_out, lgf_q, lgf_v, lgb_v, zero_state)
            p, q, k, v, sg, cb, _ = _ab_project(
                x, mod8, lat_row, norm_g[i, 1], w_ext, ret_cos, ret_sin, lgb_q, lgb_v, state_b)
            x, _ = _ret_out(x, mod8, lat_row, p, q, k, v, sg, cb, wpool_bd, pscale, rng, w_out,
                            lgf_q, lgf_v, lgb_v, state_f)
        else:
            w_in_ext, w_qb_ext, w_kv_ext, w_out = _prep_odd(mla_w_in[j], mla_w_qb[j], mla_w_kvb[j],
                                                            mla_w_out[j])
            qa_g = mla_qa_g[j].reshape(1, MLA_Q_RANK)
            kva_g = mla_kva_g[j].reshape(1, MLA_KV_RANK)
            qn_g, qn_gs = _slab_gains(mla_qn_g[j])
            kn_g, kn_gs = _slab_gains(mla_kn_g[j])
            qc, kc, vc = _mla_project(xc, mod8, ctx_mod_row, norm_g[i, 1], w_in_ext, qa_g, kva_g, w_qb_ext,
                                      w_kv_ext, qn_g, qn_gs, kn_g, kn_gs,
                                      m_ta_c * q_scale, m_tb_c, m_ta_c, m_tb_c)
            q, k, v = _mla_project(x, mod8, lat_row, norm_g[i, 1], w_in_ext, qa_g, kva_g, w_qb_ext,
                                   w_kv_ext, qn_g, qn_gs, kn_g, kn_gs,
                                   m_ta * q_scale, m_tb * q_scale, m_ta, m_tb)
            mix, w_mix = _attention(q, kc, vc, k, v), w_out
            if not last:
                mix_c = _attention(qc, kc, vc)
                xc_mixed = xc
        x = _ffn_half(x, mod8, lat_row, 2, norm_g[i, 2], w2_in, w2_out, mix, w_mix)
        if not last:
            xc = _ffn_half(xc_mixed, mod8, ctx_mod_row, 2, norm_g[i, 2], w2_in, w2_out, mix_c, w_mix)
    return x
```

```python
import functools

import numpy as np
import jax
import jax.numpy as jnp
from jax import lax
from jax.experimental import pallas as pl
from jax.experimental.pallas import tpu as pltpu

F32 = jnp.float32
BF16 = jnp.bfloat16

D_MODEL = 1024
GRID_W = 64
D_FF = 2816
FFN_RESIDUAL = 0.5
N_MOD = 9
ROPE_BASE = 10000.0
EPS = 1e-6
POOL_GROUPS = 4
POOL_CG = 64
POOL_WIDTH = POOL_GROUPS * POOL_CG
POOL_WINDOWS = (2, 4, 8, 16)
POOL_HALO = 8
POOL_PAD = 128
RET_HEADS = 6
RET_PAIRS = RET_HEADS // 2
RET_DK = 64
RET_DV = 128
RET_CHUNK = 128
RET_QK_W = RET_HEADS * RET_DK
RET_V_W = RET_HEADS * RET_DV
AB_IN = POOL_WIDTH + 2 * RET_QK_W + 2 * RET_V_W
AB_EXT = AB_IN + 2 * RET_QK_W
MLA_HEADS = 8
MLA_Q_RANK = 384
MLA_KV_RANK = 256
MLA_NOPE = 64
MLA_ROPE = 32
MLA_V = 128
MLA_QK = MLA_NOPE + MLA_ROPE
HEAD_SLAB = 128
V_ROWS = MLA_V + 16
LOG2_E = 1.4426950408889634

LANES = 128
V7X_VMEM_BYTES = 64 * 1024 * 1024
VMEM_LIMIT_BYTES = 56 * 1024 * 1024

ADA_TN = 1536
FFN_TM = 1024
FFN_SUB = 256
MIX_TM = 256
AB_TM = 1024
AB_SUB = 256
MLA_TM = 512
MLA_HEAD_UNROLL = 8
ATT_TQ = 1024
ATT_TK = 512
ATT_GROUP = 4


def _cparams(sem):
    return pltpu.CompilerParams(dimension_semantics=sem, vmem_limit_bytes=VMEM_LIMIT_BYTES)


def _resident(shape):
    nd = len(shape)
    return pl.BlockSpec(shape, lambda *_: (0,) * nd, pipeline_mode=pl.Buffered(1))


def _silu(x):
    return x * (1.0 / (1.0 + jnp.exp(-x)))


def _rms_mod(x, g, shift, scale):
    ms = jnp.mean(x * x, axis=-1, keepdims=True)
    return (x * lax.rsqrt(ms + EPS) * g) * (1.0 + scale) + shift


def _ada_kernel(c_ref, w_ref, b_ref, o_ref):
    s = _silu(c_ref[...])
    o_ref[...] = jnp.dot(s, w_ref[...], precision=lax.Precision.HIGHEST,
                         preferred_element_type=F32) + b_ref[...]


def _ada_modulation(cond8, ada_w, ada_b):
    depth, d, n = ada_w.shape
    return pl.pallas_call(
        _ada_kernel,
        grid=(depth, n // ADA_TN),
        in_specs=[
            pl.BlockSpec((8, d), lambda i, j: (0, 0)),
            pl.BlockSpec((None, d, ADA_TN), lambda i, j: (i, 0, j)),
            pl.BlockSpec((None, 1, ADA_TN), lambda i, j: (i, 0, j)),
        ],
        out_specs=pl.BlockSpec((None, 8, ADA_TN), lambda i, j: (i, 0, j)),
        out_shape=jax.ShapeDtypeStruct((depth, 8, n), F32),
        compiler_params=_cparams(("parallel", "parallel")),
        name="ada_mod",
    )(cond8, ada_w, ada_b.reshape(depth, 1, n))


def _ffn_kernel(*refs, k, mixed):
    if mixed:
        x_ref, mod_ref, g_ref, win_ref, wout_ref, mix_ref, wmix_ref, o_ref = refs
    else:
        x_ref, mod_ref, g_ref, win_ref, wout_ref, o_ref = refs
    tm = x_ref.shape[0]
    sub = min(FFN_SUB, tm)
    for s in range(tm // sub):
        rs = slice(s * sub, (s + 1) * sub)
        x = x_ref[rs, :]
        if mixed:
            x = x + mod_ref[5:6, :] * jnp.dot(mix_ref[rs, :], wmix_ref[...], preferred_element_type=F32)
        h = _rms_mod(x, g_ref[...], mod_ref[3 * k:3 * k + 1, :], mod_ref[3 * k + 1:3 * k + 2, :])
        ab = jnp.dot(h.astype(BF16), win_ref[...], preferred_element_type=F32)
        act = (_silu(ab[:, :D_FF]) * ab[:, D_FF:]).astype(BF16)
        y = jnp.dot(act, wout_ref[...], preferred_element_type=F32)
        o_ref[rs, :] = x + (FFN_RESIDUAL * mod_ref[3 * k + 2:3 * k + 3, :]) * y


def _ffn_half(x, mod8, mod_row, k, g, w_in, w_out, which, mix=None, w_mix=None):
    bsz, seq, d = x.shape
    tm = min(FFN_TM, seq)
    layer, half = which
    pick = lambda shape: pl.BlockSpec((None, None) + tuple(shape[2:]), lambda b, t: (layer, half, 0, 0),
                                      pipeline_mode=pl.Buffered(1))
    in_specs = [
        pl.BlockSpec((None, tm, d), lambda b, t: (b, t, 0)),
        pl.BlockSpec((None, N_MOD, d), lambda b, t: (mod_row(b), 0, 0)),
        pl.BlockSpec((1, d), lambda b, t: (0, 0)),
        pick(w_in.shape),
        pick(w_out.shape),
    ]
    args = [x, mod8, g.reshape(1, d), w_in, w_out]
    if mix is not None:
        in_specs += [pl.BlockSpec((None, tm, mix.shape[-1]), lambda b, t: (b, t, 0)),
                     _resident(w_mix.shape)]
        args += [mix, w_mix]
    return pl.pallas_call(
        functools.partial(_ffn_kernel, k=k, mixed=mix is not None),
        grid=(bsz, seq // tm),
        in_specs=in_specs,
        out_specs=pl.BlockSpec((None, tm, d), lambda b, t: (b, t, 0)),
        out_shape=jax.ShapeDtypeStruct(x.shape, F32),
        compiler_params=_cparams(("parallel", "parallel")),
        name="ffn_half",
    )(*args)


def _pair_tables(lg_row, expo):
    return jnp.exp(lg_row * expo)


def _bd_mask():
    r = lax.broadcasted_iota(jnp.int32, (2 * RET_DK, 2 * RET_DV), 0)
    c = lax.broadcasted_iota(jnp.int32, (2 * RET_DK, 2 * RET_DV), 1)
    same_head = jnp.where(r < RET_DK, 0, 1) == jnp.where(c < RET_DV, 0, 1)
    return jnp.where(same_head, 1.0, 0.0)


def _abproj_kernel(x_ref, mod_ref, g_ref, w_ref, cos_ref, sin_ref, lgq_ref, lgv_ref, s0_ref,
                   p_ref, q_ref, k_ref, v_ref, sg_ref, cb_ref, sfin_ref, st_scr, *, tm, nt):
    t = pl.program_id(1)

    @pl.when(t == 0)
    def _():
        st_scr[...] = s0_ref[...]

    q0, k0 = POOL_WIDTH, POOL_WIDTH + RET_QK_W
    v0, g0 = k0 + RET_QK_W, k0 + RET_QK_W + RET_V_W
    qs0, ks0 = AB_IN, AB_IN + RET_QK_W
    row = lax.broadcasted_iota(jnp.int32, (RET_CHUNK, LANES), 0).astype(F32)
    bdm = _bd_mask()
    k_scale = RET_DK ** -0.5
    sub = min(AB_SUB, tm)
    for s in reversed(range(tm // sub)):
        r0 = s * sub
        ts = slice(r0, r0 + sub)
        h = _rms_mod(x_ref[ts, :], g_ref[...], mod_ref[3:4, :], mod_ref[4:5, :])
        proj = jnp.dot(h.astype(BF16), w_ref[...], preferred_element_type=F32)
        p_ref[ts, :] = proj[:, :POOL_WIDTH]
        v_ref[ts, :] = proj[:, v0:v0 + RET_V_W].astype(BF16)
        sg_ref[ts, :] = _silu(proj[:, g0:g0 + RET_V_W]).astype(BF16)
        cos = cos_ref[ts, :]
        sin = sin_ref[ts, :]
        for j in range(RET_PAIRS):
            sl = slice(LANES * j, LANES * (j + 1))
            qr = proj[:, q0 + LANES * j:q0 + LANES * (j + 1)] * cos \
                + proj[:, qs0 + LANES * j:qs0 + LANES * (j + 1)] * sin
            kr = (proj[:, k0 + LANES * j:k0 + LANES * (j + 1)] * cos
                  + proj[:, ks0 + LANES * j:ks0 + LANES * (j + 1)] * sin) * k_scale
            q_ref[ts, sl] = qr.astype(BF16)
            k_ref[ts, sl] = kr.astype(BF16)
            lgq = lgq_ref[:, sl]
            lgv = lgv_ref[:, 2 * LANES * j:2 * LANES * (j + 1)]
            xi = _pair_tables(lgq, float(RET_CHUNK) - row)
            zeta = _pair_tables(lgq, row)
            dec = jnp.exp(lgv * float(RET_CHUNK))
            vsl = slice(2 * LANES * j, 2 * LANES * (j + 1))
            for c in reversed(range(sub // RET_CHUNK)):
                ls = slice(c * RET_CHUNK, (c + 1) * RET_CHUNK)
                rs = slice(r0 + c * RET_CHUNK, r0 + (c + 1) * RET_CHUNK)
                bd = st_scr[j]
                qx = (qr[ls] * xi).astype(BF16)
                cb_ref[rs, vsl] = jnp.dot(qx, bd.astype(BF16), preferred_element_type=F32)
                kz = (kr[ls] * zeta).astype(BF16)
                upd = lax.dot_general(kz, v_ref[rs, vsl], (((0,), (0,)), ((), ())),
                                      preferred_element_type=F32)
                st_scr[j] = dec * bd + bdm * upd

    @pl.when(t == nt - 1)
    def _():
        sfin_ref[...] = st_scr[...]


def _ab_project(x, mod8, mod_row, g, w_ext, cos, sin, lg_q, lg_v, s0):
    bsz, seq, d = x.shape
    tm = min(AB_TM, seq)
    nt = seq // tm
    rev = lambda b, t: (b, nt - 1 - t, 0)
    tok = lambda w, dt: jax.ShapeDtypeStruct((bsz, seq, w), dt)
    st_shape = (RET_PAIRS, 2 * RET_DK, 2 * RET_DV)
    return pl.pallas_call(
        functools.partial(_abproj_kernel, tm=tm, nt=nt),
        grid=(bsz, nt),
        in_specs=[
            pl.BlockSpec((None, tm, d), rev),
            pl.BlockSpec((None, N_MOD, d), lambda b, t: (mod_row(b), 0, 0)),
            pl.BlockSpec((1, d), lambda b, t: (0, 0)),
            _resident(w_ext.shape),
            pl.BlockSpec((tm, LANES), lambda b, t: (nt - 1 - t, 0)),
            pl.BlockSpec((tm, LANES), lambda b, t: (nt - 1 - t, 0)),
            pl.BlockSpec((1, RET_QK_W), lambda b, t: (0, 0)),
            pl.BlockSpec((1, RET_V_W), lambda b, t: (0, 0)),
            pl.BlockSpec((None,) + st_shape, lambda b, t: (b, 0, 0, 0)),
        ],
        out_specs=[
            pl.BlockSpec((None, tm, POOL_WIDTH), rev),
            pl.BlockSpec((None, tm, RET_QK_W), rev),
            pl.BlockSpec((None, tm, RET_QK_W), rev),
            pl.BlockSpec((None, tm, RET_V_W), rev),
            pl.BlockSpec((None, tm, RET_V_W), rev),
            pl.BlockSpec((None, tm, RET_V_W), rev),
            pl.BlockSpec((None,) + st_shape, lambda b, t: (b, 0, 0, 0)),
        ],
        out_shape=[tok(POOL_WIDTH, F32), tok(RET_QK_W, BF16), tok(RET_QK_W, BF16),
                   tok(RET_V_W, BF16), tok(RET_V_W, BF16), tok(RET_V_W, F32),
                   jax.ShapeDtypeStruct((bsz,) + st_shape, F32)],
        scratch_shapes=[pltpu.VMEM(st_shape, F32)],
        compiler_params=_cparams(("arbitrary", "arbitrary")),
        name="ab_project",
    )(x, mod8, g.reshape(1, d), w_ext, cos, sin, lg_q, lg_v, s0)


def _retout_kernel(x_ref, mod_ref, pprev_ref, pcur_ref, pnext_ref, pm_ref, invc_ref,
                   q_ref, k_ref, v_ref, sg_ref, cb_ref, wpool_ref, pscale_ref, rng_ref, wout_ref,
                   lgfq_ref, lgfv_ref, lgbv_ref, s0_ref,
                   o_ref, sfin_ref, st_scr, ycat_scr, pe_scr, *, tm, nt):
    t = pl.program_id(1)

    @pl.when(t == 0)
    def _():
        st_scr[...] = s0_ref[...]

    pcur = pcur_ref[...]
    pe_scr[0:tm, :] = pcur
    pe_scr[tm:tm + POOL_HALO, :] = pprev_ref[...]
    pe_scr[tm + POOL_HALO:tm + 2 * POOL_HALO, :] = pnext_ref[...]
    pe_scr[tm + 2 * POOL_HALO:, :] = jnp.zeros((POOL_PAD - 2 * POOL_HALO, POOL_WIDTH), F32)
    pe = pe_scr[...]
    pe_hi = pe.astype(BF16)
    pe_lo = (pe - pe_hi.astype(F32)).astype(BF16)
    lane_grp = jnp.right_shift(lax.broadcasted_iota(jnp.int32, (tm, POOL_WIDTH), 1),
                               POOL_CG.bit_length() - 1)
    pooled = jnp.zeros((tm, POOL_WIDTH), F32)
    for gi in range(POOL_GROUPS):
        win = pm_ref[gi]
        tot = jnp.dot(win, pe_hi, preferred_element_type=F32) \
            + jnp.dot(win, pe_lo, preferred_element_type=F32)
        pooled = jnp.where(lane_grp == gi, tot, pooled)
    diffs = pooled * invc_ref[...] - pcur
    pool_y = jnp.dot(diffs.astype(BF16), wpool_ref[...], preferred_element_type=F32) * pscale_ref[...]
    ycat_scr[:, 0:POOL_WIDTH] = pool_y.astype(BF16)

    ri = lax.broadcasted_iota(jnp.int32, (RET_CHUNK, RET_CHUNK), 0)
    ci = lax.broadcasted_iota(jnp.int32, (RET_CHUNK, RET_CHUNK), 1)
    rel = (ri - ci).astype(F32)
    row = lax.broadcasted_iota(jnp.int32, (RET_CHUNK, LANES), 0).astype(F32)
    lane = lax.broadcasted_iota(jnp.int32, (RET_CHUNK, LANES), 1)
    bdm = _bd_mask()
    zero_bf = jnp.zeros((RET_CHUNK, LANES), BF16)
    tables = []
    for j in range(RET_PAIRS):
        sl = slice(LANES * j, LANES * (j + 1))
        vsl = slice(2 * LANES * j, 2 * LANES * (j + 1))
        lgq = lgfq_ref[:, sl]
        xi = _pair_tables(lgq, row + 1.0)
        zeta = _pair_tables(lgq, float(RET_CHUNK - 1) - row)
        dec = jnp.exp(lgfv_ref[:, vsl] * float(RET_CHUNK))
        masks = []
        for a in range(2):
            hs = slice(2 * LANES * j + LANES * a, 2 * LANES * j + LANES * (a + 1))
            mf = jnp.exp(lgfv_ref[:, hs] * jnp.maximum(rel, 0.0))
            mb = jnp.exp(lgbv_ref[:, hs] * jnp.maximum(-rel, 0.0))
            masks.append(jnp.where(rel > 0, mf, jnp.where(rel < 0, mb, 2.0)))
        tables.append((xi, zeta, dec, masks))
    for c in range(tm // RET_CHUNK):
        rs = slice(c * RET_CHUNK, (c + 1) * RET_CHUNK)
        for j in range(RET_PAIRS):
            sl = slice(LANES * j, LANES * (j + 1))
            vsl = slice(2 * LANES * j, 2 * LANES * (j + 1))
            xi, zeta, dec, masks = tables[j]
            q2 = q_ref[rs, sl]
            k2 = k_ref[rs, sl]
            v2 = v_ref[rs, vsl]
            bd = st_scr[j]
            qx = (q2.astype(F32) * xi).astype(BF16)
            ret = jnp.dot(qx, bd.astype(BF16), preferred_element_type=F32) + cb_ref[rs, vsl]
            for a in range(2):
                qa = jnp.where(lane < RET_DK if a == 0 else lane >= RET_DK, q2, zero_bf)
                s = lax.dot_general(qa, k2, (((1,), (1,)), ((), ())), preferred_element_type=F32)
                pa = (s * masks[a]).astype(BF16)
                hs_v = slice(LANES * a, LANES * (a + 1))
                r = ret[:, hs_v] + jnp.dot(pa, v2[:, hs_v], preferred_element_type=F32)
                ms = jnp.mean(r * r, axis=-1, keepdims=True)
                hcol = 2 * LANES * j + LANES * a
                y = (r * lax.rsqrt(ms + EPS) * rng_ref[:, hcol:hcol + LANES]) \
                    * sg_ref[rs, hcol:hcol + LANES].astype(F32)
                ycat_scr[rs, POOL_WIDTH + hcol:POOL_WIDTH + hcol + LANES] = y.astype(BF16)
            kz = (k2.astype(F32) * zeta).astype(BF16)
            upd = lax.dot_general(kz, v2, (((0,), (0,)), ((), ())), preferred_element_type=F32)
            st_scr[j] = dec * bd + bdm * upd
        yy = jnp.dot(ycat_scr[rs, :], wout_ref[...], preferred_element_type=F32)
        o_ref[rs, :] = x_ref[rs, :] + mod_ref[5:6, :] * yy

    @pl.when(t == nt - 1)
    def _():
        sfin_ref[...] = st_scr[...]


def _pool_constants(tm, seq):
    nt = seq // tm
    kinds = [0] if nt == 1 else [0, 1, nt - 1]
    mats = np.zeros((len(kinds), POOL_GROUPS, tm, tm + POOL_PAD), np.float32)
    invc = np.zeros((len(kinds), tm, POOL_WIDTH), np.float32)
    i = np.arange(tm)
    for vi, tile in enumerate(kinds):
        pos = tile * tm + i
        colpos = np.full(tm + POOL_PAD, -1)
        colpos[:tm] = pos
        colpos[tm:tm + POOL_HALO] = tile * tm - POOL_HALO + np.arange(POOL_HALO)
        colpos[tm + POOL_HALO:tm + 2 * POOL_HALO] = (tile + 1) * tm + np.arange(POOL_HALO)
        for gi, w in enumerate(POOL_WINDOWS):
            lo = w // 2
            hi = w - 1 - lo
            start = np.maximum(pos - lo, 0)
            end = np.minimum(pos + hi + 1, seq)
            mats[vi, gi] = (colpos[None, :] >= start[:, None]) & (colpos[None, :] < end[:, None])
            invc[vi, :, gi * POOL_CG:(gi + 1) * POOL_CG] = (1.0 / (end - start))[:, None]
    return jnp.asarray(mats, BF16), jnp.asarray(invc, F32), nt


def _ret_out(x, mod8, mod_row, p, q, k, v, sg, cb, wpool_bd, pscale, rng, w_out, lgf_q, lgf_v, lgb_v, s0):
    bsz, seq, d = x.shape
    tm = min(MIX_TM, seq)
    pm, invc, nt = _pool_constants(tm, seq)
    hb = tm // POOL_HALO
    nhb = seq // POOL_HALO

    def variant(t):
        if nt == 1:
            return 0
        return jnp.where(t == 0, 0, jnp.where(t == nt - 1, 2, 1))

    cur = lambda b, t: (b, t, 0)
    st_shape = (RET_PAIRS, 2 * RET_DK, 2 * RET_DV)
    tokspec = lambda w: pl.BlockSpec((None, tm, w), cur)
    return pl.pallas_call(
        functools.partial(_retout_kernel, tm=tm, nt=nt),
        grid=(bsz, nt),
        in_specs=[
            tokspec(d),
            pl.BlockSpec((None, N_MOD, d), lambda b, t: (mod_row(b), 0, 0)),
            pl.BlockSpec((None, POOL_HALO, POOL_WIDTH), lambda b, t: (b, jnp.maximum(t * hb - 1, 0), 0)),
            tokspec(POOL_WIDTH),
            pl.BlockSpec((None, POOL_HALO, POOL_WIDTH),
                         lambda b, t: (b, jnp.minimum((t + 1) * hb, nhb - 1), 0)),
            pl.BlockSpec((None, POOL_GROUPS, tm, tm + POOL_PAD), lambda b, t: (variant(t), 0, 0, 0)),
            pl.BlockSpec((None, tm, POOL_WIDTH), lambda b, t: (variant(t), 0, 0)),
            tokspec(RET_QK_W), tokspec(RET_QK_W), tokspec(RET_V_W), tokspec(RET_V_W), tokspec(RET_V_W),
            _resident(wpool_bd.shape),
            pl.BlockSpec((1, POOL_WIDTH), lambda b, t: (0, 0)),
            pl.BlockSpec((1, RET_V_W), lambda b, t: (0, 0)),
            _resident(w_out.shape),
            pl.BlockSpec((1, RET_QK_W), lambda b, t: (0, 0)),
            pl.BlockSpec((1, RET_V_W), lambda b, t: (0, 0)),
            pl.BlockSpec((1, RET_V_W), lambda b, t: (0, 0)),
            pl.BlockSpec((None,) + st_shape, lambda b, t: (b, 0, 0, 0)),
        ],
        out_specs=[
            tokspec(d),
            pl.BlockSpec((None,) + st_shape, lambda b, t: (b, 0, 0, 0)),
        ],
        out_shape=[jax.ShapeDtypeStruct(x.shape, F32),
                   jax.ShapeDtypeStruct((bsz,) + st_shape, F32)],
        scratch_shapes=[pltpu.VMEM(st_shape, F32),
                        pltpu.VMEM((tm, d), BF16),
                        pltpu.VMEM((tm + POOL_PAD, POOL_WIDTH), F32)],
        compiler_params=_cparams(("arbitrary", "arbitrary")),
        name="ret_out",
    )(x, mod8, p, p, p, pm, invc, q, k, v, sg, cb, wpool_bd, pscale, rng, w_out,
      lgf_q, lgf_v, lgb_v, s0)


def _head_norm_rope(main, aux, g_main, g_aux):
    ms = jnp.sum(main * main, axis=-1, keepdims=True) * (1.0 / MLA_QK)
    r = lax.rsqrt(ms + EPS)
    return (main * r) * g_main + (aux * r) * g_aux


def _mlaproj_kernel(x_ref, mod_ref, g_ref, win_ref, qag_ref, kvag_ref, wqb_ref, wkv_ref,
                    qg_ref, qgs_ref, kg_ref, kgs_ref, qa_t_ref, qb_t_ref, ka_t_ref, kb_t_ref,
                    q_ref, k_ref, v_ref, qan_scr, kvn_scr, kr_scr, gt_scr):
    h = _rms_mod(x_ref[...], g_ref[...], mod_ref[3:4, :], mod_ref[4:5, :])
    proj = jnp.dot(h.astype(BF16), win_ref[...], preferred_element_type=F32)
    qa = proj[:, :MLA_Q_RANK]
    kva = proj[:, MLA_Q_RANK:MLA_Q_RANK + MLA_KV_RANK]
    r0 = MLA_Q_RANK + MLA_KV_RANK
    kr_scr[0] = proj[:, r0:r0 + HEAD_SLAB]
    kr_scr[1] = proj[:, r0 + HEAD_SLAB:r0 + 2 * HEAD_SLAB]
    qan = qa * lax.rsqrt(jnp.mean(qa * qa, axis=-1, keepdims=True) + EPS) * qag_ref[...]
    kvn = kva * lax.rsqrt(jnp.mean(kva * kva, axis=-1, keepdims=True) + EPS) * kvag_ref[...]
    qan_scr[...] = qan.astype(BF16)
    kvn_scr[...] = kvn.astype(BF16)
    gt_scr[0] = qg_ref[...] * qa_t_ref[...]
    gt_scr[1] = qgs_ref[...] * qb_t_ref[...]
    gt_scr[2] = kg_ref[...] * ka_t_ref[...]
    gt_scr[3] = kgs_ref[...] * kb_t_ref[...]
    tm = proj.shape[0]
    ones_rows = jnp.where(lax.broadcasted_iota(jnp.int32, (V_ROWS - MLA_V, tm), 0) == 0,
                          1.0, 0.0).astype(BF16)

    def head(hd, carry):
        c0 = pl.multiple_of(hd * (2 * HEAD_SLAB), 2 * HEAD_SLAB)
        qh = jnp.dot(qan_scr[...], wqb_ref[:, pl.ds(c0, 2 * HEAD_SLAB)], preferred_element_type=F32)
        kvh = jnp.dot(kvn_scr[...], wkv_ref[:, pl.ds(c0, 2 * HEAD_SLAB)], preferred_element_type=F32)
        q_ref[hd] = _head_norm_rope(qh[:, :HEAD_SLAB], qh[:, HEAD_SLAB:], gt_scr[0], gt_scr[1]).T.astype(BF16)
        k_ref[hd] = _head_norm_rope(kvh[:, :HEAD_SLAB] + kr_scr[0], kr_scr[1],
                                    gt_scr[2], gt_scr[3]).astype(BF16)
        v_ref[hd, 0:MLA_V, :] = kvh[:, HEAD_SLAB:].T.astype(BF16)
        v_ref[hd, MLA_V:V_ROWS, :] = ones_rows
        return carry
    lax.fori_loop(0, MLA_HEADS, head, 0, unroll=MLA_HEAD_UNROLL)


def _mla_project(x, mod8, mod_row, g, w_in, qa_g, kva_g, w_qb, w_kv, qn_g, qn_gs, kn_g, kn_gs,
                 q_ta, q_tb, k_ta, k_tb):
    bsz, seq, d = x.shape
    tm = min(MLA_TM, seq)
    head_out = jax.ShapeDtypeStruct((bsz, MLA_HEADS, seq, HEAD_SLAB), BF16)
    hspec = pl.BlockSpec((None, MLA_HEADS, tm, HEAD_SLAB), lambda b, t: (b, 0, t, 0))
    tspec = pl.BlockSpec((tm, HEAD_SLAB), lambda b, t: (t, 0))
    row = lambda n: pl.BlockSpec((1, n), lambda b, t: (0, 0))
    return pl.pallas_call(
        _mlaproj_kernel,
        grid=(bsz, seq // tm),
        in_specs=[
            pl.BlockSpec((None, tm, d), lambda b, t: (b, t, 0)),
            pl.BlockSpec((None, N_MOD, d), lambda b, t: (mod_row(b), 0, 0)),
            row(d),
            _resident(w_in.shape), row(MLA_Q_RANK), row(MLA_KV_RANK),
            _resident(w_qb.shape), _resident(w_kv.shape),
            row(HEAD_SLAB), row(HEAD_SLAB), row(HEAD_SLAB), row(HEAD_SLAB),
            tspec, tspec, tspec, tspec,
        ],
        out_specs=[pl.BlockSpec((None, MLA_HEADS, HEAD_SLAB, tm), lambda b, t: (b, 0, 0, t)),
                   hspec,
                   pl.BlockSpec((None, MLA_HEADS, V_ROWS, tm), lambda b, t: (b, 0, 0, t))],
        out_shape=[jax.ShapeDtypeStruct((bsz, MLA_HEADS, HEAD_SLAB, seq), BF16),
                   head_out,
                   jax.ShapeDtypeStruct((bsz, MLA_HEADS, V_ROWS, seq), BF16)],
        scratch_shapes=[pltpu.VMEM((tm, MLA_Q_RANK), BF16), pltpu.VMEM((tm, MLA_KV_RANK), BF16),
                        pltpu.VMEM((2, tm, HEAD_SLAB), F32), pltpu.VMEM((4, tm, HEAD_SLAB), F32)],
        compiler_params=_cparams(("parallel", "parallel")),
        name="mla_project",
    )(x, mod8, g.reshape(1, d), w_in, qa_g, kva_g, w_qb, w_kv, qn_g, qn_gs, kn_g, kn_gs,
      q_ta, q_tb, k_ta, k_tb)


def _attn_kernel(*refs, n_lat_blocks, tk, group_blocks):
    if n_lat_blocks:
        q_ref, kc_ref, vct_ref, k_ref, vt_ref, o_ref, m_scr, acc_scr, s_scr, mb_scr = refs
    else:
        q_ref, kc_ref, vct_ref, o_ref, m_scr, acc_scr = refs
    q = q_ref[...]

    def scores(kb):
        return jnp.dot(kb, q, preferred_element_type=F32)

    def colmax(st):
        return jnp.max(st, axis=0, keepdims=True)

    def produce(slot, kb):
        st = scores(kb)
        s_scr[slot, 0:kb.shape[0], :] = st
        mb_scr[slot] = colmax(st)

    def softmax_pv(st, m_blk, vtb, first):
        if first:
            m_new = m_blk
        else:
            m_prev = m_scr[...]
            m_new = jnp.maximum(m_prev, m_blk)
        p = jnp.exp2(st - m_new).astype(BF16)
        pv = jnp.dot(vtb, p, preferred_element_type=F32)
        if first:
            acc_scr[...] = pv
        else:
            acc_scr[...] = jnp.exp2(m_prev - m_new) * acc_scr[...] + pv
        m_scr[...] = m_new

    if n_lat_blocks:
        kblk = lambda o: k_ref[pl.ds(o, tk), :]
        vblk = lambda o: vt_ref[:, pl.ds(o, tk)]
        lc = kc_ref.shape[0]
        m_scr[...] = jnp.full(m_scr.shape, -jnp.inf, F32)
        acc_scr[...] = jnp.zeros(acc_scr.shape, F32)
        produce(0, k_ref[0:tk, :])

        def group(i, carry):
            base = i * (group_blocks * tk)
            for u in range(group_blocks):
                cur = pl.multiple_of(base + u * tk, tk)
                nxt = pl.multiple_of(base + (u + 1) * tk, tk)
                produce((u + 1) % 2, kblk(nxt))
                softmax_pv(s_scr[u % 2], mb_scr[u % 2], vblk(cur), False)
            return carry
        lax.fori_loop(0, n_lat_blocks // group_blocks - 1, group, 0)
        e0 = n_lat_blocks - group_blocks
        for u in range(group_blocks):
            cur = (e0 + u) * tk
            if u + 1 < group_blocks:
                produce((u + 1) % 2, k_ref[cur + tk:cur + 2 * tk, :])
            else:
                produce((u + 1) % 2, kc_ref[...])
            softmax_pv(s_scr[u % 2], mb_scr[u % 2], vt_ref[:, cur:cur + tk], False)
        c_slot = group_blocks % 2
        softmax_pv(s_scr[c_slot, 0:lc, :], mb_scr[c_slot], vct_ref[...], False)
    else:
        st_c = scores(kc_ref[...])
        softmax_pv(st_c, colmax(st_c), vct_ref[...], True)
    acc = acc_scr[...]
    out_t = acc[0:MLA_V, :] * (1.0 / acc[MLA_V:MLA_V + 1, :])
    o_ref[...] = out_t.T.astype(o_ref.dtype)


def _attention(q, kc, vct, k=None, vt=None):
    bsz, nh, hs, lq = q.shape
    lc = kc.shape[2]
    tq = min(ATT_TQ, lq)
    qspec = pl.BlockSpec((None, None, hs, tq), lambda b, h, i: (b, h, 0, i))
    kfull = lambda n: pl.BlockSpec((None, None, n, hs), lambda b, h, i: (b, h, 0, 0))
    vfull = lambda n: pl.BlockSpec((None, None, V_ROWS, n), lambda b, h, i: (b, h, 0, 0))
    in_specs = [qspec, kfull(lc), vfull(lc)]
    args = [q, kc, vct]
    n_lat = 0
    grp = 0
    tk = ATT_TK
    if k is not None:
        lk = k.shape[2]
        tk = min(ATT_TK, lk)
        n_lat = lk // tk
        grp = min(ATT_GROUP, n_lat)
        assert n_lat % grp == 0, "latent key blocks are consumed in whole groups"
        assert lc <= tk, "the context block reuses a latent score slot"
        in_specs += [kfull(lk), vfull(lk)]
        args += [k, vt]
    scratch = [pltpu.VMEM((1, tq), F32), pltpu.VMEM((V_ROWS, tq), F32)]
    if n_lat:
        scratch += [pltpu.VMEM((2, tk, tq), F32), pltpu.VMEM((2, 1, tq), F32)]
    return pl.pallas_call(
        functools.partial(_attn_kernel, n_lat_blocks=n_lat, tk=tk, group_blocks=grp),
        grid=(bsz, nh, lq // tq),
        in_specs=in_specs,
        out_specs=pl.BlockSpec((None, tq, hs), lambda b, h, i: (b, i, h)),
        out_shape=jax.ShapeDtypeStruct((bsz, lq, nh * hs), BF16),
        scratch_shapes=scratch,
        compiler_params=_cparams(("parallel", "parallel", "arbitrary")),
        name="mla_attention",
    )(*args)


def _pair_swap_cols(w):
    w2 = w.reshape(w.shape[:-1] + (w.shape[-1] // 2, 2))
    return jnp.stack([-w2[..., 1], w2[..., 0]], axis=-1).reshape(w.shape)


def _rope_angles(seq, rot_dim):
    pos = jnp.arange(seq)
    row = (pos // GRID_W).astype(F32)
    col = (pos % GRID_W).astype(F32)
    n_freq = rot_dim // 4
    inv = ROPE_BASE ** (-jnp.arange(n_freq, dtype=F32) / n_freq)
    ang = jnp.concatenate([row[:, None] * inv, col[:, None] * inv], axis=-1)
    return jnp.repeat(jnp.cos(ang), 2, axis=-1), jnp.repeat(jnp.sin(ang), 2, axis=-1)


def _ret_tables(seq, ctx_len):
    cos, sin = _rope_angles(seq, RET_DK)
    cos2, sin2 = jnp.tile(cos, (1, 2)), jnp.tile(sin, (1, 2))
    return cos2, sin2, jnp.ones((ctx_len, LANES), F32), jnp.zeros((ctx_len, LANES), F32)


def _mla_tables(seq, ctx_len):
    cos, sin = _rope_angles(seq, MLA_ROPE)
    one = jnp.ones((seq, MLA_NOPE), F32)
    zero_tail = jnp.zeros((seq, HEAD_SLAB - MLA_QK), F32)
    zero_head = jnp.zeros((seq, MLA_NOPE), F32)
    ta = jnp.concatenate([one, cos, zero_tail], axis=-1)
    tb = jnp.concatenate([zero_head, sin, zero_tail], axis=-1)
    ident = jnp.concatenate([jnp.ones((ctx_len, MLA_QK), F32),
                             jnp.zeros((ctx_len, HEAD_SLAB - MLA_QK), F32)], axis=-1)
    return ta, tb, ident, jnp.zeros((ctx_len, HEAD_SLAB), F32)


def _rope_slab(rope_cols):
    lead = rope_cols.shape[:-1]
    return jnp.concatenate([jnp.zeros(lead + (MLA_NOPE,), F32), rope_cols,
                            jnp.zeros(lead + (HEAD_SLAB - MLA_QK,), F32)], axis=-1)


def _slab_gains(gvec):
    rope = gvec[MLA_NOPE:].reshape(MLA_ROPE // 2, 2)
    swapped = jnp.stack([rope[:, 1], rope[:, 0]], axis=-1).reshape(MLA_ROPE)
    main = jnp.concatenate([gvec, jnp.zeros((HEAD_SLAB - MLA_QK,), F32)])
    return main.reshape(1, HEAD_SLAB), _rope_slab(swapped).reshape(1, HEAD_SLAB)


def _prep_even(ab_w_in, pool_w, ab_w_out):
    wq = ab_w_in[:, POOL_WIDTH:POOL_WIDTH + RET_QK_W]
    wk = ab_w_in[:, POOL_WIDTH + RET_QK_W:POOL_WIDTH + 2 * RET_QK_W]
    w_ext = jnp.concatenate([ab_w_in, _pair_swap_cols(wq), _pair_swap_cols(wk)], axis=-1).astype(BF16)
    wpool_bd = jnp.zeros((POOL_WIDTH, POOL_WIDTH), F32)
    for gi in range(POOL_GROUPS):
        s = slice(gi * POOL_CG, (gi + 1) * POOL_CG)
        wpool_bd = wpool_bd.at[s, s].set(pool_w[gi])
    return w_ext, wpool_bd.astype(BF16), ab_w_out.astype(BF16)


def _prep_odd(w_in, w_qb, w_kvb, w_out):
    wkr = w_in[:, MLA_Q_RANK + MLA_KV_RANK:]
    w_in_ext = jnp.concatenate([w_in[:, :MLA_Q_RANK + MLA_KV_RANK], _rope_slab(wkr),
                                _rope_slab(_pair_swap_cols(wkr))], axis=-1).astype(BF16)
    wq3 = w_qb.reshape(MLA_Q_RANK, MLA_HEADS, MLA_QK)
    wq_main = jnp.concatenate([wq3, jnp.zeros((MLA_Q_RANK, MLA_HEADS, HEAD_SLAB - MLA_QK), F32)], axis=-1)
    wq_aux = _rope_slab(_pair_swap_cols(wq3[..., MLA_NOPE:]))
    w_qb_ext = jnp.concatenate([wq_main, wq_aux], axis=-1).reshape(
        MLA_Q_RANK, MLA_HEADS * 2 * HEAD_SLAB).astype(BF16)
    wkv3 = w_kvb.reshape(MLA_KV_RANK, MLA_HEADS, MLA_NOPE + MLA_V)
    wk_slab = jnp.concatenate([wkv3[..., :MLA_NOPE],
                               jnp.zeros((MLA_KV_RANK, MLA_HEADS, HEAD_SLAB - MLA_NOPE), F32)], axis=-1)
    w_kv_ext = jnp.concatenate([wk_slab, wkv3[..., MLA_NOPE:]], axis=-1).reshape(
        MLA_KV_RANK, MLA_HEADS * 2 * HEAD_SLAB).astype(BF16)
    return w_in_ext, w_qb_ext, w_kv_ext, w_out.astype(BF16)


def kernel(x, c, ctx, c_ctx, ada_w, ada_b, norm_g, ffn_w_in, ffn_w_out, ab_w_in, pool_w, pool_scale,
           ret_log_decay, ret_norm_g, ab_w_out, mla_w_in, mla_qa_g, mla_kva_g, mla_w_qb, mla_w_kvb,
           mla_qn_g, mla_kn_g, mla_w_out):
    bsz, seq, d = x.shape
    ctx_len = ctx.shape[1]
    depth = ada_w.shape[0]
    ctx_row = bsz
    cond8 = jnp.zeros((8, d), F32).at[:bsz].set(c).at[ctx_row].set(c_ctx)
    mod_all = _ada_modulation(cond8, ada_w, ada_b).reshape(depth, 8, N_MOD, d)
    lat_row = lambda b: b
    ctx_mod_row = lambda b: ctx_row

    ret_cos, ret_sin, ret_cos_c, ret_sin_c = _ret_tables(seq, ctx_len)
    m_ta, m_tb, m_ta_c, m_tb_c = _mla_tables(seq, ctx_len)
    q_scale = MLA_QK ** -0.5 * LOG2_E
    zero_state = jnp.zeros((bsz, RET_PAIRS, 2 * RET_DK, 2 * RET_DV), F32)
    ffn_in_b, ffn_out_b = ffn_w_in.astype(BF16), ffn_w_out.astype(BF16)

    xc = ctx
    for i in range(depth):
        last = i == depth - 1
        j = i // 2
        mod8 = mod_all[i]
        x = _ffn_half(x, mod8, lat_row, 0, norm_g[i, 0], ffn_in_b, ffn_out_b, (i, 0))
        xc = _ffn_half(xc, mod8, ctx_mod_row, 0, norm_g[i, 0], ffn_in_b, ffn_out_b, (i, 0))
        mix = mix_c = w_mix = None
        if i % 2 == 0:
            w_ext, wpool_bd, w_out = _prep_even(ab_w_in[j], pool_w[j], ab_w_out[j])
            lg = ret_log_decay[j]
            lgf_q = jnp.repeat(lg[0], RET_DK).reshape(1, RET_QK_W)
            lgb_q = jnp.repeat(lg[1], RET_DK).reshape(1, RET_QK_W)
            lgf_v = jnp.repeat(lg[0], RET_DV).reshape(1, RET_V_W)
            lgb_v = jnp.repeat(lg[1], RET_DV).reshape(1, RET_V_W)
            pscale = pool_scale[j].reshape(1, POOL_WIDTH)
            rng = ret_norm_g[j].reshape(1, RET_V_W)
            pc, qc, kc, vc, sgc, cbc, state_b = _ab_project(
                xc, mod8, ctx_mod_row, norm_g[i, 1], w_ext, ret_cos_c, ret_sin_c, lgb_q, lgb_v, zero_state)
            xc_mixed, state_f = _ret_out(xc, mod8, ctx_mod_row, pc, qc, kc, vc, sgc, cbc, wpool_bd, pscale,
                                         rng, w_out, lgf_q, lgf_v, lgb_v, zero_state)
            p, q, k, v, sg, cb, _ = _ab_project(
                x, mod8, lat_row, norm_g[i, 1], w_ext, ret_cos, ret_sin, lgb_q, lgb_v, state_b)
            x, _ = _ret_out(x, mod8, lat_row, p, q, k, v, sg, cb, wpool_bd, pscale, rng, w_out,
                            lgf_q, lgf_v, lgb_v, state_f)
        else:
            w_in_ext, w_qb_ext, w_kv_ext, w_out = _prep_odd(mla_w_in[j], mla_w_qb[j], mla_w_kvb[j],
                                                            mla_w_out[j])
            qa_g = mla_qa_g[j].reshape(1, MLA_Q_RANK)
            kva_g = mla_kva_g[j].reshape(1, MLA_KV_RANK)
            qn_g, qn_gs = _slab_gains(mla_qn_g[j])
            kn_g, kn_gs = _slab_gains(mla_kn_g[j])
            qc, kc, vc = _mla_project(xc, mod8, ctx_mod_row, norm_g[i, 1], w_in_ext, qa_g, kva_g, w_qb_ext,
                                      w_kv_ext, qn_g, qn_gs, kn_g, kn_gs,
                                      m_ta_c * q_scale, m_tb_c, m_ta_c, m_tb_c)
            q, k, v = _mla_project(x, mod8, lat_row, norm_g[i, 1], w_in_ext, qa_g, kva_g, w_qb_ext,
                                   w_kv_ext, qn_g, qn_gs, kn_g, kn_gs,
                                   m_ta * q_scale, m_tb * q_scale, m_ta, m_tb)
            mix, w_mix = _attention(q, kc, vc, k, v), w_out
            if not last:
                mix_c = _attention(qc, kc, vc)
                xc_mixed = xc
        x = _ffn_half(x, mod8, lat_row, 2, norm_g[i, 2], ffn_in_b, ffn_out_b, (i, 1), mix, w_mix)
        if not last:
            xc = _ffn_half(xc_mixed, mod8, ctx_mod_row, 2, norm_g[i, 2], ffn_in_b, ffn_out_b, (i, 1),
                           mix_c, w_mix)
    return x
```

```python
import functools

import numpy as np
import jax
import jax.numpy as jnp
from jax import lax
from jax.experimental import pallas as pl
from jax.experimental.pallas import tpu as pltpu

F32 = jnp.float32
BF16 = jnp.bfloat16

D_MODEL = 1024
GRID_W = 64
D_FF = 2816
FFN_RESIDUAL = 0.5
N_MOD = 9
ROPE_BASE = 10000.0
EPS = 1e-6
POOL_GROUPS = 4
POOL_CG = 64
POOL_WIDTH = POOL_GROUPS * POOL_CG
POOL_WINDOWS = (2, 4, 8, 16)
POOL_HALO = 8
POOL_PAD = 128
RET_HEADS = 6
RET_PAIRS = RET_HEADS // 2
RET_DK = 64
RET_DV = 128
RET_CHUNK = 128
RET_QK_W = RET_HEADS * RET_DK
RET_V_W = RET_HEADS * RET_DV
AB_IN = POOL_WIDTH + 2 * RET_QK_W + 2 * RET_V_W
AB_EXT = AB_IN + 2 * RET_QK_W
MLA_HEADS = 8
MLA_Q_RANK = 384
MLA_KV_RANK = 256
MLA_NOPE = 64
MLA_ROPE = 32
MLA_V = 128
MLA_QK = MLA_NOPE + MLA_ROPE
HEAD_SLAB = 128
V_ROWS = MLA_V + 16
LOG2_E = 1.4426950408889634

LANES = 128
V7X_VMEM_BYTES = 64 * 1024 * 1024
VMEM_LIMIT_BYTES = 56 * 1024 * 1024

ADA_TN = 1536
FFN_TM = 1024
FFN_SUB = 256
MIX_TM = 256
AB_TM = 1024
AB_SUB = 256
MLA_TM = 512
MLA_HEAD_UNROLL = 8
ATT_TQ = 1024
ATT_TK = 512
ATT_GROUP = 4


def _cparams(sem):
    return pltpu.CompilerParams(dimension_semantics=sem, vmem_limit_bytes=VMEM_LIMIT_BYTES)


def _resident(shape):
    nd = len(shape)
    return pl.BlockSpec(shape, lambda *_: (0,) * nd, pipeline_mode=pl.Buffered(1))


def _silu(x):
    return x * (1.0 / (1.0 + jnp.exp(-x)))


def _rms_mod(x, g, shift, scale):
    ms = jnp.mean(x * x, axis=-1, keepdims=True)
    return (x * lax.rsqrt(ms + EPS) * g) * (1.0 + scale) + shift


def _ada_kernel(c_ref, w_ref, b_ref, o_ref):
    s = _silu(c_ref[...])
    o_ref[...] = jnp.dot(s, w_ref[...], precision=lax.Precision.HIGHEST,
                         preferred_element_type=F32) + b_ref[...]


def _ada_modulation(cond8, ada_w, ada_b):
    depth, d, n = ada_w.shape
    return pl.pallas_call(
        _ada_kernel,
        grid=(depth, n // ADA_TN),
        in_specs=[
            pl.BlockSpec((8, d), lambda i, j: (0, 0)),
            pl.BlockSpec((None, d, ADA_TN), lambda i, j: (i, 0, j)),
            pl.BlockSpec((None, 1, ADA_TN), lambda i, j: (i, 0, j)),
        ],
        out_specs=pl.BlockSpec((None, 8, ADA_TN), lambda i, j: (i, 0, j)),
        out_shape=jax.ShapeDtypeStruct((depth, 8, n), F32),
        compiler_params=_cparams(("parallel", "parallel")),
        name="ada_mod",
    )(cond8, ada_w, ada_b.reshape(depth, 1, n))


def _ffn_kernel(*refs, k, mixed):
    if mixed:
        x_ref, mod_ref, g_ref, win_ref, wout_ref, mix_ref, wmix_ref, o_ref = refs
    else:
        x_ref, mod_ref, g_ref, win_ref, wout_ref, o_ref = refs
    tm = x_ref.shape[0]
    sub = min(FFN_SUB, tm)
    for s in range(tm // sub):
        rs = slice(s * sub, (s + 1) * sub)
        x = x_ref[rs, :]
        if mixed:
            x = x + mod_ref[5:6, :] * jnp.dot(mix_ref[rs, :], wmix_ref[...], preferred_element_type=F32)
        h = _rms_mod(x, g_ref[...], mod_ref[3 * k:3 * k + 1, :], mod_ref[3 * k + 1:3 * k + 2, :])
        ab = jnp.dot(h.astype(BF16), win_ref[...], preferred_element_type=F32)
        act = (_silu(ab[:, :D_FF]) * ab[:, D_FF:]).astype(BF16)
        y = jnp.dot(act, wout_ref[...], preferred_element_type=F32)
        o_ref[rs, :] = x + (FFN_RESIDUAL * mod_ref[3 * k + 2:3 * k + 3, :]) * y


def _ffn_half(x, mod8, mod_row, k, g, w_in, w_out, which, mix=None, w_mix=None):
    bsz, seq, d = x.shape
    tm = min(FFN_TM, seq)
    layer, half = which
    pick = lambda shape: pl.BlockSpec((None, None) + tuple(shape[2:]), lambda b, t: (layer, half, 0, 0),
                                      pipeline_mode=pl.Buffered(1))
    in_specs = [
        pl.BlockSpec((None, tm, d), lambda b, t: (b, t, 0)),
        pl.BlockSpec((None, N_MOD, d), lambda b, t: (mod_row(b), 0, 0)),
        pl.BlockSpec((1, d), lambda b, t: (0, 0)),
        pick(w_in.shape),
        pick(w_out.shape),
    ]
    args = [x, mod8, g.reshape(1, d), w_in, w_out]
    if mix is not None:
        in_specs += [pl.BlockSpec((None, tm, mix.shape[-1]), lambda b, t: (b, t, 0)),
                     _resident(w_mix.shape)]
        args += [mix, w_mix]
    return pl.pallas_call(
        functools.partial(_ffn_kernel, k=k, mixed=mix is not None),
        grid=(bsz, seq // tm),
        in_specs=in_specs,
        out_specs=pl.BlockSpec((None, tm, d), lambda b, t: (b, t, 0)),
        out_shape=jax.ShapeDtypeStruct(x.shape, F32),
        compiler_params=_cparams(("parallel", "parallel")),
        name="ffn_half",
    )(*args)


def _pair_tables(lg_row, expo):
    return jnp.exp(lg_row * expo)


def _bd_mask():
    r = lax.broadcasted_iota(jnp.int32, (2 * RET_DK, 2 * RET_DV), 0)
    c = lax.broadcasted_iota(jnp.int32, (2 * RET_DK, 2 * RET_DV), 1)
    same_head = jnp.where(r < RET_DK, 0, 1) == jnp.where(c < RET_DV, 0, 1)
    return jnp.where(same_head, 1.0, 0.0)


def _abproj_kernel(x_ref, mod_ref, g_ref, w_ref, cos_ref, sin_ref, lgq_ref, lgv_ref, s0_ref,
                   p_ref, q_ref, k_ref, v_ref, sg_ref, cb_ref, sfin_ref, st_scr, *, tm, nt):
    t = pl.program_id(1)

    @pl.when(t == 0)
    def _():
        st_scr[...] = s0_ref[...]

    q0, k0 = POOL_WIDTH, POOL_WIDTH + RET_QK_W
    v0, g0 = k0 + RET_QK_W, k0 + RET_QK_W + RET_V_W
    qs0, ks0 = AB_IN, AB_IN + RET_QK_W
    row = lax.broadcasted_iota(jnp.int32, (RET_CHUNK, LANES), 0).astype(F32)
    bdm = _bd_mask()
    k_scale = RET_DK ** -0.5
    sub = min(AB_SUB, tm)
    for s in reversed(range(tm // sub)):
        r0 = s * sub
        ts = slice(r0, r0 + sub)
        h = _rms_mod(x_ref[ts, :], g_ref[...], mod_ref[3:4, :], mod_ref[4:5, :])
        proj = jnp.dot(h.astype(BF16), w_ref[...], preferred_element_type=F32)
        p_ref[ts, :] = proj[:, :POOL_WIDTH]
        v_ref[ts, :] = proj[:, v0:v0 + RET_V_W].astype(BF16)
        sg_ref[ts, :] = _silu(proj[:, g0:g0 + RET_V_W]).astype(BF16)
        cos = cos_ref[ts, :]
        sin = sin_ref[ts, :]
        for j in range(RET_PAIRS):
            sl = slice(LANES * j, LANES * (j + 1))
            qr = proj[:, q0 + LANES * j:q0 + LANES * (j + 1)] * cos \
                + proj[:, qs0 + LANES * j:qs0 + LANES * (j + 1)] * sin
            kr = (proj[:, k0 + LANES * j:k0 + LANES * (j + 1)] * cos
                  + proj[:, ks0 + LANES * j:ks0 + LANES * (j + 1)] * sin) * k_scale
            q_ref[ts, sl] = qr.astype(BF16)
            k_ref[ts, sl] = kr.astype(BF16)
            lgq = lgq_ref[:, sl]
            lgv = lgv_ref[:, 2 * LANES * j:2 * LANES * (j + 1)]
            xi = _pair_tables(lgq, float(RET_CHUNK) - row)
            zeta = _pair_tables(lgq, row)
            dec = jnp.exp(lgv * float(RET_CHUNK))
            vsl = slice(2 * LANES * j, 2 * LANES * (j + 1))
            for c in reversed(range(sub // RET_CHUNK)):
                ls = slice(c * RET_CHUNK, (c + 1) * RET_CHUNK)
                rs = slice(r0 + c * RET_CHUNK, r0 + (c + 1) * RET_CHUNK)
                bd = st_scr[j]
                qx = (qr[ls] * xi).astype(BF16)
                cb_ref[rs, vsl] = jnp.dot(qx, bd.astype(BF16), preferred_element_type=F32)
                kz = (kr[ls] * zeta).astype(BF16)
                upd = lax.dot_general(kz, v_ref[rs, vsl], (((0,), (0,)), ((), ())),
                                      preferred_element_type=F32)
                st_scr[j] = dec * bd + bdm * upd

    @pl.when(t == nt - 1)
    def _():
        sfin_ref[...] = st_scr[...]


def _ab_project(x, mod8, mod_row, g, w_ext, cos, sin, lg_q, lg_v, s0):
    bsz, seq, d = x.shape
    tm = min(AB_TM, seq)
    nt = seq // tm
    rev = lambda b, t: (b, nt - 1 - t, 0)
    tok = lambda w, dt: jax.ShapeDtypeStruct((bsz, seq, w), dt)
    st_shape = (RET_PAIRS, 2 * RET_DK, 2 * RET_DV)
    return pl.pallas_call(
        functools.partial(_abproj_kernel, tm=tm, nt=nt),
        grid=(bsz, nt),
        in_specs=[
            pl.BlockSpec((None, tm, d), rev),
            pl.BlockSpec((None, N_MOD, d), lambda b, t: (mod_row(b), 0, 0)),
            pl.BlockSpec((1, d), lambda b, t: (0, 0)),
            _resident(w_ext.shape),
            pl.BlockSpec((tm, LANES), lambda b, t: (nt - 1 - t, 0)),
            pl.BlockSpec((tm, LANES), lambda b, t: (nt - 1 - t, 0)),
            pl.BlockSpec((1, RET_QK_W), lambda b, t: (0, 0)),
            pl.BlockSpec((1, RET_V_W), lambda b, t: (0, 0)),
            pl.BlockSpec((None,) + st_shape, lambda b, t: (b, 0, 0, 0)),
        ],
        out_specs=[
            pl.BlockSpec((None, tm, POOL_WIDTH), rev),
            pl.BlockSpec((None, tm, RET_QK_W), rev),
            pl.BlockSpec((None, tm, RET_QK_W), rev),
            pl.BlockSpec((None, tm, RET_V_W), rev),
            pl.BlockSpec((None, tm, RET_V_W), rev),
            pl.BlockSpec((None, tm, RET_V_W), rev),
            pl.BlockSpec((None,) + st_shape, lambda b, t: (b, 0, 0, 0)),
        ],
        out_shape=[tok(POOL_WIDTH, F32), tok(RET_QK_W, BF16), tok(RET_QK_W, BF16),
                   tok(RET_V_W, BF16), tok(RET_V_W, BF16), tok(RET_V_W, F32),
                   jax.ShapeDtypeStruct((bsz,) + st_shape, F32)],
        scratch_shapes=[pltpu.VMEM(st_shape, F32)],
        compiler_params=_cparams(("arbitrary", "arbitrary")),
        name="ab_project",
    )(x, mod8, g.reshape(1, d), w_ext, cos, sin, lg_q, lg_v, s0)


def _retout_kernel(x_ref, mod_ref, pprev_ref, pcur_ref, pnext_ref, pm_ref, invc_ref,
                   q_ref, k_ref, v_ref, sg_ref, cb_ref, wpool_ref, pscale_ref, rng_ref, wout_ref,
                   lgfq_ref, lgfv_ref, lgbv_ref, s0_ref,
                   o_ref, sfin_ref, st_scr, ycat_scr, pe_scr, *, tm, nt):
    t = pl.program_id(1)

    @pl.when(t == 0)
    def _():
        st_scr[...] = s0_ref[...]

    pcur = pcur_ref[...]
    pe_scr[0:tm, :] = pcur
    pe_scr[tm:tm + POOL_HALO, :] = pprev_ref[...]
    pe_scr[tm + POOL_HALO:tm + 2 * POOL_HALO, :] = pnext_ref[...]
    pe_scr[tm + 2 * POOL_HALO:, :] = jnp.zeros((POOL_PAD - 2 * POOL_HALO, POOL_WIDTH), F32)
    pe = pe_scr[...]
    pe_hi = pe.astype(BF16)
    pe_lo = (pe - pe_hi.astype(F32)).astype(BF16)
    lane_grp = jnp.right_shift(lax.broadcasted_iota(jnp.int32, (tm, POOL_WIDTH), 1),
                               POOL_CG.bit_length() - 1)
    pooled = jnp.zeros((tm, POOL_WIDTH), F32)
    for gi in range(POOL_GROUPS):
        win = pm_ref[gi]
        tot = jnp.dot(win, pe_hi, preferred_element_type=F32) \
            + jnp.dot(win, pe_lo, preferred_element_type=F32)
        pooled = jnp.where(lane_grp == gi, tot, pooled)
    diffs = pooled * invc_ref[...] - pcur
    pool_y = jnp.dot(diffs.astype(BF16), wpool_ref[...], preferred_element_type=F32) * pscale_ref[...]
    ycat_scr[:, 0:POOL_WIDTH] = pool_y.astype(BF16)

    ri = lax.broadcasted_iota(jnp.int32, (RET_CHUNK, RET_CHUNK), 0)
    ci = lax.broadcasted_iota(jnp.int32, (RET_CHUNK, RET_CHUNK), 1)
    rel = (ri - ci).astype(F32)
    row = lax.broadcasted_iota(jnp.int32, (RET_CHUNK, LANES), 0).astype(F32)
    lane = lax.broadcasted_iota(jnp.int32, (RET_CHUNK, LANES), 1)
    bdm = _bd_mask()
    zero_bf = jnp.zeros((RET_CHUNK, LANES), BF16)
    tables = []
    for j in range(RET_PAIRS):
        sl = slice(LANES * j, LANES * (j + 1))
        vsl = slice(2 * LANES * j, 2 * LANES * (j + 1))
        lgq = lgfq_ref[:, sl]
        xi = _pair_tables(lgq, row + 1.0)
        zeta = _pair_tables(lgq, float(RET_CHUNK - 1) - row)
        dec = jnp.exp(lgfv_ref[:, vsl] * float(RET_CHUNK))
        masks = []
        for a in range(2):
            hs = slice(2 * LANES * j + LANES * a, 2 * LANES * j + LANES * (a + 1))
            mf = jnp.exp(lgfv_ref[:, hs] * jnp.maximum(rel, 0.0))
            mb = jnp.exp(lgbv_ref[:, hs] * jnp.maximum(-rel, 0.0))
            masks.append(jnp.where(rel > 0, mf, jnp.where(rel < 0, mb, 2.0)))
        tables.append((xi, zeta, dec, masks))
    for c in range(tm // RET_CHUNK):
        rs = slice(c * RET_CHUNK, (c + 1) * RET_CHUNK)
        for j in range(RET_PAIRS):
            sl = slice(LANES * j, LANES * (j + 1))
            vsl = slice(2 * LANES * j, 2 * LANES * (j + 1))
            xi, zeta, dec, masks = tables[j]
            q2 = q_ref[rs, sl]
            k2 = k_ref[rs, sl]
            v2 = v_ref[rs, vsl]
            bd = st_scr[j]
            qx = (q2.astype(F32) * xi).astype(BF16)
            ret = jnp.dot(qx, bd.astype(BF16), preferred_element_type=F32) + cb_ref[rs, vsl]
            for a in range(2):
                qa = jnp.where(lane < RET_DK if a == 0 else lane >= RET_DK, q2, zero_bf)
                s = lax.dot_general(qa, k2, (((1,), (1,)), ((), ())), preferred_element_type=F32)
                pa = (s * masks[a]).astype(BF16)
                hs_v = slice(LANES * a, LANES * (a + 1))
                r = ret[:, hs_v] + jnp.dot(pa, v2[:, hs_v], preferred_element_type=F32)
                ms = jnp.mean(r * r, axis=-1, keepdims=True)
                hcol = 2 * LANES * j + LANES * a
                y = (r * lax.rsqrt(ms + EPS) * rng_ref[:, hcol:hcol + LANES]) \
                    * sg_ref[rs, hcol:hcol + LANES].astype(F32)
                ycat_scr[rs, POOL_WIDTH + hcol:POOL_WIDTH + hcol + LANES] = y.astype(BF16)
            kz = (k2.astype(F32) * zeta).astype(BF16)
            upd = lax.dot_general(kz, v2, (((0,), (0,)), ((), ())), preferred_element_type=F32)
            st_scr[j] = dec * bd + bdm * upd
        yy = jnp.dot(ycat_scr[rs, :], wout_ref[...], preferred_element_type=F32)
        o_ref[rs, :] = x_ref[rs, :] + mod_ref[5:6, :] * yy

    @pl.when(t == nt - 1)
    def _():
        sfin_ref[...] = st_scr[...]


def _pool_constants(tm, seq):
    nt = seq // tm
    kinds = [0] if nt == 1 else [0, 1, nt - 1]
    mats = np.zeros((len(kinds), POOL_GROUPS, tm, tm + POOL_PAD), np.float32)
    invc = np.zeros((len(kinds), tm, POOL_WIDTH), np.float32)
    i = np.arange(tm)
    for vi, tile in enumerate(kinds):
        pos = tile * tm + i
        colpos = np.full(tm + POOL_PAD, -1)
        colpos[:tm] = pos
        colpos[tm:tm + POOL_HALO] = tile * tm - POOL_HALO + np.arange(POOL_HALO)
        colpos[tm + POOL_HALO:tm + 2 * POOL_HALO] = (tile + 1) * tm + np.arange(POOL_HALO)
        for gi, w in enumerate(POOL_WINDOWS):
            lo = w // 2
            hi = w - 1 - lo
            start = np.maximum(pos - lo, 0)
            end = np.minimum(pos + hi + 1, seq)
            mats[vi, gi] = (colpos[None, :] >= start[:, None]) & (colpos[None, :] < end[:, None])
            invc[vi, :, gi * POOL_CG:(gi + 1) * POOL_CG] = (1.0 / (end - start))[:, None]
    return jnp.asarray(mats, BF16), jnp.asarray(invc, F32), nt


def _ret_out(x, mod8, mod_row, p, q, k, v, sg, cb, wpool_bd, pscale, rng, w_out, lgf_q, lgf_v, lgb_v, s0):
    bsz, seq, d = x.shape
    tm = min(MIX_TM, seq)
    pm, invc, nt = _pool_constants(tm, seq)
    hb = tm // POOL_HALO
    nhb = seq // POOL_HALO

    def variant(t):
        if nt == 1:
            return 0
        return jnp.where(t == 0, 0, jnp.where(t == nt - 1, 2, 1))

    cur = lambda b, t: (b, t, 0)
    st_shape = (RET_PAIRS, 2 * RET_DK, 2 * RET_DV)
    tokspec = lambda w: pl.BlockSpec((None, tm, w), cur)
    return pl.pallas_call(
        functools.partial(_retout_kernel, tm=tm, nt=nt),
        grid=(bsz, nt),
        in_specs=[
            tokspec(d),
            pl.BlockSpec((None, N_MOD, d), lambda b, t: (mod_row(b), 0, 0)),
            pl.BlockSpec((None, POOL_HALO, POOL_WIDTH), lambda b, t: (b, jnp.maximum(t * hb - 1, 0), 0)),
            tokspec(POOL_WIDTH),
            pl.BlockSpec((None, POOL_HALO, POOL_WIDTH),
                         lambda b, t: (b, jnp.minimum((t + 1) * hb, nhb - 1), 0)),
            pl.BlockSpec((None, POOL_GROUPS, tm, tm + POOL_PAD), lambda b, t: (variant(t), 0, 0, 0)),
            pl.BlockSpec((None, tm, POOL_WIDTH), lambda b, t: (variant(t), 0, 0)),
            tokspec(RET_QK_W), tokspec(RET_QK_W), tokspec(RET_V_W), tokspec(RET_V_W), tokspec(RET_V_W),
            _resident(wpool_bd.shape),
            pl.BlockSpec((1, POOL_WIDTH), lambda b, t: (0, 0)),
            pl.BlockSpec((1, RET_V_W), lambda b, t: (0, 0)),
            _resident(w_out.shape),
            pl.BlockSpec((1, RET_QK_W), lambda b, t: (0, 0)),
            pl.BlockSpec((1, RET_V_W), lambda b, t: (0, 0)),
            pl.BlockSpec((1, RET_V_W), lambda b, t: (0, 0)),
            pl.BlockSpec((None,) + st_shape, lambda b, t: (b, 0, 0, 0)),
        ],
        out_specs=[
            tokspec(d),
            pl.BlockSpec((None,) + st_shape, lambda b, t: (b, 0, 0, 0)),
        ],
        out_shape=[jax.ShapeDtypeStruct(x.shape, F32),
                   jax.ShapeDtypeStruct((bsz,) + st_shape, F32)],
        scratch_shapes=[pltpu.VMEM(st_shape, F32),
                        pltpu.VMEM((tm, d), BF16),
                        pltpu.VMEM((tm + POOL_PAD, POOL_WIDTH), F32)],
        compiler_params=_cparams(("arbitrary", "arbitrary")),
        name="ret_out",
    )(x, mod8, p, p, p, pm, invc, q, k, v, sg, cb, wpool_bd, pscale, rng, w_out,
      lgf_q, lgf_v, lgb_v, s0)


def _head_norm_rope(main, aux, g_main, g_aux):
    ms = jnp.sum(main * main, axis=-1, keepdims=True) * (1.0 / MLA_QK)
    r = lax.rsqrt(ms + EPS)
    return (main * r) * g_main + (aux * r) * g_aux


def _mlaproj_kernel(x_ref, mod_ref, g_ref, win_ref, qag_ref, kvag_ref, wqb_ref, wkv_ref,
                    qg_ref, qgs_ref, kg_ref, kgs_ref, qa_t_ref, qb_t_ref, ka_t_ref, kb_t_ref,
                    q_ref, k_ref, v_ref, qan_scr, kvn_scr, kr_scr, gt_scr):
    h = _rms_mod(x_ref[...], g_ref[...], mod_ref[3:4, :], mod_ref[4:5, :])
    proj = jnp.dot(h.astype(BF16), win_ref[...], preferred_element_type=F32)
    qa = proj[:, :MLA_Q_RANK]
    kva = proj[:, MLA_Q_RANK:MLA_Q_RANK + MLA_KV_RANK]
    r0 = MLA_Q_RANK + MLA_KV_RANK
    kr_scr[0] = proj[:, r0:r0 + HEAD_SLAB]
    kr_scr[1] = proj[:, r0 + HEAD_SLAB:r0 + 2 * HEAD_SLAB]
    qan = qa * lax.rsqrt(jnp.mean(qa * qa, axis=-1, keepdims=True) + EPS) * qag_ref[...]
    kvn = kva * lax.rsqrt(jnp.mean(kva * kva, axis=-1, keepdims=True) + EPS) * kvag_ref[...]
    qan_scr[...] = qan.astype(BF16)
    kvn_scr[...] = kvn.astype(BF16)
    gt_scr[0] = qg_ref[...] * qa_t_ref[...]
    gt_scr[1] = qgs_ref[...] * qb_t_ref[...]
    gt_scr[2] = kg_ref[...] * ka_t_ref[...]
    gt_scr[3] = kgs_ref[...] * kb_t_ref[...]
    tm = proj.shape[0]
    ones_rows = jnp.where(lax.broadcasted_iota(jnp.int32, (V_ROWS - MLA_V, tm), 0) == 0,
                          1.0, 0.0).astype(BF16)

    def head(hd, carry):
        c0 = pl.multiple_of(hd * (2 * HEAD_SLAB), 2 * HEAD_SLAB)
        qh = jnp.dot(qan_scr[...], wqb_ref[:, pl.ds(c0, 2 * HEAD_SLAB)], preferred_element_type=F32)
        kvh = jnp.dot(kvn_scr[...], wkv_ref[:, pl.ds(c0, 2 * HEAD_SLAB)], preferred_element_type=F32)
        q_ref[hd] = _head_norm_rope(qh[:, :HEAD_SLAB], qh[:, HEAD_SLAB:], gt_scr[0], gt_scr[1]).T.astype(BF16)
        k_ref[hd] = _head_norm_rope(kvh[:, :HEAD_SLAB] + kr_scr[0], kr_scr[1],
                                    gt_scr[2], gt_scr[3]).astype(BF16)
        v_ref[hd, 0:MLA_V, :] = kvh[:, HEAD_SLAB:].T.astype(BF16)
        v_ref[hd, MLA_V:V_ROWS, :] = ones_rows
        return carry
    lax.fori_loop(0, MLA_HEADS, head, 0, unroll=MLA_HEAD_UNROLL)


def _mla_project(x, mod8, mod_row, g, w_in, qa_g, kva_g, w_qb, w_kv, qn_g, qn_gs, kn_g, kn_gs,
                 q_ta, q_tb, k_ta, k_tb):
    bsz, seq, d = x.shape
    tm = min(MLA_TM, seq)
    head_out = jax.ShapeDtypeStruct((bsz, MLA_HEADS, seq, HEAD_SLAB), BF16)
    hspec = pl.BlockSpec((None, MLA_HEADS, tm, HEAD_SLAB), lambda b, t: (b, 0, t, 0))
    tspec = pl.BlockSpec((tm, HEAD_SLAB), lambda b, t: (t, 0))
    row = lambda n: pl.BlockSpec((1, n), lambda b, t: (0, 0))
    return pl.pallas_call(
        _mlaproj_kernel,
        grid=(bsz, seq // tm),
        in_specs=[
            pl.BlockSpec((None, tm, d), lambda b, t: (b, t, 0)),
            pl.BlockSpec((None, N_MOD, d), lambda b, t: (mod_row(b), 0, 0)),
            row(d),
            _resident(w_in.shape), row(MLA_Q_RANK), row(MLA_KV_RANK),
            _resident(w_qb.shape), _resident(w_kv.shape),
            row(HEAD_SLAB), row(HEAD_SLAB), row(HEAD_SLAB), row(HEAD_SLAB),
            tspec, tspec, tspec, tspec,
        ],
        out_specs=[pl.BlockSpec((None, MLA_HEADS, HEAD_SLAB, tm), lambda b, t: (b, 0, 0, t)),
                   hspec,
                   pl.BlockSpec((None, MLA_HEADS, V_ROWS, tm), lambda b, t: (b, 0, 0, t))],
        out_shape=[jax.ShapeDtypeStruct((bsz, MLA_HEADS, HEAD_SLAB, seq), BF16),
                   head_out,
                   jax.ShapeDtypeStruct((bsz, MLA_HEADS, V_ROWS, seq), BF16)],
        scratch_shapes=[pltpu.VMEM((tm, MLA_Q_RANK), BF16), pltpu.VMEM((tm, MLA_KV_RANK), BF16),
                        pltpu.VMEM((2, tm, HEAD_SLAB), F32), pltpu.VMEM((4, tm, HEAD_SLAB), F32)],
        compiler_params=_cparams(("parallel", "parallel")),
        name="mla_project",
    )(x, mod8, g.reshape(1, d), w_in, qa_g, kva_g, w_qb, w_kv, qn_g, qn_gs, kn_g, kn_gs,
      q_ta, q_tb, k_ta, k_tb)


def _attn_kernel(*refs, n_lat_blocks, tk, group_blocks, tq, n_q):
    if n_lat_blocks:
        q_ref, kc_ref, vct_ref, k_ref, vt_ref, o_ref, m_scr, acc_scr, s_scr, mb_scr = refs
    else:
        q_ref, kc_ref, vct_ref, o_ref, m_scr, acc_scr = refs

    def colmax(st):
        return jnp.max(st, axis=0, keepdims=True)

    def finalize(row0):
        acc = acc_scr[...]
        out_t = acc[0:MLA_V, :] * (1.0 / acc[MLA_V:MLA_V + 1, :])
        o_ref[pl.ds(row0, tq), :] = out_t.T.astype(o_ref.dtype)

    def softmax_pv(st, m_blk, vtb, first):
        if first:
            m_new = m_blk
        else:
            m_prev = m_scr[...]
            m_new = jnp.maximum(m_prev, m_blk)
        p = jnp.exp2(st - m_new).astype(BF16)
        pv = jnp.dot(vtb, p, preferred_element_type=F32)
        if first:
            acc_scr[...] = pv
        else:
            acc_scr[...] = jnp.exp2(m_prev - m_new) * acc_scr[...] + pv
        m_scr[...] = m_new

    if not n_lat_blocks:
        st_c = jnp.dot(kc_ref[...], q_ref[...], preferred_element_type=F32)
        softmax_pv(st_c, colmax(st_c), vct_ref[...], True)
        finalize(0)
        return

    kblk = lambda o: k_ref[pl.ds(o, tk), :]
    vblk = lambda o: vt_ref[:, pl.ds(o, tk)]
    lc = kc_ref.shape[0]

    def query_tile(q, prev_row0):
        def produce(slot, kb):
            st = jnp.dot(kb, q, preferred_element_type=F32)
            s_scr[slot, 0:kb.shape[0], :] = st
            mb_scr[slot] = colmax(st)

        produce(0, k_ref[0:tk, :])
        if prev_row0 is not None:
            finalize(prev_row0)
        m_scr[...] = jnp.full(m_scr.shape, -jnp.inf, F32)
        acc_scr[...] = jnp.zeros(acc_scr.shape, F32)

        def group(i, carry):
            base = i * (group_blocks * tk)
            for u in range(group_blocks):
                cur = pl.multiple_of(base + u * tk, tk)
                nxt = pl.multiple_of(base + (u + 1) * tk, tk)
                produce((u + 1) % 2, kblk(nxt))
                softmax_pv(s_scr[u % 2], mb_scr[u % 2], vblk(cur), False)
            return carry
        lax.fori_loop(0, n_lat_blocks // group_blocks - 1, group, 0)
        e0 = n_lat_blocks - group_blocks
        for u in range(group_blocks):
            cur = (e0 + u) * tk
            if u + 1 < group_blocks:
                produce((u + 1) % 2, k_ref[cur + tk:cur + 2 * tk, :])
            else:
                produce((u + 1) % 2, kc_ref[...])
            softmax_pv(s_scr[u % 2], mb_scr[u % 2], vt_ref[:, cur:cur + tk], False)
        c_slot = group_blocks % 2
        softmax_pv(s_scr[c_slot, 0:lc, :], mb_scr[c_slot], vct_ref[...], False)

    query_tile(q_ref[:, 0:tq], None)

    def later_tile(i, carry):
        off = pl.multiple_of(i * tq, tq)
        query_tile(q_ref[:, pl.ds(off, tq)], pl.multiple_of(off - tq, tq))
        return carry
    lax.fori_loop(1, n_q, later_tile, 0)
    finalize((n_q - 1) * tq)


def _attention(q, kc, vct, k=None, vt=None):
    bsz, nh, hs, lq = q.shape
    lc = kc.shape[2]
    tq = min(ATT_TQ, lq)
    qspec = pl.BlockSpec((None, None, hs, lq), lambda b, h: (b, h, 0, 0))
    kfull = lambda n: pl.BlockSpec((None, None, n, hs), lambda b, h: (b, h, 0, 0))
    vfull = lambda n: pl.BlockSpec((None, None, V_ROWS, n), lambda b, h: (b, h, 0, 0))
    in_specs = [qspec, kfull(lc), vfull(lc)]
    args = [q, kc, vct]
    n_lat = 0
    grp = 0
    tk = ATT_TK
    if k is not None:
        lk = k.shape[2]
        tk = min(ATT_TK, lk)
        n_lat = lk // tk
        grp = min(ATT_GROUP, n_lat)
        assert n_lat % grp == 0, "latent key blocks are consumed in whole groups"
        assert lc <= tk, "the context block reuses a latent score slot"
        in_specs += [kfull(lk), vfull(lk)]
        args += [k, vt]
    scratch = [pltpu.VMEM((1, tq), F32), pltpu.VMEM((V_ROWS, tq), F32)]
    if n_lat:
        scratch += [pltpu.VMEM((2, tk, tq), F32), pltpu.VMEM((2, 1, tq), F32)]
    return pl.pallas_call(
        functools.partial(_attn_kernel, n_lat_blocks=n_lat, tk=tk, group_blocks=grp, tq=tq,
                          n_q=lq // tq),
        grid=(bsz, nh),
        in_specs=in_specs,
        out_specs=pl.BlockSpec((None, lq, hs), lambda b, h: (b, 0, h)),
        out_shape=jax.ShapeDtypeStruct((bsz, lq, nh * hs), BF16),
        scratch_shapes=scratch,
        compiler_params=_cparams(("parallel", "parallel")),
        name="mla_attention",
    )(*args)


def _pair_swap_cols(w):
    w2 = w.reshape(w.shape[:-1] + (w.shape[-1] // 2, 2))
    return jnp.stack([-w2[..., 1], w2[..., 0]], axis=-1).reshape(w.shape)


def _rope_angles(seq, rot_dim):
    pos = jnp.arange(seq)
    row = (pos // GRID_W).astype(F32)
    col = (pos % GRID_W).astype(F32)
    n_freq = rot_dim // 4
    inv = ROPE_BASE ** (-jnp.arange(n_freq, dtype=F32) / n_freq)
    ang = jnp.concatenate([row[:, None] * inv, col[:, None] * inv], axis=-1)
    return jnp.repeat(jnp.cos(ang), 2, axis=-1), jnp.repeat(jnp.sin(ang), 2, axis=-1)


def _ret_tables(seq, ctx_len):
    cos, sin = _rope_angles(seq, RET_DK)
    cos2, sin2 = jnp.tile(cos, (1, 2)), jnp.tile(sin, (1, 2))
    return cos2, sin2, jnp.ones((ctx_len, LANES), F32), jnp.zeros((ctx_len, LANES), F32)


def _mla_tables(seq, ctx_len):
    cos, sin = _rope_angles(seq, MLA_ROPE)
    one = jnp.ones((seq, MLA_NOPE), F32)
    zero_tail = jnp.zeros((seq, HEAD_SLAB - MLA_QK), F32)
    zero_head = jnp.zeros((seq, MLA_NOPE), F32)
    ta = jnp.concatenate([one, cos, zero_tail], axis=-1)
    tb = jnp.concatenate([zero_head, sin, zero_tail], axis=-1)
    ident = jnp.concatenate([jnp.ones((ctx_len, MLA_QK), F32),
                             jnp.zeros((ctx_len, HEAD_SLAB - MLA_QK), F32)], axis=-1)
    return ta, tb, ident, jnp.zeros((ctx_len, HEAD_SLAB), F32)


def _rope_slab(rope_cols):
    lead = rope_cols.shape[:-1]
    return jnp.concatenate([jnp.zeros(lead + (MLA_NOPE,), F32), rope_cols,
                            jnp.zeros(lead + (HEAD_SLAB - MLA_QK,), F32)], axis=-1)


def _slab_gains(gvec):
    rope = gvec[MLA_NOPE:].reshape(MLA_ROPE // 2, 2)
    swapped = jnp.stack([rope[:, 1], rope[:, 0]], axis=-1).reshape(MLA_ROPE)
    main = jnp.concatenate([gvec, jnp.zeros((HEAD_SLAB - MLA_QK,), F32)])
    return main.reshape(1, HEAD_SLAB), _rope_slab(swapped).reshape(1, HEAD_SLAB)


def _prep_even(ab_w_in, pool_w, ab_w_out):
    wq = ab_w_in[:, POOL_WIDTH:POOL_WIDTH + RET_QK_W]
    wk = ab_w_in[:, POOL_WIDTH + RET_QK_W:POOL_WIDTH + 2 * RET_QK_W]
    w_ext = jnp.concatenate([ab_w_in, _pair_swap_cols(wq), _pair_swap_cols(wk)], axis=-1).astype(BF16)
    wpool_bd = jnp.zeros((POOL_WIDTH, POOL_WIDTH), F32)
    for gi in range(POOL_GROUPS):
        s = slice(gi * POOL_CG, (gi + 1) * POOL_CG)
        wpool_bd = wpool_bd.at[s, s].set(pool_w[gi])
    return w_ext, wpool_bd.astype(BF16), ab_w_out.astype(BF16)


def _prep_odd(w_in, w_qb, w_kvb, w_out):
    wkr = w_in[:, MLA_Q_RANK + MLA_KV_RANK:]
    w_in_ext = jnp.concatenate([w_in[:, :MLA_Q_RANK + MLA_KV_RANK], _rope_slab(wkr),
                                _rope_slab(_pair_swap_cols(wkr))], axis=-1).astype(BF16)
    wq3 = w_qb.reshape(MLA_Q_RANK, MLA_HEADS, MLA_QK)
    wq_main = jnp.concatenate([wq3, jnp.zeros((MLA_Q_RANK, MLA_HEADS, HEAD_SLAB - MLA_QK), F32)], axis=-1)
    wq_aux = _rope_slab(_pair_swap_cols(wq3[..., MLA_NOPE:]))
    w_qb_ext = jnp.concatenate([wq_main, wq_aux], axis=-1).reshape(
        MLA_Q_RANK, MLA_HEADS * 2 * HEAD_SLAB).astype(BF16)
    wkv3 = w_kvb.reshape(MLA_KV_RANK, MLA_HEADS, MLA_NOPE + MLA_V)
    wk_slab = jnp.concatenate([wkv3[..., :MLA_NOPE],
                               jnp.zeros((MLA_KV_RANK, MLA_HEADS, HEAD_SLAB - MLA_NOPE), F32)], axis=-1)
    w_kv_ext = jnp.concatenate([wk_slab, wkv3[..., MLA_NOPE:]], axis=-1).reshape(
        MLA_KV_RANK, MLA_HEADS * 2 * HEAD_SLAB).astype(BF16)
    return w_in_ext, w_qb_ext, w_kv_ext, w_out.astype(BF16)


def kernel(x, c, ctx, c_ctx, ada_w, ada_b, norm_g, ffn_w_in, ffn_w_out, ab_w_in, pool_w, pool_scale,
           ret_log_decay, ret_norm_g, ab_w_out, mla_w_in, mla_qa_g, mla_kva_g, mla_w_qb, mla_w_kvb,
           mla_qn_g, mla_kn_g, mla_w_out):
    bsz, seq, d = x.shape
    ctx_len = ctx.shape[1]
    depth = ada_w.shape[0]
    ctx_row = bsz
    cond8 = jnp.zeros((8, d), F32).at[:bsz].set(c).at[ctx_row].set(c_ctx)
    mod_all = _ada_modulation(cond8, ada_w, ada_b).reshape(depth, 8, N_MOD, d)
    lat_row = lambda b: b
    ctx_mod_row = lambda b: ctx_row

    ret_cos, ret_sin, ret_cos_c, ret_sin_c = _ret_tables(seq, ctx_len)
    m_ta, m_tb, m_ta_c, m_tb_c = _mla_tables(seq, ctx_len)
    q_scale = MLA_QK ** -0.5 * LOG2_E
    zero_state = jnp.zeros((bsz, RET_PAIRS, 2 * RET_DK, 2 * RET_DV), F32)
    ffn_in_b, ffn_out_b = ffn_w_in.astype(BF16), ffn_w_out.astype(BF16)

    xc = ctx
    for i in range(depth):
        last = i == depth - 1
        j = i // 2
        mod8 = mod_all[i]
        x = _ffn_half(x, mod8, lat_row, 0, norm_g[i, 0], ffn_in_b, ffn_out_b, (i, 0))
        xc = _ffn_half(xc, mod8, ctx_mod_row, 0, norm_g[i, 0], ffn_in_b, ffn_out_b, (i, 0))
        mix = mix_c = w_mix = None
        if i % 2 == 0:
            w_ext, wpool_bd, w_out = _prep_even(ab_w_in[j], pool_w[j], ab_w_out[j])
            lg = ret_log_decay[j]
            lgf_q = jnp.repeat(lg[0], RET_DK).reshape(1, RET_QK_W)
            lgb_q = jnp.repeat(lg[1], RET_DK).reshape(1, RET_QK_W)
            lgf_v = jnp.repeat(lg[0], RET_DV).reshape(1, RET_V_W)
            lgb_v = jnp.repeat(lg[1], RET_DV).reshape(1, RET_V_W)
            pscale = pool_scale[j].reshape(1, POOL_WIDTH)
            rng = ret_norm_g[j].reshape(1, RET_V_W)
            pc, qc, kc, vc, sgc, cbc, state_b = _ab_project(
                xc, mod8, ctx_mod_row, norm_g[i, 1], w_ext, ret_cos_c, ret_sin_c, lgb_q, lgb_v, zero_state)
            xc_mixed, state_f = _ret_out(xc, mod8, ctx_mod_row, pc, qc, kc, vc, sgc, cbc, wpool_bd, pscale,
                                         rng, w_out, lgf_q, lgf_v, lgb_v, zero_state)
            p, q, k, v, sg, cb, _ = _ab_project(
                x, mod8, lat_row, norm_g[i, 1], w_ext, ret_cos, ret_sin, lgb_q, lgb_v, state_b)
            x, _ = _ret_out(x, mod8, lat_row, p, q, k, v, sg, cb, wpool_bd, pscale, rng, w_out,
                            lgf_q, lgf_v, lgb_v, state_f)
        else:
            w_in_ext, w_qb_ext, w_kv_ext, w_out = _prep_odd(mla_w_in[j], mla_w_qb[j], mla_w_kvb[j],
                                                            mla_w_out[j])
            qa_g = mla_qa_g[j].reshape(1, MLA_Q_RANK)
            kva_g = mla_kva_g[j].reshape(1, MLA_KV_RANK)
            qn_g, qn_gs = _slab_gains(mla_qn_g[j])
            kn_g, kn_gs = _slab_gains(mla_kn_g[j])
            qc, kc, vc = _mla_project(xc, mod8, ctx_mod_row, norm_g[i, 1], w_in_ext, qa_g, kva_g, w_qb_ext,
                                      w_kv_ext, qn_g, qn_gs, kn_g, kn_gs,
                                      m_ta_c * q_scale, m_tb_c, m_ta_c, m_tb_c)
            q, k, v = _mla_project(x, mod8, lat_row, norm_g[i, 1], w_in_ext, qa_g, kva_g, w_qb_ext,
                                   w_kv_ext, qn_g, qn_gs, kn_g, kn_gs,
                                   m_ta * q_scale, m_tb * q_scale, m_ta, m_tb)
            mix, w_mix = _attention(q, kc, vc, k, v), w_out
            if not last:
                mix_c = _attention(qc, kc, vc)
                xc_mixed = xc
        x = _ffn_half(x, mod8, lat_row, 2, norm_g[i, 2], ffn_in_b, ffn_out_b, (i, 1), mix, w_mix)
        if not last:
            xc = _ffn_half(xc_mixed, mod8, ctx_mod_row, 2, norm_g[i, 2], ffn_in_b, ffn_out_b, (i, 1),
                           mix_c, w_mix)
    return x
```

```python
import functools

import numpy as np
import jax
import jax.numpy as jnp
from jax import lax
from jax.experimental import pallas as pl
from jax.experimental.pallas import tpu as pltpu

F32 = jnp.float32
BF16 = jnp.bfloat16

D_MODEL = 1024
GRID_W = 64
D_FF = 2816
FFN_RESIDUAL = 0.5
N_MOD = 9
ROPE_BASE = 10000.0
EPS = 1e-6
POOL_GROUPS = 4
POOL_CG = 64
POOL_WIDTH = POOL_GROUPS * POOL_CG
POOL_WINDOWS = (2, 4, 8, 16)
POOL_HALO = 8
POOL_SUB = 128
RET_HEADS = 6
RET_PAIRS = RET_HEADS // 2
RET_DK = 64
RET_DV = 128
RET_CHUNK = 128
RET_QK_W = RET_HEADS * RET_DK
RET_V_W = RET_HEADS * RET_DV
AB_IN = POOL_WIDTH + 2 * RET_QK_W + 2 * RET_V_W
AB_EXT = AB_IN + 2 * RET_QK_W
MLA_HEADS = 8
MLA_Q_RANK = 384
MLA_KV_RANK = 256
MLA_NOPE = 64
MLA_ROPE = 32
MLA_V = 128
MLA_QK = MLA_NOPE + MLA_ROPE
HEAD_SLAB = 128
V_ROWS = MLA_V + 16
LOG2_E = 1.4426950408889634

LANES = 128
V7X_VMEM_BYTES = 64 * 1024 * 1024
VMEM_LIMIT_BYTES = 56 * 1024 * 1024

ADA_TN = 1536
FFN_TM = 1024
FFN_SUB = 256
MIX_TM = 1024
AB_TM = 1024
AB_SUB = 256
MLA_TM = 1024
MLA_HEAD_UNROLL = 8
ATT_TQ = 1024
ATT_TK = 512
ATT_SPAN = 1
ATT_GROUP = 4


def _cparams(sem):
    return pltpu.CompilerParams(dimension_semantics=sem, vmem_limit_bytes=VMEM_LIMIT_BYTES)


def _resident(shape):
    nd = len(shape)
    return pl.BlockSpec(shape, lambda *_: (0,) * nd, pipeline_mode=pl.Buffered(1))


def _silu(x):
    return x * (1.0 / (1.0 + jnp.exp(-x)))


def _rms_mod(x, g, shift, scale):
    ms = jnp.mean(x * x, axis=-1, keepdims=True)
    return (x * lax.rsqrt(ms + EPS) * g) * (1.0 + scale) + shift


def _ada_kernel(c_ref, w_ref, b_ref, o_ref):
    s = _silu(c_ref[...])
    o_ref[...] = jnp.dot(s, w_ref[...], precision=lax.Precision.HIGHEST,
                         preferred_element_type=F32) + b_ref[...]


def _ada_modulation(cond8, ada_w, ada_b):
    depth, d, n = ada_w.shape
    return pl.pallas_call(
        _ada_kernel,
        grid=(depth, n // ADA_TN),
        in_specs=[
            pl.BlockSpec((8, d), lambda i, j: (0, 0)),
            pl.BlockSpec((None, d, ADA_TN), lambda i, j: (i, 0, j)),
            pl.BlockSpec((None, 1, ADA_TN), lambda i, j: (i, 0, j)),
        ],
        out_specs=pl.BlockSpec((None, 8, ADA_TN), lambda i, j: (i, 0, j)),
        out_shape=jax.ShapeDtypeStruct((depth, 8, n), F32),
        compiler_params=_cparams(("parallel", "parallel")),
        name="ada_mod",
    )(cond8, ada_w, ada_b.reshape(depth, 1, n))


def _ffn_kernel(*refs, k, mixed):
    if mixed:
        x_ref, mod_ref, g_ref, win_ref, wout_ref, mix_ref, wmix_ref, o_ref = refs
    else:
        x_ref, mod_ref, g_ref, win_ref, wout_ref, o_ref = refs
    tm = x_ref.shape[0]
    sub = min(FFN_SUB, tm)
    for s in range(tm // sub):
        rs = slice(s * sub, (s + 1) * sub)
        x = x_ref[rs, :]
        if mixed:
            x = x + mod_ref[5:6, :] * jnp.dot(mix_ref[rs, :], wmix_ref[...], preferred_element_type=F32)
        h = _rms_mod(x, g_ref[...], mod_ref[3 * k:3 * k + 1, :], mod_ref[3 * k + 1:3 * k + 2, :])
        ab = jnp.dot(h.astype(BF16), win_ref[...], preferred_element_type=F32)
        act = (_silu(ab[:, :D_FF]) * ab[:, D_FF:]).astype(BF16)
        y = jnp.dot(act, wout_ref[...], preferred_element_type=F32)
        o_ref[rs, :] = x + (FFN_RESIDUAL * mod_ref[3 * k + 2:3 * k + 3, :]) * y


def _ffn_half(x, mod8, mod_row, k, g, w_in, w_out, which, mix=None, w_mix=None):
    bsz, seq, d = x.shape
    tm = min(FFN_TM, seq)
    layer, half = which
    pick = lambda shape: pl.BlockSpec((None, None) + tuple(shape[2:]), lambda b, t: (layer, half, 0, 0),
                                      pipeline_mode=pl.Buffered(1))
    in_specs = [
        pl.BlockSpec((None, tm, d), lambda b, t: (b, t, 0)),
        pl.BlockSpec((None, N_MOD, d), lambda b, t: (mod_row(b), 0, 0)),
        pl.BlockSpec((1, d), lambda b, t: (0, 0)),
        pick(w_in.shape),
        pick(w_out.shape),
    ]
    args = [x, mod8, g.reshape(1, d), w_in, w_out]
    if mix is not None:
        in_specs += [pl.BlockSpec((None, tm, mix.shape[-1]), lambda b, t: (b, t, 0)),
                     _resident(w_mix.shape)]
        args += [mix, w_mix]
    return pl.pallas_call(
        functools.partial(_ffn_kernel, k=k, mixed=mix is not None),
        grid=(bsz, seq // tm),
        in_specs=in_specs,
        out_specs=pl.BlockSpec((None, tm, d), lambda b, t: (b, t, 0)),
        out_shape=jax.ShapeDtypeStruct(x.shape, F32),
        compiler_params=_cparams(("parallel", "parallel")),
        name="ffn_half",
    )(*args)


def _pair_tables(lg_row, expo):
    return jnp.exp(lg_row * expo)


def _bd_mask():
    r = lax.broadcasted_iota(jnp.int32, (2 * RET_DK, 2 * RET_DV), 0)
    c = lax.broadcasted_iota(jnp.int32, (2 * RET_DK, 2 * RET_DV), 1)
    same_head = jnp.where(r < RET_DK, 0, 1) == jnp.where(c < RET_DV, 0, 1)
    return jnp.where(same_head, 1.0, 0.0)


def _abproj_kernel(x_ref, mod_ref, g_ref, w_ref, cos_ref, sin_ref, lgq_ref, lgv_ref, s0_ref,
                   p_ref, q_ref, k_ref, v_ref, sg_ref, cb_ref, sfin_ref, st_scr, *, tm, nt):
    t = pl.program_id(1)

    @pl.when(t == 0)
    def _():
        st_scr[...] = s0_ref[...]

    q0, k0 = POOL_WIDTH, POOL_WIDTH + RET_QK_W
    v0, g0 = k0 + RET_QK_W, k0 + RET_QK_W + RET_V_W
    qs0, ks0 = AB_IN, AB_IN + RET_QK_W
    row = lax.broadcasted_iota(jnp.int32, (RET_CHUNK, LANES), 0).astype(F32)
    bdm = _bd_mask()
    k_scale = RET_DK ** -0.5
    sub = min(AB_SUB, tm)
    for s in reversed(range(tm // sub)):
        r0 = s * sub
        ts = slice(r0, r0 + sub)
        h = _rms_mod(x_ref[ts, :], g_ref[...], mod_ref[3:4, :], mod_ref[4:5, :])
        proj = jnp.dot(h.astype(BF16), w_ref[...], preferred_element_type=F32)
        p_ref[ts, :] = proj[:, :POOL_WIDTH]
        v_ref[ts, :] = proj[:, v0:v0 + RET_V_W].astype(BF16)
        sg_ref[ts, :] = _silu(proj[:, g0:g0 + RET_V_W]).astype(BF16)
        cos = cos_ref[ts, :]
        sin = sin_ref[ts, :]
        for j in range(RET_PAIRS):
            sl = slice(LANES * j, LANES * (j + 1))
            qr = proj[:, q0 + LANES * j:q0 + LANES * (j + 1)] * cos \
                + proj[:, qs0 + LANES * j:qs0 + LANES * (j + 1)] * sin
            kr = (proj[:, k0 + LANES * j:k0 + LANES * (j + 1)] * cos
                  + proj[:, ks0 + LANES * j:ks0 + LANES * (j + 1)] * sin) * k_scale
            q_ref[ts, sl] = qr.astype(BF16)
            k_ref[ts, sl] = kr.astype(BF16)
            lgq = lgq_ref[:, sl]
            lgv = lgv_ref[:, 2 * LANES * j:2 * LANES * (j + 1)]
            xi = _pair_tables(lgq, float(RET_CHUNK) - row)
            zeta = _pair_tables(lgq, row)
            dec = jnp.exp(lgv * float(RET_CHUNK))
            vsl = slice(2 * LANES * j, 2 * LANES * (j + 1))
            for c in reversed(range(sub // RET_CHUNK)):
                ls = slice(c * RET_CHUNK, (c + 1) * RET_CHUNK)
                rs = slice(r0 + c * RET_CHUNK, r0 + (c + 1) * RET_CHUNK)
                bd = st_scr[j]
                qx = (qr[ls] * xi).astype(BF16)
                cb_ref[rs, vsl] = jnp.dot(qx, bd.astype(BF16), preferred_element_type=F32)
                kz = (kr[ls] * zeta).astype(BF16)
                upd = lax.dot_general(kz, v_ref[rs, vsl], (((0,), (0,)), ((), ())),
                                      preferred_element_type=F32)
                st_scr[j] = dec * bd + bdm * upd

    @pl.when(t == nt - 1)
    def _():
        sfin_ref[...] = st_scr[...]


def _ab_project(x, mod8, mod_row, g, w_ext, cos, sin, lg_q, lg_v, s0):
    bsz, seq, d = x.shape
    tm = min(AB_TM, seq)
    nt = seq // tm
    rev = lambda b, t: (b, nt - 1 - t, 0)
    tok = lambda w, dt: jax.ShapeDtypeStruct((bsz, seq, w), dt)
    st_shape = (RET_PAIRS, 2 * RET_DK, 2 * RET_DV)
    return pl.pallas_call(
        functools.partial(_abproj_kernel, tm=tm, nt=nt),
        grid=(bsz, nt),
        in_specs=[
            pl.BlockSpec((None, tm, d), rev),
            pl.BlockSpec((None, N_MOD, d), lambda b, t: (mod_row(b), 0, 0)),
            pl.BlockSpec((1, d), lambda b, t: (0, 0)),
            _resident(w_ext.shape),
            pl.BlockSpec((tm, LANES), lambda b, t: (nt - 1 - t, 0)),
            pl.BlockSpec((tm, LANES), lambda b, t: (nt - 1 - t, 0)),
            pl.BlockSpec((1, RET_QK_W), lambda b, t: (0, 0)),
            pl.BlockSpec((1, RET_V_W), lambda b, t: (0, 0)),
            pl.BlockSpec((None,) + st_shape, lambda b, t: (b, 0, 0, 0)),
        ],
        out_specs=[
            pl.BlockSpec((None, tm, POOL_WIDTH), rev),
            pl.BlockSpec((None, tm, RET_QK_W), rev),
            pl.BlockSpec((None, tm, RET_QK_W), rev),
            pl.BlockSpec((None, tm, RET_V_W), rev),
            pl.BlockSpec((None, tm, RET_V_W), rev),
            pl.BlockSpec((None, tm, RET_V_W), rev),
            pl.BlockSpec((None,) + st_shape, lambda b, t: (b, 0, 0, 0)),
        ],
        out_shape=[tok(POOL_WIDTH, F32), tok(RET_QK_W, BF16), tok(RET_QK_W, BF16),
                   tok(RET_V_W, BF16), tok(RET_V_W, BF16), tok(RET_V_W, F32),
                   jax.ShapeDtypeStruct((bsz,) + st_shape, F32)],
        scratch_shapes=[pltpu.VMEM(st_shape, F32)],
        compiler_params=_cparams(("arbitrary", "arbitrary")),
        name="ab_project",
    )(x, mod8, g.reshape(1, d), w_ext, cos, sin, lg_q, lg_v, s0)


def _retout_kernel(x_ref, mod_ref, pprev_ref, pcur_ref, pnext_ref, pm_ref, invc_ref,
                   q_ref, k_ref, v_ref, sg_ref, cb_ref, wpool_ref, pscale_ref, rng_ref, wout_ref,
                   lgfq_ref, lgfv_ref, lgbv_ref, s0_ref,
                   o_ref, sfin_ref, st_scr, ycat_scr, pe_scr, *, tm, nt):
    t = pl.program_id(1)

    @pl.when(t == 0)
    def _():
        st_scr[...] = s0_ref[...]

    pe_scr[0:POOL_HALO, :] = pprev_ref[...]
    pe_scr[POOL_HALO:POOL_HALO + tm, :] = pcur_ref[...]
    pe_scr[POOL_HALO + tm:2 * POOL_HALO + tm, :] = pnext_ref[...]
    pe_scr[2 * POOL_HALO + tm:, :] = jnp.zeros((POOL_SUB - 2 * POOL_HALO, POOL_WIDTH), F32)
    lane_grp = jnp.right_shift(lax.broadcasted_iota(jnp.int32, (POOL_SUB, POOL_WIDTH), 1),
                               POOL_CG.bit_length() - 1)
    for s in range(tm // POOL_SUB):
        r0 = s * POOL_SUB
        pe = pe_scr[r0:r0 + 2 * POOL_SUB, :]
        pe_hi = pe.astype(BF16)
        pe_lo = (pe - pe_hi.astype(F32)).astype(BF16)
        pooled = jnp.zeros((POOL_SUB, POOL_WIDTH), F32)
        for gi in range(POOL_GROUPS):
            win = pm_ref[s, gi]
            tot = jnp.dot(win, pe_hi, preferred_element_type=F32) \
                + jnp.dot(win, pe_lo, preferred_element_type=F32)
            pooled = jnp.where(lane_grp == gi, tot, pooled)
        rs = slice(r0, r0 + POOL_SUB)
        diffs = pooled * invc_ref[rs, :] - pcur_ref[rs, :]
        pool_y = jnp.dot(diffs.astype(BF16), wpool_ref[...], preferred_element_type=F32) * pscale_ref[...]
        ycat_scr[rs, 0:POOL_WIDTH] = pool_y.astype(BF16)

    ri = lax.broadcasted_iota(jnp.int32, (RET_CHUNK, RET_CHUNK), 0)
    ci = lax.broadcasted_iota(jnp.int32, (RET_CHUNK, RET_CHUNK), 1)
    rel = (ri - ci).astype(F32)
    row = lax.broadcasted_iota(jnp.int32, (RET_CHUNK, LANES), 0).astype(F32)
    lane = lax.broadcasted_iota(jnp.int32, (RET_CHUNK, LANES), 1)
    bdm = _bd_mask()
    zero_bf = jnp.zeros((RET_CHUNK, LANES), BF16)
    tables = []
    for j in range(RET_PAIRS):
        sl = slice(LANES * j, LANES * (j + 1))
        vsl = slice(2 * LANES * j, 2 * LANES * (j + 1))
        lgq = lgfq_ref[:, sl]
        xi = _pair_tables(lgq, row + 1.0)
        zeta = _pair_tables(lgq, float(RET_CHUNK - 1) - row)
        dec = jnp.exp(lgfv_ref[:, vsl] * float(RET_CHUNK))
        masks = []
        for a in range(2):
            hs = slice(2 * LANES * j + LANES * a, 2 * LANES * j + LANES * (a + 1))
            mf = jnp.exp(lgfv_ref[:, hs] * jnp.maximum(rel, 0.0))
            mb = jnp.exp(lgbv_ref[:, hs] * jnp.maximum(-rel, 0.0))
            masks.append(jnp.where(rel > 0, mf, jnp.where(rel < 0, mb, 2.0)))
        tables.append((xi, zeta, dec, masks))
    for c in range(tm // RET_CHUNK):
        rs = slice(c * RET_CHUNK, (c + 1) * RET_CHUNK)
        for j in range(RET_PAIRS):
            sl = slice(LANES * j, LANES * (j + 1))
            vsl = slice(2 * LANES * j, 2 * LANES * (j + 1))
            xi, zeta, dec, masks = tables[j]
            q2 = q_ref[rs, sl]
            k2 = k_ref[rs, sl]
            v2 = v_ref[rs, vsl]
            bd = st_scr[j]
            qx = (q2.astype(F32) * xi).astype(BF16)
            ret = jnp.dot(qx, bd.astype(BF16), preferred_element_type=F32) + cb_ref[rs, vsl]
            for a in range(2):
                qa = jnp.where(lane < RET_DK if a == 0 else lane >= RET_DK, q2, zero_bf)
                s = lax.dot_general(qa, k2, (((1,), (1,)), ((), ())), preferred_element_type=F32)
                pa = (s * masks[a]).astype(BF16)
                hs_v = slice(LANES * a, LANES * (a + 1))
                r = ret[:, hs_v] + jnp.dot(pa, v2[:, hs_v], preferred_element_type=F32)
                ms = jnp.mean(r * r, axis=-1, keepdims=True)
                hcol = 2 * LANES * j + LANES * a
                y = (r * lax.rsqrt(ms + EPS) * rng_ref[:, hcol:hcol + LANES]) \
                    * sg_ref[rs, hcol:hcol + LANES].astype(F32)
                ycat_scr[rs, POOL_WIDTH + hcol:POOL_WIDTH + hcol + LANES] = y.astype(BF16)
            kz = (k2.astype(F32) * zeta).astype(BF16)
            upd = lax.dot_general(kz, v2, (((0,), (0,)), ((), ())), preferred_element_type=F32)
            st_scr[j] = dec * bd + bdm * upd
        yy = jnp.dot(ycat_scr[rs, :], wout_ref[...], preferred_element_type=F32)
        o_ref[rs, :] = x_ref[rs, :] + mod_ref[5:6, :] * yy

    @pl.when(t == nt - 1)
    def _():
        sfin_ref[...] = st_scr[...]


def _pool_constants(tm, seq):
    nt = seq // tm
    nsub = tm // POOL_SUB
    kinds = [0] if nt == 1 else [0, 1, nt - 1]
    mats = np.zeros((len(kinds), nsub, POOL_GROUPS, POOL_SUB, 2 * POOL_SUB), np.float32)
    invc = np.zeros((len(kinds), tm, POOL_WIDTH), np.float32)
    for vi, tile in enumerate(kinds):
        for s in range(nsub):
            pos = tile * tm + s * POOL_SUB + np.arange(POOL_SUB)
            colpos = tile * tm - POOL_HALO + s * POOL_SUB + np.arange(2 * POOL_SUB)
            for gi, w in enumerate(POOL_WINDOWS):
                lo = w // 2
                hi = w - 1 - lo
                start = np.maximum(pos - lo, 0)
                end = np.minimum(pos + hi + 1, seq)
                mats[vi, s, gi] = (colpos[None, :] >= start[:, None]) & (colpos[None, :] < end[:, None])
                invc[vi, s * POOL_SUB:(s + 1) * POOL_SUB, gi * POOL_CG:(gi + 1) * POOL_CG] = \
                    (1.0 / (end - start))[:, None]
    return jnp.asarray(mats, BF16), jnp.asarray(invc, F32), nt


def _ret_out(x, mod8, mod_row, p, q, k, v, sg, cb, wpool_bd, pscale, rng, w_out, lgf_q, lgf_v, lgb_v, s0):
    bsz, seq, d = x.shape
    tm = min(MIX_TM, seq)
    pm, invc, nt = _pool_constants(tm, seq)
    hb = tm // POOL_HALO
    nhb = seq // POOL_HALO

    def variant(t):
        if nt == 1:
            return 0
        return jnp.where(t == 0, 0, jnp.where(t == nt - 1, 2, 1))

    cur = lambda b, t: (b, t, 0)
    st_shape = (RET_PAIRS, 2 * RET_DK, 2 * RET_DV)
    tokspec = lambda w: pl.BlockSpec((None, tm, w), cur)
    return pl.pallas_call(
        functools.partial(_retout_kernel, tm=tm, nt=nt),
        grid=(bsz, nt),
        in_specs=[
            tokspec(d),
            pl.BlockSpec((None, N_MOD, d), lambda b, t: (mod_row(b), 0, 0)),
            pl.BlockSpec((None, POOL_HALO, POOL_WIDTH), lambda b, t: (b, jnp.maximum(t * hb - 1, 0), 0)),
            tokspec(POOL_WIDTH),
            pl.BlockSpec((None, POOL_HALO, POOL_WIDTH),
                         lambda b, t: (b, jnp.minimum((t + 1) * hb, nhb - 1), 0)),
            pl.BlockSpec((None, tm // POOL_SUB, POOL_GROUPS, POOL_SUB, 2 * POOL_SUB),
                         lambda b, t: (variant(t), 0, 0, 0, 0)),
            pl.BlockSpec((None, tm, POOL_WIDTH), lambda b, t: (variant(t), 0, 0)),
            tokspec(RET_QK_W), tokspec(RET_QK_W), tokspec(RET_V_W), tokspec(RET_V_W), tokspec(RET_V_W),
            _resident(wpool_bd.shape),
            pl.BlockSpec((1, POOL_WIDTH), lambda b, t: (0, 0)),
            pl.BlockSpec((1, RET_V_W), lambda b, t: (0, 0)),
            _resident(w_out.shape),
            pl.BlockSpec((1, RET_QK_W), lambda b, t: (0, 0)),
            pl.BlockSpec((1, RET_V_W), lambda b, t: (0, 0)),
            pl.BlockSpec((1, RET_V_W), lambda b, t: (0, 0)),
            pl.BlockSpec((None,) + st_shape, lambda b, t: (b, 0, 0, 0)),
        ],
        out_specs=[
            tokspec(d),
            pl.BlockSpec((None,) + st_shape, lambda b, t: (b, 0, 0, 0)),
        ],
        out_shape=[jax.ShapeDtypeStruct(x.shape, F32),
                   jax.ShapeDtypeStruct((bsz,) + st_shape, F32)],
        scratch_shapes=[pltpu.VMEM(st_shape, F32),
                        pltpu.VMEM((tm, d), BF16),
                        pltpu.VMEM((tm + POOL_SUB, POOL_WIDTH), F32)],
        compiler_params=_cparams(("arbitrary", "arbitrary")),
        name="ret_out",
    )(x, mod8, p, p, p, pm, invc, q, k, v, sg, cb, wpool_bd, pscale, rng, w_out,
      lgf_q, lgf_v, lgb_v, s0)


def _head_norm_rope(main, aux, g_main, g_aux):
    ms = jnp.sum(main * main, axis=-1, keepdims=True) * (1.0 / MLA_QK)
    r = lax.rsqrt(ms + EPS)
    return (main * r) * g_main + (aux * r) * g_aux


def _mlaproj_kernel(x_ref, mod_ref, g_ref, win_ref, qag_ref, kvag_ref, wqb_ref, wkv_ref,
                    qg_ref, qgs_ref, kg_ref, kgs_ref, qa_t_ref, qb_t_ref, ka_t_ref, kb_t_ref,
                    q_ref, k_ref, v_ref, qan_scr, kvn_scr, kr_scr, gt_scr):
    h = _rms_mod(x_ref[...], g_ref[...], mod_ref[3:4, :], mod_ref[4:5, :])
    proj = jnp.dot(h.astype(BF16), win_ref[...], preferred_element_type=F32)
    qa = proj[:, :MLA_Q_RANK]
    kva = proj[:, MLA_Q_RANK:MLA_Q_RANK + MLA_KV_RANK]
    r0 = MLA_Q_RANK + MLA_KV_RANK
    kr_scr[0] = proj[:, r0:r0 + HEAD_SLAB]
    kr_scr[1] = proj[:, r0 + HEAD_SLAB:r0 + 2 * HEAD_SLAB]
    qan = qa * lax.rsqrt(jnp.mean(qa * qa, axis=-1, keepdims=True) + EPS) * qag_ref[...]
    kvn = kva * lax.rsqrt(jnp.mean(kva * kva, axis=-1, keepdims=True) + EPS) * kvag_ref[...]
    qan_scr[...] = qan.astype(BF16)
    kvn_scr[...] = kvn.astype(BF16)
    gt_scr[0] = qg_ref[...] * qa_t_ref[...]
    gt_scr[1] = qgs_ref[...] * qb_t_ref[...]
    gt_scr[2] = kg_ref[...] * ka_t_ref[...]
    gt_scr[3] = kgs_ref[...] * kb_t_ref[...]
    tm = proj.shape[0]
    ones_rows = jnp.where(lax.broadcasted_iota(jnp.int32, (V_ROWS - MLA_V, tm), 0) == 0,
                          1.0, 0.0).astype(BF16)

    def head(hd, carry):
        c0 = pl.multiple_of(hd * (2 * HEAD_SLAB), 2 * HEAD_SLAB)
        qh = jnp.dot(qan_scr[...], wqb_ref[:, pl.ds(c0, 2 * HEAD_SLAB)], preferred_element_type=F32)
        kvh = jnp.dot(kvn_scr[...], wkv_ref[:, pl.ds(c0, 2 * HEAD_SLAB)], preferred_element_type=F32)
        q_ref[hd] = _head_norm_rope(qh[:, :HEAD_SLAB], qh[:, HEAD_SLAB:], gt_scr[0], gt_scr[1]).T.astype(BF16)
        k_ref[hd] = _head_norm_rope(kvh[:, :HEAD_SLAB] + kr_scr[0], kr_scr[1],
                                    gt_scr[2], gt_scr[3]).astype(BF16)
        v_ref[hd, 0:MLA_V, :] = kvh[:, HEAD_SLAB:].T.astype(BF16)
        v_ref[hd, MLA_V:V_ROWS, :] = ones_rows
        return carry
    lax.fori_loop(0, MLA_HEADS, head, 0, unroll=MLA_HEAD_UNROLL)


def _mla_project(x, mod8, mod_row, g, w_in, qa_g, kva_g, w_qb, w_kv, qn_g, qn_gs, kn_g, kn_gs,
                 q_ta, q_tb, k_ta, k_tb):
    bsz, seq, d = x.shape
    tm = min(MLA_TM, seq)
    head_out = jax.ShapeDtypeStruct((bsz, MLA_HEADS, seq, HEAD_SLAB), BF16)
    hspec = pl.BlockSpec((None, MLA_HEADS, tm, HEAD_SLAB), lambda b, t: (b, 0, t, 0))
    tspec = pl.BlockSpec((tm, HEAD_SLAB), lambda b, t: (t, 0))
    row = lambda n: pl.BlockSpec((1, n), lambda b, t: (0, 0))
    return pl.pallas_call(
        _mlaproj_kernel,
        grid=(bsz, seq // tm),
        in_specs=[
            pl.BlockSpec((None, tm, d), lambda b, t: (b, t, 0)),
            pl.BlockSpec((None, N_MOD, d), lambda b, t: (mod_row(b), 0, 0)),
            row(d),
            _resident(w_in.shape), row(MLA_Q_RANK), row(MLA_KV_RANK),
            _resident(w_qb.shape), _resident(w_kv.shape),
            row(HEAD_SLAB), row(HEAD_SLAB), row(HEAD_SLAB), row(HEAD_SLAB),
            tspec, tspec, tspec, tspec,
        ],
        out_specs=[pl.BlockSpec((None, MLA_HEADS, HEAD_SLAB, tm), lambda b, t: (b, 0, 0, t)),
                   hspec,
                   pl.BlockSpec((None, MLA_HEADS, V_ROWS, tm), lambda b, t: (b, 0, 0, t))],
        out_shape=[jax.ShapeDtypeStruct((bsz, MLA_HEADS, HEAD_SLAB, seq), BF16),
                   head_out,
                   jax.ShapeDtypeStruct((bsz, MLA_HEADS, V_ROWS, seq), BF16)],
        scratch_shapes=[pltpu.VMEM((tm, MLA_Q_RANK), BF16), pltpu.VMEM((tm, MLA_KV_RANK), BF16),
                        pltpu.VMEM((2, tm, HEAD_SLAB), F32), pltpu.VMEM((4, tm, HEAD_SLAB), F32)],
        compiler_params=_cparams(("parallel", "parallel")),
        name="mla_project",
    )(x, mod8, g.reshape(1, d), w_in, qa_g, kva_g, w_qb, w_kv, qn_g, qn_gs, kn_g, kn_gs,
      q_ta, q_tb, k_ta, k_tb)


def _attn_kernel(*refs, n_spans, span_blocks, tk, group_spans, tq, n_q):
    if n_spans:
        q_ref, kc_ref, vct_ref, k_ref, vt_ref, o_ref, m_scr, acc_scr, s_scr, mb_scr = refs
    else:
        q_ref, kc_ref, vct_ref, o_ref, m_scr, acc_scr = refs

    def colmax(st):
        return jnp.max(st, axis=0, keepdims=True)

    def finalize(row0):
        acc = acc_scr[...]
        out_t = acc[0:MLA_V, :] * (1.0 / acc[MLA_V:MLA_V + 1, :])
        o_ref[pl.ds(row0, tq), :] = out_t.T.astype(o_ref.dtype)

    def softmax_pv(st, m_blk, vtb, first):
        if first:
            m_new = m_blk
        else:
            m_prev = m_scr[...]
            m_new = jnp.maximum(m_prev, m_blk)
        p = jnp.exp2(st - m_new).astype(BF16)
        pv = jnp.dot(vtb, p, preferred_element_type=F32)
        if first:
            acc_scr[...] = pv
        else:
            acc_scr[...] = jnp.exp2(m_prev - m_new) * acc_scr[...] + pv
        m_scr[...] = m_new

    if not n_spans:
        st_c = jnp.dot(kc_ref[...], q_ref[...], preferred_element_type=F32)
        softmax_pv(st_c, colmax(st_c), vct_ref[...], True)
        finalize(0)
        return

    span = span_blocks * tk
    lc = kc_ref.shape[0]

    def query_tile(q, prev_row0):
        def produce(slot, kb):
            rows = kb.shape[0]
            st = jnp.dot(kb, q, preferred_element_type=F32)
            s_scr[slot, 0:rows, :] = st
            blk = min(tk, rows)
            for sb in range(rows // blk):
                mb_scr[slot, sb] = colmax(st[sb * blk:(sb + 1) * blk, :])

        def consume(slot, v_off):
            for sb in range(span_blocks):
                softmax_pv(s_scr[slot, sb * tk:(sb + 1) * tk, :], mb_scr[slot, sb],
                           vt_ref[:, pl.ds(v_off + sb * tk, tk)], False)

        produce(0, k_ref[0:span, :])
        if prev_row0 is not None:
            finalize(prev_row0)
        m_scr[...] = jnp.full(m_scr.shape, -jnp.inf, F32)
        acc_scr[...] = jnp.zeros(acc_scr.shape, F32)

        def group(i, carry):
            base = i * (group_spans * span)
            for u in range(group_spans):
                cur = pl.multiple_of(base + u * span, span)
                nxt = pl.multiple_of(base + (u + 1) * span, span)
                produce((u + 1) % 2, k_ref[pl.ds(nxt, span), :])
                consume(u % 2, cur)
            return carry
        lax.fori_loop(0, n_spans // group_spans - 1, group, 0)
        e0 = n_spans - group_spans
        for u in range(group_spans):
            cur = (e0 + u) * span
            if u + 1 < group_spans:
                produce((u + 1) % 2, k_ref[cur + span:cur + 2 * span, :])
            else:
                produce((u + 1) % 2, kc_ref[...])
            consume(u % 2, cur)
        c_slot = group_spans % 2
        softmax_pv(s_scr[c_slot, 0:lc, :], mb_scr[c_slot, 0], vct_ref[...], False)

    query_tile(q_ref[:, 0:tq], None)

    def later_tile(i, carry):
        off = pl.multiple_of(i * tq, tq)
        query_tile(q_ref[:, pl.ds(off, tq)], pl.multiple_of(off - tq, tq))
        return carry
    lax.fori_loop(1, n_q, later_tile, 0)
    finalize((n_q - 1) * tq)


def _attention(q, kc, vct, k=None, vt=None):
    bsz, nh, hs, lq = q.shape
    lc = kc.shape[2]
    tq = min(ATT_TQ, lq)
    qspec = pl.BlockSpec((None, None, hs, lq), lambda b, h: (b, h, 0, 0))
    kfull = lambda n: pl.BlockSpec((None, None, n, hs), lambda b, h: (b, h, 0, 0))
    vfull = lambda n: pl.BlockSpec((None, None, V_ROWS, n), lambda b, h: (b, h, 0, 0))
    in_specs = [qspec, kfull(lc), vfull(lc)]
    args = [q, kc, vct]
    n_spans = 0
    grp = 0
    tk = ATT_TK
    span_blocks = 1
    if k is not None:
        lk = k.shape[2]
        tk = min(ATT_TK, lk)
        span_blocks = min(ATT_SPAN, lk // tk)
        n_spans = lk // (tk * span_blocks)
        grp = min(ATT_GROUP, n_spans)
        assert lk % (tk * span_blocks) == 0 and n_spans % grp == 0, "latent keys are consumed in whole groups"
        assert grp % 2 == 0 or n_spans == grp, "score slots alternate within a group"
        assert lc <= tk, "the context block reuses a latent score slot"
        in_specs += [kfull(lk), vfull(lk)]
        args += [k, vt]
    scratch = [pltpu.VMEM((1, tq), F32), pltpu.VMEM((V_ROWS, tq), F32)]
    if n_spans:
        scratch += [pltpu.VMEM((2, span_blocks * tk, tq), F32), pltpu.VMEM((2, span_blocks, 1, tq), F32)]
    return pl.pallas_call(
        functools.partial(_attn_kernel, n_spans=n_spans, span_blocks=span_blocks, tk=tk,
                          group_spans=grp, tq=tq, n_q=lq // tq),
        grid=(bsz, nh),
        in_specs=in_specs,
        out_specs=pl.BlockSpec((None, lq, hs), lambda b, h: (b, 0, h)),
        out_shape=jax.ShapeDtypeStruct((bsz, lq, nh * hs), BF16),
        scratch_shapes=scratch,
        compiler_params=_cparams(("parallel", "parallel")),
        name="mla_attention",
    )(*args)


def _pair_swap_cols(w):
    w2 = w.reshape(w.shape[:-1] + (w.shape[-1] // 2, 2))
    return jnp.stack([-w2[..., 1], w2[..., 0]], axis=-1).reshape(w.shape)


def _rope_angles(seq, rot_dim):
    pos = jnp.arange(seq)
    row = (pos // GRID_W).astype(F32)
    col = (pos % GRID_W).astype(F32)
    n_freq = rot_dim // 4
    inv = ROPE_BASE ** (-jnp.arange(n_freq, dtype=F32) / n_freq)
    ang = jnp.concatenate([row[:, None] * inv, col[:, None] * inv], axis=-1)
    return jnp.repeat(jnp.cos(ang), 2, axis=-1), jnp.repeat(jnp.sin(ang), 2, axis=-1)


def _ret_tables(seq, ctx_len):
    cos, sin = _rope_angles(seq, RET_DK)
    cos2, sin2 = jnp.tile(cos, (1, 2)), jnp.tile(sin, (1, 2))
    return cos2, sin2, jnp.ones((ctx_len, LANES), F32), jnp.zeros((ctx_len, LANES), F32)


def _mla_tables(seq, ctx_len):
    cos, sin = _rope_angles(seq, MLA_ROPE)
    one = jnp.ones((seq, MLA_NOPE), F32)
    zero_tail = jnp.zeros((seq, HEAD_SLAB - MLA_QK), F32)
    zero_head = jnp.zeros((seq, MLA_NOPE), F32)
    ta = jnp.concatenate([one, cos, zero_tail], axis=-1)
    tb = jnp.concatenate([zero_head, sin, zero_tail], axis=-1)
    ident = jnp.concatenate([jnp.ones((ctx_len, MLA_QK), F32),
                             jnp.zeros((ctx_len, HEAD_SLAB - MLA_QK), F32)], axis=-1)
    return ta, tb, ident, jnp.zeros((ctx_len, HEAD_SLAB), F32)


def _rope_slab(rope_cols):
    lead = rope_cols.shape[:-1]
    return jnp.concatenate([jnp.zeros(lead + (MLA_NOPE,), F32), rope_cols,
                            jnp.zeros(lead + (HEAD_SLAB - MLA_QK,), F32)], axis=-1)


def _slab_gains(gvec):
    rope = gvec[MLA_NOPE:].reshape(MLA_ROPE // 2, 2)
    swapped = jnp.stack([rope[:, 1], rope[:, 0]], axis=-1).reshape(MLA_ROPE)
    main = jnp.concatenate([gvec, jnp.zeros((HEAD_SLAB - MLA_QK,), F32)])
    return main.reshape(1, HEAD_SLAB), _rope_slab(swapped).reshape(1, HEAD_SLAB)


def _prep_even(ab_w_in, pool_w, ab_w_out):
    wq = ab_w_in[:, POOL_WIDTH:POOL_WIDTH + RET_QK_W]
    wk = ab_w_in[:, POOL_WIDTH + RET_QK_W:POOL_WIDTH + 2 * RET_QK_W]
    w_ext = jnp.concatenate([ab_w_in, _pair_swap_cols(wq), _pair_swap_cols(wk)], axis=-1).astype(BF16)
    wpool_bd = jnp.zeros((POOL_WIDTH, POOL_WIDTH), F32)
    for gi in range(POOL_GROUPS):
        s = slice(gi * POOL_CG, (gi + 1) * POOL_CG)
        wpool_bd = wpool_bd.at[s, s].set(pool_w[gi])
    return w_ext, wpool_bd.astype(BF16), ab_w_out.astype(BF16)


def _prep_odd(w_in, w_qb, w_kvb, w_out):
    wkr = w_in[:, MLA_Q_RANK + MLA_KV_RANK:]
    w_in_ext = jnp.concatenate([w_in[:, :MLA_Q_RANK + MLA_KV_RANK], _rope_slab(wkr),
                                _rope_slab(_pair_swap_cols(wkr))], axis=-1).astype(BF16)
    wq3 = w_qb.reshape(MLA_Q_RANK, MLA_HEADS, MLA_QK)
    wq_main = jnp.concatenate([wq3, jnp.zeros((MLA_Q_RANK, MLA_HEADS, HEAD_SLAB - MLA_QK), F32)], axis=-1)
    wq_aux = _rope_slab(_pair_swap_cols(wq3[..., MLA_NOPE:]))
    w_qb_ext = jnp.concatenate([wq_main, wq_aux], axis=-1).reshape(
        MLA_Q_RANK, MLA_HEADS * 2 * HEAD_SLAB).astype(BF16)
    wkv3 = w_kvb.reshape(MLA_KV_RANK, MLA_HEADS, MLA_NOPE + MLA_V)
    wk_slab = jnp.concatenate([wkv3[..., :MLA_NOPE],
                               jnp.zeros((MLA_KV_RANK, MLA_HEADS, HEAD_SLAB - MLA_NOPE), F32)], axis=-1)
    w_kv_ext = jnp.concatenate([wk_slab, wkv3[..., MLA_NOPE:]], axis=-1).reshape(
        MLA_KV_RANK, MLA_HEADS * 2 * HEAD_SLAB).astype(BF16)
    return w_in_ext, w_qb_ext, w_kv_ext, w_out.astype(BF16)


def kernel(x, c, ctx, c_ctx, ada_w, ada_b, norm_g, ffn_w_in, ffn_w_out, ab_w_in, pool_w, pool_scale,
           ret_log_decay, ret_norm_g, ab_w_out, mla_w_in, mla_qa_g, mla_kva_g, mla_w_qb, mla_w_kvb,
           mla_qn_g, mla_kn_g, mla_w_out):
    bsz, seq, d = x.shape
    ctx_len = ctx.shape[1]
    depth = ada_w.shape[0]
    ctx_row = bsz
    cond8 = jnp.zeros((8, d), F32).at[:bsz].set(c).at[ctx_row].set(c_ctx)
    mod_all = _ada_modulation(cond8, ada_w, ada_b).reshape(depth, 8, N_MOD, d)
    lat_row = lambda b: b
    ctx_mod_row = lambda b: ctx_row

    ret_cos, ret_sin, ret_cos_c, ret_sin_c = _ret_tables(seq, ctx_len)
    m_ta, m_tb, m_ta_c, m_tb_c = _mla_tables(seq, ctx_len)
    q_scale = MLA_QK ** -0.5 * LOG2_E
    zero_state = jnp.zeros((bsz, RET_PAIRS, 2 * RET_DK, 2 * RET_DV), F32)
    ffn_in_b, ffn_out_b = ffn_w_in.astype(BF16), ffn_w_out.astype(BF16)

    xc = ctx
    for i in range(depth):
        last = i == depth - 1
        j = i // 2
        mod8 = mod_all[i]
        x = _ffn_half(x, mod8, lat_row, 0, norm_g[i, 0], ffn_in_b, ffn_out_b, (i, 0))
        xc = _ffn_half(xc, mod8, ctx_mod_row, 0, norm_g[i, 0], ffn_in_b, ffn_out_b, (i, 0))
        mix = mix_c = w_mix = None
        if i % 2 == 0:
            w_ext, wpool_bd, w_out = _prep_even(ab_w_in[j], pool_w[j], ab_w_out[j])
            lg = ret_log_decay[j]
            lgf_q = jnp.repeat(lg[0], RET_DK).reshape(1, RET_QK_W)
            lgb_q = jnp.repeat(lg[1], RET_DK).reshape(1, RET_QK_W)
            lgf_v = jnp.repeat(lg[0], RET_DV).reshape(1, RET_V_W)
            lgb_v = jnp.repeat(lg[1], RET_DV).reshape(1, RET_V_W)
            pscale = pool_scale[j].reshape(1, POOL_WIDTH)
            rng = ret_norm_g[j].reshape(1, RET_V_W)
            pc, qc, kc, vc, sgc, cbc, state_b = _ab_project(
                xc, mod8, ctx_mod_row, norm_g[i, 1], w_ext, ret_cos_c, ret_sin_c, lgb_q, lgb_v, zero_state)
            xc_mixed, state_f = _ret_out(xc, mod8, ctx_mod_row, pc, qc, kc, vc, sgc, cbc, wpool_bd, pscale,
                                         rng, w_out, lgf_q, lgf_v, lgb_v, zero_state)
            p, q, k, v, sg, cb, _ = _ab_project(
                x, mod8, lat_row, norm_g[i, 1], w_ext, ret_cos, ret_sin, lgb_q, lgb_v, state_b)
            x, _ = _ret_out(x, mod8, lat_row, p, q, k, v, sg, cb, wpool_bd, pscale, rng, w_out,
                            lgf_q, lgf_v, lgb_v, state_f)
        else:
            w_in_ext, w_qb_ext, w_kv_ext, w_out = _prep_odd(mla_w_in[j], mla_w_qb[j], mla_w_kvb[j],
                                                            mla_w_out[j])
            qa_g = mla_qa_g[j].reshape(1, MLA_Q_RANK)
            kva_g = mla_kva_g[j].reshape(1, MLA_KV_RANK)
            qn_g, qn_gs = _slab_gains(mla_qn_g[j])
            kn_g, kn_gs = _slab_gains(mla_kn_g[j])
            qc, kc, vc = _mla_project(xc, mod8, ctx_mod_row, norm_g[i, 1], w_in_ext, qa_g, kva_g, w_qb_ext,
                                      w_kv_ext, qn_g, qn_gs, kn_g, kn_gs,
                                      m_ta_c * q_scale, m_tb_c, m_ta_c, m_tb_c)
            q, k, v = _mla_project(x, mod8, lat_row, norm_g[i, 1], w_in_ext, qa_g, kva_g, w_qb_ext,
                                   w_kv_ext, qn_g, qn_gs, kn_g, kn_gs,
                                   m_ta * q_scale, m_tb * q_scale, m_ta, m_tb)
            mix, w_mix = _attention(q, kc, vc, k, v), w_out
            if not last:
                mix_c = _attention(qc, kc, vc)
                xc_mixed = xc
        x = _ffn_half(x, mod8, lat_row, 2, norm_g[i, 2], ffn_in_b, ffn_out_b, (i, 1), mix, w_mix)
        if not last:
            xc = _ffn_half(xc_mixed, mod8, ctx_mod_row, 2, norm_g[i, 2], ffn_in_b, ffn_out_b, (i, 1),
                           mix_c, w_mix)
    return x
```

```python
import functools

import numpy as np
import jax
import jax.numpy as jnp
from jax import lax
from jax.experimental import pallas as pl
from jax.experimental.pallas import tpu as pltpu

F32 = jnp.float32
BF16 = jnp.bfloat16

D_MODEL = 1024
GRID_W = 64
D_FF = 2816
FFN_RESIDUAL = 0.5
N_MOD = 9
ROPE_BASE = 10000.0
EPS = 1e-6
POOL_GROUPS = 4
POOL_CG = 64
POOL_WIDTH = POOL_GROUPS * POOL_CG
POOL_WINDOWS = (2, 4, 8, 16)
POOL_HALO = 8
POOL_SUB = 128
RET_HEADS = 6
RET_PAIRS = RET_HEADS // 2
RET_DK = 64
RET_DV = 128
RET_CHUNK = 128
RET_QK_W = RET_HEADS * RET_DK
RET_V_W = RET_HEADS * RET_DV
AB_IN = POOL_WIDTH + 2 * RET_QK_W + 2 * RET_V_W
MLA_HEADS = 8
MLA_Q_RANK = 384
MLA_KV_RANK = 256
MLA_NOPE = 64
MLA_ROPE = 32
MLA_V = 128
MLA_QK = MLA_NOPE + MLA_ROPE
HEAD_SLAB = 128
V_ROWS = MLA_V + 16
LOG2_E = 1.4426950408889634

LANES = 128
V7X_VMEM_BYTES = 64 * 1024 * 1024
VMEM_LIMIT_BYTES = 56 * 1024 * 1024

ADA_TN = 1536
FFN_TM = 1024
FFN_SUB = 256
MIX_TM = 1024
AB_TM = 1024
AB_SUB = 256
MLA_TM = 1024
MLA_HEAD_UNROLL = 8
ATT_TQ = 1024
ATT_TK = 512
ATT_SPAN = 1
ATT_GROUP = 4


def _cparams(sem):
    return pltpu.CompilerParams(dimension_semantics=sem, vmem_limit_bytes=VMEM_LIMIT_BYTES)


def _resident(shape):
    nd = len(shape)
    return pl.BlockSpec(shape, lambda *_: (0,) * nd, pipeline_mode=pl.Buffered(1))


def _silu(x):
    return x * (1.0 / (1.0 + jnp.exp(-x)))


def _rms_mod(x, g, shift, scale):
    ms = jnp.mean(x * x, axis=-1, keepdims=True)
    return (x * lax.rsqrt(ms + EPS) * g) * (1.0 + scale) + shift


def _ada_kernel(c_ref, w_ref, b_ref, o_ref):
    s = _silu(c_ref[...])
    o_ref[...] = jnp.dot(s, w_ref[...], precision=lax.Precision.HIGHEST,
                         preferred_element_type=F32) + b_ref[...]


def _ada_modulation(cond8, ada_w, ada_b):
    depth, d, n = ada_w.shape
    return pl.pallas_call(
        _ada_kernel,
        grid=(depth, n // ADA_TN),
        in_specs=[
            pl.BlockSpec((8, d), lambda i, j: (0, 0)),
            pl.BlockSpec((None, d, ADA_TN), lambda i, j: (i, 0, j)),
            pl.BlockSpec((None, 1, ADA_TN), lambda i, j: (i, 0, j)),
        ],
        out_specs=pl.BlockSpec((None, 8, ADA_TN), lambda i, j: (i, 0, j)),
        out_shape=jax.ShapeDtypeStruct((depth, 8, n), F32),
        compiler_params=_cparams(("parallel", "parallel")),
        name="ada_mod",
    )(cond8, ada_w, ada_b.reshape(depth, 1, n))


def _ffn_kernel(*refs, k, mixed):
    if mixed:
        x_ref, mod_ref, g_ref, win_ref, wout_ref, mix_ref, wmix_ref, o_ref = refs
    else:
        x_ref, mod_ref, g_ref, win_ref, wout_ref, o_ref = refs
    tm = x_ref.shape[0]
    sub = min(FFN_SUB, tm)
    for s in range(tm // sub):
        rs = slice(s * sub, (s + 1) * sub)
        x = x_ref[rs, :]
        if mixed:
            x = x + mod_ref[5:6, :] * jnp.dot(mix_ref[rs, :], wmix_ref[...], preferred_element_type=F32)
        h = _rms_mod(x, g_ref[...], mod_ref[3 * k:3 * k + 1, :], mod_ref[3 * k + 1:3 * k + 2, :])
        ab = jnp.dot(h.astype(BF16), win_ref[...], preferred_element_type=F32)
        act = (_silu(ab[:, :D_FF]) * ab[:, D_FF:]).astype(BF16)
        y = jnp.dot(act, wout_ref[...], preferred_element_type=F32)
        o_ref[rs, :] = x + (FFN_RESIDUAL * mod_ref[3 * k + 2:3 * k + 3, :]) * y


def _ffn_half(x, mod8, mod_row, k, g, w_in, w_out, which, mix=None, w_mix=None):
    bsz, seq, d = x.shape
    tm = min(FFN_TM, seq)
    layer, half = which
    pick = lambda shape: pl.BlockSpec((None, None) + tuple(shape[2:]), lambda b, t: (layer, half, 0, 0),
                                      pipeline_mode=pl.Buffered(1))
    in_specs = [
        pl.BlockSpec((None, tm, d), lambda b, t: (b, t, 0)),
        pl.BlockSpec((None, N_MOD, d), lambda b, t: (mod_row(b), 0, 0)),
        pl.BlockSpec((1, d), lambda b, t: (0, 0)),
        pick(w_in.shape),
        pick(w_out.shape),
    ]
    args = [x, mod8, g.reshape(1, d), w_in, w_out]
    if mix is not None:
        in_specs += [pl.BlockSpec((None, tm, mix.shape[-1]), lambda b, t: (b, t, 0)),
                     _resident(w_mix.shape)]
        args += [mix, w_mix]
    return pl.pallas_call(
        functools.partial(_ffn_kernel, k=k, mixed=mix is not None),
        grid=(bsz, seq // tm),
        in_specs=in_specs,
        out_specs=pl.BlockSpec((None, tm, d), lambda b, t: (b, t, 0)),
        out_shape=jax.ShapeDtypeStruct(x.shape, F32),
        compiler_params=_cparams(("parallel", "parallel")),
        name="ffn_half",
    )(*args)


def _pair_tables(lg_row, expo):
    return jnp.exp(lg_row * expo)


def _bd_mask():
    r = lax.broadcasted_iota(jnp.int32, (2 * RET_DK, 2 * RET_DV), 0)
    c = lax.broadcasted_iota(jnp.int32, (2 * RET_DK, 2 * RET_DV), 1)
    same_head = jnp.where(r < RET_DK, 0, 1) == jnp.where(c < RET_DV, 0, 1)
    return jnp.where(same_head, 1.0, 0.0)


def _abproj_kernel(x_ref, mod_ref, g_ref, w_ref, cos_ref, sin_ref, lgq_ref, lgv_ref, s0_ref,
                   p_ref, q_ref, k_ref, v_ref, sg_ref, cb_ref, sfin_ref, st_scr, *, tm, nt):
    t = pl.program_id(1)

    @pl.when(t == 0)
    def _():
        st_scr[...] = s0_ref[...]

    q0, k0 = POOL_WIDTH, POOL_WIDTH + RET_QK_W
    v0, g0 = k0 + RET_QK_W, k0 + RET_QK_W + RET_V_W
    row = lax.broadcasted_iota(jnp.int32, (RET_CHUNK, LANES), 0).astype(F32)
    bdm = _bd_mask()
    k_scale = RET_DK ** -0.5
    sub = min(AB_SUB, tm)
    even_lane = (lax.broadcasted_iota(jnp.int32, (sub, LANES), 1) & 1) == 0

    def pair_partner(v):
        return jnp.where(even_lane, pltpu.roll(v, LANES - 1, axis=1), pltpu.roll(v, 1, axis=1))
    for s in reversed(range(tm // sub)):
        r0 = s * sub
        ts = slice(r0, r0 + sub)
        h = _rms_mod(x_ref[ts, :], g_ref[...], mod_ref[3:4, :], mod_ref[4:5, :])
        proj = jnp.dot(h.astype(BF16), w_ref[...], preferred_element_type=F32)
        p_ref[ts, :] = proj[:, :POOL_WIDTH]
        v_ref[ts, :] = proj[:, v0:v0 + RET_V_W].astype(BF16)
        sg_ref[ts, :] = _silu(proj[:, g0:g0 + RET_V_W]).astype(BF16)
        cos = cos_ref[ts, :]
        sin = sin_ref[ts, :]
        for j in range(RET_PAIRS):
            sl = slice(LANES * j, LANES * (j + 1))
            qp = proj[:, q0 + LANES * j:q0 + LANES * (j + 1)]
            kp = proj[:, k0 + LANES * j:k0 + LANES * (j + 1)]
            qr = qp * cos + pair_partner(qp) * sin
            kr = (kp * cos + pair_partner(kp) * sin) * k_scale
            q_ref[ts, sl] = qr.astype(BF16)
            k_ref[ts, sl] = kr.astype(BF16)
            lgq = lgq_ref[:, sl]
            lgv = lgv_ref[:, 2 * LANES * j:2 * LANES * (j + 1)]
            xi = _pair_tables(lgq, float(RET_CHUNK) - row)
            zeta = _pair_tables(lgq, row)
            dec = jnp.exp(lgv * float(RET_CHUNK))
            vsl = slice(2 * LANES * j, 2 * LANES * (j + 1))
            for c in reversed(range(sub // RET_CHUNK)):
                ls = slice(c * RET_CHUNK, (c + 1) * RET_CHUNK)
                rs = slice(r0 + c * RET_CHUNK, r0 + (c + 1) * RET_CHUNK)
                bd = st_scr[j]
                qx = (qr[ls] * xi).astype(BF16)
                cb_ref[rs, vsl] = jnp.dot(qx, bd.astype(BF16),
                                          preferred_element_type=F32).astype(cb_ref.dtype)
                kz = (kr[ls] * zeta).astype(BF16)
                upd = lax.dot_general(kz, v_ref[rs, vsl], (((0,), (0,)), ((), ())),
                                      preferred_element_type=F32)
                st_scr[j] = dec * bd + bdm * upd

    @pl.when(t == nt - 1)
    def _():
        sfin_ref[...] = st_scr[...]


def _ab_project(x, mod8, mod_row, g, w_ext, cos, sin, lg_q, lg_v, s0):
    bsz, seq, d = x.shape
    tm = min(AB_TM, seq)
    nt = seq // tm
    rev = lambda b, t: (b, nt - 1 - t, 0)
    tok = lambda w, dt: jax.ShapeDtypeStruct((bsz, seq, w), dt)
    st_shape = (RET_PAIRS, 2 * RET_DK, 2 * RET_DV)
    return pl.pallas_call(
        functools.partial(_abproj_kernel, tm=tm, nt=nt),
        grid=(bsz, nt),
        in_specs=[
            pl.BlockSpec((None, tm, d), rev),
            pl.BlockSpec((None, N_MOD, d), lambda b, t: (mod_row(b), 0, 0)),
            pl.BlockSpec((1, d), lambda b, t: (0, 0)),
            _resident(w_ext.shape),
            pl.BlockSpec((tm, LANES), lambda b, t: (nt - 1 - t, 0)),
            pl.BlockSpec((tm, LANES), lambda b, t: (nt - 1 - t, 0)),
            pl.BlockSpec((1, RET_QK_W), lambda b, t: (0, 0)),
            pl.BlockSpec((1, RET_V_W), lambda b, t: (0, 0)),
            pl.BlockSpec((None,) + st_shape, lambda b, t: (b, 0, 0, 0)),
        ],
        out_specs=[
            pl.BlockSpec((None, tm, POOL_WIDTH), rev),
            pl.BlockSpec((None, tm, RET_QK_W), rev),
            pl.BlockSpec((None, tm, RET_QK_W), rev),
            pl.BlockSpec((None, tm, RET_V_W), rev),
            pl.BlockSpec((None, tm, RET_V_W), rev),
            pl.BlockSpec((None, tm, RET_V_W), rev),
            pl.BlockSpec((None,) + st_shape, lambda b, t: (b, 0, 0, 0)),
        ],
        out_shape=[tok(POOL_WIDTH, F32), tok(RET_QK_W, BF16), tok(RET_QK_W, BF16),
                   tok(RET_V_W, BF16), tok(RET_V_W, BF16), tok(RET_V_W, BF16),
                   jax.ShapeDtypeStruct((bsz,) + st_shape, F32)],
        scratch_shapes=[pltpu.VMEM(st_shape, F32)],
        compiler_params=_cparams(("arbitrary", "arbitrary")),
        name="ab_project",
    )(x, mod8, g.reshape(1, d), w_ext, cos, sin, lg_q, lg_v, s0)


def _retout_kernel(x_ref, mod_ref, pprev_ref, pcur_ref, pnext_ref, pm_ref, invc_ref,
                   q_ref, k_ref, v_ref, sg_ref, cb_ref, wpool_ref, pscale_ref, rng_ref, wout_ref,
                   lgfq_ref, lgfv_ref, lgbv_ref, s0_ref,
                   o_ref, sfin_ref, st_scr, ycat_scr, pe_scr, *, tm, nt):
    t = pl.program_id(1)

    @pl.when(t == 0)
    def _():
        st_scr[...] = s0_ref[...]

    pe_scr[0:POOL_HALO, :] = pprev_ref[...]
    pe_scr[POOL_HALO:POOL_HALO + tm, :] = pcur_ref[...]
    pe_scr[POOL_HALO + tm:2 * POOL_HALO + tm, :] = pnext_ref[...]
    pe_scr[2 * POOL_HALO + tm:, :] = jnp.zeros((POOL_SUB - 2 * POOL_HALO, POOL_WIDTH), F32)
    lane_grp = jnp.right_shift(lax.broadcasted_iota(jnp.int32, (POOL_SUB, POOL_WIDTH), 1),
                               POOL_CG.bit_length() - 1)
    for s in range(tm // POOL_SUB):
        r0 = s * POOL_SUB
        pe = pe_scr[r0:r0 + 2 * POOL_SUB, :]
        pe_hi = pe.astype(BF16)
        pe_lo = (pe - pe_hi.astype(F32)).astype(BF16)
        pooled = jnp.zeros((POOL_SUB, POOL_WIDTH), F32)
        for gi in range(POOL_GROUPS):
            win = pm_ref[s, gi]
            tot = jnp.dot(win, pe_hi, preferred_element_type=F32) \
                + jnp.dot(win, pe_lo, preferred_element_type=F32)
            pooled = jnp.where(lane_grp == gi, tot, pooled)
        rs = slice(r0, r0 + POOL_SUB)
        diffs = pooled * invc_ref[rs, :] - pcur_ref[rs, :]
        pool_y = jnp.dot(diffs.astype(BF16), wpool_ref[...], preferred_element_type=F32) * pscale_ref[...]
        ycat_scr[rs, 0:POOL_WIDTH] = pool_y.astype(BF16)

    ri = lax.broadcasted_iota(jnp.int32, (RET_CHUNK, RET_CHUNK), 0)
    ci = lax.broadcasted_iota(jnp.int32, (RET_CHUNK, RET_CHUNK), 1)
    rel = (ri - ci).astype(F32)
    row = lax.broadcasted_iota(jnp.int32, (RET_CHUNK, LANES), 0).astype(F32)
    lane = lax.broadcasted_iota(jnp.int32, (RET_CHUNK, LANES), 1)
    bdm = _bd_mask()
    zero_bf = jnp.zeros((RET_CHUNK, LANES), BF16)
    tables = []
    for j in range(RET_PAIRS):
        sl = slice(LANES * j, LANES * (j + 1))
        vsl = slice(2 * LANES * j, 2 * LANES * (j + 1))
        lgq = lgfq_ref[:, sl]
        xi = _pair_tables(lgq, row + 1.0)
        zeta = _pair_tables(lgq, float(RET_CHUNK - 1) - row)
        dec = jnp.exp(lgfv_ref[:, vsl] * float(RET_CHUNK))
        masks = []
        for a in range(2):
            hs = slice(2 * LANES * j + LANES * a, 2 * LANES * j + LANES * (a + 1))
            mf = jnp.exp(lgfv_ref[:, hs] * jnp.maximum(rel, 0.0))
            mb = jnp.exp(lgbv_ref[:, hs] * jnp.maximum(-rel, 0.0))
            masks.append(jnp.where(rel > 0, mf, jnp.where(rel < 0, mb, 2.0)))
        tables.append((xi, zeta, dec, masks))
    for c in range(tm // RET_CHUNK):
        rs = slice(c * RET_CHUNK, (c + 1) * RET_CHUNK)
        for j in range(RET_PAIRS):
            sl = slice(LANES * j, LANES * (j + 1))
            vsl = slice(2 * LANES * j, 2 * LANES * (j + 1))
            xi, zeta, dec, masks = tables[j]
            q2 = q_ref[rs, sl]
            k2 = k_ref[rs, sl]
            v2 = v_ref[rs, vsl]
            bd = st_scr[j]
            qx = (q2.astype(F32) * xi).astype(BF16)
            ret = jnp.dot(qx, bd.astype(BF16), preferred_element_type=F32) + cb_ref[rs, vsl].astype(F32)
            for a in range(2):
                qa = jnp.where(lane < RET_DK if a == 0 else lane >= RET_DK, q2, zero_bf)
                s = lax.dot_general(qa, k2, (((1,), (1,)), ((), ())), preferred_element_type=F32)
                pa = (s * masks[a]).astype(BF16)
                hs_v = slice(LANES * a, LANES * (a + 1))
                r = ret[:, hs_v] + jnp.dot(pa, v2[:, hs_v], preferred_element_type=F32)
                ms = jnp.mean(r * r, axis=-1, keepdims=True)
                hcol = 2 * LANES * j + LANES * a
                y = (r * lax.rsqrt(ms + EPS) * rng_ref[:, hcol:hcol + LANES]) \
                    * sg_ref[rs, hcol:hcol + LANES].astype(F32)
                ycat_scr[rs, POOL_WIDTH + hcol:POOL_WIDTH + hcol + LANES] = y.astype(BF16)
            kz = (k2.astype(F32) * zeta).astype(BF16)
            upd = lax.dot_general(kz, v2, (((0,), (0,)), ((), ())), preferred_element_type=F32)
            st_scr[j] = dec * bd + bdm * upd
        yy = jnp.dot(ycat_scr[rs, :], wout_ref[...], preferred_element_type=F32)
        o_ref[rs, :] = x_ref[rs, :] + mod_ref[5:6, :] * yy

    @pl.when(t == nt - 1)
    def _():
        sfin_ref[...] = st_scr[...]


def _pool_constants(tm, seq):
    nt = seq // tm
    nsub = tm // POOL_SUB
    kinds = [0] if nt == 1 else [0, 1, nt - 1]
    mats = np.zeros((len(kinds), nsub, POOL_GROUPS, POOL_SUB, 2 * POOL_SUB), np.float32)
    invc = np.zeros((len(kinds), tm, POOL_WIDTH), np.float32)
    for vi, tile in enumerate(kinds):
        for s in range(nsub):
            pos = tile * tm + s * POOL_SUB + np.arange(POOL_SUB)
            colpos = tile * tm - POOL_HALO + s * POOL_SUB + np.arange(2 * POOL_SUB)
            for gi, w in enumerate(POOL_WINDOWS):
                lo = w // 2
                hi = w - 1 - lo
                start = np.maximum(pos - lo, 0)
                end = np.minimum(pos + hi + 1, seq)
                mats[vi, s, gi] = (colpos[None, :] >= start[:, None]) & (colpos[None, :] < end[:, None])
                invc[vi, s * POOL_SUB:(s + 1) * POOL_SUB, gi * POOL_CG:(gi + 1) * POOL_CG] = \
                    (1.0 / (end - start))[:, None]
    return jnp.asarray(mats, BF16), jnp.asarray(invc, F32), nt


def _ret_out(x, mod8, mod_row, p, q, k, v, sg, cb, wpool_bd, pscale, rng, w_out, lgf_q, lgf_v, lgb_v, s0):
    bsz, seq, d = x.shape
    tm = min(MIX_TM, seq)
    pm, invc, nt = _pool_constants(tm, seq)
    hb = tm // POOL_HALO
    nhb = seq // POOL_HALO

    def variant(t):
        if nt == 1:
            return 0
        return jnp.where(t == 0, 0, jnp.where(t == nt - 1, 2, 1))

    cur = lambda b, t: (b, t, 0)
    st_shape = (RET_PAIRS, 2 * RET_DK, 2 * RET_DV)
    tokspec = lambda w: pl.BlockSpec((None, tm, w), cur)
    return pl.pallas_call(
        functools.partial(_retout_kernel, tm=tm, nt=nt),
        grid=(bsz, nt),
        in_specs=[
            tokspec(d),
            pl.BlockSpec((None, N_MOD, d), lambda b, t: (mod_row(b), 0, 0)),
            pl.BlockSpec((None, POOL_HALO, POOL_WIDTH), lambda b, t: (b, jnp.maximum(t * hb - 1, 0), 0)),
            tokspec(POOL_WIDTH),
            pl.BlockSpec((None, POOL_HALO, POOL_WIDTH),
                         lambda b, t: (b, jnp.minimum((t + 1) * hb, nhb - 1), 0)),
            pl.BlockSpec((None, tm // POOL_SUB, POOL_GROUPS, POOL_SUB, 2 * POOL_SUB),
                         lambda b, t: (variant(t), 0, 0, 0, 0)),
            pl.BlockSpec((None, tm, POOL_WIDTH), lambda b, t: (variant(t), 0, 0)),
            tokspec(RET_QK_W), tokspec(RET_QK_W), tokspec(RET_V_W), tokspec(RET_V_W), tokspec(RET_V_W),
            _resident(wpool_bd.shape),
            pl.BlockSpec((1, POOL_WIDTH), lambda b, t: (0, 0)),
            pl.BlockSpec((1, RET_V_W), lambda b, t: (0, 0)),
            _resident(w_out.shape),
            pl.BlockSpec((1, RET_QK_W), lambda b, t: (0, 0)),
            pl.BlockSpec((1, RET_V_W), lambda b, t: (0, 0)),
            pl.BlockSpec((1, RET_V_W), lambda b, t: (0, 0)),
            pl.BlockSpec((None,) + st_shape, lambda b, t: (b, 0, 0, 0)),
        ],
        out_specs=[
            tokspec(d),
            pl.BlockSpec((None,) + st_shape, lambda b, t: (b, 0, 0, 0)),
        ],
        out_shape=[jax.ShapeDtypeStruct(x.shape, F32),
                   jax.ShapeDtypeStruct((bsz,) + st_shape, F32)],
        scratch_shapes=[pltpu.VMEM(st_shape, F32),
                        pltpu.VMEM((tm, d), BF16),
                        pltpu.VMEM((tm + POOL_SUB, POOL_WIDTH), F32)],
        compiler_params=_cparams(("arbitrary", "arbitrary")),
        name="ret_out",
    )(x, mod8, p, p, p, pm, invc, q, k, v, sg, cb, wpool_bd, pscale, rng, w_out,
      lgf_q, lgf_v, lgb_v, s0)


def _head_norm_rope(main, aux, g_main, g_aux):
    ms = jnp.sum(main * main, axis=-1, keepdims=True) * (1.0 / MLA_QK)
    r = lax.rsqrt(ms + EPS)
    return (main * r) * g_main + (aux * r) * g_aux


def _mlaproj_kernel(x_ref, mod_ref, g_ref, win_ref, qag_ref, kvag_ref, wqb_ref, wkv_ref,
                    qg_ref, qgs_ref, kg_ref, kgs_ref, qa_t_ref, qb_t_ref, ka_t_ref, kb_t_ref,
                    q_ref, k_ref, v_ref, qan_scr, kvn_scr, kr_scr, gt_scr):
    h = _rms_mod(x_ref[...], g_ref[...], mod_ref[3:4, :], mod_ref[4:5, :])
    proj = jnp.dot(h.astype(BF16), win_ref[...], preferred_element_type=F32)
    qa = proj[:, :MLA_Q_RANK]
    kva = proj[:, MLA_Q_RANK:MLA_Q_RANK + MLA_KV_RANK]
    r0 = MLA_Q_RANK + MLA_KV_RANK
    kr_scr[0] = proj[:, r0:r0 + HEAD_SLAB]
    kr_scr[1] = proj[:, r0 + HEAD_SLAB:r0 + 2 * HEAD_SLAB]
    qan = qa * lax.rsqrt(jnp.mean(qa * qa, axis=-1, keepdims=True) + EPS) * qag_ref[...]
    kvn = kva * lax.rsqrt(jnp.mean(kva * kva, axis=-1, keepdims=True) + EPS) * kvag_ref[...]
    qan_scr[...] = qan.astype(BF16)
    kvn_scr[...] = kvn.astype(BF16)
    gt_scr[0] = qg_ref[...] * qa_t_ref[...]
    gt_scr[1] = qgs_ref[...] * qb_t_ref[...]
    gt_scr[2] = kg_ref[...] * ka_t_ref[...]
    gt_scr[3] = kgs_ref[...] * kb_t_ref[...]
    tm = proj.shape[0]
    ones_rows = jnp.where(lax.broadcasted_iota(jnp.int32, (V_ROWS - MLA_V, tm), 0) == 0,
                          1.0, 0.0).astype(BF16)

    def head(hd, carry):
        c0 = pl.multiple_of(hd * (2 * HEAD_SLAB), 2 * HEAD_SLAB)
        qh = jnp.dot(qan_scr[...], wqb_ref[:, pl.ds(c0, 2 * HEAD_SLAB)], preferred_element_type=F32)
        kvh = jnp.dot(kvn_scr[...], wkv_ref[:, pl.ds(c0, 2 * HEAD_SLAB)], preferred_element_type=F32)
        q_ref[hd] = _head_norm_rope(qh[:, :HEAD_SLAB], qh[:, HEAD_SLAB:], gt_scr[0], gt_scr[1]).T.astype(BF16)
        k_ref[hd] = _head_norm_rope(kvh[:, :HEAD_SLAB] + kr_scr[0], kr_scr[1],
                                    gt_scr[2], gt_scr[3]).astype(BF16)
        v_ref[hd, 0:MLA_V, :] = kvh[:, HEAD_SLAB:].T.astype(BF16)
        v_ref[hd, MLA_V:V_ROWS, :] = ones_rows
        return carry
    lax.fori_loop(0, MLA_HEADS, head, 0, unroll=MLA_HEAD_UNROLL)


def _mla_project(x, mod8, mod_row, g, w_in, qa_g, kva_g, w_qb, w_kv, qn_g, qn_gs, kn_g, kn_gs,
                 q_ta, q_tb, k_ta, k_tb):
    bsz, seq, d = x.shape
    tm = min(MLA_TM, seq)
    head_out = jax.ShapeDtypeStruct((bsz, MLA_HEADS, seq, HEAD_SLAB), BF16)
    hspec = pl.BlockSpec((None, MLA_HEADS, tm, HEAD_SLAB), lambda b, t: (b, 0, t, 0))
    tspec = pl.BlockSpec((tm, HEAD_SLAB), lambda b, t: (t, 0))
    row = lambda n: pl.BlockSpec((1, n), lambda b, t: (0, 0))
    return pl.pallas_call(
        _mlaproj_kernel,
        grid=(bsz, seq // tm),
        in_specs=[
            pl.BlockSpec((None, tm, d), lambda b, t: (b, t, 0)),
            pl.BlockSpec((None, N_MOD, d), lambda b, t: (mod_row(b), 0, 0)),
            row(d),
            _resident(w_in.shape), row(MLA_Q_RANK), row(MLA_KV_RANK),
            _resident(w_qb.shape), _resident(w_kv.shape),
            row(HEAD_SLAB), row(HEAD_SLAB), row(HEAD_SLAB), row(HEAD_SLAB),
            tspec, tspec, tspec, tspec,
        ],
        out_specs=[pl.BlockSpec((None, MLA_HEADS, HEAD_SLAB, tm), lambda b, t: (b, 0, 0, t)),
                   hspec,
                   pl.BlockSpec((None, MLA_HEADS, V_ROWS, tm), lambda b, t: (b, 0, 0, t))],
        out_shape=[jax.ShapeDtypeStruct((bsz, MLA_HEADS, HEAD_SLAB, seq), BF16),
                   head_out,
                   jax.ShapeDtypeStruct((bsz, MLA_HEADS, V_ROWS, seq), BF16)],
        scratch_shapes=[pltpu.VMEM((tm, MLA_Q_RANK), BF16), pltpu.VMEM((tm, MLA_KV_RANK), BF16),
                        pltpu.VMEM((2, tm, HEAD_SLAB), F32), pltpu.VMEM((4, tm, HEAD_SLAB), F32)],
        compiler_params=_cparams(("parallel", "parallel")),
        name="mla_project",
    )(x, mod8, g.reshape(1, d), w_in, qa_g, kva_g, w_qb, w_kv, qn_g, qn_gs, kn_g, kn_gs,
      q_ta, q_tb, k_ta, k_tb)


def _attn_kernel(*refs, n_spans, span_blocks, tk, group_spans, tq, n_q):
    if n_spans:
        q_ref, kc_ref, vct_ref, k_ref, vt_ref, o_ref, m_scr, acc_scr, s_scr, mb_scr = refs
    else:
        q_ref, kc_ref, vct_ref, o_ref, m_scr, acc_scr = refs

    def colmax(st):
        return jnp.max(st, axis=0, keepdims=True)

    def finalize(row0):
        acc = acc_scr[...]
        out_t = acc[0:MLA_V, :] * (1.0 / acc[MLA_V:MLA_V + 1, :])
        o_ref[pl.ds(row0, tq), :] = out_t.T.astype(o_ref.dtype)

    def softmax_pv(st, m_blk, vtb, first):
        if first:
            m_new = m_blk
        else:
            m_prev = m_scr[...]
            m_new = jnp.maximum(m_prev, m_blk)
        p = jnp.exp2(st - m_new).astype(BF16)
        pv = jnp.dot(vtb, p, preferred_element_type=F32)
        if first:
            acc_scr[...] = pv
        else:
            acc_scr[...] = jnp.exp2(m_prev - m_new) * acc_scr[...] + pv
        m_scr[...] = m_new

    if not n_spans:
        st_c = jnp.dot(kc_ref[...], q_ref[...], preferred_element_type=F32)
        softmax_pv(st_c, colmax(st_c), vct_ref[...], True)
        finalize(0)
        return

    span = span_blocks * tk
    lc = kc_ref.shape[0]

    def query_tile(q, prev_row0):
        def produce(slot, kb):
            rows = kb.shape[0]
            st = jnp.dot(kb, q, preferred_element_type=F32)
            s_scr[slot, 0:rows, :] = st
            blk = min(tk, rows)
            for sb in range(rows // blk):
                mb_scr[slot, sb] = colmax(st[sb * blk:(sb + 1) * blk, :])

        def consume(slot, v_off):
            for sb in range(span_blocks):
                softmax_pv(s_scr[slot, sb * tk:(sb + 1) * tk, :], mb_scr[slot, sb],
                           vt_ref[:, pl.ds(v_off + sb * tk, tk)], False)

        produce(0, k_ref[0:span, :])
        if prev_row0 is not None:
            finalize(prev_row0)
        m_scr[...] = jnp.full(m_scr.shape, -jnp.inf, F32)
        acc_scr[...] = jnp.zeros(acc_scr.shape, F32)

        def group(i, carry):
            base = i * (group_spans * span)
            for u in range(group_spans):
                cur = pl.multiple_of(base + u * span, span)
                nxt = pl.multiple_of(base + (u + 1) * span, span)
                produce((u + 1) % 2, k_ref[pl.ds(nxt, span), :])
                consume(u % 2, cur)
            return carry
        lax.fori_loop(0, n_spans // group_spans - 1, group, 0)
        e0 = n_spans - group_spans
        for u in range(group_spans):
            cur = (e0 + u) * span
            if u + 1 < group_spans:
                produce((u + 1) % 2, k_ref[cur + span:cur + 2 * span, :])
            else:
                produce((u + 1) % 2, kc_ref[...])
            consume(u % 2, cur)
        c_slot = group_spans % 2
        softmax_pv(s_scr[c_slot, 0:lc, :], mb_scr[c_slot, 0], vct_ref[...], False)

    query_tile(q_ref[:, 0:tq], None)

    def later_tile(i, carry):
        off = pl.multiple_of(i * tq, tq)
        query_tile(q_ref[:, pl.ds(off, tq)], pl.multiple_of(off - tq, tq))
        return carry
    lax.fori_loop(1, n_q, later_tile, 0)
    finalize((n_q - 1) * tq)


def _attention(q, kc, vct, k=None, vt=None):
    bsz, nh, hs, lq = q.shape
    lc = kc.shape[2]
    tq = min(ATT_TQ, lq)
    qspec = pl.BlockSpec((None, None, hs, lq), lambda b, h: (b, h, 0, 0))
    kfull = lambda n: pl.BlockSpec((None, None, n, hs), lambda b, h: (b, h, 0, 0))
    vfull = lambda n: pl.BlockSpec((None, None, V_ROWS, n), lambda b, h: (b, h, 0, 0))
    in_specs = [qspec, kfull(lc), vfull(lc)]
    args = [q, kc, vct]
    n_spans = 0
    grp = 0
    tk = ATT_TK
    span_blocks = 1
    if k is not None:
        lk = k.shape[2]
        tk = min(ATT_TK, lk)
        span_blocks = min(ATT_SPAN, lk // tk)
        n_spans = lk // (tk * span_blocks)
        grp = min(ATT_GROUP, n_spans)
        assert lk % (tk * span_blocks) == 0 and n_spans % grp == 0, "latent keys are consumed in whole groups"
        assert grp % 2 == 0 or n_spans == grp, "score slots alternate within a group"
        assert lc <= tk, "the context block reuses a latent score slot"
        in_specs += [kfull(lk), vfull(lk)]
        args += [k, vt]
    scratch = [pltpu.VMEM((1, tq), F32), pltpu.VMEM((V_ROWS, tq), F32)]
    if n_spans:
        scratch += [pltpu.VMEM((2, span_blocks * tk, tq), F32), pltpu.VMEM((2, span_blocks, 1, tq), F32)]
    return pl.pallas_call(
        functools.partial(_attn_kernel, n_spans=n_spans, span_blocks=span_blocks, tk=tk,
                          group_spans=grp, tq=tq, n_q=lq // tq),
        grid=(bsz, nh),
        in_specs=in_specs,
        out_specs=pl.BlockSpec((None, lq, hs), lambda b, h: (b, 0, h)),
        out_shape=jax.ShapeDtypeStruct((bsz, lq, nh * hs), BF16),
        scratch_shapes=scratch,
        compiler_params=_cparams(("parallel", "parallel")),
        name="mla_attention",
    )(*args)


def _pair_swap_cols(w):
    w2 = w.reshape(w.shape[:-1] + (w.shape[-1] // 2, 2))
    return jnp.stack([-w2[..., 1], w2[..., 0]], axis=-1).reshape(w.shape)


def _rope_angles(seq, rot_dim):
    pos = jnp.arange(seq)
    row = (pos // GRID_W).astype(F32)
    col = (pos % GRID_W).astype(F32)
    n_freq = rot_dim // 4
    inv = ROPE_BASE ** (-jnp.arange(n_freq, dtype=F32) / n_freq)
    ang = jnp.concatenate([row[:, None] * inv, col[:, None] * inv], axis=-1)
    return jnp.repeat(jnp.cos(ang), 2, axis=-1), jnp.repeat(jnp.sin(ang), 2, axis=-1)


def _ret_tables(seq, ctx_len):
    cos, sin = _rope_angles(seq, RET_DK)
    cos2, sin2 = jnp.tile(cos, (1, 2)), jnp.tile(sin, (1, 2))
    pair_sign = jnp.tile(jnp.asarray([-1.0, 1.0], F32), LANES // 2)
    return cos2, sin2 * pair_sign, jnp.ones((ctx_len, LANES), F32), jnp.zeros((ctx_len, LANES), F32)


def _mla_tables(seq, ctx_len):
    cos, sin = _rope_angles(seq, MLA_ROPE)
    one = jnp.ones((seq, MLA_NOPE), F32)
    zero_tail = jnp.zeros((seq, HEAD_SLAB - MLA_QK), F32)
    zero_head = jnp.zeros((seq, MLA_NOPE), F32)
    ta = jnp.concatenate([one, cos, zero_tail], axis=-1)
    tb = jnp.concatenate([zero_head, sin, zero_tail], axis=-1)
    ident = jnp.concatenate([jnp.ones((ctx_len, MLA_QK), F32),
                             jnp.zeros((ctx_len, HEAD_SLAB - MLA_QK), F32)], axis=-1)
    return ta, tb, ident, jnp.zeros((ctx_len, HEAD_SLAB), F32)


def _rope_slab(rope_cols):
    lead = rope_cols.shape[:-1]
    return jnp.concatenate([jnp.zeros(lead + (MLA_NOPE,), F32), rope_cols,
                            jnp.zeros(lead + (HEAD_SLAB - MLA_QK,), F32)], axis=-1)


def _slab_gains(gvec):
    rope = gvec[MLA_NOPE:].reshape(MLA_ROPE // 2, 2)
    swapped = jnp.stack([rope[:, 1], rope[:, 0]], axis=-1).reshape(MLA_ROPE)
    main = jnp.concatenate([gvec, jnp.zeros((HEAD_SLAB - MLA_QK,), F32)])
    return main.reshape(1, HEAD_SLAB), _rope_slab(swapped).reshape(1, HEAD_SLAB)


def _prep_even(ab_w_in, pool_w, ab_w_out):
    w_ext = ab_w_in.astype(BF16)
    wpool_bd = jnp.zeros((POOL_WIDTH, POOL_WIDTH), F32)
    for gi in range(POOL_GROUPS):
        s = slice(gi * POOL_CG, (gi + 1) * POOL_CG)
        wpool_bd = wpool_bd.at[s, s].set(pool_w[gi])
    return w_ext, wpool_bd.astype(BF16), ab_w_out.astype(BF16)


def _prep_odd(w_in, w_qb, w_kvb, w_out):
    wkr = w_in[:, MLA_Q_RANK + MLA_KV_RANK:]
    w_in_ext = jnp.concatenate([w_in[:, :MLA_Q_RANK + MLA_KV_RANK], _rope_slab(wkr),
                                _rope_slab(_pair_swap_cols(wkr))], axis=-1).astype(BF16)
    wq3 = w_qb.reshape(MLA_Q_RANK, MLA_HEADS, MLA_QK)
    wq_main = jnp.concatenate([wq3, jnp.zeros((MLA_Q_RANK, MLA_HEADS, HEAD_SLAB - MLA_QK), F32)], axis=-1)
    wq_aux = _rope_slab(_pair_swap_cols(wq3[..., MLA_NOPE:]))
    w_qb_ext = jnp.concatenate([wq_main, wq_aux], axis=-1).reshape(
        MLA_Q_RANK, MLA_HEADS * 2 * HEAD_SLAB).astype(BF16)
    wkv3 = w_kvb.reshape(MLA_KV_RANK, MLA_HEADS, MLA_NOPE + MLA_V)
    wk_slab = jnp.concatenate([wkv3[..., :MLA_NOPE],
                               jnp.zeros((MLA_KV_RANK, MLA_HEADS, HEAD_SLAB - MLA_NOPE), F32)], axis=-1)
    w_kv_ext = jnp.concatenate([wk_slab, wkv3[..., MLA_NOPE:]], axis=-1).reshape(
        MLA_KV_RANK, MLA_HEADS * 2 * HEAD_SLAB).astype(BF16)
    return w_in_ext, w_qb_ext, w_kv_ext, w_out.astype(BF16)


def kernel(x, c, ctx, c_ctx, ada_w, ada_b, norm_g, ffn_w_in, ffn_w_out, ab_w_in, pool_w, pool_scale,
           ret_log_decay, ret_norm_g, ab_w_out, mla_w_in, mla_qa_g, mla_kva_g, mla_w_qb, mla_w_kvb,
           mla_qn_g, mla_kn_g, mla_w_out):
    bsz, seq, d = x.shape
    ctx_len = ctx.shape[1]
    depth = ada_w.shape[0]
    ctx_row = bsz
    cond8 = jnp.zeros((8, d), F32).at[:bsz].set(c).at[ctx_row].set(c_ctx)
    mod_all = _ada_modulation(cond8, ada_w, ada_b).reshape(depth, 8, N_MOD, d)
    lat_row = lambda b: b
    ctx_mod_row = lambda b: ctx_row

    ret_cos, ret_sin, ret_cos_c, ret_sin_c = _ret_tables(seq, ctx_len)
    m_ta, m_tb, m_ta_c, m_tb_c = _mla_tables(seq, ctx_len)
    q_scale = MLA_QK ** -0.5 * LOG2_E
    zero_state = jnp.zeros((bsz, RET_PAIRS, 2 * RET_DK, 2 * RET_DV), F32)
    ffn_in_b, ffn_out_b = ffn_w_in.astype(BF16), ffn_w_out.astype(BF16)

    xc = ctx
    for i in range(depth):
        last = i == depth - 1
        j = i // 2
        mod8 = mod_all[i]
        x = _ffn_half(x, mod8, lat_row, 0, norm_g[i, 0], ffn_in_b, ffn_out_b, (i, 0))
        xc = _ffn_half(xc, mod8, ctx_mod_row, 0, norm_g[i, 0], ffn_in_b, ffn_out_b, (i, 0))
        mix = mix_c = w_mix = None
        if i % 2 == 0:
            w_ext, wpool_bd, w_out = _prep_even(ab_w_in[j], pool_w[j], ab_w_out[j])
            lg = ret_log_decay[j]
            lgf_q = jnp.repeat(lg[0], RET_DK).reshape(1, RET_QK_W)
            lgb_q = jnp.repeat(lg[1], RET_DK).reshape(1, RET_QK_W)
            lgf_v = jnp.repeat(lg[0], RET_DV).reshape(1, RET_V_W)
            lgb_v = jnp.repeat(lg[1], RET_DV).reshape(1, RET_V_W)
            pscale = pool_scale[j].reshape(1, POOL_WIDTH)
            rng = ret_norm_g[j].reshape(1, RET_V_W)
            pc, qc, kc, vc, sgc, cbc, state_b = _ab_project(
                xc, mod8, ctx_mod_row, norm_g[i, 1], w_ext, ret_cos_c, ret_sin_c, lgb_q, lgb_v, zero_state)
            xc_mixed, state_f = _ret_out(xc, mod8, ctx_mod_row, pc, qc, kc, vc, sgc, cbc, wpool_bd, pscale,
                                         rng, w_out, lgf_q, lgf_v, lgb_v, zero_state)
            p, q, k, v, sg, cb, _ = _ab_project(
                x, mod8, lat_row, norm_g[i, 1], w_ext, ret_cos, ret_sin, lgb_q, lgb_v, state_b)
            x, _ = _ret_out(x, mod8, lat_row, p, q, k, v, sg, cb, wpool_bd, pscale, rng, w_out,
                            lgf_q, lgf_v, lgb_v, state_f)
        else:
            w_in_ext, w_qb_ext, w_kv_ext, w_out = _prep_odd(mla_w_in[j], mla_w_qb[j], mla_w_kvb[j],
                                                            mla_w_out[j])
            qa_g = mla_qa_g[j].reshape(1, MLA_Q_RANK)
            kva_g = mla_kva_g[j].reshape(1, MLA_KV_RANK)
            qn_g, qn_gs = _slab_gains(mla_qn_g[j])
            kn_g, kn_gs = _slab_gains(mla_kn_g[j])
            qc, kc, vc = _mla_project(xc, mod8, ctx_mod_row, norm_g[i, 1], w_in_ext, qa_g, kva_g, w_qb_ext,
                                      w_kv_ext, qn_g, qn_gs, kn_g, kn_gs,
                                      m_ta_c * q_scale, m_tb_c, m_ta_c, m_tb_c)
            q, k, v = _mla_project(x, mod8, lat_row, norm_g[i, 1], w_in_ext, qa_g, kva_g, w_qb_ext,
                                   w_kv_ext, qn_g, qn_gs, kn_g, kn_gs,
                                   m_ta * q_scale, m_tb * q_scale, m_ta, m_tb)
            mix, w_mix = _attention(q, kc, vc, k, v), w_out
            if not last:
                mix_c = _attention(qc, kc, vc)
                xc_mixed = xc
        x = _ffn_half(x, mod8, lat_row, 2, norm_g[i, 2], ffn_in_b, ffn_out_b, (i, 1), mix, w_mix)
        if not last:
            xc = _ffn_half(xc_mixed, mod8, ctx_mod_row, 2, norm_g[i, 2], ffn_in_b, ffn_out_b, (i, 1),
                           mix_c, w_mix)
    return x
```

```python
import functools

import numpy as np
import jax
import jax.numpy as jnp
from jax import lax
from jax.experimental import pallas as pl
from jax.experimental.pallas import tpu as pltpu

F32 = jnp.float32
BF16 = jnp.bfloat16

D_MODEL = 1024
GRID_W = 64
D_FF = 2816
FFN_RESIDUAL = 0.5
N_MOD = 9
ROPE_BASE = 10000.0
EPS = 1e-6
POOL_GROUPS = 4
POOL_CG = 64
POOL_WIDTH = POOL_GROUPS * POOL_CG
POOL_WINDOWS = (2, 4, 8, 16)
POOL_HALO = 8
POOL_SUB = 128
RET_HEADS = 6
RET_PAIRS = RET_HEADS // 2
RET_DK = 64
RET_DV = 128
RET_CHUNK = 128
RET_QK_W = RET_HEADS * RET_DK
RET_V_W = RET_HEADS * RET_DV
AB_IN = POOL_WIDTH + 2 * RET_QK_W + 2 * RET_V_W
MLA_HEADS = 8
MLA_Q_RANK = 384
MLA_KV_RANK = 256
MLA_NOPE = 64
MLA_ROPE = 32
MLA_V = 128
MLA_QK = MLA_NOPE + MLA_ROPE
HEAD_SLAB = 128
V_ROWS = MLA_V + 16
LOG2_E = 1.4426950408889634

LANES = 128
V7X_VMEM_BYTES = 64 * 1024 * 1024
VMEM_LIMIT_BYTES = 56 * 1024 * 1024

ADA_TN = 1536
FFN_TM = 1024
FFN_SUB = 256
MIX_TM = 1024
AB_TM = 1024
AB_SUB = 256
MLA_TM = 1024
MLA_HEAD_UNROLL = 8
ATT_TQ = 1024
ATT_TK = 512
ATT_SPAN = 1
ATT_GROUP = 4


def _cparams(sem):
    return pltpu.CompilerParams(dimension_semantics=sem, vmem_limit_bytes=VMEM_LIMIT_BYTES)


def _resident(shape):
    nd = len(shape)
    return pl.BlockSpec(shape, lambda *_: (0,) * nd, pipeline_mode=pl.Buffered(1))


def _silu(x):
    return x * (1.0 / (1.0 + jnp.exp(-x)))


def _rms_mod(x, g, shift, scale):
    ms = jnp.mean(x * x, axis=-1, keepdims=True)
    return (x * lax.rsqrt(ms + EPS) * g) * (1.0 + scale) + shift


def _ada_kernel(c_ref, w_ref, b_ref, o_ref):
    s = _silu(c_ref[...])
    o_ref[...] = jnp.dot(s, w_ref[...], precision=lax.Precision.HIGHEST,
                         preferred_element_type=F32) + b_ref[...]


def _ada_modulation(cond8, ada_w, ada_b):
    depth, d, n = ada_w.shape
    return pl.pallas_call(
        _ada_kernel,
        grid=(depth, n // ADA_TN),
        in_specs=[
            pl.BlockSpec((8, d), lambda i, j: (0, 0)),
            pl.BlockSpec((None, d, ADA_TN), lambda i, j: (i, 0, j)),
            pl.BlockSpec((None, 1, ADA_TN), lambda i, j: (i, 0, j)),
        ],
        out_specs=pl.BlockSpec((None, 8, ADA_TN), lambda i, j: (i, 0, j)),
        out_shape=jax.ShapeDtypeStruct((depth, 8, n), F32),
        compiler_params=_cparams(("parallel", "parallel")),
        name="ada_mod",
    )(cond8, ada_w, ada_b.reshape(depth, 1, n))


def _ffn_kernel(*refs, k, mixed):
    if mixed:
        x_ref, mod_ref, g_ref, win_ref, wout_ref, mix_ref, wmix_ref, o_ref = refs
    else:
        x_ref, mod_ref, g_ref, win_ref, wout_ref, o_ref = refs
    tm = x_ref.shape[0]
    sub = min(FFN_SUB, tm)
    for s in range(tm // sub):
        rs = slice(s * sub, (s + 1) * sub)
        x = x_ref[rs, :]
        if mixed:
            x = x + mod_ref[5:6, :] * jnp.dot(mix_ref[rs, :], wmix_ref[...], preferred_element_type=F32)
        h = _rms_mod(x, g_ref[...], mod_ref[3 * k:3 * k + 1, :], mod_ref[3 * k + 1:3 * k + 2, :])
        ab = jnp.dot(h.astype(BF16), win_ref[...], preferred_element_type=F32)
        act = (_silu(ab[:, :D_FF]) * ab[:, D_FF:]).astype(BF16)
        y = jnp.dot(act, wout_ref[...], preferred_element_type=F32)
        o_ref[rs, :] = x + (FFN_RESIDUAL * mod_ref[3 * k + 2:3 * k + 3, :]) * y


def _ffn_half(x, mod8, mod_row, k, g, w_in, w_out, which, mix=None, w_mix=None):
    bsz, seq, d = x.shape
    tm = min(FFN_TM, seq)
    layer, half = which
    pick = lambda shape: pl.BlockSpec((None, None) + tuple(shape[2:]), lambda b, t: (layer, half, 0, 0),
                                      pipeline_mode=pl.Buffered(1))
    in_specs = [
        pl.BlockSpec((None, tm, d), lambda b, t: (b, t, 0)),
        pl.BlockSpec((None, N_MOD, d), lambda b, t: (mod_row(b), 0, 0)),
        pl.BlockSpec((1, d), lambda b, t: (0, 0)),
        pick(w_in.shape),
        pick(w_out.shape),
    ]
    args = [x, mod8, g.reshape(1, d), w_in, w_out]
    if mix is not None:
        in_specs += [pl.BlockSpec((None, tm, mix.shape[-1]), lambda b, t: (b, t, 0)),
                     _resident(w_mix.shape)]
        args += [mix, w_mix]
    return pl.pallas_call(
        functools.partial(_ffn_kernel, k=k, mixed=mix is not None),
        grid=(bsz, seq // tm),
        in_specs=in_specs,
        out_specs=pl.BlockSpec((None, tm, d), lambda b, t: (b, t, 0)),
        out_shape=jax.ShapeDtypeStruct(x.shape, F32),
        compiler_params=_cparams(("parallel", "parallel")),
        name="ffn_half",
    )(*args)


def _pair_tables(lg_row, expo):
    return jnp.exp(lg_row * expo)


def _bd_mask():
    r = lax.broadcasted_iota(jnp.int32, (2 * RET_DK, 2 * RET_DV), 0)
    c = lax.broadcasted_iota(jnp.int32, (2 * RET_DK, 2 * RET_DV), 1)
    same_head = jnp.where(r < RET_DK, 0, 1) == jnp.where(c < RET_DV, 0, 1)
    return jnp.where(same_head, 1.0, 0.0)


def _abproj_kernel(x_ref, mod_ref, g_ref, w_ref, cos_ref, sin_ref, lgq_ref, lgv_ref, s0_ref,
                   p_ref, q_ref, k_ref, v_ref, sg_ref, cb_ref, sfin_ref, st_scr, *, tm, nt):
    t = pl.program_id(1)

    @pl.when(t == 0)
    def _():
        st_scr[...] = s0_ref[...]

    q0, k0 = POOL_WIDTH, POOL_WIDTH + RET_QK_W
    v0, g0 = k0 + RET_QK_W, k0 + RET_QK_W + RET_V_W
    row = lax.broadcasted_iota(jnp.int32, (RET_CHUNK, LANES), 0).astype(F32)
    bdm = _bd_mask()
    k_scale = RET_DK ** -0.5
    sub = min(AB_SUB, tm)
    even_lane = (lax.broadcasted_iota(jnp.int32, (sub, LANES), 1) & 1) == 0

    def pair_partner(v):
        return jnp.where(even_lane, pltpu.roll(v, LANES - 1, axis=1), pltpu.roll(v, 1, axis=1))
    for s in reversed(range(tm // sub)):
        r0 = s * sub
        ts = slice(r0, r0 + sub)
        h = _rms_mod(x_ref[ts, :], g_ref[...], mod_ref[3:4, :], mod_ref[4:5, :])
        proj = jnp.dot(h.astype(BF16), w_ref[...], preferred_element_type=F32)
        p_ref[ts, :] = proj[:, :POOL_WIDTH]
        v_ref[ts, :] = proj[:, v0:v0 + RET_V_W].astype(BF16)
        sg_ref[ts, :] = _silu(proj[:, g0:g0 + RET_V_W]).astype(BF16)
        cos = cos_ref[ts, :]
        sin = sin_ref[ts, :]
        for j in range(RET_PAIRS):
            sl = slice(LANES * j, LANES * (j + 1))
            qp = proj[:, q0 + LANES * j:q0 + LANES * (j + 1)]
            kp = proj[:, k0 + LANES * j:k0 + LANES * (j + 1)]
            qr = qp * cos + pair_partner(qp) * sin
            kr = (kp * cos + pair_partner(kp) * sin) * k_scale
            q_ref[ts, sl] = qr.astype(BF16)
            k_ref[ts, sl] = kr.astype(BF16)
            lgq = lgq_ref[:, sl]
            lgv = lgv_ref[:, 2 * LANES * j:2 * LANES * (j + 1)]
            xi = _pair_tables(lgq, float(RET_CHUNK) - row)
            zeta = _pair_tables(lgq, row)
            dec = jnp.exp(lgv * float(RET_CHUNK))
            vsl = slice(2 * LANES * j, 2 * LANES * (j + 1))
            for c in reversed(range(sub // RET_CHUNK)):
                ls = slice(c * RET_CHUNK, (c + 1) * RET_CHUNK)
                rs = slice(r0 + c * RET_CHUNK, r0 + (c + 1) * RET_CHUNK)
                bd = st_scr[j]
                qx = (qr[ls] * xi).astype(BF16)
                cb_ref[rs, vsl] = jnp.dot(qx, bd.astype(BF16),
                                          preferred_element_type=F32).astype(cb_ref.dtype)
                kz = (kr[ls] * zeta).astype(BF16)
                upd = lax.dot_general(kz, v_ref[rs, vsl], (((0,), (0,)), ((), ())),
                                      preferred_element_type=F32)
                st_scr[j] = dec * bd + bdm * upd

    @pl.when(t == nt - 1)
    def _():
        sfin_ref[...] = st_scr[...]


def _ab_project(x, mod8, mod_row, g, w_ext, cos, sin, lg_q, lg_v, s0):
    bsz, seq, d = x.shape
    tm = min(AB_TM, seq)
    nt = seq // tm
    rev = lambda b, t: (b, nt - 1 - t, 0)
    tok = lambda w, dt: jax.ShapeDtypeStruct((bsz, seq, w), dt)
    st_shape = (RET_PAIRS, 2 * RET_DK, 2 * RET_DV)
    return pl.pallas_call(
        functools.partial(_abproj_kernel, tm=tm, nt=nt),
        grid=(bsz, nt),
        in_specs=[
            pl.BlockSpec((None, tm, d), rev),
            pl.BlockSpec((None, N_MOD, d), lambda b, t: (mod_row(b), 0, 0)),
            pl.BlockSpec((1, d), lambda b, t: (0, 0)),
            _resident(w_ext.shape),
            pl.BlockSpec((tm, LANES), lambda b, t: (nt - 1 - t, 0)),
            pl.BlockSpec((tm, LANES), lambda b, t: (nt - 1 - t, 0)),
            pl.BlockSpec((1, RET_QK_W), lambda b, t: (0, 0)),
            pl.BlockSpec((1, RET_V_W), lambda b, t: (0, 0)),
            pl.BlockSpec((None,) + st_shape, lambda b, t: (b, 0, 0, 0)),
        ],
        out_specs=[
            pl.BlockSpec((None, tm, POOL_WIDTH), rev),
            pl.BlockSpec((None, tm, RET_QK_W), rev),
            pl.BlockSpec((None, tm, RET_QK_W), rev),
            pl.BlockSpec((None, tm, RET_V_W), rev),
            pl.BlockSpec((None, tm, RET_V_W), rev),
            pl.BlockSpec((None, tm, RET_V_W), rev),
            pl.BlockSpec((None,) + st_shape, lambda b, t: (b, 0, 0, 0)),
        ],
        out_shape=[tok(POOL_WIDTH, F32), tok(RET_QK_W, BF16), tok(RET_QK_W, BF16),
                   tok(RET_V_W, BF16), tok(RET_V_W, BF16), tok(RET_V_W, BF16),
                   jax.ShapeDtypeStruct((bsz,) + st_shape, F32)],
        scratch_shapes=[pltpu.VMEM(st_shape, F32)],
        compiler_params=_cparams(("arbitrary", "arbitrary")),
        name="ab_project",
    )(x, mod8, g.reshape(1, d), w_ext, cos, sin, lg_q, lg_v, s0)


def _retout_kernel(x_ref, mod_ref, pprev_ref, pcur_ref, pnext_ref, pm_ref, invc_ref,
                   q_ref, k_ref, v_ref, sg_ref, cb_ref, wpool_ref, pscale_ref, rng_ref, wout_ref,
                   lgfq_ref, lgfv_ref, lgbv_ref, s0_ref,
                   o_ref, sfin_ref, st_scr, ycat_scr, pe_scr, *, tm, nt):
    t = pl.program_id(1)

    @pl.when(t == 0)
    def _():
        st_scr[...] = s0_ref[...]

    pe_scr[0:POOL_HALO, :] = pprev_ref[...]
    pe_scr[POOL_HALO:POOL_HALO + tm, :] = pcur_ref[...]
    pe_scr[POOL_HALO + tm:2 * POOL_HALO + tm, :] = pnext_ref[...]
    pe_scr[2 * POOL_HALO + tm:, :] = jnp.zeros((POOL_SUB - 2 * POOL_HALO, POOL_WIDTH), F32)
    lane_grp = jnp.right_shift(lax.broadcasted_iota(jnp.int32, (POOL_SUB, POOL_WIDTH), 1),
                               POOL_CG.bit_length() - 1)
    for s in range(tm // POOL_SUB):
        r0 = s * POOL_SUB
        pe = pe_scr[r0:r0 + 2 * POOL_SUB, :]
        pe_hi = pe.astype(BF16)
        pe_lo = (pe - pe_hi.astype(F32)).astype(BF16)
        pooled = jnp.zeros((POOL_SUB, POOL_WIDTH), F32)
        for gi in range(POOL_GROUPS):
            win = pm_ref[s, gi]
            tot = jnp.dot(win, pe_hi, preferred_element_type=F32) \
                + jnp.dot(win, pe_lo, preferred_element_type=F32)
            pooled = jnp.where(lane_grp == gi, tot, pooled)
        rs = slice(r0, r0 + POOL_SUB)
        diffs = pooled * invc_ref[rs, :] - pcur_ref[rs, :]
        pool_y = jnp.dot(diffs.astype(BF16), wpool_ref[...], preferred_element_type=F32) * pscale_ref[...]
        ycat_scr[rs, 0:POOL_WIDTH] = pool_y.astype(BF16)

    ri = lax.broadcasted_iota(jnp.int32, (RET_CHUNK, RET_CHUNK), 0)
    ci = lax.broadcasted_iota(jnp.int32, (RET_CHUNK, RET_CHUNK), 1)
    rel = (ri - ci).astype(F32)
    row = lax.broadcasted_iota(jnp.int32, (RET_CHUNK, LANES), 0).astype(F32)
    lane = lax.broadcasted_iota(jnp.int32, (RET_CHUNK, LANES), 1)
    bdm = _bd_mask()
    zero_bf = jnp.zeros((RET_CHUNK, LANES), BF16)
    lane2 = lax.broadcasted_iota(jnp.int32, (RET_CHUNK, 2 * RET_DV), 1)
    zero_bf2 = jnp.zeros((RET_CHUNK, 2 * RET_DV), BF16)
    tables = []
    for j in range(RET_PAIRS):
        sl = slice(LANES * j, LANES * (j + 1))
        vsl = slice(2 * LANES * j, 2 * LANES * (j + 1))
        lgq = lgfq_ref[:, sl]
        xi = _pair_tables(lgq, row + 1.0)
        zeta = _pair_tables(lgq, float(RET_CHUNK - 1) - row)
        dec = jnp.exp(lgfv_ref[:, vsl] * float(RET_CHUNK))
        masks = []
        for a in range(2):
            hs = slice(2 * LANES * j + LANES * a, 2 * LANES * j + LANES * (a + 1))
            mf = jnp.exp(lgfv_ref[:, hs] * jnp.maximum(rel, 0.0))
            mb = jnp.exp(lgbv_ref[:, hs] * jnp.maximum(-rel, 0.0))
            masks.append(jnp.where(rel > 0, mf, jnp.where(rel < 0, mb, 2.0)))
        tables.append((xi, zeta, dec, jnp.concatenate(masks, axis=1)))
    for c in range(tm // RET_CHUNK):
        rs = slice(c * RET_CHUNK, (c + 1) * RET_CHUNK)
        for j in range(RET_PAIRS):
            sl = slice(LANES * j, LANES * (j + 1))
            vsl = slice(2 * LANES * j, 2 * LANES * (j + 1))
            xi, zeta, dec, mask2 = tables[j]
            q2 = q_ref[rs, sl]
            k2 = k_ref[rs, sl]
            v2 = v_ref[rs, vsl]
            bd = st_scr[j]
            qx = (q2.astype(F32) * xi).astype(BF16)
            k_bd = jnp.concatenate([jnp.where(lane < RET_DK, k2, zero_bf),
                                    jnp.where(lane >= RET_DK, k2, zero_bf)], axis=0)
            s2 = lax.dot_general(q2, k_bd, (((1,), (1,)), ((), ())), preferred_element_type=F32)
            p2 = (s2 * mask2).astype(BF16)
            v_bd = jnp.concatenate([jnp.where(lane2 < RET_DV, v2, zero_bf2),
                                    jnp.where(lane2 >= RET_DV, v2, zero_bf2)], axis=0)
            ret = jnp.dot(qx, bd.astype(BF16), preferred_element_type=F32) + cb_ref[rs, vsl].astype(F32) \
                + jnp.dot(p2, v_bd, preferred_element_type=F32)
            for a in range(2):
                hs_v = slice(LANES * a, LANES * (a + 1))
                r = ret[:, hs_v]
                ms = jnp.mean(r * r, axis=-1, keepdims=True)
                hcol = 2 * LANES * j + LANES * a
                y = (r * lax.rsqrt(ms + EPS) * rng_ref[:, hcol:hcol + LANES]) \
                    * sg_ref[rs, hcol:hcol + LANES].astype(F32)
                ycat_scr[rs, POOL_WIDTH + hcol:POOL_WIDTH + hcol + LANES] = y.astype(BF16)
            kz = (k2.astype(F32) * zeta).astype(BF16)
            upd = lax.dot_general(kz, v2, (((0,), (0,)), ((), ())), preferred_element_type=F32)
            st_scr[j] = dec * bd + bdm * upd
        yy = jnp.dot(ycat_scr[rs, :], wout_ref[...], preferred_element_type=F32)
        o_ref[rs, :] = x_ref[rs, :] + mod_ref[5:6, :] * yy

    @pl.when(t == nt - 1)
    def _():
        sfin_ref[...] = st_scr[...]


def _pool_constants(tm, seq):
    nt = seq // tm
    nsub = tm // POOL_SUB
    kinds = [0] if nt == 1 else [0, 1, nt - 1]
    mats = np.zeros((len(kinds), nsub, POOL_GROUPS, POOL_SUB, 2 * POOL_SUB), np.float32)
    invc = np.zeros((len(kinds), tm, POOL_WIDTH), np.float32)
    for vi, tile in enumerate(kinds):
        for s in range(nsub):
            pos = tile * tm + s * POOL_SUB + np.arange(POOL_SUB)
            colpos = tile * tm - POOL_HALO + s * POOL_SUB + np.arange(2 * POOL_SUB)
            for gi, w in enumerate(POOL_WINDOWS):
                lo = w // 2
                hi = w - 1 - lo
                start = np.maximum(pos - lo, 0)
                end = np.minimum(pos + hi + 1, seq)
                mats[vi, s, gi] = (colpos[None, :] >= start[:, None]) & (colpos[None, :] < end[:, None])
                invc[vi, s * POOL_SUB:(s + 1) * POOL_SUB, gi * POOL_CG:(gi + 1) * POOL_CG] = \
                    (1.0 / (end - start))[:, None]
    return jnp.asarray(mats, BF16), jnp.asarray(invc, F32), nt


def _ret_out(x, mod8, mod_row, p, q, k, v, sg, cb, wpool_bd, pscale, rng, w_out, lgf_q, lgf_v, lgb_v, s0):
    bsz, seq, d = x.shape
    tm = min(MIX_TM, seq)
    pm, invc, nt = _pool_constants(tm, seq)
    hb = tm // POOL_HALO
    nhb = seq // POOL_HALO

    def variant(t):
        if nt == 1:
            return 0
        return jnp.where(t == 0, 0, jnp.where(t == nt - 1, 2, 1))

    cur = lambda b, t: (b, t, 0)
    st_shape = (RET_PAIRS, 2 * RET_DK, 2 * RET_DV)
    tokspec = lambda w: pl.BlockSpec((None, tm, w), cur)
    return pl.pallas_call(
        functools.partial(_retout_kernel, tm=tm, nt=nt),
        grid=(bsz, nt),
        in_specs=[
            tokspec(d),
            pl.BlockSpec((None, N_MOD, d), lambda b, t: (mod_row(b), 0, 0)),
            pl.BlockSpec((None, POOL_HALO, POOL_WIDTH), lambda b, t: (b, jnp.maximum(t * hb - 1, 0), 0)),
            tokspec(POOL_WIDTH),
            pl.BlockSpec((None, POOL_HALO, POOL_WIDTH),
                         lambda b, t: (b, jnp.minimum((t + 1) * hb, nhb - 1), 0)),
            pl.BlockSpec((None, tm // POOL_SUB, POOL_GROUPS, POOL_SUB, 2 * POOL_SUB),
                         lambda b, t: (variant(t), 0, 0, 0, 0)),
            pl.BlockSpec((None, tm, POOL_WIDTH), lambda b, t: (variant(t), 0, 0)),
            tokspec(RET_QK_W), tokspec(RET_QK_W), tokspec(RET_V_W), tokspec(RET_V_W), tokspec(RET_V_W),
            _resident(wpool_bd.shape),
            pl.BlockSpec((1, POOL_WIDTH), lambda b, t: (0, 0)),
            pl.BlockSpec((1, RET_V_W), lambda b, t: (0, 0)),
            _resident(w_out.shape),
            pl.BlockSpec((1, RET_QK_W), lambda b, t: (0, 0)),
            pl.BlockSpec((1, RET_V_W), lambda b, t: (0, 0)),
            pl.BlockSpec((1, RET_V_W), lambda b, t: (0, 0)),
            pl.BlockSpec((None,) + st_shape, lambda b, t: (b, 0, 0, 0)),
        ],
        out_specs=[
            tokspec(d),
            pl.BlockSpec((None,) + st_shape, lambda b, t: (b, 0, 0, 0)),
        ],
        out_shape=[jax.ShapeDtypeStruct(x.shape, F32),
                   jax.ShapeDtypeStruct((bsz,) + st_shape, F32)],
        scratch_shapes=[pltpu.VMEM(st_shape, F32),
                        pltpu.VMEM((tm, d), BF16),
                        pltpu.VMEM((tm + POOL_SUB, POOL_WIDTH), F32)],
        compiler_params=_cparams(("arbitrary", "arbitrary")),
        name="ret_out",
    )(x, mod8, p, p, p, pm, invc, q, k, v, sg, cb, wpool_bd, pscale, rng, w_out,
      lgf_q, lgf_v, lgb_v, s0)


def _head_norm_rope(main, aux, g_main, g_aux):
    ms = jnp.sum(main * main, axis=-1, keepdims=True) * (1.0 / MLA_QK)
    r = lax.rsqrt(ms + EPS)
    return (main * r) * g_main + (aux * r) * g_aux


def _mlaproj_kernel(x_ref, mod_ref, g_ref, win_ref, qag_ref, kvag_ref, wqb_ref, wkv_ref,
                    qg_ref, qgs_ref, kg_ref, kgs_ref, qa_t_ref, qb_t_ref, ka_t_ref, kb_t_ref,
                    q_ref, k_ref, v_ref, qan_scr, kvn_scr, kr_scr, gt_scr):
    h = _rms_mod(x_ref[...], g_ref[...], mod_ref[3:4, :], mod_ref[4:5, :])
    proj = jnp.dot(h.astype(BF16), win_ref[...], preferred_element_type=F32)
    qa = proj[:, :MLA_Q_RANK]
    kva = proj[:, MLA_Q_RANK:MLA_Q_RANK + MLA_KV_RANK]
    r0 = MLA_Q_RANK + MLA_KV_RANK
    kr_scr[0] = proj[:, r0:r0 + HEAD_SLAB]
    kr_scr[1] = proj[:, r0 + HEAD_SLAB:r0 + 2 * HEAD_SLAB]
    qan = qa * lax.rsqrt(jnp.mean(qa * qa, axis=-1, keepdims=True) + EPS) * qag_ref[...]
    kvn = kva * lax.rsqrt(jnp.mean(kva * kva, axis=-1, keepdims=True) + EPS) * kvag_ref[...]
    qan_scr[...] = qan.astype(BF16)
    kvn_scr[...] = kvn.astype(BF16)
    gt_scr[0] = qg_ref[...] * qa_t_ref[...]
    gt_scr[1] = qgs_ref[...] * qb_t_ref[...]
    gt_scr[2] = kg_ref[...] * ka_t_ref[...]
    gt_scr[3] = kgs_ref[...] * kb_t_ref[...]
    tm = proj.shape[0]
    ones_rows = jnp.where(lax.broadcasted_iota(jnp.int32, (V_ROWS - MLA_V, tm), 0) == 0,
                          1.0, 0.0).astype(BF16)

    def head(hd, carry):
        c0 = pl.multiple_of(hd * (2 * HEAD_SLAB), 2 * HEAD_SLAB)
        qh = jnp.dot(qan_scr[...], wqb_ref[:, pl.ds(c0, 2 * HEAD_SLAB)], preferred_element_type=F32)
        kvh = jnp.dot(kvn_scr[...], wkv_ref[:, pl.ds(c0, 2 * HEAD_SLAB)], preferred_element_type=F32)
        q_ref[hd] = _head_norm_rope(qh[:, :HEAD_SLAB], qh[:, HEAD_SLAB:], gt_scr[0], gt_scr[1]).T.astype(BF16)
        k_ref[hd] = _head_norm_rope(kvh[:, :HEAD_SLAB] + kr_scr[0], kr_scr[1],
                                    gt_scr[2], gt_scr[3]).astype(BF16)
        v_ref[hd, 0:MLA_V, :] = kvh[:, HEAD_SLAB:].T.astype(BF16)
        v_ref[hd, MLA_V:V_ROWS, :] = ones_rows
        return carry
    lax.fori_loop(0, MLA_HEADS, head, 0, unroll=MLA_HEAD_UNROLL)


def _mla_project(x, mod8, mod_row, g, w_in, qa_g, kva_g, w_qb, w_kv, qn_g, qn_gs, kn_g, kn_gs,
                 q_ta, q_tb, k_ta, k_tb):
    bsz, seq, d = x.shape
    tm = min(MLA_TM, seq)
    head_out = jax.ShapeDtypeStruct((bsz, MLA_HEADS, seq, HEAD_SLAB), BF16)
    hspec = pl.BlockSpec((None, MLA_HEADS, tm, HEAD_SLAB), lambda b, t: (b, 0, t, 0))
    tspec = pl.BlockSpec((tm, HEAD_SLAB), lambda b, t: (t, 0))
    row = lambda n: pl.BlockSpec((1, n), lambda b, t: (0, 0))
    return pl.pallas_call(
        _mlaproj_kernel,
        grid=(bsz, seq // tm),
        in_specs=[
            pl.BlockSpec((None, tm, d), lambda b, t: (b, t, 0)),
            pl.BlockSpec((None, N_MOD, d), lambda b, t: (mod_row(b), 0, 0)),
            row(d),
            _resident(w_in.shape), row(MLA_Q_RANK), row(MLA_KV_RANK),
            _resident(w_qb.shape), _resident(w_kv.shape),
            row(HEAD_SLAB), row(HEAD_SLAB), row(HEAD_SLAB), row(HEAD_SLAB),
            tspec, tspec, tspec, tspec,
        ],
        out_specs=[pl.BlockSpec((None, MLA_HEADS, HEAD_SLAB, tm), lambda b, t: (b, 0, 0, t)),
                   hspec,
                   pl.BlockSpec((None, MLA_HEADS, V_ROWS, tm), lambda b, t: (b, 0, 0, t))],
        out_shape=[jax.ShapeDtypeStruct((bsz, MLA_HEADS, HEAD_SLAB, seq), BF16),
                   head_out,
                   jax.ShapeDtypeStruct((bsz, MLA_HEADS, V_ROWS, seq), BF16)],
        scratch_shapes=[pltpu.VMEM((tm, MLA_Q_RANK), BF16), pltpu.VMEM((tm, MLA_KV_RANK), BF16),
                        pltpu.VMEM((2, tm, HEAD_SLAB), F32), pltpu.VMEM((4, tm, HEAD_SLAB), F32)],
        compiler_params=_cparams(("parallel", "parallel")),
        name="mla_project",
    )(x, mod8, g.reshape(1, d), w_in, qa_g, kva_g, w_qb, w_kv, qn_g, qn_gs, kn_g, kn_gs,
      q_ta, q_tb, k_ta, k_tb)


def _attn_kernel(*refs, n_spans, span_blocks, tk, group_spans, tq, n_q):
    if n_spans:
        q_ref, kc_ref, vct_ref, k_ref, vt_ref, o_ref, m_scr, acc_scr, s_scr, mb_scr = refs
    else:
        q_ref, kc_ref, vct_ref, o_ref, m_scr, acc_scr = refs

    def colmax(st):
        return jnp.max(st, axis=0, keepdims=True)

    def finalize(row0):
        acc = acc_scr[...]
        out_t = acc[0:MLA_V, :] * (1.0 / acc[MLA_V:MLA_V + 1, :])
        o_ref[pl.ds(row0, tq), :] = out_t.T.astype(o_ref.dtype)

    def softmax_pv(st, m_blk, vtb, first):
        if first:
            m_new = m_blk
        else:
            m_prev = m_scr[...]
            m_new = jnp.maximum(m_prev, m_blk)
        p = jnp.exp2(st - m_new).astype(BF16)
        pv = jnp.dot(vtb, p, preferred_element_type=F32)
        if first:
            acc_scr[...] = pv
        else:
            acc_scr[...] = jnp.exp2(m_prev - m_new) * acc_scr[...] + pv
        m_scr[...] = m_new

    if not n_spans:
        st_c = jnp.dot(kc_ref[...], q_ref[...], preferred_element_type=F32)
        softmax_pv(st_c, colmax(st_c), vct_ref[...], True)
        finalize(0)
        return

    span = span_blocks * tk
    lc = kc_ref.shape[0]

    def query_tile(q, prev_row0):
        def produce(slot, kb):
            rows = kb.shape[0]
            st = jnp.dot(kb, q, preferred_element_type=F32)
            s_scr[slot, 0:rows, :] = st
            blk = min(tk, rows)
            for sb in range(rows // blk):
                mb_scr[slot, sb] = colmax(st[sb * blk:(sb + 1) * blk, :])

        def consume(slot, v_off):
            for sb in range(span_blocks):
                softmax_pv(s_scr[slot, sb * tk:(sb + 1) * tk, :], mb_scr[slot, sb],
                           vt_ref[:, pl.ds(v_off + sb * tk, tk)], False)

        produce(0, k_ref[0:span, :])
        if prev_row0 is not None:
            finalize(prev_row0)
        m_scr[...] = jnp.full(m_scr.shape, -jnp.inf, F32)
        acc_scr[...] = jnp.zeros(acc_scr.shape, F32)

        def group(i, carry):
            base = i * (group_spans * span)
            for u in range(group_spans):
                cur = pl.multiple_of(base + u * span, span)
                nxt = pl.multiple_of(base + (u + 1) * span, span)
                produce((u + 1) % 2, k_ref[pl.ds(nxt, span), :])
                consume(u % 2, cur)
            return carry
        lax.fori_loop(0, n_spans // group_spans - 1, group, 0)
        e0 = n_spans - group_spans
        for u in range(group_spans):
            cur = (e0 + u) * span
            if u + 1 < group_spans:
                produce((u + 1) % 2, k_ref[cur + span:cur + 2 * span, :])
            else:
                produce((u + 1) % 2, kc_ref[...])
            consume(u % 2, cur)
        c_slot = group_spans % 2
        softmax_pv(s_scr[c_slot, 0:lc, :], mb_scr[c_slot, 0], vct_ref[...], False)

    query_tile(q_ref[:, 0:tq], None)

    def later_tile(i, carry):
        off = pl.multiple_of(i * tq, tq)
        query_tile(q_ref[:, pl.ds(off, tq)], pl.multiple_of(off - tq, tq))
        return carry
    lax.fori_loop(1, n_q, later_tile, 0)
    finalize((n_q - 1) * tq)


def _attention(q, kc, vct, k=None, vt=None):
    bsz, nh, hs, lq = q.shape
    lc = kc.shape[2]
    tq = min(ATT_TQ, lq)
    qspec = pl.BlockSpec((None, None, hs, lq), lambda b, h: (b, h, 0, 0))
    kfull = lambda n: pl.BlockSpec((None, None, n, hs), lambda b, h: (b, h, 0, 0))
    vfull = lambda n: pl.BlockSpec((None, None, V_ROWS, n), lambda b, h: (b, h, 0, 0))
    in_specs = [qspec, kfull(lc), vfull(lc)]
    args = [q, kc, vct]
    n_spans = 0
    grp = 0
    tk = ATT_TK
    span_blocks = 1
    if k is not None:
        lk = k.shape[2]
        tk = min(ATT_TK, lk)
        span_blocks = min(ATT_SPAN, lk // tk)
        n_spans = lk // (tk * span_blocks)
        grp = min(ATT_GROUP, n_spans)
        assert lk % (tk * span_blocks) == 0 and n_spans % grp == 0, "latent keys are consumed in whole groups"
        assert grp % 2 == 0 or n_spans == grp, "score slots alternate within a group"
        assert lc <= tk, "the context block reuses a latent score slot"
        in_specs += [kfull(lk), vfull(lk)]
        args += [k, vt]
    scratch = [pltpu.VMEM((1, tq), F32), pltpu.VMEM((V_ROWS, tq), F32)]
    if n_spans:
        scratch += [pltpu.VMEM((2, span_blocks * tk, tq), F32), pltpu.VMEM((2, span_blocks, 1, tq), F32)]
    return pl.pallas_call(
        functools.partial(_attn_kernel, n_spans=n_spans, span_blocks=span_blocks, tk=tk,
                          group_spans=grp, tq=tq, n_q=lq // tq),
        grid=(bsz, nh),
        in_specs=in_specs,
        out_specs=pl.BlockSpec((None, lq, hs), lambda b, h: (b, 0, h)),
        out_shape=jax.ShapeDtypeStruct((bsz, lq, nh * hs), BF16),
        scratch_shapes=scratch,
        compiler_params=_cparams(("parallel", "parallel")),
        name="mla_attention",
    )(*args)


def _pair_swap_cols(w):
    w2 = w.reshape(w.shape[:-1] + (w.shape[-1] // 2, 2))
    return jnp.stack([-w2[..., 1], w2[..., 0]], axis=-1).reshape(w.shape)


def _rope_angles(seq, rot_dim):
    pos = jnp.arange(seq)
    row = (pos // GRID_W).astype(F32)
    col = (pos % GRID_W).astype(F32)
    n_freq = rot_dim // 4
    inv = ROPE_BASE ** (-jnp.arange(n_freq, dtype=F32) / n_freq)
    ang = jnp.concatenate([row[:, None] * inv, col[:, None] * inv], axis=-1)
    return jnp.repeat(jnp.cos(ang), 2, axis=-1), jnp.repeat(jnp.sin(ang), 2, axis=-1)


def _ret_tables(seq, ctx_len):
    cos, sin = _rope_angles(seq, RET_DK)
    cos2, sin2 = jnp.tile(cos, (1, 2)), jnp.tile(sin, (1, 2))
    pair_sign = jnp.tile(jnp.asarray([-1.0, 1.0], F32), LANES // 2)
    return cos2, sin2 * pair_sign, jnp.ones((ctx_len, LANES), F32), jnp.zeros((ctx_len, LANES), F32)


def _mla_tables(seq, ctx_len):
    cos, sin = _rope_angles(seq, MLA_ROPE)
    one = jnp.ones((seq, MLA_NOPE), F32)
    zero_tail = jnp.zeros((seq, HEAD_SLAB - MLA_QK), F32)
    zero_head = jnp.zeros((seq, MLA_NOPE), F32)
    ta = jnp.concatenate([one, cos, zero_tail], axis=-1)
    tb = jnp.concatenate([zero_head, sin, zero_tail], axis=-1)
    ident = jnp.concatenate([jnp.ones((ctx_len, MLA_QK), F32),
                             jnp.zeros((ctx_len, HEAD_SLAB - MLA_QK), F32)], axis=-1)
    return ta, tb, ident, jnp.zeros((ctx_len, HEAD_SLAB), F32)


def _rope_slab(rope_cols):
    lead = rope_cols.shape[:-1]
    return jnp.concatenate([jnp.zeros(lead + (MLA_NOPE,), F32), rope_cols,
                            jnp.zeros(lead + (HEAD_SLAB - MLA_QK,), F32)], axis=-1)


def _slab_gains(gvec):
    rope = gvec[MLA_NOPE:].reshape(MLA_ROPE // 2, 2)
    swapped = jnp.stack([rope[:, 1], rope[:, 0]], axis=-1).reshape(MLA_ROPE)
    main = jnp.concatenate([gvec, jnp.zeros((HEAD_SLAB - MLA_QK,), F32)])
    return main.reshape(1, HEAD_SLAB), _rope_slab(swapped).reshape(1, HEAD_SLAB)


def _prep_even(ab_w_in, pool_w, ab_w_out):
    w_ext = ab_w_in.astype(BF16)
    wpool_bd = jnp.zeros((POOL_WIDTH, POOL_WIDTH), F32)
    for gi in range(POOL_GROUPS):
        s = slice(gi * POOL_CG, (gi + 1) * POOL_CG)
        wpool_bd = wpool_bd.at[s, s].set(pool_w[gi])
    return w_ext, wpool_bd.astype(BF16), ab_w_out.astype(BF16)


def _prep_odd(w_in, w_qb, w_kvb, w_out):
    wkr = w_in[:, MLA_Q_RANK + MLA_KV_RANK:]
    w_in_ext = jnp.concatenate([w_in[:, :MLA_Q_RANK + MLA_KV_RANK], _rope_slab(wkr),
                                _rope_slab(_pair_swap_cols(wkr))], axis=-1).astype(BF16)
    wq3 = w_qb.reshape(MLA_Q_RANK, MLA_HEADS, MLA_QK)
    wq_main = jnp.concatenate([wq3, jnp.zeros((MLA_Q_RANK, MLA_HEADS, HEAD_SLAB - MLA_QK), F32)], axis=-1)
    wq_aux = _rope_slab(_pair_swap_cols(wq3[..., MLA_NOPE:]))
    w_qb_ext = jnp.concatenate([wq_main, wq_aux], axis=-1).reshape(
        MLA_Q_RANK, MLA_HEADS * 2 * HEAD_SLAB).astype(BF16)
    wkv3 = w_kvb.reshape(MLA_KV_RANK, MLA_HEADS, MLA_NOPE + MLA_V)
    wk_slab = jnp.concatenate([wkv3[..., :MLA_NOPE],
                               jnp.zeros((MLA_KV_RANK, MLA_HEADS, HEAD_SLAB - MLA_NOPE), F32)], axis=-1)
    w_kv_ext = jnp.concatenate([wk_slab, wkv3[..., MLA_NOPE:]], axis=-1).reshape(
        MLA_KV_RANK, MLA_HEADS * 2 * HEAD_SLAB).astype(BF16)
    return w_in_ext, w_qb_ext, w_kv_ext, w_out.astype(BF16)


def kernel(x, c, ctx, c_ctx, ada_w, ada_b, norm_g, ffn_w_in, ffn_w_out, ab_w_in, pool_w, pool_scale,
           ret_log_decay, ret_norm_g, ab_w_out, mla_w_in, mla_qa_g, mla_kva_g, mla_w_qb, mla_w_kvb,
           mla_qn_g, mla_kn_g, mla_w_out):
    bsz, seq, d = x.shape
    ctx_len = ctx.shape[1]
    depth = ada_w.shape[0]
    ctx_row = bsz
    cond8 = jnp.zeros((8, d), F32).at[:bsz].set(c).at[ctx_row].set(c_ctx)
    mod_all = _ada_modulation(cond8, ada_w, ada_b).reshape(depth, 8, N_MOD, d)
    lat_row = lambda b: b
    ctx_mod_row = lambda b: ctx_row

    ret_cos, ret_sin, ret_cos_c, ret_sin_c = _ret_tables(seq, ctx_len)
    m_ta, m_tb, m_ta_c, m_tb_c = _mla_tables(seq, ctx_len)
    q_scale = MLA_QK ** -0.5 * LOG2_E
    zero_state = jnp.zeros((bsz, RET_PAIRS, 2 * RET_DK, 2 * RET_DV), F32)
    ffn_in_b, ffn_out_b = ffn_w_in.astype(BF16), ffn_w_out.astype(BF16)

    xc = ctx
    for i in range(depth):
        last = i == depth - 1
        j = i // 2
        mod8 = mod_all[i]

        def ctx_ffn(xc_in, k, g, which, mix_in=None, w_mix_in=None, mod8=mod8):
            flat = lambda a: None if a is None else a.reshape(1, bsz * ctx_len, a.shape[-1])
            out = _ffn_half(flat(xc_in), mod8, ctx_mod_row, k, g, ffn_in_b, ffn_out_b, which,
                            flat(mix_in), w_mix_in)
            return out.reshape(bsz, ctx_len, d)

        x = _ffn_half(x, mod8, lat_row, 0, norm_g[i, 0], ffn_in_b, ffn_out_b, (i, 0))
        xc = ctx_ffn(xc, 0, norm_g[i, 0], (i, 0))
        mix = mix_c = w_mix = None
        if i % 2 == 0:
            w_ext, wpool_bd, w_out = _prep_even(ab_w_in[j], pool_w[j], ab_w_out[j])
            lg = ret_log_decay[j]
            lgf_q = jnp.repeat(lg[0], RET_DK).reshape(1, RET_QK_W)
            lgb_q = jnp.repeat(lg[1], RET_DK).reshape(1, RET_QK_W)
            lgf_v = jnp.repeat(lg[0], RET_DV).reshape(1, RET_V_W)
            lgb_v = jnp.repeat(lg[1], RET_DV).reshape(1, RET_V_W)
            pscale = pool_scale[j].reshape(1, POOL_WIDTH)
            rng = ret_norm_g[j].reshape(1, RET_V_W)
            pc, qc, kc, vc, sgc, cbc, state_b = _ab_project(
                xc, mod8, ctx_mod_row, norm_g[i, 1], w_ext, ret_cos_c, ret_sin_c, lgb_q, lgb_v, zero_state)
            xc_mixed, state_f = _ret_out(xc, mod8, ctx_mod_row, pc, qc, kc, vc, sgc, cbc, wpool_bd, pscale,
                                         rng, w_out, lgf_q, lgf_v, lgb_v, zero_state)
            p, q, k, v, sg, cb, _ = _ab_project(
                x, mod8, lat_row, norm_g[i, 1], w_ext, ret_cos, ret_sin, lgb_q, lgb_v, state_b)
            x, _ = _ret_out(x, mod8, lat_row, p, q, k, v, sg, cb, wpool_bd, pscale, rng, w_out,
                            lgf_q, lgf_v, lgb_v, state_f)
        else:
            w_in_ext, w_qb_ext, w_kv_ext, w_out = _prep_odd(mla_w_in[j], mla_w_qb[j], mla_w_kvb[j],
                                                            mla_w_out[j])
            qa_g = mla_qa_g[j].reshape(1, MLA_Q_RANK)
            kva_g = mla_kva_g[j].reshape(1, MLA_KV_RANK)
            qn_g, qn_gs = _slab_gains(mla_qn_g[j])
            kn_g, kn_gs = _slab_gains(mla_kn_g[j])
            qc, kc, vc = _mla_project(xc, mod8, ctx_mod_row, norm_g[i, 1], w_in_ext, qa_g, kva_g, w_qb_ext,
                                      w_kv_ext, qn_g, qn_gs, kn_g, kn_gs,
                                      m_ta_c * q_scale, m_tb_c, m_ta_c, m_tb_c)
            q, k, v = _mla_project(x, mod8, lat_row, norm_g[i, 1], w_in_ext, qa_g, kva_g, w_qb_ext,
                                   w_kv_ext, qn_g, qn_gs, kn_g, kn_gs,
                                   m_ta * q_scale, m_tb * q_scale, m_ta, m_tb)
            mix, w_mix = _attention(q, kc, vc, k, v), w_out
            if not last:
                mix_c = _attention(qc, kc, vc)
                xc_mixed = xc
        x = _ffn_half(x, mod8, lat_row, 2, norm_g[i, 2], ffn_in_b, ffn_out_b, (i, 1), mix, w_mix)
        if not last:
            xc = ctx_ffn(xc_mixed, 2, norm_g[i, 2], (i, 1), mix_c, w_mix)
    return x
```

```python
import functools

import numpy as np
import jax
import jax.numpy as jnp
from jax import lax
from jax.experimental import pallas as pl
from jax.experimental.pallas import tpu as pltpu

F32 = jnp.float32
BF16 = jnp.bfloat16

D_MODEL = 1024
GRID_W = 64
D_FF = 2816
FFN_RESIDUAL = 0.5
N_MOD = 9
ROPE_BASE = 10000.0
EPS = 1e-6
POOL_GROUPS = 4
POOL_CG = 64
POOL_WIDTH = POOL_GROUPS * POOL_CG
POOL_WINDOWS = (2, 4, 8, 16)
POOL_HALO = 8
POOL_SUB = 128
RET_HEADS = 6
RET_PAIRS = RET_HEADS // 2
RET_DK = 64
RET_DV = 128
RET_CHUNK = 128
RET_QK_W = RET_HEADS * RET_DK
RET_V_W = RET_HEADS * RET_DV
AB_IN = POOL_WIDTH + 2 * RET_QK_W + 2 * RET_V_W
MLA_HEADS = 8
MLA_Q_RANK = 384
MLA_KV_RANK = 256
MLA_NOPE = 64
MLA_ROPE = 32
MLA_V = 128
MLA_QK = MLA_NOPE + MLA_ROPE
HEAD_SLAB = 128
V_ROWS = MLA_V + 16
LOG2_E = 1.4426950408889634

LANES = 128
V7X_VMEM_BYTES = 64 * 1024 * 1024
VMEM_LIMIT_BYTES = 56 * 1024 * 1024

ADA_TN = 1536
FFN_TM = 1024
FFN_SUB = 256
MIX_TM = 1024
AB_TM = 1024
AB_SUB = 256
MLA_TM = 1024
MLA_HEAD_UNROLL = 8
ATT_TQ = 1024
ATT_TK = 256
ATT_SPAN = 1
ATT_GROUP = 8


def _cparams(sem):
    return pltpu.CompilerParams(dimension_semantics=sem, vmem_limit_bytes=VMEM_LIMIT_BYTES)


def _resident(shape):
    nd = len(shape)
    return pl.BlockSpec(shape, lambda *_: (0,) * nd, pipeline_mode=pl.Buffered(1))


def _silu(x):
    return x * (1.0 / (1.0 + jnp.exp(-x)))


def _rms_mod(x, g, shift, scale):
    ms = jnp.mean(x * x, axis=-1, keepdims=True)
    return (x * lax.rsqrt(ms + EPS) * g) * (1.0 + scale) + shift


def _ada_kernel(c_ref, w_ref, b_ref, o_ref):
    s = _silu(c_ref[...])
    o_ref[...] = jnp.dot(s, w_ref[...], precision=lax.Precision.HIGHEST,
                         preferred_element_type=F32) + b_ref[...]


def _ada_modulation(cond8, ada_w, ada_b):
    depth, d, n = ada_w.shape
    return pl.pallas_call(
        _ada_kernel,
        grid=(depth, n // ADA_TN),
        in_specs=[
            pl.BlockSpec((8, d), lambda i, j: (0, 0)),
            pl.BlockSpec((None, d, ADA_TN), lambda i, j: (i, 0, j)),
            pl.BlockSpec((None, 1, ADA_TN), lambda i, j: (i, 0, j)),
        ],
        out_specs=pl.BlockSpec((None, 8, ADA_TN), lambda i, j: (i, 0, j)),
        out_shape=jax.ShapeDtypeStruct((depth, 8, n), F32),
        compiler_params=_cparams(("parallel", "parallel")),
        name="ada_mod",
    )(cond8, ada_w, ada_b.reshape(depth, 1, n))


def _ffn_kernel(*refs, k, mixed):
    if mixed:
        x_ref, mod_ref, g_ref, win_ref, wout_ref, mix_ref, wmix_ref, o_ref = refs
    else:
        x_ref, mod_ref, g_ref, win_ref, wout_ref, o_ref = refs
    tm = x_ref.shape[0]
    sub = min(FFN_SUB, tm)
    for s in range(tm // sub):
        rs = slice(s * sub, (s + 1) * sub)
        x = x_ref[rs, :]
        if mixed:
            x = x + mod_ref[5:6, :] * jnp.dot(mix_ref[rs, :], wmix_ref[...], preferred_element_type=F32)
        h = _rms_mod(x, g_ref[...], mod_ref[3 * k:3 * k + 1, :], mod_ref[3 * k + 1:3 * k + 2, :])
        ab = jnp.dot(h.astype(BF16), win_ref[...], preferred_element_type=F32)
        act = (_silu(ab[:, :D_FF]) * ab[:, D_FF:]).astype(BF16)
        y = jnp.dot(act, wout_ref[...], preferred_element_type=F32)
        o_ref[rs, :] = x + (FFN_RESIDUAL * mod_ref[3 * k + 2:3 * k + 3, :]) * y


def _ffn_half(x, mod8, mod_row, k, g, w_in, w_out, which, mix=None, w_mix=None):
    bsz, seq, d = x.shape
    tm = min(FFN_TM, seq)
    layer, half = which
    pick = lambda shape: pl.BlockSpec((None, None) + tuple(shape[2:]), lambda b, t: (layer, half, 0, 0),
                                      pipeline_mode=pl.Buffered(1))
    in_specs = [
        pl.BlockSpec((None, tm, d), lambda b, t: (b, t, 0)),
        pl.BlockSpec((None, N_MOD, d), lambda b, t: (mod_row(b), 0, 0)),
        pl.BlockSpec((1, d), lambda b, t: (0, 0)),
        pick(w_in.shape),
        pick(w_out.shape),
    ]
    args = [x, mod8, g.reshape(1, d), w_in, w_out]
    if mix is not None:
        in_specs += [pl.BlockSpec((None, tm, mix.shape[-1]), lambda b, t: (b, t, 0)),
                     _resident(w_mix.shape)]
        args += [mix, w_mix]
    return pl.pallas_call(
        functools.partial(_ffn_kernel, k=k, mixed=mix is not None),
        grid=(bsz, seq // tm),
        in_specs=in_specs,
        out_specs=pl.BlockSpec((None, tm, d), lambda b, t: (b, t, 0)),
        out_shape=jax.ShapeDtypeStruct(x.shape, F32),
        compiler_params=_cparams(("parallel", "parallel")),
        name="ffn_half",
    )(*args)


def _pair_tables(lg_row, expo):
    return jnp.exp(lg_row * expo)


def _bd_mask():
    r = lax.broadcasted_iota(jnp.int32, (2 * RET_DK, 2 * RET_DV), 0)
    c = lax.broadcasted_iota(jnp.int32, (2 * RET_DK, 2 * RET_DV), 1)
    same_head = jnp.where(r < RET_DK, 0, 1) == jnp.where(c < RET_DV, 0, 1)
    return jnp.where(same_head, 1.0, 0.0)


def _abproj_kernel(x_ref, mod_ref, g_ref, w_ref, cos_ref, sin_ref, lgq_ref, lgv_ref, s0_ref,
                   p_ref, q_ref, k_ref, v_ref, sg_ref, cb_ref, sfin_ref, st_scr, *, tm, nt):
    t = pl.program_id(1)

    @pl.when(t == 0)
    def _():
        st_scr[...] = s0_ref[...]

    q0, k0 = POOL_WIDTH, POOL_WIDTH + RET_QK_W
    v0, g0 = k0 + RET_QK_W, k0 + RET_QK_W + RET_V_W
    row = lax.broadcasted_iota(jnp.int32, (RET_CHUNK, LANES), 0).astype(F32)
    bdm = _bd_mask()
    k_scale = RET_DK ** -0.5
    sub = min(AB_SUB, tm)
    even_lane = (lax.broadcasted_iota(jnp.int32, (sub, LANES), 1) & 1) == 0

    def pair_partner(v):
        return jnp.where(even_lane, pltpu.roll(v, LANES - 1, axis=1), pltpu.roll(v, 1, axis=1))
    for s in reversed(range(tm // sub)):
        r0 = s * sub
        ts = slice(r0, r0 + sub)
        h = _rms_mod(x_ref[ts, :], g_ref[...], mod_ref[3:4, :], mod_ref[4:5, :])
        proj = jnp.dot(h.astype(BF16), w_ref[...], preferred_element_type=F32)
        p_ref[ts, :] = proj[:, :POOL_WIDTH]
        v_ref[ts, :] = proj[:, v0:v0 + RET_V_W].astype(BF16)
        sg_ref[ts, :] = _silu(proj[:, g0:g0 + RET_V_W]).astype(BF16)
        cos = cos_ref[ts, :]
        sin = sin_ref[ts, :]
        for j in range(RET_PAIRS):
            sl = slice(LANES * j, LANES * (j + 1))
            qp = proj[:, q0 + LANES * j:q0 + LANES * (j + 1)]
            kp = proj[:, k0 + LANES * j:k0 + LANES * (j + 1)]
            qr = qp * cos + pair_partner(qp) * sin
            kr = (kp * cos + pair_partner(kp) * sin) * k_scale
            q_ref[ts, sl] = qr.astype(BF16)
            k_ref[ts, sl] = kr.astype(BF16)
            lgq = lgq_ref[:, sl]
            lgv = lgv_ref[:, 2 * LANES * j:2 * LANES * (j + 1)]
            xi = _pair_tables(lgq, float(RET_CHUNK) - row)
            zeta = _pair_tables(lgq, row)
            dec = jnp.exp(lgv * float(RET_CHUNK))
            vsl = slice(2 * LANES * j, 2 * LANES * (j + 1))
            for c in reversed(range(sub // RET_CHUNK)):
                ls = slice(c * RET_CHUNK, (c + 1) * RET_CHUNK)
                rs = slice(r0 + c * RET_CHUNK, r0 + (c + 1) * RET_CHUNK)
                bd = st_scr[j]
                qx = (qr[ls] * xi).astype(BF16)
                cb_ref[rs, vsl] = jnp.dot(qx, bd.astype(BF16),
                                          preferred_element_type=F32).astype(cb_ref.dtype)
                kz = (kr[ls] * zeta).astype(BF16)
                upd = lax.dot_general(kz, v_ref[rs, vsl], (((0,), (0,)), ((), ())),
                                      preferred_element_type=F32)
                st_scr[j] = dec * bd + bdm * upd

    @pl.when(t == nt - 1)
    def _():
        sfin_ref[...] = st_scr[...]


def _ab_project(x, mod8, mod_row, g, w_ext, cos, sin, lg_q, lg_v, s0):
    bsz, seq, d = x.shape
    tm = min(AB_TM, seq)
    nt = seq // tm
    rev = lambda b, t: (b, nt - 1 - t, 0)
    tok = lambda w, dt: jax.ShapeDtypeStruct((bsz, seq, w), dt)
    st_shape = (RET_PAIRS, 2 * RET_DK, 2 * RET_DV)
    return pl.pallas_call(
        functools.partial(_abproj_kernel, tm=tm, nt=nt),
        grid=(bsz, nt),
        in_specs=[
            pl.BlockSpec((None, tm, d), rev),
            pl.BlockSpec((None, N_MOD, d), lambda b, t: (mod_row(b), 0, 0)),
            pl.BlockSpec((1, d), lambda b, t: (0, 0)),
            _resident(w_ext.shape),
            pl.BlockSpec((tm, LANES), lambda b, t: (nt - 1 - t, 0)),
            pl.BlockSpec((tm, LANES), lambda b, t: (nt - 1 - t, 0)),
            pl.BlockSpec((1, RET_QK_W), lambda b, t: (0, 0)),
            pl.BlockSpec((1, RET_V_W), lambda b, t: (0, 0)),
            pl.BlockSpec((None,) + st_shape, lambda b, t: (b, 0, 0, 0)),
        ],
        out_specs=[
            pl.BlockSpec((None, tm, POOL_WIDTH), rev),
            pl.BlockSpec((None, tm, RET_QK_W), rev),
            pl.BlockSpec((None, tm, RET_QK_W), rev),
            pl.BlockSpec((None, tm, RET_V_W), rev),
            pl.BlockSpec((None, tm, RET_V_W), rev),
            pl.BlockSpec((None, tm, RET_V_W), rev),
            pl.BlockSpec((None,) + st_shape, lambda b, t: (b, 0, 0, 0)),
        ],
        out_shape=[tok(POOL_WIDTH, F32), tok(RET_QK_W, BF16), tok(RET_QK_W, BF16),
                   tok(RET_V_W, BF16), tok(RET_V_W, BF16), tok(RET_V_W, BF16),
                   jax.ShapeDtypeStruct((bsz,) + st_shape, F32)],
        scratch_shapes=[pltpu.VMEM(st_shape, F32)],
        compiler_params=_cparams(("arbitrary", "arbitrary")),
        name="ab_project",
    )(x, mod8, g.reshape(1, d), w_ext, cos, sin, lg_q, lg_v, s0)


def _retout_kernel(x_ref, mod_ref, pprev_ref, pcur_ref, pnext_ref, pm_ref, invc_ref,
                   q_ref, k_ref, v_ref, sg_ref, cb_ref, wpool_ref, pscale_ref, rng_ref, wout_ref,
                   lgfq_ref, lgfv_ref, lgbv_ref, s0_ref,
                   o_ref, sfin_ref, st_scr, ycat_scr, pe_scr, *, tm, nt):
    t = pl.program_id(1)

    @pl.when(t == 0)
    def _():
        st_scr[...] = s0_ref[...]

    pe_scr[0:POOL_HALO, :] = pprev_ref[...]
    pe_scr[POOL_HALO:POOL_HALO + tm, :] = pcur_ref[...]
    pe_scr[POOL_HALO + tm:2 * POOL_HALO + tm, :] = pnext_ref[...]
    pe_scr[2 * POOL_HALO + tm:, :] = jnp.zeros((POOL_SUB - 2 * POOL_HALO, POOL_WIDTH), F32)
    lane_grp = jnp.right_shift(lax.broadcasted_iota(jnp.int32, (POOL_SUB, POOL_WIDTH), 1),
                               POOL_CG.bit_length() - 1)
    for s in range(tm // POOL_SUB):
        r0 = s * POOL_SUB
        pe = pe_scr[r0:r0 + 2 * POOL_SUB, :]
        pe_hi = pe.astype(BF16)
        pe_lo = (pe - pe_hi.astype(F32)).astype(BF16)
        pooled = jnp.zeros((POOL_SUB, POOL_WIDTH), F32)
        for gi in range(POOL_GROUPS):
            win = pm_ref[s, gi]
            tot = jnp.dot(win, pe_hi, preferred_element_type=F32) \
                + jnp.dot(win, pe_lo, preferred_element_type=F32)
            pooled = jnp.where(lane_grp == gi, tot, pooled)
        rs = slice(r0, r0 + POOL_SUB)
        diffs = pooled * invc_ref[rs, :] - pcur_ref[rs, :]
        pool_y = jnp.dot(diffs.astype(BF16), wpool_ref[...], preferred_element_type=F32) * pscale_ref[...]
        ycat_scr[rs, 0:POOL_WIDTH] = pool_y.astype(BF16)

    ri = lax.broadcasted_iota(jnp.int32, (RET_CHUNK, RET_CHUNK), 0)
    ci = lax.broadcasted_iota(jnp.int32, (RET_CHUNK, RET_CHUNK), 1)
    rel = (ri - ci).astype(F32)
    row = lax.broadcasted_iota(jnp.int32, (RET_CHUNK, LANES), 0).astype(F32)
    lane = lax.broadcasted_iota(jnp.int32, (RET_CHUNK, LANES), 1)
    bdm = _bd_mask()
    zero_bf = jnp.zeros((RET_CHUNK, LANES), BF16)
    lane2 = lax.broadcasted_iota(jnp.int32, (RET_CHUNK, 2 * RET_DV), 1)
    zero_bf2 = jnp.zeros((RET_CHUNK, 2 * RET_DV), BF16)
    tables = []
    for j in range(RET_PAIRS):
        sl = slice(LANES * j, LANES * (j + 1))
        vsl = slice(2 * LANES * j, 2 * LANES * (j + 1))
        lgq = lgfq_ref[:, sl]
        xi = _pair_tables(lgq, row + 1.0)
        zeta = _pair_tables(lgq, float(RET_CHUNK - 1) - row)
        dec = jnp.exp(lgfv_ref[:, vsl] * float(RET_CHUNK))
        masks = []
        for a in range(2):
            hs = slice(2 * LANES * j + LANES * a, 2 * LANES * j + LANES * (a + 1))
            mf = jnp.exp(lgfv_ref[:, hs] * jnp.maximum(rel, 0.0))
            mb = jnp.exp(lgbv_ref[:, hs] * jnp.maximum(-rel, 0.0))
            masks.append(jnp.where(rel > 0, mf, jnp.where(rel < 0, mb, 2.0)))
        tables.append((xi, zeta, dec, jnp.concatenate(masks, axis=1)))
    for c in range(tm // RET_CHUNK):
        rs = slice(c * RET_CHUNK, (c + 1) * RET_CHUNK)
        for j in range(RET_PAIRS):
            sl = slice(LANES * j, LANES * (j + 1))
            vsl = slice(2 * LANES * j, 2 * LANES * (j + 1))
            xi, zeta, dec, mask2 = tables[j]
            q2 = q_ref[rs, sl]
            k2 = k_ref[rs, sl]
            v2 = v_ref[rs, vsl]
            bd = st_scr[j]
            qx = (q2.astype(F32) * xi).astype(BF16)
            k_bd = jnp.concatenate([jnp.where(lane < RET_DK, k2, zero_bf),
                                    jnp.where(lane >= RET_DK, k2, zero_bf)], axis=0)
            s2 = lax.dot_general(q2, k_bd, (((1,), (1,)), ((), ())), preferred_element_type=F32)
            p2 = (s2 * mask2).astype(BF16)
            v_bd = jnp.concatenate([jnp.where(lane2 < RET_DV, v2, zero_bf2),
                                    jnp.where(lane2 >= RET_DV, v2, zero_bf2)], axis=0)
            ret = jnp.dot(qx, bd.astype(BF16), preferred_element_type=F32) + cb_ref[rs, vsl].astype(F32) \
                + jnp.dot(p2, v_bd, preferred_element_type=F32)
            for a in range(2):
                hs_v = slice(LANES * a, LANES * (a + 1))
                r = ret[:, hs_v]
                ms = jnp.mean(r * r, axis=-1, keepdims=True)
                hcol = 2 * LANES * j + LANES * a
                y = (r * lax.rsqrt(ms + EPS) * rng_ref[:, hcol:hcol + LANES]) \
                    * sg_ref[rs, hcol:hcol + LANES].astype(F32)
                ycat_scr[rs, POOL_WIDTH + hcol:POOL_WIDTH + hcol + LANES] = y.astype(BF16)
            kz = (k2.astype(F32) * zeta).astype(BF16)
            upd = lax.dot_general(kz, v2, (((0,), (0,)), ((), ())), preferred_element_type=F32)
            st_scr[j] = dec * bd + bdm * upd
        yy = jnp.dot(ycat_scr[rs, :], wout_ref[...], preferred_element_type=F32)
        o_ref[rs, :] = x_ref[rs, :] + mod_ref[5:6, :] * yy

    @pl.when(t == nt - 1)
    def _():
        sfin_ref[...] = st_scr[...]


def _pool_constants(tm, seq):
    nt = seq // tm
    nsub = tm // POOL_SUB
    kinds = [0] if nt == 1 else [0, 1, nt - 1]
    mats = np.zeros((len(kinds), nsub, POOL_GROUPS, POOL_SUB, 2 * POOL_SUB), np.float32)
    invc = np.zeros((len(kinds), tm, POOL_WIDTH), np.float32)
    for vi, tile in enumerate(kinds):
        for s in range(nsub):
            pos = tile * tm + s * POOL_SUB + np.arange(POOL_SUB)
            colpos = tile * tm - POOL_HALO + s * POOL_SUB + np.arange(2 * POOL_SUB)
            for gi, w in enumerate(POOL_WINDOWS):
                lo = w // 2
                hi = w - 1 - lo
                start = np.maximum(pos - lo, 0)
                end = np.minimum(pos + hi + 1, seq)
                mats[vi, s, gi] = (colpos[None, :] >= start[:, None]) & (colpos[None, :] < end[:, None])
                invc[vi, s * POOL_SUB:(s + 1) * POOL_SUB, gi * POOL_CG:(gi + 1) * POOL_CG] = \
                    (1.0 / (end - start))[:, None]
    return jnp.asarray(mats, BF16), jnp.asarray(invc, F32), nt


def _ret_out(x, mod8, mod_row, p, q, k, v, sg, cb, wpool_bd, pscale, rng, w_out, lgf_q, lgf_v, lgb_v, s0):
    bsz, seq, d = x.shape
    tm = min(MIX_TM, seq)
    pm, invc, nt = _pool_constants(tm, seq)
    hb = tm // POOL_HALO
    nhb = seq // POOL_HALO

    def variant(t):
        if nt == 1:
            return 0
        return jnp.where(t == 0, 0, jnp.where(t == nt - 1, 2, 1))

    cur = lambda b, t: (b, t, 0)
    st_shape = (RET_PAIRS, 2 * RET_DK, 2 * RET_DV)
    tokspec = lambda w: pl.BlockSpec((None, tm, w), cur)
    return pl.pallas_call(
        functools.partial(_retout_kernel, tm=tm, nt=nt),
        grid=(bsz, nt),
        in_specs=[
            tokspec(d),
            pl.BlockSpec((None, N_MOD, d), lambda b, t: (mod_row(b), 0, 0)),
            pl.BlockSpec((None, POOL_HALO, POOL_WIDTH), lambda b, t: (b, jnp.maximum(t * hb - 1, 0), 0)),
            tokspec(POOL_WIDTH),
            pl.BlockSpec((None, POOL_HALO, POOL_WIDTH),
                         lambda b, t: (b, jnp.minimum((t + 1) * hb, nhb - 1), 0)),
            pl.BlockSpec((None, tm // POOL_SUB, POOL_GROUPS, POOL_SUB, 2 * POOL_SUB),
                         lambda b, t: (variant(t), 0, 0, 0, 0)),
            pl.BlockSpec((None, tm, POOL_WIDTH), lambda b, t: (variant(t), 0, 0)),
            tokspec(RET_QK_W), tokspec(RET_QK_W), tokspec(RET_V_W), tokspec(RET_V_W), tokspec(RET_V_W),
            _resident(wpool_bd.shape),
            pl.BlockSpec((1, POOL_WIDTH), lambda b, t: (0, 0)),
            pl.BlockSpec((1, RET_V_W), lambda b, t: (0, 0)),
            _resident(w_out.shape),
            pl.BlockSpec((1, RET_QK_W), lambda b, t: (0, 0)),
            pl.BlockSpec((1, RET_V_W), lambda b, t: (0, 0)),
            pl.BlockSpec((1, RET_V_W), lambda b, t: (0, 0)),
            pl.BlockSpec((None,) + st_shape, lambda b, t: (b, 0, 0, 0)),
        ],
        out_specs=[
            tokspec(d),
            pl.BlockSpec((None,) + st_shape, lambda b, t: (b, 0, 0, 0)),
        ],
        out_shape=[jax.ShapeDtypeStruct(x.shape, F32),
                   jax.ShapeDtypeStruct((bsz,) + st_shape, F32)],
        scratch_shapes=[pltpu.VMEM(st_shape, F32),
                        pltpu.VMEM((tm, d), BF16),
                        pltpu.VMEM((tm + POOL_SUB, POOL_WIDTH), F32)],
        compiler_params=_cparams(("arbitrary", "arbitrary")),
        name="ret_out",
    )(x, mod8, p, p, p, pm, invc, q, k, v, sg, cb, wpool_bd, pscale, rng, w_out,
      lgf_q, lgf_v, lgb_v, s0)


def _head_norm_rope(main, aux, g_main, g_aux):
    ms = jnp.sum(main * main, axis=-1, keepdims=True) * (1.0 / MLA_QK)
    r = lax.rsqrt(ms + EPS)
    return (main * r) * g_main + (aux * r) * g_aux


def _mlaproj_kernel(x_ref, mod_ref, g_ref, win_ref, qag_ref, kvag_ref, wqb_ref, wkv_ref,
                    qg_ref, qgs_ref, kg_ref, kgs_ref, qa_t_ref, qb_t_ref, ka_t_ref, kb_t_ref,
                    q_ref, k_ref, v_ref, qan_scr, kvn_scr, kr_scr, gt_scr):
    h = _rms_mod(x_ref[...], g_ref[...], mod_ref[3:4, :], mod_ref[4:5, :])
    proj = jnp.dot(h.astype(BF16), win_ref[...], preferred_element_type=F32)
    qa = proj[:, :MLA_Q_RANK]
    kva = proj[:, MLA_Q_RANK:MLA_Q_RANK + MLA_KV_RANK]
    r0 = MLA_Q_RANK + MLA_KV_RANK
    kr_scr[0] = proj[:, r0:r0 + HEAD_SLAB]
    kr_scr[1] = proj[:, r0 + HEAD_SLAB:r0 + 2 * HEAD_SLAB]
    qan = qa * lax.rsqrt(jnp.mean(qa * qa, axis=-1, keepdims=True) + EPS) * qag_ref[...]
    kvn = kva * lax.rsqrt(jnp.mean(kva * kva, axis=-1, keepdims=True) + EPS) * kvag_ref[...]
    qan_scr[...] = qan.astype(BF16)
    kvn_scr[...] = kvn.astype(BF16)
    gt_scr[0] = qg_ref[...] * qa_t_ref[...]
    gt_scr[1] = qgs_ref[...] * qb_t_ref[...]
    gt_scr[2] = kg_ref[...] * ka_t_ref[...]
    gt_scr[3] = kgs_ref[...] * kb_t_ref[...]
    tm = proj.shape[0]
    ones_rows = jnp.where(lax.broadcasted_iota(jnp.int32, (V_ROWS - MLA_V, tm), 0) == 0,
                          1.0, 0.0).astype(BF16)

    def head(hd, carry):
        c0 = pl.multiple_of(hd * (2 * HEAD_SLAB), 2 * HEAD_SLAB)
        qh = jnp.dot(qan_scr[...], wqb_ref[:, pl.ds(c0, 2 * HEAD_SLAB)], preferred_element_type=F32)
        kvh = jnp.dot(kvn_scr[...], wkv_ref[:, pl.ds(c0, 2 * HEAD_SLAB)], preferred_element_type=F32)
        q_ref[hd] = _head_norm_rope(qh[:, :HEAD_SLAB], qh[:, HEAD_SLAB:], gt_scr[0], gt_scr[1]).T.astype(BF16)
        k_ref[hd] = _head_norm_rope(kvh[:, :HEAD_SLAB] + kr_scr[0], kr_scr[1],
                                    gt_scr[2], gt_scr[3]).astype(BF16)
        v_ref[hd, 0:MLA_V, :] = kvh[:, HEAD_SLAB:].T.astype(BF16)
        v_ref[hd, MLA_V:V_ROWS, :] = ones_rows
        return carry
    lax.fori_loop(0, MLA_HEADS, head, 0, unroll=MLA_HEAD_UNROLL)


def _mla_project(x, mod8, mod_row, g, w_in, qa_g, kva_g, w_qb, w_kv, qn_g, qn_gs, kn_g, kn_gs,
                 q_ta, q_tb, k_ta, k_tb):
    bsz, seq, d = x.shape
    tm = min(MLA_TM, seq)
    head_out = jax.ShapeDtypeStruct((bsz, MLA_HEADS, seq, HEAD_SLAB), BF16)
    hspec = pl.BlockSpec((None, MLA_HEADS, tm, HEAD_SLAB), lambda b, t: (b, 0, t, 0))
    tspec = pl.BlockSpec((tm, HEAD_SLAB), lambda b, t: (t, 0))
    row = lambda n: pl.BlockSpec((1, n), lambda b, t: (0, 0))
    return pl.pallas_call(
        _mlaproj_kernel,
        grid=(bsz, seq // tm),
        in_specs=[
            pl.BlockSpec((None, tm, d), lambda b, t: (b, t, 0)),
            pl.BlockSpec((None, N_MOD, d), lambda b, t: (mod_row(b), 0, 0)),
            row(d),
            _resident(w_in.shape), row(MLA_Q_RANK), row(MLA_KV_RANK),
            _resident(w_qb.shape), _resident(w_kv.shape),
            row(HEAD_SLAB), row(HEAD_SLAB), row(HEAD_SLAB), row(HEAD_SLAB),
            tspec, tspec, tspec, tspec,
        ],
        out_specs=[pl.BlockSpec((None, MLA_HEADS, HEAD_SLAB, tm), lambda b, t: (b, 0, 0, t)),
                   hspec,
                   pl.BlockSpec((None, MLA_HEADS, V_ROWS, tm), lambda b, t: (b, 0, 0, t))],
        out_shape=[jax.ShapeDtypeStruct((bsz, MLA_HEADS, HEAD_SLAB, seq), BF16),
                   head_out,
                   jax.ShapeDtypeStruct((bsz, MLA_HEADS, V_ROWS, seq), BF16)],
        scratch_shapes=[pltpu.VMEM((tm, MLA_Q_RANK), BF16), pltpu.VMEM((tm, MLA_KV_RANK), BF16),
                        pltpu.VMEM((2, tm, HEAD_SLAB), F32), pltpu.VMEM((4, tm, HEAD_SLAB), F32)],
        compiler_params=_cparams(("parallel", "parallel")),
        name="mla_project",
    )(x, mod8, g.reshape(1, d), w_in, qa_g, kva_g, w_qb, w_kv, qn_g, qn_gs, kn_g, kn_gs,
      q_ta, q_tb, k_ta, k_tb)


def _attn_kernel(*refs, n_spans, span_blocks, tk, group_spans, tq, n_q):
    if n_spans:
        q_ref, kc_ref, vct_ref, k_ref, vt_ref, o_ref, m_scr, acc_scr, s_scr, mb_scr = refs
    else:
        q_ref, kc_ref, vct_ref, o_ref, m_scr, acc_scr = refs

    def colmax(st):
        return jnp.max(st, axis=0, keepdims=True)

    def finalize(row0):
        acc = acc_scr[...]
        out_t = acc[0:MLA_V, :] * (1.0 / acc[MLA_V:MLA_V + 1, :])
        o_ref[pl.ds(row0, tq), :] = out_t.T.astype(o_ref.dtype)

    def softmax_pv(st, m_blk, vtb, first):
        if first:
            m_new = m_blk
        else:
            m_prev = m_scr[...]
            m_new = jnp.maximum(m_prev, m_blk)
        p = jnp.exp2(st - m_new).astype(BF16)
        pv = jnp.dot(vtb, p, preferred_element_type=F32)
        if first:
            acc_scr[...] = pv
        else:
            acc_scr[...] = jnp.exp2(m_prev - m_new) * acc_scr[...] + pv
        m_scr[...] = m_new

    if not n_spans:
        st_c = jnp.dot(kc_ref[...], q_ref[...], preferred_element_type=F32)
        softmax_pv(st_c, colmax(st_c), vct_ref[...], True)
        finalize(0)
        return

    span = span_blocks * tk
    lc = kc_ref.shape[0]

    def query_tile(q, prev_row0):
        def produce(slot, kb):
            rows = kb.shape[0]
            st = jnp.dot(kb, q, preferred_element_type=F32)
            s_scr[slot, 0:rows, :] = st
            blk = min(tk, rows)
            for sb in range(rows // blk):
                mb_scr[slot, sb] = colmax(st[sb * blk:(sb + 1) * blk, :])

        def consume(slot, v_off):
            for sb in range(span_blocks):
                softmax_pv(s_scr[slot, sb * tk:(sb + 1) * tk, :], mb_scr[slot, sb],
                           vt_ref[:, pl.ds(v_off + sb * tk, tk)], False)

        produce(0, k_ref[0:span, :])
        if prev_row0 is not None:
            finalize(prev_row0)
        m_scr[...] = jnp.full(m_scr.shape, -jnp.inf, F32)
        acc_scr[...] = jnp.zeros(acc_scr.shape, F32)

        def group(i, carry):
            base = i * (group_spans * span)
            for u in range(group_spans):
                cur = pl.multiple_of(base + u * span, span)
                nxt = pl.multiple_of(base + (u + 1) * span, span)
                produce((u + 1) % 2, k_ref[pl.ds(nxt, span), :])
                consume(u % 2, cur)
            return carry
        lax.fori_loop(0, n_spans // group_spans - 1, group, 0)
        e0 = n_spans - group_spans
        for u in range(group_spans):
            cur = (e0 + u) * span
            if u + 1 < group_spans:
                produce((u + 1) % 2, k_ref[cur + span:cur + 2 * span, :])
            else:
                produce((u + 1) % 2, kc_ref[...])
            consume(u % 2, cur)
        c_slot = group_spans % 2
        softmax_pv(s_scr[c_slot, 0:lc, :], mb_scr[c_slot, 0], vct_ref[...], False)

    query_tile(q_ref[:, 0:tq], None)

    def later_tile(i, carry):
        off = pl.multiple_of(i * tq, tq)
        query_tile(q_ref[:, pl.ds(off, tq)], pl.multiple_of(off - tq, tq))
        return carry
    lax.fori_loop(1, n_q, later_tile, 0)
    finalize((n_q - 1) * tq)


def _attention(q, kc, vct, k=None, vt=None):
    bsz, nh, hs, lq = q.shape
    lc = kc.shape[2]
    tq = min(ATT_TQ, lq)
    qspec = pl.BlockSpec((None, None, hs, lq), lambda b, h: (b, h, 0, 0))
    kfull = lambda n: pl.BlockSpec((None, None, n, hs), lambda b, h: (b, h, 0, 0))
    vfull = lambda n: pl.BlockSpec((None, None, V_ROWS, n), lambda b, h: (b, h, 0, 0))
    in_specs = [qspec, kfull(lc), vfull(lc)]
    args = [q, kc, vct]
    n_spans = 0
    grp = 0
    tk = ATT_TK
    span_blocks = 1
    if k is not None:
        lk = k.shape[2]
        tk = min(ATT_TK, lk)
        span_blocks = min(ATT_SPAN, lk // tk)
        n_spans = lk // (tk * span_blocks)
        grp = min(ATT_GROUP, n_spans)
        assert lk % (tk * span_blocks) == 0 and n_spans % grp == 0, "latent keys are consumed in whole groups"
        assert grp % 2 == 0 or n_spans == grp, "score slots alternate within a group"
        assert lc <= tk, "the context block reuses a latent score slot"
        in_specs += [kfull(lk), vfull(lk)]
        args += [k, vt]
    scratch = [pltpu.VMEM((1, tq), F32), pltpu.VMEM((V_ROWS, tq), F32)]
    if n_spans:
        scratch += [pltpu.VMEM((2, span_blocks * tk, tq), F32), pltpu.VMEM((2, span_blocks, 1, tq), F32)]
    return pl.pallas_call(
        functools.partial(_attn_kernel, n_spans=n_spans, span_blocks=span_blocks, tk=tk,
                          group_spans=grp, tq=tq, n_q=lq // tq),
        grid=(bsz, nh),
        in_specs=in_specs,
        out_specs=pl.BlockSpec((None, lq, hs), lambda b, h: (b, 0, h)),
        out_shape=jax.ShapeDtypeStruct((bsz, lq, nh * hs), BF16),
        scratch_shapes=scratch,
        compiler_params=_cparams(("parallel", "parallel")),
        name="mla_attention",
    )(*args)


def _pair_swap_cols(w):
    w2 = w.reshape(w.shape[:-1] + (w.shape[-1] // 2, 2))
    return jnp.stack([-w2[..., 1], w2[..., 0]], axis=-1).reshape(w.shape)


def _rope_angles(seq, rot_dim):
    pos = jnp.arange(seq)
    row = (pos // GRID_W).astype(F32)
    col = (pos % GRID_W).astype(F32)
    n_freq = rot_dim // 4
    inv = ROPE_BASE ** (-jnp.arange(n_freq, dtype=F32) / n_freq)
    ang = jnp.concatenate([row[:, None] * inv, col[:, None] * inv], axis=-1)
    return jnp.repeat(jnp.cos(ang), 2, axis=-1), jnp.repeat(jnp.sin(ang), 2, axis=-1)


def _ret_tables(seq, ctx_len):
    cos, sin = _rope_angles(seq, RET_DK)
    cos2, sin2 = jnp.tile(cos, (1, 2)), jnp.tile(sin, (1, 2))
    pair_sign = jnp.tile(jnp.asarray([-1.0, 1.0], F32), LANES // 2)
    return cos2, sin2 * pair_sign, jnp.ones((ctx_len, LANES), F32), jnp.zeros((ctx_len, LANES), F32)


def _mla_tables(seq, ctx_len):
    cos, sin = _rope_angles(seq, MLA_ROPE)
    one = jnp.ones((seq, MLA_NOPE), F32)
    zero_tail = jnp.zeros((seq, HEAD_SLAB - MLA_QK), F32)
    zero_head = jnp.zeros((seq, MLA_NOPE), F32)
    ta = jnp.concatenate([one, cos, zero_tail], axis=-1)
    tb = jnp.concatenate([zero_head, sin, zero_tail], axis=-1)
    ident = jnp.concatenate([jnp.ones((ctx_len, MLA_QK), F32),
                             jnp.zeros((ctx_len, HEAD_SLAB - MLA_QK), F32)], axis=-1)
    return ta, tb, ident, jnp.zeros((ctx_len, HEAD_SLAB), F32)


def _rope_slab(rope_cols):
    lead = rope_cols.shape[:-1]
    return jnp.concatenate([jnp.zeros(lead + (MLA_NOPE,), F32), rope_cols,
                            jnp.zeros(lead + (HEAD_SLAB - MLA_QK,), F32)], axis=-1)


def _slab_gains(gvec):
    rope = gvec[MLA_NOPE:].reshape(MLA_ROPE // 2, 2)
    swapped = jnp.stack([rope[:, 1], rope[:, 0]], axis=-1).reshape(MLA_ROPE)
    main = jnp.concatenate([gvec, jnp.zeros((HEAD_SLAB - MLA_QK,), F32)])
    return main.reshape(1, HEAD_SLAB), _rope_slab(swapped).reshape(1, HEAD_SLAB)


def _prep_even(ab_w_in, pool_w, ab_w_out):
    w_ext = ab_w_in.astype(BF16)
    wpool_bd = jnp.zeros((POOL_WIDTH, POOL_WIDTH), F32)
    for gi in range(POOL_GROUPS):
        s = slice(gi * POOL_CG, (gi + 1) * POOL_CG)
        wpool_bd = wpool_bd.at[s, s].set(pool_w[gi])
    return w_ext, wpool_bd.astype(BF16), ab_w_out.astype(BF16)


def _prep_odd(w_in, w_qb, w_kvb, w_out):
    wkr = w_in[:, MLA_Q_RANK + MLA_KV_RANK:]
    w_in_ext = jnp.concatenate([w_in[:, :MLA_Q_RANK + MLA_KV_RANK], _rope_slab(wkr),
                                _rope_slab(_pair_swap_cols(wkr))], axis=-1).astype(BF16)
    wq3 = w_qb.reshape(MLA_Q_RANK, MLA_HEADS, MLA_QK)
    wq_main = jnp.concatenate([wq3, jnp.zeros((MLA_Q_RANK, MLA_HEADS, HEAD_SLAB - MLA_QK), F32)], axis=-1)
    wq_aux = _rope_slab(_pair_swap_cols(wq3[..., MLA_NOPE:]))
    w_qb_ext = jnp.concatenate([wq_main, wq_aux], axis=-1).reshape(
        MLA_Q_RANK, MLA_HEADS * 2 * HEAD_SLAB).astype(BF16)
    wkv3 = w_kvb.reshape(MLA_KV_RANK, MLA_HEADS, MLA_NOPE + MLA_V)
    wk_slab = jnp.concatenate([wkv3[..., :MLA_NOPE],
                               jnp.zeros((MLA_KV_RANK, MLA_HEADS, HEAD_SLAB - MLA_NOPE), F32)], axis=-1)
    w_kv_ext = jnp.concatenate([wk_slab, wkv3[..., MLA_NOPE:]], axis=-1).reshape(
        MLA_KV_RANK, MLA_HEADS * 2 * HEAD_SLAB).astype(BF16)
    return w_in_ext, w_qb_ext, w_kv_ext, w_out.astype(BF16)


def kernel(x, c, ctx, c_ctx, ada_w, ada_b, norm_g, ffn_w_in, ffn_w_out, ab_w_in, pool_w, pool_scale,
           ret_log_decay, ret_norm_g, ab_w_out, mla_w_in, mla_qa_g, mla_kva_g, mla_w_qb, mla_w_kvb,
           mla_qn_g, mla_kn_g, mla_w_out):
    bsz, seq, d = x.shape
    ctx_len = ctx.shape[1]
    depth = ada_w.shape[0]
    ctx_row = bsz
    cond8 = jnp.zeros((8, d), F32).at[:bsz].set(c).at[ctx_row].set(c_ctx)
    mod_all = _ada_modulation(cond8, ada_w, ada_b).reshape(depth, 8, N_MOD, d)
    lat_row = lambda b: b
    ctx_mod_row = lambda b: ctx_row

    ret_cos, ret_sin, ret_cos_c, ret_sin_c = _ret_tables(seq, ctx_len)
    m_ta, m_tb, m_ta_c, m_tb_c = _mla_tables(seq, ctx_len)
    q_scale = MLA_QK ** -0.5 * LOG2_E
    zero_state = jnp.zeros((bsz, RET_PAIRS, 2 * RET_DK, 2 * RET_DV), F32)
    ffn_in_b, ffn_out_b = ffn_w_in.astype(BF16), ffn_w_out.astype(BF16)

    xc = ctx
    for i in range(depth):
        last = i == depth - 1
        j = i // 2
        mod8 = mod_all[i]

        def ctx_ffn(xc_in, k, g, which, mix_in=None, w_mix_in=None, mod8=mod8):
            flat = lambda a: None if a is None else a.reshape(1, bsz * ctx_len, a.shape[-1])
            out = _ffn_half(flat(xc_in), mod8, ctx_mod_row, k, g, ffn_in_b, ffn_out_b, which,
                            flat(mix_in), w_mix_in)
            return out.reshape(bsz, ctx_len, d)

        x = _ffn_half(x, mod8, lat_row, 0, norm_g[i, 0], ffn_in_b, ffn_out_b, (i, 0))
        xc = ctx_ffn(xc, 0, norm_g[i, 0], (i, 0))
        mix = mix_c = w_mix = None
        if i % 2 == 0:
            w_ext, wpool_bd, w_out = _prep_even(ab_w_in[j], pool_w[j], ab_w_out[j])
            lg = ret_log_decay[j]
            lgf_q = jnp.repeat(lg[0], RET_DK).reshape(1, RET_QK_W)
            lgb_q = jnp.repeat(lg[1], RET_DK).reshape(1, RET_QK_W)
            lgf_v = jnp.repeat(lg[0], RET_DV).reshape(1, RET_V_W)
            lgb_v = jnp.repeat(lg[1], RET_DV).reshape(1, RET_V_W)
            pscale = pool_scale[j].reshape(1, POOL_WIDTH)
            rng = ret_norm_g[j].reshape(1, RET_V_W)
            pc, qc, kc, vc, sgc, cbc, state_b = _ab_project(
                xc, mod8, ctx_mod_row, norm_g[i, 1], w_ext, ret_cos_c, ret_sin_c, lgb_q, lgb_v, zero_state)
            xc_mixed, state_f = _ret_out(xc, mod8, ctx_mod_row, pc, qc, kc, vc, sgc, cbc, wpool_bd, pscale,
                                         rng, w_out, lgf_q, lgf_v, lgb_v, zero_state)
            p, q, k, v, sg, cb, _ = _ab_project(
                x, mod8, lat_row, norm_g[i, 1], w_ext, ret_cos, ret_sin, lgb_q, lgb_v, state_b)
            x, _ = _ret_out(x, mod8, lat_row, p, q, k, v, sg, cb, wpool_bd, pscale, rng, w_out,
                            lgf_q, lgf_v, lgb_v, state_f)
        else:
            w_in_ext, w_qb_ext, w_kv_ext, w_out = _prep_odd(mla_w_in[j], mla_w_qb[j], mla_w_kvb[j],
                                                            mla_w_out[j])
            qa_g = mla_qa_g[j].reshape(1, MLA_Q_RANK)
            kva_g = mla_kva_g[j].reshape(1, MLA_KV_RANK)
            qn_g, qn_gs = _slab_gains(mla_qn_g[j])
            kn_g, kn_gs = _slab_gains(mla_kn_g[j])
            qc, kc, vc = _mla_project(xc, mod8, ctx_mod_row, norm_g[i, 1], w_in_ext, qa_g, kva_g, w_qb_ext,
                                      w_kv_ext, qn_g, qn_gs, kn_g, kn_gs,
                                      m_ta_c * q_scale, m_tb_c, m_ta_c, m_tb_c)
            q, k, v = _mla_project(x, mod8, lat_row, norm_g[i, 1], w_in_ext, qa_g, kva_g, w_qb_ext,
                                   w_kv_ext, qn_g, qn_gs, kn_g, kn_gs,
                                   m_ta * q_scale, m_tb * q_scale, m_ta, m_tb)
            mix, w_mix = _attention(q, kc, vc, k, v), w_out
            if not last:
                mix_c = _attention(qc, kc, vc)
                xc_mixed = xc
        x = _ffn_half(x, mod8, lat_row, 2, norm_g[i, 2], ffn_in_b, ffn_out_b, (i, 1), mix, w_mix)
        if not last:
            xc = ctx_ffn(xc_mixed, 2, norm_g[i, 2], (i, 1), mix_c, w_mix)
    return x
```

```python
import functools

import numpy as np
import jax
import jax.numpy as jnp
from jax import lax
from jax.experimental import pallas as pl
from jax.experimental.pallas import tpu as pltpu

F32 = jnp.float32
BF16 = jnp.bfloat16

D_MODEL = 1024
GRID_W = 64
D_FF = 2816
FFN_RESIDUAL = 0.5
N_MOD = 9
ROPE_BASE = 10000.0
EPS = 1e-6
POOL_GROUPS = 4
POOL_CG = 64
POOL_WIDTH = POOL_GROUPS * POOL_CG
POOL_WINDOWS = (2, 4, 8, 16)
POOL_HALO = 8
POOL_SUB = 128
RET_HEADS = 6
RET_PAIRS = RET_HEADS // 2
RET_DK = 64
RET_DV = 128
RET_CHUNK = 128
RET_QK_W = RET_HEADS * RET_DK
RET_V_W = RET_HEADS * RET_DV
AB_IN = POOL_WIDTH + 2 * RET_QK_W + 2 * RET_V_W
MLA_HEADS = 8
MLA_Q_RANK = 384
MLA_KV_RANK = 256
MLA_NOPE = 64
MLA_ROPE = 32
MLA_V = 128
MLA_QK = MLA_NOPE + MLA_ROPE
HEAD_SLAB = 128
V_ROWS = MLA_V + 16
LOG2_E = 1.4426950408889634

LANES = 128
V7X_VMEM_BYTES = 64 * 1024 * 1024
VMEM_LIMIT_BYTES = 56 * 1024 * 1024

ADA_TN = 1536
FFN_TM = 1024
FFN_SUB = 256
MIX_TM = 1024
AB_TM = 1024
AB_SUB = 256
MLA_TM = 1024
MLA_HEAD_UNROLL = 8
ATT_TQ = 1024
ATT_TK = 512
ATT_SPAN = 1
ATT_GROUP = 4


def _cparams(sem):
    return pltpu.CompilerParams(dimension_semantics=sem, vmem_limit_bytes=VMEM_LIMIT_BYTES)


def _resident(shape):
    nd = len(shape)
    return pl.BlockSpec(shape, lambda *_: (0,) * nd, pipeline_mode=pl.Buffered(1))


def _silu(x):
    hx = 0.5 * x
    return hx + hx * jnp.tanh(hx)


def _rms_mod(x, g, shift, scale):
    ms = jnp.mean(x * x, axis=-1, keepdims=True)
    return (x * lax.rsqrt(ms + EPS) * g) * (1.0 + scale) + shift


def _ada_kernel(c_ref, w_ref, b_ref, o_ref):
    s = _silu(c_ref[...])
    o_ref[...] = jnp.dot(s, w_ref[...], precision=lax.Precision.HIGHEST,
                         preferred_element_type=F32) + b_ref[...]


def _ada_modulation(cond8, ada_w, ada_b):
    depth, d, n = ada_w.shape
    return pl.pallas_call(
        _ada_kernel,
        grid=(depth, n // ADA_TN),
        in_specs=[
            pl.BlockSpec((8, d), lambda i, j: (0, 0)),
            pl.BlockSpec((None, d, ADA_TN), lambda i, j: (i, 0, j)),
            pl.BlockSpec((None, 1, ADA_TN), lambda i, j: (i, 0, j)),
        ],
        out_specs=pl.BlockSpec((None, 8, ADA_TN), lambda i, j: (i, 0, j)),
        out_shape=jax.ShapeDtypeStruct((depth, 8, n), F32),
        compiler_params=_cparams(("parallel", "parallel")),
        name="ada_mod",
    )(cond8, ada_w, ada_b.reshape(depth, 1, n))


def _ffn_kernel(*refs, k, mixed):
    if mixed:
        x_ref, mod_ref, g_ref, win_ref, wout_ref, mix_ref, wmix_ref, o_ref = refs
    else:
        x_ref, mod_ref, g_ref, win_ref, wout_ref, o_ref = refs
    tm = x_ref.shape[0]
    sub = min(FFN_SUB, tm)
    for s in range(tm // sub):
        rs = slice(s * sub, (s + 1) * sub)
        x = x_ref[rs, :]
        if mixed:
            x = x + mod_ref[5:6, :] * jnp.dot(mix_ref[rs, :], wmix_ref[...], preferred_element_type=F32)
        h = _rms_mod(x, g_ref[...], mod_ref[3 * k:3 * k + 1, :], mod_ref[3 * k + 1:3 * k + 2, :])
        ab = jnp.dot(h.astype(BF16), win_ref[...], preferred_element_type=F32)
        act = (_silu(ab[:, :D_FF]) * ab[:, D_FF:]).astype(BF16)
        y = jnp.dot(act, wout_ref[...], preferred_element_type=F32)
        o_ref[rs, :] = x + (FFN_RESIDUAL * mod_ref[3 * k + 2:3 * k + 3, :]) * y


def _ffn_half(x, mod8, mod_row, k, g, w_in, w_out, which, mix=None, w_mix=None):
    bsz, seq, d = x.shape
    tm = min(FFN_TM, seq)
    layer, half = which
    pick = lambda shape: pl.BlockSpec((None, None) + tuple(shape[2:]), lambda b, t: (layer, half, 0, 0),
                                      pipeline_mode=pl.Buffered(1))
    in_specs = [
        pl.BlockSpec((None, tm, d), lambda b, t: (b, t, 0)),
        pl.BlockSpec((None, N_MOD, d), lambda b, t: (mod_row(b), 0, 0)),
        pl.BlockSpec((1, d), lambda b, t: (0, 0)),
        pick(w_in.shape),
        pick(w_out.shape),
    ]
    args = [x, mod8, g.reshape(1, d), w_in, w_out]
    if mix is not None:
        in_specs += [pl.BlockSpec((None, tm, mix.shape[-1]), lambda b, t: (b, t, 0)),
                     _resident(w_mix.shape)]
        args += [mix, w_mix]
    return pl.pallas_call(
        functools.partial(_ffn_kernel, k=k, mixed=mix is not None),
        grid=(bsz, seq // tm),
        in_specs=in_specs,
        out_specs=pl.BlockSpec((None, tm, d), lambda b, t: (b, t, 0)),
        out_shape=jax.ShapeDtypeStruct(x.shape, F32),
        compiler_params=_cparams(("parallel", "parallel")),
        name="ffn_half",
    )(*args)


def _pair_tables(lg_row, expo):
    return jnp.exp(lg_row * expo)


def _bd_mask():
    r = lax.broadcasted_iota(jnp.int32, (2 * RET_DK, 2 * RET_DV), 0)
    c = lax.broadcasted_iota(jnp.int32, (2 * RET_DK, 2 * RET_DV), 1)
    same_head = jnp.where(r < RET_DK, 0, 1) == jnp.where(c < RET_DV, 0, 1)
    return jnp.where(same_head, 1.0, 0.0)


def _abproj_kernel(x_ref, mod_ref, g_ref, w_ref, cos_ref, sin_ref, lgq_ref, lgv_ref, s0_ref,
                   p_ref, q_ref, k_ref, v_ref, sg_ref, cb_ref, sfin_ref, st_scr, *, tm, nt):
    t = pl.program_id(1)

    @pl.when(t == 0)
    def _():
        st_scr[...] = s0_ref[...]

    q0, k0 = POOL_WIDTH, POOL_WIDTH + RET_QK_W
    v0, g0 = k0 + RET_QK_W, k0 + RET_QK_W + RET_V_W
    row = lax.broadcasted_iota(jnp.int32, (RET_CHUNK, LANES), 0).astype(F32)
    bdm = _bd_mask()
    k_scale = RET_DK ** -0.5
    sub = min(AB_SUB, tm)
    even_lane = (lax.broadcasted_iota(jnp.int32, (sub, LANES), 1) & 1) == 0

    def pair_partner(v):
        return jnp.where(even_lane, pltpu.roll(v, LANES - 1, axis=1), pltpu.roll(v, 1, axis=1))
    for s in reversed(range(tm // sub)):
        r0 = s * sub
        ts = slice(r0, r0 + sub)
        h = _rms_mod(x_ref[ts, :], g_ref[...], mod_ref[3:4, :], mod_ref[4:5, :])
        proj = jnp.dot(h.astype(BF16), w_ref[...], preferred_element_type=F32)
        p_ref[ts, :] = proj[:, :POOL_WIDTH]
        v_ref[ts, :] = proj[:, v0:v0 + RET_V_W].astype(BF16)
        sg_ref[ts, :] = _silu(proj[:, g0:g0 + RET_V_W]).astype(BF16)
        cos = cos_ref[ts, :]
        sin = sin_ref[ts, :]
        for j in range(RET_PAIRS):
            sl = slice(LANES * j, LANES * (j + 1))
            qp = proj[:, q0 + LANES * j:q0 + LANES * (j + 1)]
            kp = proj[:, k0 + LANES * j:k0 + LANES * (j + 1)]
            qr = qp * cos + pair_partner(qp) * sin
            kr = (kp * cos + pair_partner(kp) * sin) * k_scale
            q_ref[ts, sl] = qr.astype(BF16)
            k_ref[ts, sl] = kr.astype(BF16)
            lgq = lgq_ref[:, sl]
            lgv = lgv_ref[:, 2 * LANES * j:2 * LANES * (j + 1)]
            xi = _pair_tables(lgq, float(RET_CHUNK) - row)
            zeta = _pair_tables(lgq, row)
            dec = jnp.exp(lgv * float(RET_CHUNK))
            vsl = slice(2 * LANES * j, 2 * LANES * (j + 1))
            for c in reversed(range(sub // RET_CHUNK)):
                ls = slice(c * RET_CHUNK, (c + 1) * RET_CHUNK)
                rs = slice(r0 + c * RET_CHUNK, r0 + (c + 1) * RET_CHUNK)
                bd = st_scr[j]
                qx = (qr[ls] * xi).astype(BF16)
                cb_ref[rs, vsl] = jnp.dot(qx, bd.astype(BF16),
                                          preferred_element_type=F32).astype(cb_ref.dtype)
                kz = (kr[ls] * zeta).astype(BF16)
                upd = lax.dot_general(kz, v_ref[rs, vsl], (((0,), (0,)), ((), ())),
                                      preferred_element_type=F32)
                st_scr[j] = dec * bd + bdm * upd

    @pl.when(t == nt - 1)
    def _():
        sfin_ref[...] = st_scr[...]


def _ab_project(x, mod8, mod_row, g, w_ext, cos, sin, lg_q, lg_v, s0):
    bsz, seq, d = x.shape
    tm = min(AB_TM, seq)
    nt = seq // tm
    rev = lambda b, t: (b, nt - 1 - t, 0)
    tok = lambda w, dt: jax.ShapeDtypeStruct((bsz, seq, w), dt)
    st_shape = (RET_PAIRS, 2 * RET_DK, 2 * RET_DV)
    return pl.pallas_call(
        functools.partial(_abproj_kernel, tm=tm, nt=nt),
        grid=(bsz, nt),
        in_specs=[
            pl.BlockSpec((None, tm, d), rev),
            pl.BlockSpec((None, N_MOD, d), lambda b, t: (mod_row(b), 0, 0)),
            pl.BlockSpec((1, d), lambda b, t: (0, 0)),
            _resident(w_ext.shape),
            pl.BlockSpec((tm, LANES), lambda b, t: (nt - 1 - t, 0)),
            pl.BlockSpec((tm, LANES), lambda b, t: (nt - 1 - t, 0)),
            pl.BlockSpec((1, RET_QK_W), lambda b, t: (0, 0)),
            pl.BlockSpec((1, RET_V_W), lambda b, t: (0, 0)),
            pl.BlockSpec((None,) + st_shape, lambda b, t: (b, 0, 0, 0)),
        ],
        out_specs=[
            pl.BlockSpec((None, tm, POOL_WIDTH), rev),
            pl.BlockSpec((None, tm, RET_QK_W), rev),
            pl.BlockSpec((None, tm, RET_QK_W), rev),
            pl.BlockSpec((None, tm, RET_V_W), rev),
            pl.BlockSpec((None, tm, RET_V_W), rev),
            pl.BlockSpec((None, tm, RET_V_W), rev),
            pl.BlockSpec((None,) + st_shape, lambda b, t: (b, 0, 0, 0)),
        ],
        out_shape=[tok(POOL_WIDTH, F32), tok(RET_QK_W, BF16), tok(RET_QK_W, BF16),
                   tok(RET_V_W, BF16), tok(RET_V_W, BF16), tok(RET_V_W, BF16),
                   jax.ShapeDtypeStruct((bsz,) + st_shape, F32)],
        scratch_shapes=[pltpu.VMEM(st_shape, F32)],
        compiler_params=_cparams(("arbitrary", "arbitrary")),
        name="ab_project",
    )(x, mod8, g.reshape(1, d), w_ext, cos, sin, lg_q, lg_v, s0)


def _retout_kernel(x_ref, mod_ref, pprev_ref, pcur_ref, pnext_ref, pm_ref, invc_ref,
                   q_ref, k_ref, v_ref, sg_ref, cb_ref, wpool_ref, pscale_ref, rng_ref, wout_ref,
                   lgfq_ref, lgfv_ref, lgbv_ref, s0_ref,
                   o_ref, sfin_ref, st_scr, ycat_scr, pe_scr, *, tm, nt):
    t = pl.program_id(1)

    @pl.when(t == 0)
    def _():
        st_scr[...] = s0_ref[...]

    pe_scr[0:POOL_HALO, :] = pprev_ref[...]
    pe_scr[POOL_HALO:POOL_HALO + tm, :] = pcur_ref[...]
    pe_scr[POOL_HALO + tm:2 * POOL_HALO + tm, :] = pnext_ref[...]
    pe_scr[2 * POOL_HALO + tm:, :] = jnp.zeros((POOL_SUB - 2 * POOL_HALO, POOL_WIDTH), F32)
    lane_grp = jnp.right_shift(lax.broadcasted_iota(jnp.int32, (POOL_SUB, POOL_WIDTH), 1),
                               POOL_CG.bit_length() - 1)
    for s in range(tm // POOL_SUB):
        r0 = s * POOL_SUB
        pe = pe_scr[r0:r0 + 2 * POOL_SUB, :]
        pe_hi = pe.astype(BF16)
        pe_lo = (pe - pe_hi.astype(F32)).astype(BF16)
        pooled = jnp.zeros((POOL_SUB, POOL_WIDTH), F32)
        for gi in range(POOL_GROUPS):
            win = pm_ref[s, gi]
            tot = jnp.dot(win, pe_hi, preferred_element_type=F32) \
                + jnp.dot(win, pe_lo, preferred_element_type=F32)
            pooled = jnp.where(lane_grp == gi, tot, pooled)
        rs = slice(r0, r0 + POOL_SUB)
        diffs = pooled * invc_ref[rs, :] - pcur_ref[rs, :]
        pool_y = jnp.dot(diffs.astype(BF16), wpool_ref[...], preferred_element_type=F32) * pscale_ref[...]
        ycat_scr[rs, 0:POOL_WIDTH] = pool_y.astype(BF16)

    ri = lax.broadcasted_iota(jnp.int32, (RET_CHUNK, RET_CHUNK), 0)
    ci = lax.broadcasted_iota(jnp.int32, (RET_CHUNK, RET_CHUNK), 1)
    rel = (ri - ci).astype(F32)
    row = lax.broadcasted_iota(jnp.int32, (RET_CHUNK, LANES), 0).astype(F32)
    lane = lax.broadcasted_iota(jnp.int32, (RET_CHUNK, LANES), 1)
    bdm = _bd_mask()
    zero_bf = jnp.zeros((RET_CHUNK, LANES), BF16)
    lane2 = lax.broadcasted_iota(jnp.int32, (RET_CHUNK, 2 * RET_DV), 1)
    zero_bf2 = jnp.zeros((RET_CHUNK, 2 * RET_DV), BF16)
    tables = []
    for j in range(RET_PAIRS):
        sl = slice(LANES * j, LANES * (j + 1))
        vsl = slice(2 * LANES * j, 2 * LANES * (j + 1))
        lgq = lgfq_ref[:, sl]
        xi = _pair_tables(lgq, row + 1.0)
        zeta = _pair_tables(lgq, float(RET_CHUNK - 1) - row)
        dec = jnp.exp(lgfv_ref[:, vsl] * float(RET_CHUNK))
        masks = []
        for a in range(2):
            hs = slice(2 * LANES * j + LANES * a, 2 * LANES * j + LANES * (a + 1))
            mf = jnp.exp(lgfv_ref[:, hs] * jnp.maximum(rel, 0.0))
            mb = jnp.exp(lgbv_ref[:, hs] * jnp.maximum(-rel, 0.0))
            masks.append(jnp.where(rel > 0, mf, jnp.where(rel < 0, mb, 2.0)))
        tables.append((xi, zeta, dec, jnp.concatenate(masks, axis=1)))
    for c in range(tm // RET_CHUNK):
        rs = slice(c * RET_CHUNK, (c + 1) * RET_CHUNK)
        for j in range(RET_PAIRS):
            sl = slice(LANES * j, LANES * (j + 1))
            vsl = slice(2 * LANES * j, 2 * LANES * (j + 1))
            xi, zeta, dec, mask2 = tables[j]
            q2 = q_ref[rs, sl]
            k2 = k_ref[rs, sl]
            v2 = v_ref[rs, vsl]
            bd = st_scr[j]
            qx = (q2.astype(F32) * xi).astype(BF16)
            k_bd = jnp.concatenate([jnp.where(lane < RET_DK, k2, zero_bf),
                                    jnp.where(lane >= RET_DK, k2, zero_bf)], axis=0)
            s2 = lax.dot_general(q2, k_bd, (((1,), (1,)), ((), ())), preferred_element_type=F32)
            p2 = (s2 * mask2).astype(BF16)
            v_bd = jnp.concatenate([jnp.where(lane2 < RET_DV, v2, zero_bf2),
                                    jnp.where(lane2 >= RET_DV, v2, zero_bf2)], axis=0)
            ret = jnp.dot(qx, bd.astype(BF16), preferred_element_type=F32) + cb_ref[rs, vsl].astype(F32) \
                + jnp.dot(p2, v_bd, preferred_element_type=F32)
            for a in range(2):
                hs_v = slice(LANES * a, LANES * (a + 1))
                r = ret[:, hs_v]
                ms = jnp.mean(r * r, axis=-1, keepdims=True)
                hcol = 2 * LANES * j + LANES * a
                y = (r * lax.rsqrt(ms + EPS) * rng_ref[:, hcol:hcol + LANES]) \
                    * sg_ref[rs, hcol:hcol + LANES].astype(F32)
                ycat_scr[rs, POOL_WIDTH + hcol:POOL_WIDTH + hcol + LANES] = y.astype(BF16)
            kz = (k2.astype(F32) * zeta).astype(BF16)
            upd = lax.dot_general(kz, v2, (((0,), (0,)), ((), ())), preferred_element_type=F32)
            st_scr[j] = dec * bd + bdm * upd
        yy = jnp.dot(ycat_scr[rs, :], wout_ref[...], preferred_element_type=F32)
        o_ref[rs, :] = x_ref[rs, :] + mod_ref[5:6, :] * yy

    @pl.when(t == nt - 1)
    def _():
        sfin_ref[...] = st_scr[...]


def _pool_constants(tm, seq):
    nt = seq // tm
    nsub = tm // POOL_SUB
    kinds = [0] if nt == 1 else [0, 1, nt - 1]
    mats = np.zeros((len(kinds), nsub, POOL_GROUPS, POOL_SUB, 2 * POOL_SUB), np.float32)
    invc = np.zeros((len(kinds), tm, POOL_WIDTH), np.float32)
    for vi, tile in enumerate(kinds):
        for s in range(nsub):
            pos = tile * tm + s * POOL_SUB + np.arange(POOL_SUB)
            colpos = tile * tm - POOL_HALO + s * POOL_SUB + np.arange(2 * POOL_SUB)
            for gi, w in enumerate(POOL_WINDOWS):
                lo = w // 2
                hi = w - 1 - lo
                start = np.maximum(pos - lo, 0)
                end = np.minimum(pos + hi + 1, seq)
                mats[vi, s, gi] = (colpos[None, :] >= start[:, None]) & (colpos[None, :] < end[:, None])
                invc[vi, s * POOL_SUB:(s + 1) * POOL_SUB, gi * POOL_CG:(gi + 1) * POOL_CG] = \
                    (1.0 / (end - start))[:, None]
    return jnp.asarray(mats, BF16), jnp.asarray(invc, F32), nt


def _ret_out(x, mod8, mod_row, p, q, k, v, sg, cb, wpool_bd, pscale, rng, w_out, lgf_q, lgf_v, lgb_v, s0):
    bsz, seq, d = x.shape
    tm = min(MIX_TM, seq)
    pm, invc, nt = _pool_constants(tm, seq)
    hb = tm // POOL_HALO
    nhb = seq // POOL_HALO

    def variant(t):
        if nt == 1:
            return 0
        return jnp.where(t == 0, 0, jnp.where(t == nt - 1, 2, 1))

    cur = lambda b, t: (b, t, 0)
    st_shape = (RET_PAIRS, 2 * RET_DK, 2 * RET_DV)
    tokspec = lambda w: pl.BlockSpec((None, tm, w), cur)
    return pl.pallas_call(
        functools.partial(_retout_kernel, tm=tm, nt=nt),
        grid=(bsz, nt),
        in_specs=[
            tokspec(d),
            pl.BlockSpec((None, N_MOD, d), lambda b, t: (mod_row(b), 0, 0)),
            pl.BlockSpec((None, POOL_HALO, POOL_WIDTH), lambda b, t: (b, jnp.maximum(t * hb - 1, 0), 0)),
            tokspec(POOL_WIDTH),
            pl.BlockSpec((None, POOL_HALO, POOL_WIDTH),
                         lambda b, t: (b, jnp.minimum((t + 1) * hb, nhb - 1), 0)),
            pl.BlockSpec((None, tm // POOL_SUB, POOL_GROUPS, POOL_SUB, 2 * POOL_SUB),
                         lambda b, t: (variant(t), 0, 0, 0, 0)),
            pl.BlockSpec((None, tm, POOL_WIDTH), lambda b, t: (variant(t), 0, 0)),
            tokspec(RET_QK_W), tokspec(RET_QK_W), tokspec(RET_V_W), tokspec(RET_V_W), tokspec(RET_V_W),
            _resident(wpool_bd.shape),
            pl.BlockSpec((1, POOL_WIDTH), lambda b, t: (0, 0)),
            pl.BlockSpec((1, RET_V_W), lambda b, t: (0, 0)),
            _resident(w_out.shape),
            pl.BlockSpec((1, RET_QK_W), lambda b, t: (0, 0)),
            pl.BlockSpec((1, RET_V_W), lambda b, t: (0, 0)),
            pl.BlockSpec((1, RET_V_W), lambda b, t: (0, 0)),
            pl.BlockSpec((None,) + st_shape, lambda b, t: (b, 0, 0, 0)),
        ],
        out_specs=[
            tokspec(d),
            pl.BlockSpec((None,) + st_shape, lambda b, t: (b, 0, 0, 0)),
        ],
        out_shape=[jax.ShapeDtypeStruct(x.shape, F32),
                   jax.ShapeDtypeStruct((bsz,) + st_shape, F32)],
        scratch_shapes=[pltpu.VMEM(st_shape, F32),
                        pltpu.VMEM((tm, d), BF16),
                        pltpu.VMEM((tm + POOL_SUB, POOL_WIDTH), F32)],
        compiler_params=_cparams(("arbitrary", "arbitrary")),
        name="ret_out",
    )(x, mod8, p, p, p, pm, invc, q, k, v, sg, cb, wpool_bd, pscale, rng, w_out,
      lgf_q, lgf_v, lgb_v, s0)


def _head_norm_rope(main, aux, g_main, g_aux):
    ms = jnp.sum(main * main, axis=-1, keepdims=True) * (1.0 / MLA_QK)
    r = lax.rsqrt(ms + EPS)
    return (main * r) * g_main + (aux * r) * g_aux


def _mlaproj_kernel(x_ref, mod_ref, g_ref, win_ref, qag_ref, kvag_ref, wqb_ref, wkv_ref,
                    qg_ref, qgs_ref, kg_ref, kgs_ref, qa_t_ref, qb_t_ref, ka_t_ref, kb_t_ref,
                    q_ref, k_ref, v_ref, qan_scr, kvn_scr, kr_scr, gt_scr):
    h = _rms_mod(x_ref[...], g_ref[...], mod_ref[3:4, :], mod_ref[4:5, :])
    proj = jnp.dot(h.astype(BF16), win_ref[...], preferred_element_type=F32)
    qa = proj[:, :MLA_Q_RANK]
    kva = proj[:, MLA_Q_RANK:MLA_Q_RANK + MLA_KV_RANK]
    r0 = MLA_Q_RANK + MLA_KV_RANK
    kr_scr[0] = proj[:, r0:r0 + HEAD_SLAB]
    kr_scr[1] = proj[:, r0 + HEAD_SLAB:r0 + 2 * HEAD_SLAB]
    qan = qa * lax.rsqrt(jnp.mean(qa * qa, axis=-1, keepdims=True) + EPS) * qag_ref[...]
    kvn = kva * lax.rsqrt(jnp.mean(kva * kva, axis=-1, keepdims=True) + EPS) * kvag_ref[...]
    qan_scr[...] = qan.astype(BF16)
    kvn_scr[...] = kvn.astype(BF16)
    gt_scr[0] = qg_ref[...] * qa_t_ref[...]
    gt_scr[1] = qgs_ref[...] * qb_t_ref[...]
    gt_scr[2] = kg_ref[...] * ka_t_ref[...]
    gt_scr[3] = kgs_ref[...] * kb_t_ref[...]
    tm = proj.shape[0]
    ones_rows = jnp.where(lax.broadcasted_iota(jnp.int32, (V_ROWS - MLA_V, tm), 0) == 0,
                          1.0, 0.0).astype(BF16)

    def head(hd, carry):
        c0 = pl.multiple_of(hd * (2 * HEAD_SLAB), 2 * HEAD_SLAB)
        qh = jnp.dot(qan_scr[...], wqb_ref[:, pl.ds(c0, 2 * HEAD_SLAB)], preferred_element_type=F32)
        kvh = jnp.dot(kvn_scr[...], wkv_ref[:, pl.ds(c0, 2 * HEAD_SLAB)], preferred_element_type=F32)
        q_ref[hd] = _head_norm_rope(qh[:, :HEAD_SLAB], qh[:, HEAD_SLAB:], gt_scr[0], gt_scr[1]).T.astype(BF16)
        k_ref[hd] = _head_norm_rope(kvh[:, :HEAD_SLAB] + kr_scr[0], kr_scr[1],
                                    gt_scr[2], gt_scr[3]).astype(BF16)
        v_ref[hd, 0:MLA_V, :] = kvh[:, HEAD_SLAB:].T.astype(BF16)
        v_ref[hd, MLA_V:V_ROWS, :] = ones_rows
        return carry
    lax.fori_loop(0, MLA_HEADS, head, 0, unroll=MLA_HEAD_UNROLL)


def _mla_project(x, mod8, mod_row, g, w_in, qa_g, kva_g, w_qb, w_kv, qn_g, qn_gs, kn_g, kn_gs,
                 q_ta, q_tb, k_ta, k_tb):
    bsz, seq, d = x.shape
    tm = min(MLA_TM, seq)
    head_out = jax.ShapeDtypeStruct((bsz, MLA_HEADS, seq, HEAD_SLAB), BF16)
    hspec = pl.BlockSpec((None, MLA_HEADS, tm, HEAD_SLAB), lambda b, t: (b, 0, t, 0))
    tspec = pl.BlockSpec((tm, HEAD_SLAB), lambda b, t: (t, 0))
    row = lambda n: pl.BlockSpec((1, n), lambda b, t: (0, 0))
    return pl.pallas_call(
        _mlaproj_kernel,
        grid=(bsz, seq // tm),
        in_specs=[
            pl.BlockSpec((None, tm, d), lambda b, t: (b, t, 0)),
            pl.BlockSpec((None, N_MOD, d), lambda b, t: (mod_row(b), 0, 0)),
            row(d),
            _resident(w_in.shape), row(MLA_Q_RANK), row(MLA_KV_RANK),
            _resident(w_qb.shape), _resident(w_kv.shape),
            row(HEAD_SLAB), row(HEAD_SLAB), row(HEAD_SLAB), row(HEAD_SLAB),
            tspec, tspec, tspec, tspec,
        ],
        out_specs=[pl.BlockSpec((None, MLA_HEADS, HEAD_SLAB, tm), lambda b, t: (b, 0, 0, t)),
                   hspec,
                   pl.BlockSpec((None, MLA_HEADS, V_ROWS, tm), lambda b, t: (b, 0, 0, t))],
        out_shape=[jax.ShapeDtypeStruct((bsz, MLA_HEADS, HEAD_SLAB, seq), BF16),
                   head_out,
                   jax.ShapeDtypeStruct((bsz, MLA_HEADS, V_ROWS, seq), BF16)],
        scratch_shapes=[pltpu.VMEM((tm, MLA_Q_RANK), BF16), pltpu.VMEM((tm, MLA_KV_RANK), BF16),
                        pltpu.VMEM((2, tm, HEAD_SLAB), F32), pltpu.VMEM((4, tm, HEAD_SLAB), F32)],
        compiler_params=_cparams(("parallel", "parallel")),
        name="mla_project",
    )(x, mod8, g.reshape(1, d), w_in, qa_g, kva_g, w_qb, w_kv, qn_g, qn_gs, kn_g, kn_gs,
      q_ta, q_tb, k_ta, k_tb)


def _attn_kernel(*refs, n_spans, span_blocks, tk, group_spans, tq, n_q):
    if n_spans:
        q_ref, kc_ref, vct_ref, k_ref, vt_ref, o_ref, m_scr, acc_scr, s_scr, mb_scr = refs
    else:
        q_ref, kc_ref, vct_ref, o_ref, m_scr, acc_scr = refs

    def colmax(st):
        return jnp.max(st, axis=0, keepdims=True)

    def finalize(row0):
        acc = acc_scr[...]
        out_t = acc[0:MLA_V, :] * (1.0 / acc[MLA_V:MLA_V + 1, :])
        o_ref[pl.ds(row0, tq), :] = out_t.T.astype(o_ref.dtype)

    def softmax_pv(st, m_blk, vtb, first):
        if first:
            m_new = m_blk
        else:
            m_prev = m_scr[...]
            m_new = jnp.maximum(m_prev, m_blk)
        p = jnp.exp2(st - m_new).astype(BF16)
        pv = jnp.dot(vtb, p, preferred_element_type=F32)
        if first:
            acc_scr[...] = pv
        else:
            acc_scr[...] = jnp.exp2(m_prev - m_new) * acc_scr[...] + pv
        m_scr[...] = m_new

    if not n_spans:
        st_c = jnp.dot(kc_ref[...], q_ref[...], preferred_element_type=F32)
        softmax_pv(st_c, colmax(st_c), vct_ref[...], True)
        finalize(0)
        return

    span = span_blocks * tk
    lc = kc_ref.shape[0]

    def query_tile(q, prev_row0):
        def produce(slot, kb):
            rows = kb.shape[0]
            st = jnp.dot(kb, q, preferred_element_type=F32)
            s_scr[slot, 0:rows, :] = st
            blk = min(tk, rows)
            for sb in range(rows // blk):
                mb_scr[slot, sb] = colmax(st[sb * blk:(sb + 1) * blk, :])

        def consume(slot, v_off):
            for sb in range(span_blocks):
                softmax_pv(s_scr[slot, sb * tk:(sb + 1) * tk, :], mb_scr[slot, sb],
                           vt_ref[:, pl.ds(v_off + sb * tk, tk)], False)

        produce(0, k_ref[0:span, :])
        if prev_row0 is not None:
            finalize(prev_row0)
        m_scr[...] = jnp.full(m_scr.shape, -jnp.inf, F32)
        acc_scr[...] = jnp.zeros(acc_scr.shape, F32)

        def group(i, carry):
            base = i * (group_spans * span)
            for u in range(group_spans):
                cur = pl.multiple_of(base + u * span, span)
                nxt = pl.multiple_of(base + (u + 1) * span, span)
                produce((u + 1) % 2, k_ref[pl.ds(nxt, span), :])
                consume(u % 2, cur)
            return carry
        lax.fori_loop(0, n_spans // group_spans - 1, group, 0)
        e0 = n_spans - group_spans
        for u in range(group_spans):
            cur = (e0 + u) * span
            if u + 1 < group_spans:
                produce((u + 1) % 2, k_ref[cur + span:cur + 2 * span, :])
            else:
                produce((u + 1) % 2, kc_ref[...])
            consume(u % 2, cur)
        c_slot = group_spans % 2
        softmax_pv(s_scr[c_slot, 0:lc, :], mb_scr[c_slot, 0], vct_ref[...], False)

    query_tile(q_ref[:, 0:tq], None)

    def later_tile(i, carry):
        off = pl.multiple_of(i * tq, tq)
        query_tile(q_ref[:, pl.ds(off, tq)], pl.multiple_of(off - tq, tq))
        return carry
    lax.fori_loop(1, n_q, later_tile, 0)
    finalize((n_q - 1) * tq)


def _attention(q, kc, vct, k=None, vt=None):
    bsz, nh, hs, lq = q.shape
    lc = kc.shape[2]
    tq = min(ATT_TQ, lq)
    qspec = pl.BlockSpec((None, None, hs, lq), lambda b, h: (b, h, 0, 0))
    kfull = lambda n: pl.BlockSpec((None, None, n, hs), lambda b, h: (b, h, 0, 0))
    vfull = lambda n: pl.BlockSpec((None, None, V_ROWS, n), lambda b, h: (b, h, 0, 0))
    in_specs = [qspec, kfull(lc), vfull(lc)]
    args = [q, kc, vct]
    n_spans = 0
    grp = 0
    tk = ATT_TK
    span_blocks = 1
    if k is not None:
        lk = k.shape[2]
        tk = min(ATT_TK, lk)
        span_blocks = min(ATT_SPAN, lk // tk)
        n_spans = lk // (tk * span_blocks)
        grp = min(ATT_GROUP, n_spans)
        assert lk % (tk * span_blocks) == 0 and n_spans % grp == 0, "latent keys are consumed in whole groups"
        assert grp % 2 == 0 or n_spans == grp, "score slots alternate within a group"
        assert lc <= tk, "the context block reuses a latent score slot"
        in_specs += [kfull(lk), vfull(lk)]
        args += [k, vt]
    scratch = [pltpu.VMEM((1, tq), F32), pltpu.VMEM((V_ROWS, tq), F32)]
    if n_spans:
        scratch += [pltpu.VMEM((2, span_blocks * tk, tq), F32), pltpu.VMEM((2, span_blocks, 1, tq), F32)]
    return pl.pallas_call(
        functools.partial(_attn_kernel, n_spans=n_spans, span_blocks=span_blocks, tk=tk,
                          group_spans=grp, tq=tq, n_q=lq // tq),
        grid=(bsz, nh),
        in_specs=in_specs,
        out_specs=pl.BlockSpec((None, lq, hs), lambda b, h: (b, 0, h)),
        out_shape=jax.ShapeDtypeStruct((bsz, lq, nh * hs), BF16),
        scratch_shapes=scratch,
        compiler_params=_cparams(("parallel", "parallel")),
        name="mla_attention",
    )(*args)


def _pair_swap_cols(w):
    w2 = w.reshape(w.shape[:-1] + (w.shape[-1] // 2, 2))
    return jnp.stack([-w2[..., 1], w2[..., 0]], axis=-1).reshape(w.shape)


def _rope_angles(seq, rot_dim):
    pos = jnp.arange(seq)
    row = (pos // GRID_W).astype(F32)
    col = (pos % GRID_W).astype(F32)
    n_freq = rot_dim // 4
    inv = ROPE_BASE ** (-jnp.arange(n_freq, dtype=F32) / n_freq)
    ang = jnp.concatenate([row[:, None] * inv, col[:, None] * inv], axis=-1)
    return jnp.repeat(jnp.cos(ang), 2, axis=-1), jnp.repeat(jnp.sin(ang), 2, axis=-1)


def _ret_tables(seq, ctx_len):
    cos, sin = _rope_angles(seq, RET_DK)
    cos2, sin2 = jnp.tile(cos, (1, 2)), jnp.tile(sin, (1, 2))
    pair_sign = jnp.tile(jnp.asarray([-1.0, 1.0], F32), LANES // 2)
    return cos2, sin2 * pair_sign, jnp.ones((ctx_len, LANES), F32), jnp.zeros((ctx_len, LANES), F32)


def _mla_tables(seq, ctx_len):
    cos, sin = _rope_angles(seq, MLA_ROPE)
    one = jnp.ones((seq, MLA_NOPE), F32)
    zero_tail = jnp.zeros((seq, HEAD_SLAB - MLA_QK), F32)
    zero_head = jnp.zeros((seq, MLA_NOPE), F32)
    ta = jnp.concatenate([one, cos, zero_tail], axis=-1)
    tb = jnp.concatenate([zero_head, sin, zero_tail], axis=-1)
    ident = jnp.concatenate([jnp.ones((ctx_len, MLA_QK), F32),
                             jnp.zeros((ctx_len, HEAD_SLAB - MLA_QK), F32)], axis=-1)
    return ta, tb, ident, jnp.zeros((ctx_len, HEAD_SLAB), F32)


def _rope_slab(rope_cols):
    lead = rope_cols.shape[:-1]
    return jnp.concatenate([jnp.zeros(lead + (MLA_NOPE,), F32), rope_cols,
                            jnp.zeros(lead + (HEAD_SLAB - MLA_QK,), F32)], axis=-1)


def _slab_gains(gvec):
    rope = gvec[MLA_NOPE:].reshape(MLA_ROPE // 2, 2)
    swapped = jnp.stack([rope[:, 1], rope[:, 0]], axis=-1).reshape(MLA_ROPE)
    main = jnp.concatenate([gvec, jnp.zeros((HEAD_SLAB - MLA_QK,), F32)])
    return main.reshape(1, HEAD_SLAB), _rope_slab(swapped).reshape(1, HEAD_SLAB)


def _prep_even(ab_w_in, pool_w, ab_w_out):
    w_ext = ab_w_in.astype(BF16)
    wpool_bd = jnp.zeros((POOL_WIDTH, POOL_WIDTH), F32)
    for gi in range(POOL_GROUPS):
        s = slice(gi * POOL_CG, (gi + 1) * POOL_CG)
        wpool_bd = wpool_bd.at[s, s].set(pool_w[gi])
    return w_ext, wpool_bd.astype(BF16), ab_w_out.astype(BF16)


def _prep_odd(w_in, w_qb, w_kvb, w_out):
    wkr = w_in[:, MLA_Q_RANK + MLA_KV_RANK:]
    w_in_ext = jnp.concatenate([w_in[:, :MLA_Q_RANK + MLA_KV_RANK], _rope_slab(wkr),
                                _rope_slab(_pair_swap_cols(wkr))], axis=-1).astype(BF16)
    wq3 = w_qb.reshape(MLA_Q_RANK, MLA_HEADS, MLA_QK)
    wq_main = jnp.concatenate([wq3, jnp.zeros((MLA_Q_RANK, MLA_HEADS, HEAD_SLAB - MLA_QK), F32)], axis=-1)
    wq_aux = _rope_slab(_pair_swap_cols(wq3[..., MLA_NOPE:]))
    w_qb_ext = jnp.concatenate([wq_main, wq_aux], axis=-1).reshape(
        MLA_Q_RANK, MLA_HEADS * 2 * HEAD_SLAB).astype(BF16)
    wkv3 = w_kvb.reshape(MLA_KV_RANK, MLA_HEADS, MLA_NOPE + MLA_V)
    wk_slab = jnp.concatenate([wkv3[..., :MLA_NOPE],
                               jnp.zeros((MLA_KV_RANK, MLA_HEADS, HEAD_SLAB - MLA_NOPE), F32)], axis=-1)
    w_kv_ext = jnp.concatenate([wk_slab, wkv3[..., MLA_NOPE:]], axis=-1).reshape(
        MLA_KV_RANK, MLA_HEADS * 2 * HEAD_SLAB).astype(BF16)
    return w_in_ext, w_qb_ext, w_kv_ext, w_out.astype(BF16)


def kernel(x, c, ctx, c_ctx, ada_w, ada_b, norm_g, ffn_w_in, ffn_w_out, ab_w_in, pool_w, pool_scale,
           ret_log_decay, ret_norm_g, ab_w_out, mla_w_in, mla_qa_g, mla_kva_g, mla_w_qb, mla_w_kvb,
           mla_qn_g, mla_kn_g, mla_w_out):
    bsz, seq, d = x.shape
    ctx_len = ctx.shape[1]
    depth = ada_w.shape[0]
    ctx_row = bsz
    cond8 = jnp.zeros((8, d), F32).at[:bsz].set(c).at[ctx_row].set(c_ctx)
    mod_all = _ada_modulation(cond8, ada_w, ada_b).reshape(depth, 8, N_MOD, d)
    lat_row = lambda b: b
    ctx_mod_row = lambda b: ctx_row

    ret_cos, ret_sin, ret_cos_c, ret_sin_c = _ret_tables(seq, ctx_len)
    m_ta, m_tb, m_ta_c, m_tb_c = _mla_tables(seq, ctx_len)
    q_scale = MLA_QK ** -0.5 * LOG2_E
    zero_state = jnp.zeros((bsz, RET_PAIRS, 2 * RET_DK, 2 * RET_DV), F32)
    ffn_in_b, ffn_out_b = ffn_w_in.astype(BF16), ffn_w_out.astype(BF16)

    xc = ctx
    for i in range(depth):
        last = i == depth - 1
        j = i // 2
        mod8 = mod_all[i]

        def ctx_ffn(xc_in, k, g, which, mix_in=None, w_mix_in=None, mod8=mod8):
            return _ffn_half(xc_in, mod8, ctx_mod_row, k, g, ffn_in_b, ffn_out_b, which, mix_in, w_mix_in)

        x = _ffn_half(x, mod8, lat_row, 0, norm_g[i, 0], ffn_in_b, ffn_out_b, (i, 0))
        xc = ctx_ffn(xc, 0, norm_g[i, 0], (i, 0))
        mix = mix_c = w_mix = None
        if i % 2 == 0:
            w_ext, wpool_bd, w_out = _prep_even(ab_w_in[j], pool_w[j], ab_w_out[j])
            lg = ret_log_decay[j]
            lgf_q = jnp.repeat(lg[0], RET_DK).reshape(1, RET_QK_W)
            lgb_q = jnp.repeat(lg[1], RET_DK).reshape(1, RET_QK_W)
            lgf_v = jnp.repeat(lg[0], RET_DV).reshape(1, RET_V_W)
            lgb_v = jnp.repeat(lg[1], RET_DV).reshape(1, RET_V_W)
            pscale = pool_scale[j].reshape(1, POOL_WIDTH)
            rng = ret_norm_g[j].reshape(1, RET_V_W)
            pc, qc, kc, vc, sgc, cbc, state_b = _ab_project(
                xc, mod8, ctx_mod_row, norm_g[i, 1], w_ext, ret_cos_c, ret_sin_c, lgb_q, lgb_v, zero_state)
            xc_mixed, state_f = _ret_out(xc, mod8, ctx_mod_row, pc, qc, kc, vc, sgc, cbc, wpool_bd, pscale,
                                         rng, w_out, lgf_q, lgf_v, lgb_v, zero_state)
            p, q, k, v, sg, cb, _ = _ab_project(
                x, mod8, lat_row, norm_g[i, 1], w_ext, ret_cos, ret_sin, lgb_q, lgb_v, state_b)
            x, _ = _ret_out(x, mod8, lat_row, p, q, k, v, sg, cb, wpool_bd, pscale, rng, w_out,
                            lgf_q, lgf_v, lgb_v, state_f)
        else:
            w_in_ext, w_qb_ext, w_kv_ext, w_out = _prep_odd(mla_w_in[j], mla_w_qb[j], mla_w_kvb[j],
                                                            mla_w_out[j])
            qa_g = mla_qa_g[j].reshape(1, MLA_Q_RANK)
            kva_g = mla_kva_g[j].reshape(1, MLA_KV_RANK)
            qn_g, qn_gs = _slab_gains(mla_qn_g[j])
            kn_g, kn_gs = _slab_gains(mla_kn_g[j])
            qc, kc, vc = _mla_project(xc, mod8, ctx_mod_row, norm_g[i, 1], w_in_ext, qa_g, kva_g, w_qb_ext,
                                      w_kv_ext, qn_g, qn_gs, kn_g, kn_gs,
                                      m_ta_c * q_scale, m_tb_c, m_ta_c, m_tb_c)
            q, k, v = _mla_project(x, mod8, lat_row, norm_g[i, 1], w_in_ext, qa_g, kva_g, w_qb_ext,
                                   w_kv_ext, qn_g, qn_gs, kn_g, kn_gs,
                                   m_ta * q_scale, m_tb * q_scale, m_ta, m_tb)
            mix, w_mix = _attention(q, kc, vc, k, v), w_out
            if not last:
                mix_c = _attention(qc, kc, vc)
                xc_mixed = xc
        x = _ffn_half(x, mod8, lat_row, 2, norm_g[i, 2], ffn_in_b, ffn_out_b, (i, 1), mix, w_mix)
        if not last:
            xc = ctx_ffn(xc_mixed, 2, norm_g[i, 2], (i, 1), mix_c, w_mix)
    return x
```

```python
import functools

import numpy as np
import jax
import jax.numpy as jnp
from jax import lax
from jax.experimental import pallas as pl
from jax.experimental.pallas import tpu as pltpu

F32 = jnp.float32
BF16 = jnp.bfloat16

D_MODEL = 1024
GRID_W = 64
D_FF = 2816
FFN_RESIDUAL = 0.5
N_MOD = 9
ROPE_BASE = 10000.0
EPS = 1e-6
POOL_GROUPS = 4
POOL_CG = 64
POOL_WIDTH = POOL_GROUPS * POOL_CG
POOL_WINDOWS = (2, 4, 8, 16)
POOL_HALO = 8
POOL_SUB = 128
RET_HEADS = 6
RET_PAIRS = RET_HEADS // 2
RET_DK = 64
RET_DV = 128
RET_CHUNK = 128
RET_QK_W = RET_HEADS * RET_DK
RET_V_W = RET_HEADS * RET_DV
AB_IN = POOL_WIDTH + 2 * RET_QK_W + 2 * RET_V_W
MLA_HEADS = 8
MLA_Q_RANK = 384
MLA_KV_RANK = 256
MLA_NOPE = 64
MLA_ROPE = 32
MLA_V = 128
MLA_QK = MLA_NOPE + MLA_ROPE
HEAD_SLAB = 128
V_ROWS = MLA_V + 16
LOG2_E = 1.4426950408889634

LANES = 128
V7X_VMEM_BYTES = 64 * 1024 * 1024
VMEM_LIMIT_BYTES = 56 * 1024 * 1024

ADA_TN = 1536
FFN_TM = 1024
FFN_SUB = 256
MIX_TM = 1024
AB_TM = 1024
AB_SUB = 256
MLA_TM = 1024
MLA_HEAD_UNROLL = 8
ATT_TQ = 1024
ATT_TK = 512
ATT_SPAN = 1
ATT_GROUP = 4


def _cparams(sem):
    return pltpu.CompilerParams(dimension_semantics=sem, vmem_limit_bytes=VMEM_LIMIT_BYTES)


def _resident(shape):
    nd = len(shape)
    return pl.BlockSpec(shape, lambda *_: (0,) * nd, pipeline_mode=pl.Buffered(1))


def _silu(x):
    return x * (1.0 / (1.0 + jnp.exp(-x)))


def _rms_mod(x, g, shift, scale):
    ms = jnp.mean(x * x, axis=-1, keepdims=True)
    return (x * lax.rsqrt(ms + EPS) * g) * (1.0 + scale) + shift


def _ada_kernel(c_ref, w_ref, b_ref, o_ref):
    s = _silu(c_ref[...])
    o_ref[...] = jnp.dot(s, w_ref[...], precision=lax.Precision.HIGHEST,
                         preferred_element_type=F32) + b_ref[...]


def _ada_modulation(cond8, ada_w, ada_b):
    depth, d, n = ada_w.shape
    return pl.pallas_call(
        _ada_kernel,
        grid=(depth, n // ADA_TN),
        in_specs=[
            pl.BlockSpec((8, d), lambda i, j: (0, 0)),
            pl.BlockSpec((None, d, ADA_TN), lambda i, j: (i, 0, j)),
            pl.BlockSpec((None, 1, ADA_TN), lambda i, j: (i, 0, j)),
        ],
        out_specs=pl.BlockSpec((None, 8, ADA_TN), lambda i, j: (i, 0, j)),
        out_shape=jax.ShapeDtypeStruct((depth, 8, n), F32),
        compiler_params=_cparams(("parallel", "parallel")),
        name="ada_mod",
    )(cond8, ada_w, ada_b.reshape(depth, 1, n))


def _ffn_kernel(*refs, k, mixed):
    if mixed:
        x_ref, mod_ref, g_ref, win_ref, wout_ref, mix_ref, wmix_ref, o_ref = refs
    else:
        x_ref, mod_ref, g_ref, win_ref, wout_ref, o_ref = refs
    tm = x_ref.shape[0]
    sub = min(FFN_SUB, tm)
    for s in range(tm // sub):
        rs = slice(s * sub, (s + 1) * sub)
        x = x_ref[rs, :]
        if mixed:
            x = x + mod_ref[5:6, :] * jnp.dot(mix_ref[rs, :], wmix_ref[...], preferred_element_type=F32)
        h = _rms_mod(x, g_ref[...], mod_ref[3 * k:3 * k + 1, :], mod_ref[3 * k + 1:3 * k + 2, :])
        ab = jnp.dot(h.astype(BF16), win_ref[...], preferred_element_type=F32)
        act = (_silu(ab[:, :D_FF]) * ab[:, D_FF:]).astype(BF16)
        y = jnp.dot(act, wout_ref[...], preferred_element_type=F32)
        o_ref[rs, :] = x + (FFN_RESIDUAL * mod_ref[3 * k + 2:3 * k + 3, :]) * y


def _ffn_half(x, mod8, mod_row, k, g, w_in, w_out, which, mix=None, w_mix=None):
    bsz, seq, d = x.shape
    tm = min(FFN_TM, seq)
    layer, half = which
    pick = lambda shape: pl.BlockSpec((None, None) + tuple(shape[2:]), lambda b, t: (layer, half, 0, 0),
                                      pipeline_mode=pl.Buffered(1))
    in_specs = [
        pl.BlockSpec((None, tm, d), lambda b, t: (b, t, 0)),
        pl.BlockSpec((None, N_MOD, d), lambda b, t: (mod_row(b), 0, 0)),
        pl.BlockSpec((1, d), lambda b, t: (0, 0)),
        pick(w_in.shape),
        pick(w_out.shape),
    ]
    args = [x, mod8, g.reshape(1, d), w_in, w_out]
    if mix is not None:
        in_specs += [pl.BlockSpec((None, tm, mix.shape[-1]), lambda b, t: (b, t, 0)),
                     _resident(w_mix.shape)]
        args += [mix, w_mix]
    return pl.pallas_call(
        functools.partial(_ffn_kernel, k=k, mixed=mix is not None),
        grid=(bsz, seq // tm),
        in_specs=in_specs,
        out_specs=pl.BlockSpec((None, tm, d), lambda b, t: (b, t, 0)),
        out_shape=jax.ShapeDtypeStruct(x.shape, F32),
        compiler_params=_cparams(("parallel", "parallel")),
        name="ffn_half",
    )(*args)


def _pair_tables(lg_row, expo):
    return jnp.exp(lg_row * expo)


def _bd_mask():
    r = lax.broadcasted_iota(jnp.int32, (2 * RET_DK, 2 * RET_DV), 0)
    c = lax.broadcasted_iota(jnp.int32, (2 * RET_DK, 2 * RET_DV), 1)
    same_head = jnp.where(r < RET_DK, 0, 1) == jnp.where(c < RET_DV, 0, 1)
    return jnp.where(same_head, 1.0, 0.0)


def _abproj_kernel(x_ref, mod_ref, g_ref, w_ref, cos_ref, sin_ref, lgq_ref, lgv_ref, s0_ref,
                   p_ref, q_ref, k_ref, v_ref, sg_ref, cb_ref, sfin_ref, st_scr, *, tm, nt):
    t = pl.program_id(1)

    @pl.when(t == 0)
    def _():
        st_scr[...] = s0_ref[...]

    q0, k0 = POOL_WIDTH, POOL_WIDTH + RET_QK_W
    v0, g0 = k0 + RET_QK_W, k0 + RET_QK_W + RET_V_W
    row = lax.broadcasted_iota(jnp.int32, (RET_CHUNK, LANES), 0).astype(F32)
    bdm = _bd_mask()
    k_scale = RET_DK ** -0.5
    sub = min(AB_SUB, tm)
    even_lane = (lax.broadcasted_iota(jnp.int32, (sub, LANES), 1) & 1) == 0

    def pair_partner(v):
        return jnp.where(even_lane, pltpu.roll(v, LANES - 1, axis=1), pltpu.roll(v, 1, axis=1))
    for s in reversed(range(tm // sub)):
        r0 = s * sub
        ts = slice(r0, r0 + sub)
        h = _rms_mod(x_ref[ts, :], g_ref[...], mod_ref[3:4, :], mod_ref[4:5, :])
        proj = jnp.dot(h.astype(BF16), w_ref[...], preferred_element_type=F32)
        p_ref[ts, :] = proj[:, :POOL_WIDTH]
        v_ref[ts, :] = proj[:, v0:v0 + RET_V_W].astype(BF16)
        sg_ref[ts, :] = _silu(proj[:, g0:g0 + RET_V_W]).astype(BF16)
        cos = cos_ref[ts, :]
        sin = sin_ref[ts, :]
        for j in range(RET_PAIRS):
            sl = slice(LANES * j, LANES * (j + 1))
            qp = proj[:, q0 + LANES * j:q0 + LANES * (j + 1)]
            kp = proj[:, k0 + LANES * j:k0 + LANES * (j + 1)]
            qr = qp * cos + pair_partner(qp) * sin
            kr = (kp * cos + pair_partner(kp) * sin) * k_scale
            q_ref[ts, sl] = qr.astype(BF16)
            k_ref[ts, sl] = kr.astype(BF16)
            lgq = lgq_ref[:, sl]
            lgv = lgv_ref[:, 2 * LANES * j:2 * LANES * (j + 1)]
            xi = _pair_tables(lgq, float(RET_CHUNK) - row)
            zeta = _pair_tables(lgq, row)
            dec = jnp.exp(lgv * float(RET_CHUNK))
            vsl = slice(2 * LANES * j, 2 * LANES * (j + 1))
            for c in reversed(range(sub // RET_CHUNK)):
                ls = slice(c * RET_CHUNK, (c + 1) * RET_CHUNK)
                rs = slice(r0 + c * RET_CHUNK, r0 + (c + 1) * RET_CHUNK)
                bd = st_scr[j]
                qx = (qr[ls] * xi).astype(BF16)
                cb_ref[rs, vsl] = jnp.dot(qx, bd.astype(BF16),
                                          preferred_element_type=F32).astype(cb_ref.dtype)
                kz = (kr[ls] * zeta).astype(BF16)
                upd = lax.dot_general(kz, v_ref[rs, vsl], (((0,), (0,)), ((), ())),
                                      preferred_element_type=F32)
                st_scr[j] = dec * bd + bdm * upd

    @pl.when(t == nt - 1)
    def _():
        sfin_ref[...] = st_scr[...]


def _ab_project(x, mod8, mod_row, g, w_ext, cos, sin, lg_q, lg_v, s0):
    bsz, seq, d = x.shape
    tm = min(AB_TM, seq)
    nt = seq // tm
    rev = lambda b, t: (b, nt - 1 - t, 0)
    tok = lambda w, dt: jax.ShapeDtypeStruct((bsz, seq, w), dt)
    st_shape = (RET_PAIRS, 2 * RET_DK, 2 * RET_DV)
    return pl.pallas_call(
        functools.partial(_abproj_kernel, tm=tm, nt=nt),
        grid=(bsz, nt),
        in_specs=[
            pl.BlockSpec((None, tm, d), rev),
            pl.BlockSpec((None, N_MOD, d), lambda b, t: (mod_row(b), 0, 0)),
            pl.BlockSpec((1, d), lambda b, t: (0, 0)),
            _resident(w_ext.shape),
            pl.BlockSpec((tm, LANES), lambda b, t: (nt - 1 - t, 0)),
            pl.BlockSpec((tm, LANES), lambda b, t: (nt - 1 - t, 0)),
            pl.BlockSpec((1, RET_QK_W), lambda b, t: (0, 0)),
            pl.BlockSpec((1, RET_V_W), lambda b, t: (0, 0)),
            pl.BlockSpec((None,) + st_shape, lambda b, t: (b, 0, 0, 0)),
        ],
        out_specs=[
            pl.BlockSpec((None, tm, POOL_WIDTH), rev),
            pl.BlockSpec((None, tm, RET_QK_W), rev),
            pl.BlockSpec((None, tm, RET_QK_W), rev),
            pl.BlockSpec((None, tm, RET_V_W), rev),
            pl.BlockSpec((None, tm, RET_V_W), rev),
            pl.BlockSpec((None, tm, RET_V_W), rev),
            pl.BlockSpec((None,) + st_shape, lambda b, t: (b, 0, 0, 0)),
        ],
        out_shape=[tok(POOL_WIDTH, F32), tok(RET_QK_W, BF16), tok(RET_QK_W, BF16),
                   tok(RET_V_W, BF16), tok(RET_V_W, BF16), tok(RET_V_W, BF16),
                   jax.ShapeDtypeStruct((bsz,) + st_shape, F32)],
        scratch_shapes=[pltpu.VMEM(st_shape, F32)],
        compiler_params=_cparams(("arbitrary", "arbitrary")),
        name="ab_project",
    )(x, mod8, g.reshape(1, d), w_ext, cos, sin, lg_q, lg_v, s0)


def _retout_kernel(x_ref, mod_ref, pprev_ref, pcur_ref, pnext_ref, pm_ref, invc_ref,
                   q_ref, k_ref, v_ref, sg_ref, cb_ref, wpool_ref, pscale_ref, rng_ref, wout_ref,
                   lgfq_ref, lgfv_ref, lgbv_ref, s0_ref,
                   o_ref, sfin_ref, st_scr, ycat_scr, pe_scr, *, tm, nt):
    t = pl.program_id(1)

    @pl.when(t == 0)
    def _():
        st_scr[...] = s0_ref[...]

    pe_scr[0:POOL_HALO, :] = pprev_ref[...]
    pe_scr[POOL_HALO:POOL_HALO + tm, :] = pcur_ref[...]
    pe_scr[POOL_HALO + tm:2 * POOL_HALO + tm, :] = pnext_ref[...]
    pe_scr[2 * POOL_HALO + tm:, :] = jnp.zeros((POOL_SUB - 2 * POOL_HALO, POOL_WIDTH), F32)
    lane_grp = jnp.right_shift(lax.broadcasted_iota(jnp.int32, (POOL_SUB, POOL_WIDTH), 1),
                               POOL_CG.bit_length() - 1)
    for s in range(tm // POOL_SUB):
        r0 = s * POOL_SUB
        pe = pe_scr[r0:r0 + 2 * POOL_SUB, :]
        pe_hi = pe.astype(BF16)
        pe_lo = (pe - pe_hi.astype(F32)).astype(BF16)
        pooled = jnp.zeros((POOL_SUB, POOL_WIDTH), F32)
        for gi in range(POOL_GROUPS):
            win = pm_ref[s, gi]
            tot = jnp.dot(win, pe_hi, preferred_element_type=F32) \
                + jnp.dot(win, pe_lo, preferred_element_type=F32)
            pooled = jnp.where(lane_grp == gi, tot, pooled)
        rs = slice(r0, r0 + POOL_SUB)
        diffs = pooled * invc_ref[rs, :] - pcur_ref[rs, :]
        pool_y = jnp.dot(diffs.astype(BF16), wpool_ref[...], preferred_element_type=F32) * pscale_ref[...]
        ycat_scr[rs, 0:POOL_WIDTH] = pool_y.astype(BF16)

    ri = lax.broadcasted_iota(jnp.int32, (RET_CHUNK, RET_CHUNK), 0)
    ci = lax.broadcasted_iota(jnp.int32, (RET_CHUNK, RET_CHUNK), 1)
    rel = (ri - ci).astype(F32)
    row = lax.broadcasted_iota(jnp.int32, (RET_CHUNK, LANES), 0).astype(F32)
    lane = lax.broadcasted_iota(jnp.int32, (RET_CHUNK, LANES), 1)
    bdm = _bd_mask()
    zero_bf = jnp.zeros((RET_CHUNK, LANES), BF16)
    lane2 = lax.broadcasted_iota(jnp.int32, (RET_CHUNK, 2 * RET_DV), 1)
    zero_bf2 = jnp.zeros((RET_CHUNK, 2 * RET_DV), BF16)
    tables = []
    for j in range(RET_PAIRS):
        sl = slice(LANES * j, LANES * (j + 1))
        vsl = slice(2 * LANES * j, 2 * LANES * (j + 1))
        lgq = lgfq_ref[:, sl]
        xi = _pair_tables(lgq, row + 1.0)
        zeta = _pair_tables(lgq, float(RET_CHUNK - 1) - row)
        dec = jnp.exp(lgfv_ref[:, vsl] * float(RET_CHUNK))
        masks = []
        for a in range(2):
            hs = slice(2 * LANES * j + LANES * a, 2 * LANES * j + LANES * (a + 1))
            mf = jnp.exp(lgfv_ref[:, hs] * jnp.maximum(rel, 0.0))
            mb = jnp.exp(lgbv_ref[:, hs] * jnp.maximum(-rel, 0.0))
            masks.append(jnp.where(rel > 0, mf, jnp.where(rel < 0, mb, 2.0)))
        tables.append((xi, zeta, dec, jnp.concatenate(masks, axis=1)))
    for c in range(tm // RET_CHUNK):
        rs = slice(c * RET_CHUNK, (c + 1) * RET_CHUNK)
        for j in range(RET_PAIRS):
            sl = slice(LANES * j, LANES * (j + 1))
            vsl = slice(2 * LANES * j, 2 * LANES * (j + 1))
            xi, zeta, dec, mask2 = tables[j]
            q2 = q_ref[rs, sl]
            k2 = k_ref[rs, sl]
            v2 = v_ref[rs, vsl]
            bd = st_scr[j]
            qx = (q2.astype(F32) * xi).astype(BF16)
            k_bd = jnp.concatenate([jnp.where(lane < RET_DK, k2, zero_bf),
                                    jnp.where(lane >= RET_DK, k2, zero_bf)], axis=0)
            s2 = lax.dot_general(q2, k_bd, (((1,), (1,)), ((), ())), preferred_element_type=F32)
            p2 = (s2 * mask2).astype(BF16)
            v_bd = jnp.concatenate([jnp.where(lane2 < RET_DV, v2, zero_bf2),
                                    jnp.where(lane2 >= RET_DV, v2, zero_bf2)], axis=0)
            ret = jnp.dot(qx, bd.astype(BF16), preferred_element_type=F32) + cb_ref[rs, vsl].astype(F32) \
                + jnp.dot(p2, v_bd, preferred_element_type=F32)
            for a in range(2):
                hs_v = slice(LANES * a, LANES * (a + 1))
                r = ret[:, hs_v]
                ms = jnp.mean(r * r, axis=-1, keepdims=True)
                hcol = 2 * LANES * j + LANES * a
                y = (r * lax.rsqrt(ms + EPS) * rng_ref[:, hcol:hcol + LANES]) \
                    * sg_ref[rs, hcol:hcol + LANES].astype(F32)
                ycat_scr[rs, POOL_WIDTH + hcol:POOL_WIDTH + hcol + LANES] = y.astype(BF16)
            kz = (k2.astype(F32) * zeta).astype(BF16)
            upd = lax.dot_general(kz, v2, (((0,), (0,)), ((), ())), preferred_element_type=F32)
            st_scr[j] = dec * bd + bdm * upd
        yy = jnp.dot(ycat_scr[rs, :], wout_ref[...], preferred_element_type=F32)
        o_ref[rs, :] = x_ref[rs, :] + mod_ref[5:6, :] * yy

    @pl.when(t == nt - 1)
    def _():
        sfin_ref[...] = st_scr[...]


def _pool_constants(tm, seq):
    nt = seq // tm
    nsub = tm // POOL_SUB
    kinds = [0] if nt == 1 else [0, 1, nt - 1]
    mats = np.zeros((len(kinds), nsub, POOL_GROUPS, POOL_SUB, 2 * POOL_SUB), np.float32)
    invc = np.zeros((len(kinds), tm, POOL_WIDTH), np.float32)
    for vi, tile in enumerate(kinds):
        for s in range(nsub):
            pos = tile * tm + s * POOL_SUB + np.arange(POOL_SUB)
            colpos = tile * tm - POOL_HALO + s * POOL_SUB + np.arange(2 * POOL_SUB)
            for gi, w in enumerate(POOL_WINDOWS):
                lo = w // 2
                hi = w - 1 - lo
                start = np.maximum(pos - lo, 0)
                end = np.minimum(pos + hi + 1, seq)
                mats[vi, s, gi] = (colpos[None, :] >= start[:, None]) & (colpos[None, :] < end[:, None])
                invc[vi, s * POOL_SUB:(s + 1) * POOL_SUB, gi * POOL_CG:(gi + 1) * POOL_CG] = \
                    (1.0 / (end - start))[:, None]
    return jnp.asarray(mats, BF16), jnp.asarray(invc, F32), nt


def _ret_out(x, mod8, mod_row, p, q, k, v, sg, cb, wpool_bd, pscale, rng, w_out, lgf_q, lgf_v, lgb_v, s0):
    bsz, seq, d = x.shape
    tm = min(MIX_TM, seq)
    pm, invc, nt = _pool_constants(tm, seq)
    hb = tm // POOL_HALO
    nhb = seq // POOL_HALO

    def variant(t):
        if nt == 1:
            return 0
        return jnp.where(t == 0, 0, jnp.where(t == nt - 1, 2, 1))

    cur = lambda b, t: (b, t, 0)
    st_shape = (RET_PAIRS, 2 * RET_DK, 2 * RET_DV)
    tokspec = lambda w: pl.BlockSpec((None, tm, w), cur)
    return pl.pallas_call(
        functools.partial(_retout_kernel, tm=tm, nt=nt),
        grid=(bsz, nt),
        in_specs=[
            tokspec(d),
            pl.BlockSpec((None, N_MOD, d), lambda b, t: (mod_row(b), 0, 0)),
            pl.BlockSpec((None, POOL_HALO, POOL_WIDTH), lambda b, t: (b, jnp.maximum(t * hb - 1, 0), 0)),
            tokspec(POOL_WIDTH),
            pl.BlockSpec((None, POOL_HALO, POOL_WIDTH),
                         lambda b, t: (b, jnp.minimum((t + 1) * hb, nhb - 1), 0)),
            pl.BlockSpec((None, tm // POOL_SUB, POOL_GROUPS, POOL_SUB, 2 * POOL_SUB),
                         lambda b, t: (variant(t), 0, 0, 0, 0)),
            pl.BlockSpec((None, tm, POOL_WIDTH), lambda b, t: (variant(t), 0, 0)),
            tokspec(RET_QK_W), tokspec(RET_QK_W), tokspec(RET_V_W), tokspec(RET_V_W), tokspec(RET_V_W),
            _resident(wpool_bd.shape),
            pl.BlockSpec((1, POOL_WIDTH), lambda b, t: (0, 0)),
            pl.BlockSpec((1, RET_V_W), lambda b, t: (0, 0)),
            _resident(w_out.shape),
            pl.BlockSpec((1, RET_QK_W), lambda b, t: (0, 0)),
            pl.BlockSpec((1, RET_V_W), lambda b, t: (0, 0)),
            pl.BlockSpec((1, RET_V_W), lambda b, t: (0, 0)),
            pl.BlockSpec((None,) + st_shape, lambda b, t: (b, 0, 0, 0)),
        ],
        out_specs=[
            tokspec(d),
            pl.BlockSpec((None,) + st_shape, lambda b, t: (b, 0, 0, 0)),
        ],
        out_shape=[jax.ShapeDtypeStruct(x.shape, F32),
                   jax.ShapeDtypeStruct((bsz,) + st_shape, F32)],
        scratch_shapes=[pltpu.VMEM(st_shape, F32),
                        pltpu.VMEM((tm, d), BF16),
                        pltpu.VMEM((tm + POOL_SUB, POOL_WIDTH), F32)],
        compiler_params=_cparams(("arbitrary", "arbitrary")),
        name="ret_out",
    )(x, mod8, p, p, p, pm, invc, q, k, v, sg, cb, wpool_bd, pscale, rng, w_out,
      lgf_q, lgf_v, lgb_v, s0)


def _head_norm_rope(main, aux, g_main, g_aux):
    ms = jnp.sum(main * main, axis=-1, keepdims=True) * (1.0 / MLA_QK)
    r = lax.rsqrt(ms + EPS)
    return (main * r) * g_main + (aux * r) * g_aux


def _mlaproj_kernel(x_ref, mod_ref, g_ref, win_ref, qag_ref, kvag_ref, wqb_ref, wkv_ref,
                    qg_ref, qgs_ref, kg_ref, kgs_ref, qa_t_ref, qb_t_ref, ka_t_ref, kb_t_ref,
                    q_ref, k_ref, v_ref, qan_scr, kvn_scr, kr_scr, gt_scr):
    h = _rms_mod(x_ref[...], g_ref[...], mod_ref[3:4, :], mod_ref[4:5, :])
    proj = jnp.dot(h.astype(BF16), win_ref[...], preferred_element_type=F32)
    qa = proj[:, :MLA_Q_RANK]
    kva = proj[:, MLA_Q_RANK:MLA_Q_RANK + MLA_KV_RANK]
    r0 = MLA_Q_RANK + MLA_KV_RANK
    kr_scr[0] = proj[:, r0:r0 + HEAD_SLAB]
    kr_scr[1] = proj[:, r0 + HEAD_SLAB:r0 + 2 * HEAD_SLAB]
    qan = qa * lax.rsqrt(jnp.mean(qa * qa, axis=-1, keepdims=True) + EPS) * qag_ref[...]
    kvn = kva * lax.rsqrt(jnp.mean(kva * kva, axis=-1, keepdims=True) + EPS) * kvag_ref[...]
    qan_scr[...] = qan.astype(BF16)
    kvn_scr[...] = kvn.astype(BF16)
    gt_scr[0] = qg_ref[...] * qa_t_ref[...]
    gt_scr[1] = qgs_ref[...] * qb_t_ref[...]
    gt_scr[2] = kg_ref[...] * ka_t_ref[...]
    gt_scr[3] = kgs_ref[...] * kb_t_ref[...]
    tm = proj.shape[0]
    ones_rows = jnp.where(lax.broadcasted_iota(jnp.int32, (V_ROWS - MLA_V, tm), 0) == 0,
                          1.0, 0.0).astype(BF16)

    def head(hd, carry):
        c0 = pl.multiple_of(hd * (2 * HEAD_SLAB), 2 * HEAD_SLAB)
        qh = jnp.dot(qan_scr[...], wqb_ref[:, pl.ds(c0, 2 * HEAD_SLAB)], preferred_element_type=F32)
        kvh = jnp.dot(kvn_scr[...], wkv_ref[:, pl.ds(c0, 2 * HEAD_SLAB)], preferred_element_type=F32)
        q_ref[hd] = _head_norm_rope(qh[:, :HEAD_SLAB], qh[:, HEAD_SLAB:], gt_scr[0], gt_scr[1]).astype(BF16)
        k_ref[hd] = _head_norm_rope(kvh[:, :HEAD_SLAB] + kr_scr[0], kr_scr[1],
                                    gt_scr[2], gt_scr[3]).astype(BF16)
        v_ref[hd, 0:MLA_V, :] = kvh[:, HEAD_SLAB:].T.astype(BF16)
        v_ref[hd, MLA_V:V_ROWS, :] = ones_rows
        return carry
    lax.fori_loop(0, MLA_HEADS, head, 0, unroll=MLA_HEAD_UNROLL)


def _mla_project(x, mod8, mod_row, g, w_in, qa_g, kva_g, w_qb, w_kv, qn_g, qn_gs, kn_g, kn_gs,
                 q_ta, q_tb, k_ta, k_tb):
    bsz, seq, d = x.shape
    tm = min(MLA_TM, seq)
    head_out = jax.ShapeDtypeStruct((bsz, MLA_HEADS, seq, HEAD_SLAB), BF16)
    hspec = pl.BlockSpec((None, MLA_HEADS, tm, HEAD_SLAB), lambda b, t: (b, 0, t, 0))
    tspec = pl.BlockSpec((tm, HEAD_SLAB), lambda b, t: (t, 0))
    row = lambda n: pl.BlockSpec((1, n), lambda b, t: (0, 0))
    return pl.pallas_call(
        _mlaproj_kernel,
        grid=(bsz, seq // tm),
        in_specs=[
            pl.BlockSpec((None, tm, d), lambda b, t: (b, t, 0)),
            pl.BlockSpec((None, N_MOD, d), lambda b, t: (mod_row(b), 0, 0)),
            row(d),
            _resident(w_in.shape), row(MLA_Q_RANK), row(MLA_KV_RANK),
            _resident(w_qb.shape), _resident(w_kv.shape),
            row(HEAD_SLAB), row(HEAD_SLAB), row(HEAD_SLAB), row(HEAD_SLAB),
            tspec, tspec, tspec, tspec,
        ],
        out_specs=[hspec,
                   hspec,
                   pl.BlockSpec((None, MLA_HEADS, V_ROWS, tm), lambda b, t: (b, 0, 0, t))],
        out_shape=[head_out,
                   head_out,
                   jax.ShapeDtypeStruct((bsz, MLA_HEADS, V_ROWS, seq), BF16)],
        scratch_shapes=[pltpu.VMEM((tm, MLA_Q_RANK), BF16), pltpu.VMEM((tm, MLA_KV_RANK), BF16),
                        pltpu.VMEM((2, tm, HEAD_SLAB), F32), pltpu.VMEM((4, tm, HEAD_SLAB), F32)],
        compiler_params=_cparams(("parallel", "parallel")),
        name="mla_project",
    )(x, mod8, g.reshape(1, d), w_in, qa_g, kva_g, w_qb, w_kv, qn_g, qn_gs, kn_g, kn_gs,
      q_ta, q_tb, k_ta, k_tb)


def _attn_kernel(*refs, n_spans, span_blocks, tk, group_spans, tq, n_q):
    if n_spans:
        q_ref, kc_ref, vct_ref, k_ref, vt_ref, o_ref, m_scr, acc_scr, s_scr, mb_scr = refs
    else:
        q_ref, kc_ref, vct_ref, o_ref, m_scr, acc_scr = refs

    def colmax(st):
        return jnp.max(st, axis=0, keepdims=True)

    def finalize(row0):
        acc = acc_scr[...]
        out_t = acc[0:MLA_V, :] * (1.0 / acc[MLA_V:MLA_V + 1, :])
        o_ref[pl.ds(row0, tq), :] = out_t.T.astype(o_ref.dtype)

    def softmax_pv(st, m_blk, vtb, first):
        if first:
            m_new = m_blk
        else:
            m_prev = m_scr[...]
            m_new = jnp.maximum(m_prev, m_blk)
        p = jnp.exp2(st - m_new).astype(BF16)
        pv = jnp.dot(vtb, p, preferred_element_type=F32)
        if first:
            acc_scr[...] = pv
        else:
            acc_scr[...] = jnp.exp2(m_prev - m_new) * acc_scr[...] + pv
        m_scr[...] = m_new

    if not n_spans:
        st_c = lax.dot_general(kc_ref[...], q_ref[...], (((1,), (1,)), ((), ())),
                               preferred_element_type=F32)
        softmax_pv(st_c, colmax(st_c), vct_ref[...], True)
        finalize(0)
        return

    span = span_blocks * tk
    lc = kc_ref.shape[0]

    def query_tile(q, prev_row0):
        def produce(slot, kb):
            rows = kb.shape[0]
            st = jnp.dot(kb, q, preferred_element_type=F32)
            s_scr[slot, 0:rows, :] = st
            blk = min(tk, rows)
            for sb in range(rows // blk):
                mb_scr[slot, sb] = colmax(st[sb * blk:(sb + 1) * blk, :])

        def consume(slot, v_off):
            for sb in range(span_blocks):
                softmax_pv(s_scr[slot, sb * tk:(sb + 1) * tk, :], mb_scr[slot, sb],
                           vt_ref[:, pl.ds(v_off + sb * tk, tk)], False)

        produce(0, k_ref[0:span, :])
        if prev_row0 is not None:
            finalize(prev_row0)
        m_scr[...] = jnp.full(m_scr.shape, -jnp.inf, F32)
        acc_scr[...] = jnp.zeros(acc_scr.shape, F32)

        def group(i, carry):
            base = i * (group_spans * span)
            for u in range(group_spans):
                cur = pl.multiple_of(base + u * span, span)
                nxt = pl.multiple_of(base + (u + 1) * span, span)
                produce((u + 1) % 2, k_ref[pl.ds(nxt, span), :])
                consume(u % 2, cur)
            return carry
        lax.fori_loop(0, n_spans // group_spans - 1, group, 0)
        e0 = n_spans - group_spans
        for u in range(group_spans):
            cur = (e0 + u) * span
            if u + 1 < group_spans:
                produce((u + 1) % 2, k_ref[cur + span:cur + 2 * span, :])
            else:
                produce((u + 1) % 2, kc_ref[...])
            consume(u % 2, cur)
        c_slot = group_spans % 2
        softmax_pv(s_scr[c_slot, 0:lc, :], mb_scr[c_slot, 0], vct_ref[...], False)

    def q_tile_t(row0):
        return q_ref[pl.ds(row0, tq), :].astype(F32).T.astype(BF16)

    query_tile(q_tile_t(0), None)

    def later_tile(i, carry):
        off = pl.multiple_of(i * tq, tq)
        query_tile(q_tile_t(off), pl.multiple_of(off - tq, tq))
        return carry
    lax.fori_loop(1, n_q, later_tile, 0)
    finalize((n_q - 1) * tq)


def _attention(q, kc, vct, k=None, vt=None):
    bsz, nh, lq, hs = q.shape
    lc = kc.shape[2]
    tq = min(ATT_TQ, lq)
    qspec = pl.BlockSpec((None, None, lq, hs), lambda b, h: (b, h, 0, 0))
    kfull = lambda n: pl.BlockSpec((None, None, n, hs), lambda b, h: (b, h, 0, 0))
    vfull = lambda n: pl.BlockSpec((None, None, V_ROWS, n), lambda b, h: (b, h, 0, 0))
    in_specs = [qspec, kfull(lc), vfull(lc)]
    args = [q, kc, vct]
    n_spans = 0
    grp = 0
    tk = ATT_TK
    span_blocks = 1
    if k is not None:
        lk = k.shape[2]
        tk = min(ATT_TK, lk)
        span_blocks = min(ATT_SPAN, lk // tk)
        n_spans = lk // (tk * span_blocks)
        grp = min(ATT_GROUP, n_spans)
        assert lk % (tk * span_blocks) == 0 and n_spans % grp == 0, "latent keys are consumed in whole groups"
        assert grp % 2 == 0 or n_spans == grp, "score slots alternate within a group"
        assert lc <= tk, "the context block reuses a latent score slot"
        in_specs += [kfull(lk), vfull(lk)]
        args += [k, vt]
    scratch = [pltpu.VMEM((1, tq), F32), pltpu.VMEM((V_ROWS, tq), F32)]
    if n_spans:
        scratch += [pltpu.VMEM((2, span_blocks * tk, tq), F32), pltpu.VMEM((2, span_blocks, 1, tq), F32)]
    return pl.pallas_call(
        functools.partial(_attn_kernel, n_spans=n_spans, span_blocks=span_blocks, tk=tk,
                          group_spans=grp, tq=tq, n_q=lq // tq),
        grid=(bsz, nh),
        in_specs=in_specs,
        out_specs=pl.BlockSpec((None, lq, hs), lambda b, h: (b, 0, h)),
        out_shape=jax.ShapeDtypeStruct((bsz, lq, nh * hs), BF16),
        scratch_shapes=scratch,
        compiler_params=_cparams(("parallel", "parallel")),
        name="mla_attention",
    )(*args)


def _pair_swap_cols(w):
    w2 = w.reshape(w.shape[:-1] + (w.shape[-1] // 2, 2))
    return jnp.stack([-w2[..., 1], w2[..., 0]], axis=-1).reshape(w.shape)


def _rope_angles(seq, rot_dim):
    pos = jnp.arange(seq)
    row = (pos // GRID_W).astype(F32)
    col = (pos % GRID_W).astype(F32)
    n_freq = rot_dim // 4
    inv = ROPE_BASE ** (-jnp.arange(n_freq, dtype=F32) / n_freq)
    ang = jnp.concatenate([row[:, None] * inv, col[:, None] * inv], axis=-1)
    return jnp.repeat(jnp.cos(ang), 2, axis=-1), jnp.repeat(jnp.sin(ang), 2, axis=-1)


def _ret_tables(seq, ctx_len):
    cos, sin = _rope_angles(seq, RET_DK)
    cos2, sin2 = jnp.tile(cos, (1, 2)), jnp.tile(sin, (1, 2))
    pair_sign = jnp.tile(jnp.asarray([-1.0, 1.0], F32), LANES // 2)
    return cos2, sin2 * pair_sign, jnp.ones((ctx_len, LANES), F32), jnp.zeros((ctx_len, LANES), F32)


def _mla_tables(seq, ctx_len):
    cos, sin = _rope_angles(seq, MLA_ROPE)
    one = jnp.ones((seq, MLA_NOPE), F32)
    zero_tail = jnp.zeros((seq, HEAD_SLAB - MLA_QK), F32)
    zero_head = jnp.zeros((seq, MLA_NOPE), F32)
    ta = jnp.concatenate([one, cos, zero_tail], axis=-1)
    tb = jnp.concatenate([zero_head, sin, zero_tail], axis=-1)
    ident = jnp.concatenate([jnp.ones((ctx_len, MLA_QK), F32),
                             jnp.zeros((ctx_len, HEAD_SLAB - MLA_QK), F32)], axis=-1)
    return ta, tb, ident, jnp.zeros((ctx_len, HEAD_SLAB), F32)


def _rope_slab(rope_cols):
    lead = rope_cols.shape[:-1]
    return jnp.concatenate([jnp.zeros(lead + (MLA_NOPE,), F32), rope_cols,
                            jnp.zeros(lead + (HEAD_SLAB - MLA_QK,), F32)], axis=-1)


def _slab_gains(gvec):
    rope = gvec[MLA_NOPE:].reshape(MLA_ROPE // 2, 2)
    swapped = jnp.stack([rope[:, 1], rope[:, 0]], axis=-1).reshape(MLA_ROPE)
    main = jnp.concatenate([gvec, jnp.zeros((HEAD_SLAB - MLA_QK,), F32)])
    return main.reshape(1, HEAD_SLAB), _rope_slab(swapped).reshape(1, HEAD_SLAB)


def _prep_even(ab_w_in, pool_w, ab_w_out):
    w_ext = ab_w_in.astype(BF16)
    wpool_bd = jnp.zeros((POOL_WIDTH, POOL_WIDTH), F32)
    for gi in range(POOL_GROUPS):
        s = slice(gi * POOL_CG, (gi + 1) * POOL_CG)
        wpool_bd = wpool_bd.at[s, s].set(pool_w[gi])
    return w_ext, wpool_bd.astype(BF16), ab_w_out.astype(BF16)


def _prep_odd(w_in, w_qb, w_kvb, w_out):
    wkr = w_in[:, MLA_Q_RANK + MLA_KV_RANK:]
    w_in_ext = jnp.concatenate([w_in[:, :MLA_Q_RANK + MLA_KV_RANK], _rope_slab(wkr),
                                _rope_slab(_pair_swap_cols(wkr))], axis=-1).astype(BF16)
    wq3 = w_qb.reshape(MLA_Q_RANK, MLA_HEADS, MLA_QK)
    wq_main = jnp.concatenate([wq3, jnp.zeros((MLA_Q_RANK, MLA_HEADS, HEAD_SLAB - MLA_QK), F32)], axis=-1)
    wq_aux = _rope_slab(_pair_swap_cols(wq3[..., MLA_NOPE:]))
    w_qb_ext = jnp.concatenate([wq_main, wq_aux], axis=-1).reshape(
        MLA_Q_RANK, MLA_HEADS * 2 * HEAD_SLAB).astype(BF16)
    wkv3 = w_kvb.reshape(MLA_KV_RANK, MLA_HEADS, MLA_NOPE + MLA_V)
    wk_slab = jnp.concatenate([wkv3[..., :MLA_NOPE],
                               jnp.zeros((MLA_KV_RANK, MLA_HEADS, HEAD_SLAB - MLA_NOPE), F32)], axis=-1)
    w_kv_ext = jnp.concatenate([wk_slab, wkv3[..., MLA_NOPE:]], axis=-1).reshape(
        MLA_KV_RANK, MLA_HEADS * 2 * HEAD_SLAB).astype(BF16)
    return w_in_ext, w_qb_ext, w_kv_ext, w_out.astype(BF16)


def kernel(x, c, ctx, c_ctx, ada_w, ada_b, norm_g, ffn_w_in, ffn_w_out, ab_w_in, pool_w, pool_scale,
           ret_log_decay, ret_norm_g, ab_w_out, mla_w_in, mla_qa_g, mla_kva_g, mla_w_qb, mla_w_kvb,
           mla_qn_g, mla_kn_g, mla_w_out):
    bsz, seq, d = x.shape
    ctx_len = ctx.shape[1]
    depth = ada_w.shape[0]
    ctx_row = bsz
    cond8 = jnp.zeros((8, d), F32).at[:bsz].set(c).at[ctx_row].set(c_ctx)
    mod_all = _ada_modulation(cond8, ada_w, ada_b).reshape(depth, 8, N_MOD, d)
    lat_row = lambda b: b
    ctx_mod_row = lambda b: ctx_row

    ret_cos, ret_sin, ret_cos_c, ret_sin_c = _ret_tables(seq, ctx_len)
    m_ta, m_tb, m_ta_c, m_tb_c = _mla_tables(seq, ctx_len)
    q_scale = MLA_QK ** -0.5 * LOG2_E
    zero_state = jnp.zeros((bsz, RET_PAIRS, 2 * RET_DK, 2 * RET_DV), F32)
    ffn_in_b, ffn_out_b = ffn_w_in.astype(BF16), ffn_w_out.astype(BF16)

    xc = ctx
    for i in range(depth):
        last = i == depth - 1
        j = i // 2
        mod8 = mod_all[i]

        def ctx_ffn(xc_in, k, g, which, mix_in=None, w_mix_in=None, mod8=mod8):
            flat = lambda a: None if a is None else a.reshape(1, bsz * ctx_len, a.shape[-1])
            out = _ffn_half(flat(xc_in), mod8, ctx_mod_row, k, g, ffn_in_b, ffn_out_b, which,
                            flat(mix_in), w_mix_in)
            return out.reshape(bsz, ctx_len, d)

        x = _ffn_half(x, mod8, lat_row, 0, norm_g[i, 0], ffn_in_b, ffn_out_b, (i, 0))
        xc = ctx_ffn(xc, 0, norm_g[i, 0], (i, 0))
        mix = mix_c = w_mix = None
        if i % 2 == 0:
            w_ext, wpool_bd, w_out = _prep_even(ab_w_in[j], pool_w[j], ab_w_out[j])
            lg = ret_log_decay[j]
            lgf_q = jnp.repeat(lg[0], RET_DK).reshape(1, RET_QK_W)
            lgb_q = jnp.repeat(lg[1], RET_DK).reshape(1, RET_QK_W)
            lgf_v = jnp.repeat(lg[0], RET_DV).reshape(1, RET_V_W)
            lgb_v = jnp.repeat(lg[1], RET_DV).reshape(1, RET_V_W)
            pscale = pool_scale[j].reshape(1, POOL_WIDTH)
            rng = ret_norm_g[j].reshape(1, RET_V_W)
            pc, qc, kc, vc, sgc, cbc, state_b = _ab_project(
                xc, mod8, ctx_mod_row, norm_g[i, 1], w_ext, ret_cos_c, ret_sin_c, lgb_q, lgb_v, zero_state)
            xc_mixed, state_f = _ret_out(xc, mod8, ctx_mod_row, pc, qc, kc, vc, sgc, cbc, wpool_bd, pscale,
                                         rng, w_out, lgf_q, lgf_v, lgb_v, zero_state)
            p, q, k, v, sg, cb, _ = _ab_project(
                x, mod8, lat_row, norm_g[i, 1], w_ext, ret_cos, ret_sin, lgb_q, lgb_v, state_b)
            x, _ = _ret_out(x, mod8, lat_row, p, q, k, v, sg, cb, wpool_bd, pscale, rng, w_out,
                            lgf_q, lgf_v, lgb_v, state_f)
        else:
            w_in_ext, w_qb_ext, w_kv_ext, w_out = _prep_odd(mla_w_in[j], mla_w_qb[j], mla_w_kvb[j],
                                                            mla_w_out[j])
            qa_g = mla_qa_g[j].reshape(1, MLA_Q_RANK)
            kva_g = mla_kva_g[j].reshape(1, MLA_KV_RANK)
            qn_g, qn_gs = _slab_gains(mla_qn_g[j])
            kn_g, kn_gs = _slab_gains(mla_kn_g[j])
            qc, kc, vc = _mla_project(xc, mod8, ctx_mod_row, norm_g[i, 1], w_in_ext, qa_g, kva_g, w_qb_ext,
                                      w_kv_ext, qn_g, qn_gs, kn_g, kn_gs,
                                      m_ta_c * q_scale, m_tb_c, m_ta_c, m_tb_c)
            q, k, v = _mla_project(x, mod8, lat_row, norm_g[i, 1], w_in_ext, qa_g, kva_g, w_qb_ext,
                                   w_kv_ext, qn_g, qn_gs, kn_g, kn_gs,
                                   m_ta * q_scale, m_tb * q_scale, m_ta, m_tb)
            mix, w_mix = _attention(q, kc, vc, k, v), w_out
            if not last:
                mix_c = _attention(qc, kc, vc)
                xc_mixed = xc
        x = _ffn_half(x, mod8, lat_row, 2, norm_g[i, 2], ffn_in_b, ffn_out_b, (i, 1), mix, w_mix)
        if not last:
            xc = ctx_ffn(xc_mixed, 2, norm_g[i, 2], (i, 1), mix_c, w_mix)
    return x
```
